```python
import math
import jax, jax.numpy as jnp
from jax import lax
import numpy as np

D_MODEL = 2048
BATCH = 1
SEQ = 8192
DEPTH = 1

CHUNK = 64
Q_BLOCK = 128
D_MIX = D_MODEL
M_HEADS = 4
M_DV = D_MIX // 2 // M_HEADS
M_DQK = M_DV // 2
M_QK = 2 * M_HEADS * M_DQK
CONV_W = 4
A_HEADS = 4
A_DV = D_MIX // 2 // A_HEADS
A_DQK = A_DV // 2
D_FF = -(-8 * D_MODEL // (3 * 256)) * 256
RMS_EPS = 1e-6
NEG = -1e30
IN_SIZES = (M_QK, M_HEADS * M_DV, M_HEADS * M_DV, M_HEADS, M_HEADS,
            A_HEADS * 2 * A_DQK, A_HEADS * 2 * A_DQK, A_HEADS * A_DV)
N_IN = sum(IN_SIZES)

kernel_name = 'hymba_mlstm_diffattn_block'


def rms_norm(x, g):
    xf = x.astype(jnp.float32)
    y = xf * lax.rsqrt(jnp.mean(xf * xf, axis=-1, keepdims=True) + RMS_EPS)
    return (y * g.astype(jnp.float32)).astype(x.dtype)


def causal_dwconv(x, w, b):
    c = x.shape[-1]
    y = lax.conv_general_dilated(x, w[:, None, :].astype(x.dtype), window_strides=(1,),
                                 padding=[(CONV_W - 1, 0)],
                                 dimension_numbers=('NWC', 'WIO', 'NWC'),
                                 feature_group_count=c)
    return y + b.astype(x.dtype)


def mlstm_chunkwise(q, k, v, i_pre, f_pre):
    B, H, S, dqk = q.shape
    dv = v.shape[-1]
    L = CHUNK
    NC = S // L
    f32 = jnp.float32
    qc = q.astype(f32).reshape(B, H, NC, L, dqk)
    kc = k.astype(f32).reshape(B, H, NC, L, dqk)
    vc = v.astype(f32).reshape(B, H, NC, L, dv)
    log_f = jax.nn.log_sigmoid(f_pre).reshape(B, H, NC, L)
    log_i = i_pre.reshape(B, H, NC, L)
    b = jnp.cumsum(log_f, axis=-1)
    g = b[..., -1]
    a = g[..., None] - b + log_i
    m_loc = jnp.max(a, axis=-1)
    w_loc = jnp.exp(a - m_loc[..., None])
    C_loc = jnp.einsum('bhcs,bhcsv,bhcsk->bhcvk', w_loc, vc, kc)
    n_loc = jnp.einsum('bhcs,bhcsk->bhck', w_loc, kc)

    def step(carry, inp):
        C, n, m = carry
        g_c, m_c, C_c, n_c = inp
        m_new = jnp.maximum(g_c + m, m_c)
        s_old = jnp.exp(g_c + m - m_new)
        s_loc = jnp.exp(m_c - m_new)
        C_new = s_old[..., None, None] * C + s_loc[..., None, None] * C_c
        n_new = s_old[..., None] * n + s_loc[..., None] * n_c
        return (C_new, n_new, m_new), (C, n, m)

    init = (jnp.zeros((B, H, dv, dqk), f32), jnp.zeros((B, H, dqk), f32),
            jnp.full((B, H), NEG, f32))
    xs = (jnp.moveaxis(g, 2, 0), jnp.moveaxis(m_loc, 2, 0),
          jnp.moveaxis(C_loc, 2, 0), jnp.moveaxis(n_loc, 2, 0))
    _, (C_prev, n_prev, m_prev) = lax.scan(step, init, xs)
    C_prev = jnp.moveaxis(C_prev, 0, 2)
    n_prev = jnp.moveaxis(n_prev, 0, 2)
    m_prev = jnp.moveaxis(m_prev, 0, 2)

    causal = jnp.tril(jnp.ones((L, L), dtype=bool))
    D = b[..., :, None] - b[..., None, :] + log_i[..., None, :]
    D = jnp.where(causal, D, NEG)
    inter_log = b + m_prev[..., None]
    m_t = jnp.maximum(inter_log, jnp.max(D, axis=-1))
    W = jnp.exp(D - m_t[..., None]) * jnp.einsum('bhctk,bhcsk->bhcts', qc, kc)
    s_inter = jnp.exp(inter_log - m_t)
    num = (jnp.einsum('bhcts,bhcsv->bhctv', W, vc)
           + s_inter[..., None] * jnp.einsum('bhcvk,bhctk->bhctv', C_prev, qc))
    den = W.sum(-1) + s_inter * jnp.einsum('bhck,bhctk->bhct', n_prev, qc)
    h = num / jnp.maximum(jnp.abs(den), jnp.exp(-m_t))[..., None]
    return h.reshape(B, H, S, dv).astype(v.dtype)


def diff_attention(q, k, v, lam):
    B, H, _, S, d = q.shape
    dv = v.shape[-1]
    nqb = S // Q_BLOCK
    scale = d ** -0.5
    k_chunk = jnp.arange(S) // CHUNK
    vf = v.astype(jnp.float32)
    qb = jnp.moveaxis(q.reshape(B, H, 2, nqb, Q_BLOCK, d), 3, 0)

    def block(args):
        q_blk, blk = args
        s = jnp.einsum('bhcqd,bhckd->bhcqk', q_blk, k).astype(jnp.float32) * scale
        q_chunk = (blk * Q_BLOCK + jnp.arange(Q_BLOCK)) // CHUNK
        mask = k_chunk[None, :] <= q_chunk[:, None]
        p = jax.nn.softmax(jnp.where(mask, s, NEG), axis=-1)
        attn = p[:, :, 0] - lam * p[:, :, 1]
        return jnp.einsum('bhqk,bhkv->bhqv', attn, vf)

    out = lax.map(block, (qb, jnp.arange(nqb)))
    return jnp.moveaxis(out, 0, 2).reshape(B, H, S, dv).astype(v.dtype)


def mixer(h, layer_idx, w_in, conv_w, conv_b, i_bias, f_bias, m_norm_w, q_norm_w, k_norm_w,
          lambda_q1, lambda_k1, lambda_q2, lambda_k2, a_norm_w, w_out):
    B, S, _ = h.shape
    f32 = jnp.float32
    split_idx = np.cumsum(IN_SIZES)[:-1].tolist()
    proj = h @ w_in
    m_qk, m_v, m_o, m_i, m_f, a_q, a_k, a_v = jnp.split(proj, split_idx, axis=-1)

    def heads(t, d):
        return t.reshape(B, S, -1, d).transpose(0, 2, 1, 3)

    m_qk = jax.nn.silu(causal_dwconv(m_qk, conv_w, conv_b))
    m_q, m_k = jnp.split(m_qk, 2, axis=-1)
    m_q = heads(m_q, M_DQK)
    m_k = heads(m_k, M_DQK) * (M_DQK ** -0.5)
    m_v = heads(m_v, M_DV)
    i_pre = (m_i.astype(f32) + i_bias.astype(f32)).transpose(0, 2, 1)
    f_pre = (m_f.astype(f32) + f_bias.astype(f32)).transpose(0, 2, 1)
    hm = mlstm_chunkwise(m_q, m_k, m_v, i_pre, f_pre)
    hm = rms_norm(hm.transpose(0, 2, 1, 3), m_norm_w).reshape(B, S, -1) * jax.nn.sigmoid(m_o)

    a_q = rms_norm(a_q.reshape(B, S, A_HEADS, 2, A_DQK), q_norm_w).transpose(0, 2, 3, 1, 4)
    a_k = rms_norm(a_k.reshape(B, S, A_HEADS, 2, A_DQK), k_norm_w).transpose(0, 2, 3, 1, 4)
    a_v = heads(a_v, A_DV)
    lam_init = 0.8 - 0.6 * math.exp(-0.3 * layer_idx)
    lam = (jnp.exp(jnp.sum(lambda_q1.astype(f32) * lambda_k1.astype(f32)))
           - jnp.exp(jnp.sum(lambda_q2.astype(f32) * lambda_k2.astype(f32))) + lam_init)
    ha = diff_attention(a_q, a_k, a_v, lam)
    ha = (rms_norm(ha.transpose(0, 2, 1, 3), a_norm_w) * (1.0 - lam_init)).reshape(B, S, -1)

    return jnp.concatenate([hm, ha.astype(hm.dtype)], axis=-1) @ w_out


def swiglu(h, w_gate, w_up, w_down):
    return (jax.nn.silu(h @ w_gate) * (h @ w_up)) @ w_down


def setup_inputs(seed: int = 0) -> dict:
    key = jax.random.key(seed)
    ks = jax.random.split(key, 20)
    f32 = jnp.float32

    def nrm(k, shape, scale):
        return jax.random.normal(k, shape, f32) * scale

    return {
        'x': nrm(ks[0], (BATCH, SEQ, D_MODEL), 1.0),
        'norm1_w': 1.0 + nrm(ks[1], (DEPTH, D_MODEL), 0.02),
        'w_in': nrm(ks[2], (DEPTH, D_MODEL, N_IN), D_MODEL ** -0.5),
        'conv_w': nrm(ks[3], (DEPTH, CONV_W, M_QK), CONV_W ** -0.5),
        'conv_b': nrm(ks[4], (DEPTH, M_QK), 0.02),
        'i_bias': nrm(ks[5], (DEPTH, M_HEADS), 0.1),
        'f_bias': jnp.linspace(3.0, 6.0, M_HEADS, dtype=f32)[None, :] + nrm(ks[6], (DEPTH, M_HEADS), 0.1),
        'm_norm_w': 1.0 + nrm(ks[7], (DEPTH, M_HEADS, M_DV), 0.02),
        'q_norm_w': 1.0 + nrm(ks[8], (DEPTH, A_DQK), 0.02),
        'k_norm_w': 1.0 + nrm(ks[9], (DEPTH, A_DQK), 0.02),
        'lambda_q1': nrm(ks[10], (DEPTH, A_DQK), 0.1),
        'lambda_k1': nrm(ks[11], (DEPTH, A_DQK), 0.1),
        'lambda_q2': nrm(ks[12], (DEPTH, A_DQK), 0.1),
        'lambda_k2': nrm(ks[13], (DEPTH, A_DQK), 0.1),
        'a_norm_w': 1.0 + nrm(ks[14], (DEPTH, A_HEADS, A_DV), 0.02),
        'w_out': nrm(ks[15], (DEPTH, D_MIX, D_MODEL), D_MIX ** -0.5),
        'norm2_w': 1.0 + nrm(ks[16], (DEPTH, D_MODEL), 0.02),
        'w_gate': nrm(ks[17], (DEPTH, D_MODEL, D_FF), D_MODEL ** -0.5),
        'w_up': nrm(ks[18], (DEPTH, D_MODEL, D_FF), D_MODEL ** -0.5),
        'w_down': nrm(ks[19], (DEPTH, D_FF, D_MODEL), D_FF ** -0.5),
    }


def reference(x, norm1_w, w_in, conv_w, conv_b, i_bias, f_bias, m_norm_w, q_norm_w, k_norm_w,
              lambda_q1, lambda_k1, lambda_q2, lambda_k2, a_norm_w, w_out, norm2_w,
              w_gate, w_up, w_down):
    for l in range(DEPTH):
        h = rms_norm(x, norm1_w[l])
        x = x + mixer(h, l, w_in[l], conv_w[l], conv_b[l], i_bias[l], f_bias[l], m_norm_w[l],
                      q_norm_w[l], k_norm_w[l], lambda_q1[l], lambda_k1[l], lambda_q2[l],
                      lambda_k2[l], a_norm_w[l], w_out[l])
        h = rms_norm(x, norm2_w[l])
        x = x + swiglu(h, w_gate[l], w_up[l], w_down[l])
    return x
```

```python
import functools
import math

import jax
import jax.numpy as jnp
from jax import lax
from jax.experimental import pallas as pl
from jax.experimental.pallas import tpu as pltpu

F32 = jnp.float32
BF16 = jnp.bfloat16

D_MODEL = 2048
CHUNK = 64
SEC = 1024
M_HEADS = 4
M_DV = 256
M_DQK = 128
CONV_W = 4
A_HEADS = 4
A_DV = 256
A_DQK = 128
D_FF = 5632
RMS_EPS = 1e-6
NEG = -1e30
LAM_INIT = 0.8 - 0.6 * math.exp(-0.3 * 0)
GATE_PAD = 128
N_GATES = 2 * M_HEADS
HALO = 8

TM_IN = 512
TM_OUT = 512
TM_FF = 512
TF_FF = 512
R_ML = 512
TQ = 512

VMEM_LIMIT = 56 * 1024 * 1024


def _cparams(sem):
    return pltpu.CompilerParams(dimension_semantics=sem, vmem_limit_bytes=VMEM_LIMIT)


def _inproj_kernel(x_ref, nw_ref, w_ref, wg_ref, gb_ref, qw_ref, kw_ref,
                   out_ref, gt_ref, h_scr, acc_scr):
    j = pl.program_id(1)

    @pl.when(j == 0)
    def _():
        xf = x_ref[...]
        ms = jnp.mean(xf * xf, axis=1, keepdims=True)
        hb = (xf * lax.rsqrt(ms + RMS_EPS) * nw_ref[...]).astype(BF16)
        h_scr[...] = hb
        g = jnp.dot(hb, wg_ref[...], preferred_element_type=F32) + gb_ref[...]
        gt_ref[...] = g.T[0:N_GATES, :]

    acc_scr[...] = jnp.dot(h_scr[...], w_ref[...], preferred_element_type=F32)
    is_q = j == 3
    is_k = j == 4
    is_qk = jnp.logical_or(is_q, is_k)

    @pl.when(is_qk)
    def _():
        w = jnp.where(is_q, qw_ref[...] * (A_DQK ** -0.5), kw_ref[...])
        for g in range(SEC // A_DQK):
            sl = slice(g * A_DQK, (g + 1) * A_DQK)
            a = acc_scr[:, sl]
            ms = jnp.mean(a * a, axis=1, keepdims=True)
            out_ref[:, sl] = (a * lax.rsqrt(ms + RMS_EPS) * w).astype(BF16)

    @pl.when(jnp.logical_not(is_qk))
    def _():
        out_ref[...] = acc_scr[...].astype(BF16)


def _inproj(x2, nw, w_main, w_gate, gbias, qw, kw):
    s = x2.shape[0]
    n_sec = w_main.shape[1] // SEC
    return pl.pallas_call(
        _inproj_kernel,
        grid=(s // TM_IN, n_sec),
        in_specs=[
            pl.BlockSpec((TM_IN, D_MODEL), lambda i, j: (i, 0)),
            pl.BlockSpec((1, D_MODEL), lambda i, j: (0, 0)),
            pl.BlockSpec((D_MODEL, SEC), lambda i, j: (0, j)),
            pl.BlockSpec((D_MODEL, GATE_PAD), lambda i, j: (0, 0)),
            pl.BlockSpec((1, GATE_PAD), lambda i, j: (0, 0)),
            pl.BlockSpec((1, A_DQK), lambda i, j: (0, 0)),
            pl.BlockSpec((1, A_DQK), lambda i, j: (0, 0)),
        ],
        out_specs=[
            pl.BlockSpec((TM_IN, SEC), lambda i, j: (i, j)),
            pl.BlockSpec((N_GATES, TM_IN), lambda i, j: (0, i)),
        ],
        out_shape=[
            jax.ShapeDtypeStruct((s, n_sec * SEC), BF16),
            jax.ShapeDtypeStruct((N_GATES, s), F32),
        ],
        scratch_shapes=[
            pltpu.VMEM((TM_IN, D_MODEL), BF16),
            pltpu.VMEM((TM_IN, SEC), F32),
        ],
        compiler_params=_cparams(("arbitrary", "arbitrary")),
        name="inproj",
    )(x2, nw, w_main, w_gate, gbias, qw, kw)


def _log_sigmoid(x):
    return jnp.minimum(x, 0.0) - jnp.log1p(jnp.exp(-jnp.abs(x)))


def _row_to_col(row, eye):
    return jnp.sum(jnp.where(eye, row, 0.0), axis=1, keepdims=True)


def _mlstm_kernel(qk_ref, halo_ref, v_ref, o_ref, g_ref, cw_ref, cb_ref, nw_ref,
                  out_ref, stage, act, b_scr, li_scr, ct_scr, n_scr, m_scr):
    i = pl.program_id(0)
    n_chunks = R_ML // CHUNK

    @pl.when(i == 0)
    def _():
        ct_scr[...] = jnp.zeros_like(ct_scr)
        n_scr[...] = jnp.zeros_like(n_scr)
        m_scr[...] = jnp.full_like(m_scr, NEG)

    halo = halo_ref[...].astype(F32)
    stage[0:HALO, :] = jnp.where(i == 0, 0.0, halo)
    stage[HALO:HALO + R_ML, :] = qk_ref[...].astype(F32)
    k_scale = M_DQK ** -0.5
    for cs in range(SEC // 128):
        sl = slice(cs * 128, (cs + 1) * 128)
        y = cb_ref[:, sl]
        for t in range(CONV_W):
            y = y + cw_ref[t:t + 1, sl] * stage[HALO - (CONV_W - 1) + t:HALO - (CONV_W - 1) + t + R_ML, sl]
        a = y * jax.nn.sigmoid(y)
        if cs >= M_HEADS:
            a = a * k_scale
        act[:, sl] = a.astype(BF16)

    li_scr[...] = g_ref[0:M_HEADS].reshape(M_HEADS * n_chunks, CHUNK)
    lf = _log_sigmoid(g_ref[M_HEADS:N_GATES].reshape(M_HEADS * n_chunks, CHUNK))
    r_i = lax.broadcasted_iota(jnp.int32, (CHUNK, CHUNK), 0)
    c_i = lax.broadcasted_iota(jnp.int32, (CHUNK, CHUNK), 1)
    upper = jnp.where(r_i <= c_i, 1.0, 0.0).astype(BF16)
    p0 = lf.astype(BF16)
    r1 = lf - p0.astype(F32)
    p1 = r1.astype(BF16)
    p2 = (r1 - p1.astype(F32)).astype(BF16)
    b_scr[...] = (jnp.dot(p0, upper, preferred_element_type=F32)
                  + jnp.dot(p1, upper, preferred_element_type=F32)
                  + jnp.dot(p2, upper, preferred_element_type=F32))

    eye = r_i == c_i
    tril = r_i >= c_i

    def chunk_body(c, carry):
        r0 = pl.multiple_of(c * CHUNK, CHUNK)
        for h in range(M_HEADS):
            q = act[pl.ds(r0, CHUNK), h * M_DQK:(h + 1) * M_DQK]
            k = act[pl.ds(r0, CHUNK), (M_HEADS + h) * M_DQK:(M_HEADS + h + 1) * M_DQK]
            v = v_ref[pl.ds(r0, CHUNK), h * M_DV:(h + 1) * M_DV]
            b_row = b_scr[pl.ds(h * n_chunks + c, 1), :]
            li_row = li_scr[pl.ds(h * n_chunks + c, 1), :]
            m_prev = m_scr[h, 0:1, 0:1]
            ct_prev = ct_scr[h]
            n_prev = n_scr[h, 0:1, :]

            b_col = _row_to_col(b_row, eye)
            d = jnp.where(tril, b_col - b_row + li_row, NEG)
            m_intra = jnp.max(d, axis=1, keepdims=True)
            inter_log = b_col + m_prev
            m_t = jnp.maximum(inter_log, m_intra)
            s = lax.dot_general(q, k, (((1,), (1,)), ((), ())), preferred_element_type=F32)
            w = jnp.exp(d - m_t) * s
            s_inter = jnp.exp(inter_log - m_t)
            qf = q.astype(F32)
            num = (jnp.dot(w.astype(BF16), v, preferred_element_type=F32)
                   + s_inter * jnp.dot(q, ct_prev.astype(BF16), preferred_element_type=F32))
            den = (jnp.sum(w, axis=1, keepdims=True)
                   + s_inter * jnp.sum(qf * n_prev, axis=1, keepdims=True))
            hout = num / jnp.maximum(jnp.abs(den), jnp.exp(-m_t))

            ms = jnp.mean(hout * hout, axis=1, keepdims=True)
            hn = hout * lax.rsqrt(ms + RMS_EPS) * nw_ref[:, h * M_DV:(h + 1) * M_DV]
            og = o_ref[pl.ds(r0, CHUNK), h * M_DV:(h + 1) * M_DV].astype(F32)
            out_ref[pl.ds(r0, CHUNK), h * M_DV:(h + 1) * M_DV] = (hn * jax.nn.sigmoid(og)).astype(BF16)

            g_tot = b_row[:, CHUNK - 1:CHUNK]
            a_row = g_tot - b_row + li_row
            m_loc = jnp.max(a_row, axis=1, keepdims=True)
            w_col = _row_to_col(jnp.exp(a_row - m_loc), eye)
            kw = k.astype(F32) * w_col
            ct_loc = lax.dot_general(kw.astype(BF16), v, (((0,), (0,)), ((), ())),
                                     preferred_element_type=F32)
            n_loc = jnp.sum(kw, axis=0, keepdims=True)
            m_new = jnp.maximum(g_tot + m_prev, m_loc)
            s_old = jnp.exp(g_tot + m_prev - m_new)
            s_loc = jnp.exp(m_loc - m_new)
            ct_scr[h] = s_old * ct_prev + s_loc * ct_loc
            n_scr[h] = jnp.broadcast_to(s_old * n_prev + s_loc * n_loc, (8, M_DQK))
            m_scr[h] = jnp.broadcast_to(m_new, (8, 128))
        return carry

    lax.fori_loop(0, n_chunks, chunk_body, 0)


def _mlstm(proj, gates3, conv_w, conv_b, m_norm_w):
    s = proj.shape[0]
    n_chunks = R_ML // CHUNK
    hb = R_ML // HALO
    return pl.pallas_call(
        _mlstm_kernel,
        grid=(s // R_ML,),
        in_specs=[
            pl.BlockSpec((R_ML, SEC), lambda i: (i, 0)),
            pl.BlockSpec((HALO, SEC), lambda i: (jnp.maximum(i * hb - 1, 0), 0)),
            pl.BlockSpec((R_ML, SEC), lambda i: (i, 1)),
            pl.BlockSpec((R_ML, SEC), lambda i: (i, 2)),
            pl.BlockSpec((N_GATES, n_chunks, CHUNK), lambda i: (0, i, 0)),
            pl.BlockSpec((CONV_W, SEC), lambda i: (0, 0)),
            pl.BlockSpec((1, SEC), lambda i: (0, 0)),
            pl.BlockSpec((1, SEC), lambda i: (0, 0)),
        ],
        out_specs=pl.BlockSpec((R_ML, SEC), lambda i: (i, 0)),
        out_shape=jax.ShapeDtypeStruct((s, SEC), BF16),
        scratch_shapes=[
            pltpu.VMEM((R_ML + HALO, SEC), F32),
            pltpu.VMEM((R_ML, SEC), BF16),
            pltpu.VMEM((M_HEADS * n_chunks, CHUNK), F32),
            pltpu.VMEM((M_HEADS * n_chunks, CHUNK), F32),
            pltpu.VMEM((M_HEADS, M_DQK, M_DV), F32),
            pltpu.VMEM((M_HEADS, 8, M_DQK), F32),
            pltpu.VMEM((M_HEADS, 8, 128), F32),
        ],
        compiler_params=_cparams(("arbitrary",)),
        name="mlstm",
    )(proj, proj, proj, proj, gates3, conv_w, conv_b, m_norm_w)


def _attn_kernel(q_ref, k_ref, v_ref, lq1_ref, lk1_ref, lq2_ref, lk2_ref, nw_ref,
                 out_ref, m_scr, l_scr, acc_scr):
    i = pl.program_id(1)

    m_scr[...] = jnp.full_like(m_scr, NEG)
    l_scr[...] = jnp.zeros_like(l_scr)
    acc_scr[...] = jnp.zeros_like(acc_scr)

    def tile(j, masked):
        r0 = pl.multiple_of(j * TQ, TQ)
        v = v_ref[pl.ds(r0, TQ), :]
        for c in range(2):
            q = q_ref[:, c * A_DQK:(c + 1) * A_DQK]
            k = k_ref[pl.ds(r0, TQ), c * A_DQK:(c + 1) * A_DQK]
            s = lax.dot_general(q, k, (((1,), (1,)), ((), ())), preferred_element_type=F32)
            if masked:
                qc = lax.broadcasted_iota(jnp.int32, (TQ, TQ), 0) // CHUNK
                kc = lax.broadcasted_iota(jnp.int32, (TQ, TQ), 1) // CHUNK
                s = jnp.where(kc <= qc, s, NEG)
            m_prev = m_scr[c]
            m_new = jnp.maximum(m_prev, jnp.max(s, axis=1, keepdims=True))
            alpha = jnp.exp(m_prev - m_new)
            p = jnp.exp(s - m_new[:, 0:1])
            l_scr[c] = alpha * l_scr[c] + jnp.sum(p, axis=1, keepdims=True)
            acc_scr[c] = alpha[:, 0:1] * acc_scr[c] + jnp.dot(p.astype(BF16), v,
                                                              preferred_element_type=F32)
            m_scr[c] = m_new

    def body(j, carry):
        tile(j, False)
        return carry

    lax.fori_loop(0, i, body, 0)
    tile(i, True)

    lam = (jnp.exp(jnp.sum(lq1_ref[...] * lk1_ref[...], axis=1, keepdims=True))
           - jnp.exp(jnp.sum(lq2_ref[...] * lk2_ref[...], axis=1, keepdims=True)) + LAM_INIT)
    o = acc_scr[0] / l_scr[0][:, 0:1] - lam * (acc_scr[1] / l_scr[1][:, 0:1])
    ms = jnp.mean(o * o, axis=1, keepdims=True)
    out_ref[...] = (o * lax.rsqrt(ms + RMS_EPS) * nw_ref[0] * (1.0 - LAM_INIT)).astype(BF16)


def _attn(proj, lq1, lk1, lq2, lk2, a_norm_w3):
    s = proj.shape[0]
    qb = 3 * SEC // A_DV
    kb = 4 * SEC // A_DV
    vb = 5 * SEC // A_DV
    vec = pl.BlockSpec((1, A_DQK), lambda h, i: (0, 0))
    return pl.pallas_call(
        _attn_kernel,
        grid=(A_HEADS, s // TQ),
        in_specs=[
            pl.BlockSpec((TQ, 2 * A_DQK), lambda h, i: (i, qb + h)),
            pl.BlockSpec((s, 2 * A_DQK), lambda h, i: (0, kb + h)),
            pl.BlockSpec((s, A_DV), lambda h, i: (0, vb + h)),
            vec, vec, vec, vec,
            pl.BlockSpec((1, 1, A_DV), lambda h, i: (h, 0, 0)),
        ],
        out_specs=pl.BlockSpec((TQ, A_DV), lambda h, i: (i, h)),
        out_shape=jax.ShapeDtypeStruct((s, A_HEADS * A_DV), BF16),
        scratch_shapes=[
            pltpu.VMEM((2, TQ, 128), F32),
            pltpu.VMEM((2, TQ, 128), F32),
            pltpu.VMEM((2, TQ, A_DV), F32),
        ],
        compiler_params=_cparams(("arbitrary", "arbitrary")),
        name="diffattn",
    )(proj, proj, proj, lq1, lk1, lq2, lk2, a_norm_w3)


def _outproj_kernel(hm_ref, ha_ref, wt_ref, wb_ref, x_ref, out_ref):
    out_ref[...] = (x_ref[...]
                    + jnp.dot(hm_ref[...], wt_ref[...], preferred_element_type=F32)
                    + jnp.dot(ha_ref[...], wb_ref[...], preferred_element_type=F32))


def _outproj(hm, ha, w_out, x2):
    s = x2.shape[0]
    return pl.pallas_call(
        _outproj_kernel,
        grid=(s // TM_OUT,),
        in_specs=[
            pl.BlockSpec((TM_OUT, SEC), lambda i: (i, 0)),
            pl.BlockSpec((TM_OUT, SEC), lambda i: (i, 0)),
            pl.BlockSpec((SEC, D_MODEL), lambda i: (0, 0)),
            pl.BlockSpec((SEC, D_MODEL), lambda i: (1, 0)),
            pl.BlockSpec((TM_OUT, D_MODEL), lambda i: (i, 0)),
        ],
        out_specs=pl.BlockSpec((TM_OUT, D_MODEL), lambda i: (i, 0)),
        out_shape=jax.ShapeDtypeStruct((s, D_MODEL), F32),
        compiler_params=_cparams(("arbitrary",)),
        name="outproj",
    )(hm, ha, w_out, w_out, x2)


def _ffn_kernel(x_ref, nw_ref, wg_ref, wu_ref, wd_ref, out_ref, h_scr):
    j = pl.program_id(1)

    @pl.when(j == 0)
    def _():
        xf = x_ref[...]
        ms = jnp.mean(xf * xf, axis=1, keepdims=True)
        h_scr[...] = (xf * lax.rsqrt(ms + RMS_EPS) * nw_ref[...]).astype(BF16)
        out_ref[...] = xf

    h = h_scr[...]
    g = jnp.dot(h, wg_ref[...], preferred_element_type=F32)
    u = jnp.dot(h, wu_ref[...], preferred_element_type=F32)
    a = (g * jax.nn.sigmoid(g) * u).astype(BF16)
    out_ref[...] += jnp.dot(a, wd_ref[...], preferred_element_type=F32)


def _ffn(x1, nw, w_gate, w_up, w_down):
    s = x1.shape[0]
    return pl.pallas_call(
        _ffn_kernel,
        grid=(s // TM_FF, D_FF // TF_FF),
        in_specs=[
            pl.BlockSpec((TM_FF, D_MODEL), lambda i, j: (i, 0)),
            pl.BlockSpec((1, D_MODEL), lambda i, j: (0, 0)),
            pl.BlockSpec((D_MODEL, TF_FF), lambda i, j: (0, j)),
            pl.BlockSpec((D_MODEL, TF_FF), lambda i, j: (0, j)),
            pl.BlockSpec((TF_FF, D_MODEL), lambda i, j: (j, 0)),
        ],
        out_specs=pl.BlockSpec((TM_FF, D_MODEL), lambda i, j: (i, 0)),
        out_shape=jax.ShapeDtypeStruct((s, D_MODEL), F32),
        scratch_shapes=[pltpu.VMEM((TM_FF, D_MODEL), BF16)],
        compiler_params=_cparams(("arbitrary", "arbitrary")),
        name="swiglu",
    )(x1, nw, w_gate, w_up, w_down)


def _layer(x2, norm1_w, w_in, conv_w, conv_b, i_bias, f_bias, m_norm_w, q_norm_w, k_norm_w,
           lambda_q1, lambda_k1, lambda_q2, lambda_k2, a_norm_w, w_out, norm2_w,
           w_gate, w_up, w_down):
    s = x2.shape[0]
    g0 = 3 * SEC
    w_main = jnp.concatenate([w_in[:, :g0], w_in[:, g0 + N_GATES:]], axis=1).astype(BF16)
    w_g = jnp.pad(w_in[:, g0:g0 + N_GATES], ((0, 0), (0, GATE_PAD - N_GATES))).astype(BF16)
    gbias = jnp.pad(jnp.concatenate([i_bias, f_bias]), (0, GATE_PAD - N_GATES))[None, :]

    proj, gates_t = _inproj(x2, norm1_w[None, :], w_main, w_g, gbias,
                            q_norm_w[None, :], k_norm_w[None, :])
    gates3 = gates_t.reshape(N_GATES, s // CHUNK, CHUNK)
    hm = _mlstm(proj, gates3, conv_w, conv_b[None, :], m_norm_w.reshape(1, SEC))
    ha = _attn(proj, lambda_q1[None, :], lambda_k1[None, :], lambda_q2[None, :],
               lambda_k2[None, :], a_norm_w[:, None, :])
    x1 = _outproj(hm, ha, w_out.astype(BF16), x2)
    return _ffn(x1, norm2_w[None, :], w_gate.astype(BF16), w_up.astype(BF16),
                w_down.astype(BF16))


def kernel(x, norm1_w, w_in, conv_w, conv_b, i_bias, f_bias, m_norm_w, q_norm_w, k_norm_w,
           lambda_q1, lambda_k1, lambda_q2, lambda_k2, a_norm_w, w_out, norm2_w,
           w_gate, w_up, w_down):
    b, s, d = x.shape
    assert d == D_MODEL and b == 1 and norm1_w.shape[0] == 1
    assert s % R_ML == 0 and s % TQ == 0 and s % TM_IN == 0 and s % TM_FF == 0
    y = _layer(x.reshape(s, d), norm1_w[0], w_in[0], conv_w[0], conv_b[0], i_bias[0], f_bias[0],
               m_norm_w[0], q_norm_w[0], k_norm_w[0], lambda_q1[0], lambda_k1[0],
               lambda_q2[0], lambda_k2[0], a_norm_w[0], w_out[0], norm2_w[0],
               w_gate[0], w_up[0], w_down[0])
    return y.reshape(b, s, d)
```

```python
import functools
import math

import jax
import jax.numpy as jnp
from jax import lax
from jax.experimental import pallas as pl
from jax.experimental.pallas import tpu as pltpu

F32 = jnp.float32
BF16 = jnp.bfloat16

D_MODEL = 2048
CHUNK = 64
SEC = 1024
M_HEADS = 4
M_DV = 256
M_DQK = 128
CONV_W = 4
A_HEADS = 4
A_DV = 256
A_DQK = 128
D_FF = 5632
RMS_EPS = 1e-6
NEG = -1e30
LAM_INIT = 0.8 - 0.6 * math.exp(-0.3 * 0)
LOG2E = 1.4426950408889634
MAX_FIXED_SHIFT = 60.0
GATE_PAD = 128
N_GATES = 2 * M_HEADS
HALO = 8

TM_IN = 512
TM_OUT = 512
TM_FF = 512
TF_FF = 512
R_ML = 512
TQ = 512

VMEM_LIMIT = 56 * 1024 * 1024


def _cparams(sem):
    return pltpu.CompilerParams(dimension_semantics=sem, vmem_limit_bytes=VMEM_LIMIT)


def _inproj_kernel(x_ref, nw_ref, w_ref, wg_ref, gb_ref, qw_ref, kw_ref,
                   out_ref, gt_ref, h_scr, acc_scr):
    j = pl.program_id(1)

    @pl.when(j == 0)
    def _():
        xf = x_ref[...]
        ms = jnp.mean(xf * xf, axis=1, keepdims=True)
        hb = (xf * lax.rsqrt(ms + RMS_EPS) * nw_ref[...]).astype(BF16)
        h_scr[...] = hb
        g = jnp.dot(hb, wg_ref[...], preferred_element_type=F32) + gb_ref[...]
        gt_ref[...] = g.T[0:N_GATES, :]

    acc_scr[...] = jnp.dot(h_scr[...], w_ref[...], preferred_element_type=F32)
    is_q = j == 3
    is_k = j == 4
    is_qk = jnp.logical_or(is_q, is_k)

    @pl.when(is_qk)
    def _():
        w = jnp.where(is_q, qw_ref[...] * (A_DQK ** -0.5 * LOG2E), kw_ref[...])
        for g in range(SEC // A_DQK):
            sl = slice(g * A_DQK, (g + 1) * A_DQK)
            a = acc_scr[:, sl]
            ms = jnp.mean(a * a, axis=1, keepdims=True)
            out_ref[:, sl] = (a * lax.rsqrt(ms + RMS_EPS) * w).astype(BF16)

    @pl.when(jnp.logical_not(is_qk))
    def _():
        out_ref[...] = acc_scr[...].astype(BF16)


def _inproj(x2, nw, w_main, w_gate, gbias, qw, kw):
    s = x2.shape[0]
    n_sec = w_main.shape[1] // SEC
    return pl.pallas_call(
        _inproj_kernel,
        grid=(s // TM_IN, n_sec),
        in_specs=[
            pl.BlockSpec((TM_IN, D_MODEL), lambda i, j: (i, 0)),
            pl.BlockSpec((1, D_MODEL), lambda i, j: (0, 0)),
            pl.BlockSpec((D_MODEL, SEC), lambda i, j: (0, j)),
            pl.BlockSpec((D_MODEL, GATE_PAD), lambda i, j: (0, 0)),
            pl.BlockSpec((1, GATE_PAD), lambda i, j: (0, 0)),
            pl.BlockSpec((1, A_DQK), lambda i, j: (0, 0)),
            pl.BlockSpec((1, A_DQK), lambda i, j: (0, 0)),
        ],
        out_specs=[
            pl.BlockSpec((TM_IN, SEC), lambda i, j: (i, j)),
            pl.BlockSpec((N_GATES, TM_IN), lambda i, j: (0, i)),
        ],
        out_shape=[
            jax.ShapeDtypeStruct((s, n_sec * SEC), BF16),
            jax.ShapeDtypeStruct((N_GATES, s), F32),
        ],
        scratch_shapes=[
            pltpu.VMEM((TM_IN, D_MODEL), BF16),
            pltpu.VMEM((TM_IN, SEC), F32),
        ],
        compiler_params=_cparams(("arbitrary", "arbitrary")),
        name="inproj",
    )(x2, nw, w_main, w_gate, gbias, qw, kw)


def _log_sigmoid(x):
    return jnp.minimum(x, 0.0) - jnp.log1p(jnp.exp(-jnp.abs(x)))


def _row_to_col(row, eye):
    return jnp.sum(jnp.where(eye, row, 0.0), axis=1, keepdims=True)


def _mlstm_kernel(qk_ref, halo_ref, v_ref, o_ref, g_ref, cw_ref, cb_ref, nw_ref,
                  out_ref, stage, act, b_scr, li_scr, ct_scr, n_scr, m_scr):
    i = pl.program_id(0)
    n_chunks = R_ML // CHUNK

    @pl.when(i == 0)
    def _():
        ct_scr[...] = jnp.zeros_like(ct_scr)
        n_scr[...] = jnp.zeros_like(n_scr)
        m_scr[...] = jnp.full_like(m_scr, NEG)

    halo = halo_ref[...].astype(F32)
    stage[0:HALO, :] = jnp.where(i == 0, 0.0, halo)
    stage[HALO:HALO + R_ML, :] = qk_ref[...].astype(F32)
    k_scale = M_DQK ** -0.5
    for cs in range(SEC // 128):
        sl = slice(cs * 128, (cs + 1) * 128)
        y = cb_ref[:, sl]
        for t in range(CONV_W):
            y = y + cw_ref[t:t + 1, sl] * stage[HALO - (CONV_W - 1) + t:HALO - (CONV_W - 1) + t + R_ML, sl]
        a = y * jax.nn.sigmoid(y)
        if cs >= M_HEADS:
            a = a * k_scale
        act[:, sl] = a.astype(BF16)

    li_scr[...] = g_ref[0:M_HEADS].reshape(M_HEADS * n_chunks, CHUNK)
    lf = _log_sigmoid(g_ref[M_HEADS:N_GATES].reshape(M_HEADS * n_chunks, CHUNK))
    r_i = lax.broadcasted_iota(jnp.int32, (CHUNK, CHUNK), 0)
    c_i = lax.broadcasted_iota(jnp.int32, (CHUNK, CHUNK), 1)
    upper = jnp.where(r_i <= c_i, 1.0, 0.0).astype(BF16)
    p0 = lf.astype(BF16)
    r1 = lf - p0.astype(F32)
    p1 = r1.astype(BF16)
    p2 = (r1 - p1.astype(F32)).astype(BF16)
    b_scr[...] = (jnp.dot(p0, upper, preferred_element_type=F32)
                  + jnp.dot(p1, upper, preferred_element_type=F32)
                  + jnp.dot(p2, upper, preferred_element_type=F32))

    eye = r_i == c_i
    tril = r_i >= c_i

    def chunk_body(c, carry):
        r0 = pl.multiple_of(c * CHUNK, CHUNK)
        for h in range(M_HEADS):
            q = act[pl.ds(r0, CHUNK), h * M_DQK:(h + 1) * M_DQK]
            k = act[pl.ds(r0, CHUNK), (M_HEADS + h) * M_DQK:(M_HEADS + h + 1) * M_DQK]
            v = v_ref[pl.ds(r0, CHUNK), h * M_DV:(h + 1) * M_DV]
            b_row = b_scr[pl.ds(h * n_chunks + c, 1), :]
            li_row = li_scr[pl.ds(h * n_chunks + c, 1), :]
            m_prev = m_scr[h, 0:1, 0:1]
            ct_prev = ct_scr[h]
            n_prev = n_scr[h, 0:1, :]

            b_col = _row_to_col(b_row, eye)
            d = jnp.where(tril, b_col - b_row + li_row, NEG)
            m_intra = jnp.max(d, axis=1, keepdims=True)
            inter_log = b_col + m_prev
            m_t = jnp.maximum(inter_log, m_intra)
            s = lax.dot_general(q, k, (((1,), (1,)), ((), ())), preferred_element_type=F32)
            w = jnp.exp(d - m_t) * s
            s_inter = jnp.exp(inter_log - m_t)
            qf = q.astype(F32)
            num = (jnp.dot(w.astype(BF16), v, preferred_element_type=F32)
                   + s_inter * jnp.dot(q, ct_prev.astype(BF16), preferred_element_type=F32))
            den = (jnp.sum(w, axis=1, keepdims=True)
                   + s_inter * jnp.sum(qf * n_prev, axis=1, keepdims=True))
            hout = num / jnp.maximum(jnp.abs(den), jnp.exp(-m_t))

            ms = jnp.mean(hout * hout, axis=1, keepdims=True)
            hn = hout * lax.rsqrt(ms + RMS_EPS) * nw_ref[:, h * M_DV:(h + 1) * M_DV]
            og = o_ref[pl.ds(r0, CHUNK), h * M_DV:(h + 1) * M_DV].astype(F32)
            out_ref[pl.ds(r0, CHUNK), h * M_DV:(h + 1) * M_DV] = (hn * jax.nn.sigmoid(og)).astype(BF16)

            g_tot = b_row[:, CHUNK - 1:CHUNK]
            a_row = g_tot - b_row + li_row
            m_loc = jnp.max(a_row, axis=1, keepdims=True)
            w_col = _row_to_col(jnp.exp(a_row - m_loc), eye)
            kw = k.astype(F32) * w_col
            ct_loc = lax.dot_general(kw.astype(BF16), v, (((0,), (0,)), ((), ())),
                                     preferred_element_type=F32)
            n_loc = jnp.sum(kw, axis=0, keepdims=True)
            m_new = jnp.maximum(g_tot + m_prev, m_loc)
            s_old = jnp.exp(g_tot + m_prev - m_new)
            s_loc = jnp.exp(m_loc - m_new)
            ct_scr[h] = s_old * ct_prev + s_loc * ct_loc
            n_scr[h] = jnp.broadcast_to(s_old * n_prev + s_loc * n_loc, (8, M_DQK))
            m_scr[h] = jnp.broadcast_to(m_new, (8, 128))
        return carry

    lax.fori_loop(0, n_chunks, chunk_body, 0)


def _mlstm(proj, gates3, conv_w, conv_b, m_norm_w):
    s = proj.shape[0]
    n_chunks = R_ML // CHUNK
    hb = R_ML // HALO
    return pl.pallas_call(
        _mlstm_kernel,
        grid=(s // R_ML,),
        in_specs=[
            pl.BlockSpec((R_ML, SEC), lambda i: (i, 0)),
            pl.BlockSpec((HALO, SEC), lambda i: (jnp.maximum(i * hb - 1, 0), 0)),
            pl.BlockSpec((R_ML, SEC), lambda i: (i, 1)),
            pl.BlockSpec((R_ML, SEC), lambda i: (i, 2)),
            pl.BlockSpec((N_GATES, n_chunks, CHUNK), lambda i: (0, i, 0)),
            pl.BlockSpec((CONV_W, SEC), lambda i: (0, 0)),
            pl.BlockSpec((1, SEC), lambda i: (0, 0)),
            pl.BlockSpec((1, SEC), lambda i: (0, 0)),
        ],
        out_specs=pl.BlockSpec((R_ML, SEC), lambda i: (i, 0)),
        out_shape=jax.ShapeDtypeStruct((s, SEC), BF16),
        scratch_shapes=[
            pltpu.VMEM((R_ML + HALO, SEC), F32),
            pltpu.VMEM((R_ML, SEC), BF16),
            pltpu.VMEM((M_HEADS * n_chunks, CHUNK), F32),
            pltpu.VMEM((M_HEADS * n_chunks, CHUNK), F32),
            pltpu.VMEM((M_HEADS, M_DQK, M_DV), F32),
            pltpu.VMEM((M_HEADS, 8, M_DQK), F32),
            pltpu.VMEM((M_HEADS, 8, 128), F32),
        ],
        compiler_params=_cparams(("arbitrary",)),
        name="mlstm",
    )(proj, proj, proj, proj, gates3, conv_w, conv_b, m_norm_w)


def _lane_partial_sum(p):
    acc = p[:, 0:128]
    for t in range(1, p.shape[1] // 128):
        acc = acc + p[:, t * 128:(t + 1) * 128]
    return acc


def _attn_kernel(par_ref, q_ref, k_ref, v_ref, lq1_ref, lk1_ref, lq2_ref, lk2_ref, nw_ref,
                 out_ref, m_scr, l_scr, acc_scr):
    i = pl.program_id(1)
    shift = par_ref[0]
    fixed = par_ref[1] > 0.5

    l_scr[...] = jnp.zeros_like(l_scr)
    acc_scr[...] = jnp.zeros_like(acc_scr)

    def chunk_mask():
        qc = lax.broadcasted_iota(jnp.int32, (TQ, TQ), 0) // CHUNK
        kc = lax.broadcasted_iota(jnp.int32, (TQ, TQ), 1) // CHUNK
        return kc <= qc

    def scores(j, c):
        r0 = pl.multiple_of(j * TQ, TQ)
        q = q_ref[:, c * A_DQK:(c + 1) * A_DQK]
        k = k_ref[pl.ds(r0, TQ), c * A_DQK:(c + 1) * A_DQK]
        return lax.dot_general(q, k, (((1,), (1,)), ((), ())), preferred_element_type=F32)

    def fixed_tile(j, masked):
        v = v_ref[pl.ds(pl.multiple_of(j * TQ, TQ), TQ), :]
        for c in range(2):
            p = jnp.exp2(scores(j, c) - shift)
            if masked:
                p = jnp.where(chunk_mask(), p, 0.0)
            l_scr[c] += _lane_partial_sum(p)
            acc_scr[c] += jnp.dot(p.astype(BF16), v, preferred_element_type=F32)

    def online_tile(j, masked):
        v = v_ref[pl.ds(pl.multiple_of(j * TQ, TQ), TQ), :]
        for c in range(2):
            s = scores(j, c)
            if masked:
                s = jnp.where(chunk_mask(), s, NEG)
            m_prev = m_scr[c]
            m_new = jnp.maximum(m_prev, jnp.max(s, axis=1, keepdims=True))
            alpha = jnp.exp2(m_prev - m_new)
            p = jnp.exp2(s - m_new[:, 0:1])
            l_scr[c] = alpha * l_scr[c] + _lane_partial_sum(p)
            acc_scr[c] = alpha[:, 0:1] * acc_scr[c] + jnp.dot(p.astype(BF16), v,
                                                              preferred_element_type=F32)
            m_scr[c] = m_new

    @pl.when(fixed)
    def _():
        def body(j, carry):
            fixed_tile(j, False)
            return carry
        lax.fori_loop(0, i, body, 0)
        fixed_tile(i, True)

    @pl.when(jnp.logical_not(fixed))
    def _():
        m_scr[...] = jnp.full_like(m_scr, NEG)

        def body(j, carry):
            online_tile(j, False)
            return carry
        lax.fori_loop(0, i, body, 0)
        online_tile(i, True)

    lam = (jnp.exp(jnp.sum(lq1_ref[...] * lk1_ref[...], axis=1, keepdims=True))
           - jnp.exp(jnp.sum(lq2_ref[...] * lk2_ref[...], axis=1, keepdims=True)) + LAM_INIT)
    l0 = jnp.sum(l_scr[0], axis=1, keepdims=True)
    l1 = jnp.sum(l_scr[1], axis=1, keepdims=True)
    o = acc_scr[0] / l0 - lam * (acc_scr[1] / l1)
    ms = jnp.mean(o * o, axis=1, keepdims=True)
    out_ref[...] = (o * lax.rsqrt(ms + RMS_EPS) * nw_ref[0] * (1.0 - LAM_INIT)).astype(BF16)


def _attn(proj, q_norm_w, k_norm_w, lq1, lk1, lq2, lk2, a_norm_w3):
    s = proj.shape[0]
    qb = 3 * SEC // A_DV
    kb = 4 * SEC // A_DV
    vb = 5 * SEC // A_DV
    bound = (A_DQK ** 0.5 * LOG2E) * jnp.max(jnp.abs(q_norm_w * k_norm_w))
    shift = jnp.ceil(bound * 1.02) + 1.0
    params = jnp.stack([shift, (shift <= MAX_FIXED_SHIFT).astype(F32)]).astype(F32)
    vec = pl.BlockSpec((1, A_DQK), lambda h, i: (0, 0))
    return pl.pallas_call(
        _attn_kernel,
        grid=(A_HEADS, s // TQ),
        in_specs=[
            pl.BlockSpec(memory_space=pltpu.SMEM),
            pl.BlockSpec((TQ, 2 * A_DQK), lambda h, i: (i, qb + h)),
            pl.BlockSpec((s, 2 * A_DQK), lambda h, i: (0, kb + h)),
            pl.BlockSpec((s, A_DV), lambda h, i: (0, vb + h)),
            vec, vec, vec, vec,
            pl.BlockSpec((1, 1, A_DV), lambda h, i: (h, 0, 0)),
        ],
        out_specs=pl.BlockSpec((TQ, A_DV), lambda h, i: (i, h)),
        out_shape=jax.ShapeDtypeStruct((s, A_HEADS * A_DV), BF16),
        scratch_shapes=[
            pltpu.VMEM((2, TQ, 128), F32),
            pltpu.VMEM((2, TQ, 128), F32),
            pltpu.VMEM((2, TQ, A_DV), F32),
        ],
        compiler_params=_cparams(("arbitrary", "arbitrary")),
        name="diffattn",
    )(params, proj, proj, proj, lq1, lk1, lq2, lk2, a_norm_w3)


def _outproj_kernel(hm_ref, ha_ref, wt_ref, wb_ref, x_ref, out_ref):
    out_ref[...] = (x_ref[...]
                    + jnp.dot(hm_ref[...], wt_ref[...], preferred_element_type=F32)
                    + jnp.dot(ha_ref[...], wb_ref[...], preferred_element_type=F32))


def _outproj(hm, ha, w_out, x2):
    s = x2.shape[0]
    return pl.pallas_call(
        _outproj_kernel,
        grid=(s // TM_OUT,),
        in_specs=[
            pl.BlockSpec((TM_OUT, SEC), lambda i: (i, 0)),
            pl.BlockSpec((TM_OUT, SEC), lambda i: (i, 0)),
            pl.BlockSpec((SEC, D_MODEL), lambda i: (0, 0)),
            pl.BlockSpec((SEC, D_MODEL), lambda i: (1, 0)),
            pl.BlockSpec((TM_OUT, D_MODEL), lambda i: (i, 0)),
        ],
        out_specs=pl.BlockSpec((TM_OUT, D_MODEL), lambda i: (i, 0)),
        out_shape=jax.ShapeDtypeStruct((s, D_MODEL), F32),
        compiler_params=_cparams(("arbitrary",)),
        name="outproj",
    )(hm, ha, w_out, w_out, x2)


def _ffn_kernel(x_ref, nw_ref, wg_ref, wu_ref, wd_ref, out_ref, h_scr):
    j = pl.program_id(1)

    @pl.when(j == 0)
    def _():
        xf = x_ref[...]
        ms = jnp.mean(xf * xf, axis=1, keepdims=True)
        h_scr[...] = (xf * lax.rsqrt(ms + RMS_EPS) * nw_ref[...]).astype(BF16)
        out_ref[...] = xf

    h = h_scr[...]
    g = jnp.dot(h, wg_ref[...], preferred_element_type=F32)
    u = jnp.dot(h, wu_ref[...], preferred_element_type=F32)
    a = (g * jax.nn.sigmoid(g) * u).astype(BF16)
    out_ref[...] += jnp.dot(a, wd_ref[...], preferred_element_type=F32)


def _ffn(x1, nw, w_gate, w_up, w_down):
    s = x1.shape[0]
    return pl.pallas_call(
        _ffn_kernel,
        grid=(s // TM_FF, D_FF // TF_FF),
        in_specs=[
            pl.BlockSpec((TM_FF, D_MODEL), lambda i, j: (i, 0)),
            pl.BlockSpec((1, D_MODEL), lambda i, j: (0, 0)),
            pl.BlockSpec((D_MODEL, TF_FF), lambda i, j: (0, j)),
            pl.BlockSpec((D_MODEL, TF_FF), lambda i, j: (0, j)),
            pl.BlockSpec((TF_FF, D_MODEL), lambda i, j: (j, 0)),
        ],
        out_specs=pl.BlockSpec((TM_FF, D_MODEL), lambda i, j: (i, 0)),
        out_shape=jax.ShapeDtypeStruct((s, D_MODEL), F32),
        scratch_shapes=[pltpu.VMEM((TM_FF, D_MODEL), BF16)],
        compiler_params=_cparams(("arbitrary", "arbitrary")),
        name="swiglu",
    )(x1, nw, w_gate, w_up, w_down)


def _layer(x2, norm1_w, w_in, conv_w, conv_b, i_bias, f_bias, m_norm_w, q_norm_w, k_norm_w,
           lambda_q1, lambda_k1, lambda_q2, lambda_k2, a_norm_w, w_out, norm2_w,
           w_gate, w_up, w_down):
    s = x2.shape[0]
    g0 = 3 * SEC
    w_main = jnp.concatenate([w_in[:, :g0], w_in[:, g0 + N_GATES:]], axis=1).astype(BF16)
    w_g = jnp.pad(w_in[:, g0:g0 + N_GATES], ((0, 0), (0, GATE_PAD - N_GATES))).astype(BF16)
    gbias = jnp.pad(jnp.concatenate([i_bias, f_bias]), (0, GATE_PAD - N_GATES))[None, :]

    proj, gates_t = _inproj(x2, norm1_w[None, :], w_main, w_g, gbias,
                            q_norm_w[None, :], k_norm_w[None, :])
    gates3 = gates_t.reshape(N_GATES, s // CHUNK, CHUNK)
    hm = _mlstm(proj, gates3, conv_w, conv_b[None, :], m_norm_w.reshape(1, SEC))
    ha = _attn(proj, q_norm_w, k_norm_w, lambda_q1[None, :], lambda_k1[None, :],
               lambda_q2[None, :], lambda_k2[None, :], a_norm_w[:, None, :])
    x1 = _outproj(hm, ha, w_out.astype(BF16), x2)
    return _ffn(x1, norm2_w[None, :], w_gate.astype(BF16), w_up.astype(BF16),
                w_down.astype(BF16))


def kernel(x, norm1_w, w_in, conv_w, conv_b, i_bias, f_bias, m_norm_w, q_norm_w, k_norm_w,
           lambda_q1, lambda_k1, lambda_q2, lambda_k2, a_norm_w, w_out, norm2_w,
           w_gate, w_up, w_down):
    b, s, d = x.shape
    assert d == D_MODEL and b == 1 and norm1_w.shape[0] == 1
    assert s % R_ML == 0 and s % TQ == 0 and s % TM_IN == 0 and s % TM_FF == 0
    y = _layer(x.reshape(s, d), norm1_w[0], w_in[0], conv_w[0], conv_b[0], i_bias[0], f_bias[0],
               m_norm_w[0], q_norm_w[0], k_norm_w[0], lambda_q1[0], lambda_k1[0],
               lambda_q2[0], lambda_k2[0], a_norm_w[0], w_out[0], norm2_w[0],
               w_gate[0], w_up[0], w_down[0])
    return y.reshape(b, s, d)
```

```python
import functools
import math

import jax
import jax.numpy as jnp
from jax import lax
from jax.experimental import pallas as pl
from jax.experimental.pallas import tpu as pltpu

F32 = jnp.float32
BF16 = jnp.bfloat16

D_MODEL = 2048
CHUNK = 64
SEC = 1024
N_SEC_LO = 3
M_HEADS = 4
M_DV = 256
M_DQK = 128
CONV_W = 4
A_HEADS = 4
A_DV = 256
A_DQK = 128
D_FF = 5632
RMS_EPS = 1e-6
NEG = -1e30
LAM_INIT = 0.8 - 0.6 * math.exp(-0.3 * 0)
LOG2E = 1.4426950408889634
MAX_FIXED_SHIFT = 60.0
GATE_PAD = 128
N_GATES = 2 * M_HEADS
HALO = 8

TM_IN = 1024
TM_OUT = 512
TM_FF = 1024
TF_FF = 512
R_ML = 512
TQ = 512

VMEM_LIMIT = 56 * 1024 * 1024


def _cparams(sem):
    return pltpu.CompilerParams(dimension_semantics=sem, vmem_limit_bytes=VMEM_LIMIT)


def _inproj_kernel(x_ref, nw_ref, wlo_ref, whi_ref, wg_ref, gb_ref, qw_ref, kw_ref,
                   out_ref, gt_ref, h_scr, acc_scr):
    j = pl.program_id(1)

    @pl.when(j == 0)
    def _():
        xf = x_ref[...]
        ms = jnp.mean(xf * xf, axis=1, keepdims=True)
        hb = (xf * lax.rsqrt(ms + RMS_EPS) * nw_ref[...]).astype(BF16)
        h_scr[...] = hb
        g = jnp.dot(hb, wg_ref[...], preferred_element_type=F32) + gb_ref[...]
        gt_ref[...] = g.T[0:N_GATES, :]

    @pl.when(j < N_SEC_LO)
    def _():
        acc_scr[...] = jnp.dot(h_scr[...], wlo_ref[...], preferred_element_type=F32)

    @pl.when(j >= N_SEC_LO)
    def _():
        acc_scr[...] = jnp.dot(h_scr[...], whi_ref[...], preferred_element_type=F32)

    is_q = j == N_SEC_LO
    is_k = j == N_SEC_LO + 1
    is_qk = jnp.logical_or(is_q, is_k)

    @pl.when(is_qk)
    def _():
        w = jnp.where(is_q, qw_ref[...] * (A_DQK ** -0.5 * LOG2E), kw_ref[...])
        for g in range(SEC // A_DQK):
            sl = slice(g * A_DQK, (g + 1) * A_DQK)
            a = acc_scr[:, sl]
            ms = jnp.mean(a * a, axis=1, keepdims=True)
            out_ref[:, sl] = (a * lax.rsqrt(ms + RMS_EPS) * w).astype(BF16)

    @pl.when(jnp.logical_not(is_qk))
    def _():
        out_ref[...] = acc_scr[...].astype(BF16)


def _inproj(x2, nw, w_lo, w_hi, w_gate, gbias, qw, kw):
    s = x2.shape[0]
    n_hi = w_hi.shape[1] // SEC
    assert w_lo.shape[1] == N_SEC_LO * SEC
    n_sec = N_SEC_LO + n_hi
    return pl.pallas_call(
        _inproj_kernel,
        grid=(s // TM_IN, n_sec),
        in_specs=[
            pl.BlockSpec((TM_IN, D_MODEL), lambda i, j: (i, 0)),
            pl.BlockSpec((1, D_MODEL), lambda i, j: (0, 0)),
            pl.BlockSpec((D_MODEL, SEC), lambda i, j: (0, jnp.minimum(j, N_SEC_LO - 1))),
            pl.BlockSpec((D_MODEL, SEC), lambda i, j: (0, jnp.maximum(j - N_SEC_LO, 0))),
            pl.BlockSpec((D_MODEL, GATE_PAD), lambda i, j: (0, 0)),
            pl.BlockSpec((1, GATE_PAD), lambda i, j: (0, 0)),
            pl.BlockSpec((1, A_DQK), lambda i, j: (0, 0)),
            pl.BlockSpec((1, A_DQK), lambda i, j: (0, 0)),
        ],
        out_specs=[
            pl.BlockSpec((TM_IN, SEC), lambda i, j: (i, j)),
            pl.BlockSpec((N_GATES, TM_IN), lambda i, j: (0, i)),
        ],
        out_shape=[
            jax.ShapeDtypeStruct((s, n_sec * SEC), BF16),
            jax.ShapeDtypeStruct((N_GATES, s), F32),
        ],
        scratch_shapes=[
            pltpu.VMEM((TM_IN, D_MODEL), BF16),
            pltpu.VMEM((TM_IN, SEC), F32),
        ],
        compiler_params=_cparams(("arbitrary", "arbitrary")),
        name="inproj",
    )(x2, nw, w_lo, w_hi, w_gate, gbias, qw, kw)


def _log_sigmoid(x):
    return jnp.minimum(x, 0.0) - jnp.log1p(jnp.exp(-jnp.abs(x)))


def _row_to_col(row, eye):
    return jnp.sum(jnp.where(eye, row, 0.0), axis=1, keepdims=True)


def _mlstm_kernel(qk_ref, halo_ref, v_ref, o_ref, g_ref, cw_ref, cb_ref, nw_ref,
                  out_ref, stage, act, b_scr, li_scr, ct_scr, n_scr, m_scr):
    i = pl.program_id(0)
    n_chunks = R_ML // CHUNK

    @pl.when(i == 0)
    def _():
        ct_scr[...] = jnp.zeros_like(ct_scr)
        n_scr[...] = jnp.zeros_like(n_scr)
        m_scr[...] = jnp.full_like(m_scr, NEG)

    halo = halo_ref[...].astype(F32)
    stage[0:HALO, :] = jnp.where(i == 0, 0.0, halo)
    stage[HALO:HALO + R_ML, :] = qk_ref[...].astype(F32)
    k_scale = M_DQK ** -0.5
    for cs in range(SEC // 128):
        sl = slice(cs * 128, (cs + 1) * 128)
        y = cb_ref[:, sl]
        for t in range(CONV_W):
            y = y + cw_ref[t:t + 1, sl] * stage[HALO - (CONV_W - 1) + t:HALO - (CONV_W - 1) + t + R_ML, sl]
        a = y * jax.nn.sigmoid(y)
        if cs >= M_HEADS:
            a = a * k_scale
        act[:, sl] = a.astype(BF16)

    li_scr[...] = g_ref[0:M_HEADS].reshape(M_HEADS * n_chunks, CHUNK)
    lf = _log_sigmoid(g_ref[M_HEADS:N_GATES].reshape(M_HEADS * n_chunks, CHUNK))
    r_i = lax.broadcasted_iota(jnp.int32, (CHUNK, CHUNK), 0)
    c_i = lax.broadcasted_iota(jnp.int32, (CHUNK, CHUNK), 1)
    upper = jnp.where(r_i <= c_i, 1.0, 0.0).astype(BF16)
    p0 = lf.astype(BF16)
    r1 = lf - p0.astype(F32)
    p1 = r1.astype(BF16)
    p2 = (r1 - p1.astype(F32)).astype(BF16)
    b_scr[...] = (jnp.dot(p0, upper, preferred_element_type=F32)
                  + jnp.dot(p1, upper, preferred_element_type=F32)
                  + jnp.dot(p2, upper, preferred_element_type=F32))

    eye = r_i == c_i
    tril = r_i >= c_i

    def chunk_body(c, carry):
        r0 = pl.multiple_of(c * CHUNK, CHUNK)
        for h in range(M_HEADS):
            q = act[pl.ds(r0, CHUNK), h * M_DQK:(h + 1) * M_DQK]
            k = act[pl.ds(r0, CHUNK), (M_HEADS + h) * M_DQK:(M_HEADS + h + 1) * M_DQK]
            v = v_ref[pl.ds(r0, CHUNK), h * M_DV:(h + 1) * M_DV]
            b_row = b_scr[pl.ds(h * n_chunks + c, 1), :]
            li_row = li_scr[pl.ds(h * n_chunks + c, 1), :]
            m_prev = m_scr[h, 0:1, 0:1]
            ct_prev = ct_scr[h]
            n_prev = n_scr[h, 0:1, :]

            b_col = _row_to_col(b_row, eye)
            d = jnp.where(tril, b_col - b_row + li_row, NEG)
            m_intra = jnp.max(d, axis=1, keepdims=True)
            inter_log = b_col + m_prev
            m_t = jnp.maximum(inter_log, m_intra)
            s = lax.dot_general(q, k, (((1,), (1,)), ((), ())), preferred_element_type=F32)
            w = jnp.exp(d - m_t) * s
            s_inter = jnp.exp(inter_log - m_t)
            qf = q.astype(F32)
            num = (jnp.dot(w.astype(BF16), v, preferred_element_type=F32)
                   + s_inter * jnp.dot(q, ct_prev.astype(BF16), preferred_element_type=F32))
            den = (jnp.sum(w, axis=1, keepdims=True)
                   + s_inter * jnp.sum(qf * n_prev, axis=1, keepdims=True))
            hout = num / jnp.maximum(jnp.abs(den), jnp.exp(-m_t))

            ms = jnp.mean(hout * hout, axis=1, keepdims=True)
            hn = hout * lax.rsqrt(ms + RMS_EPS) * nw_ref[:, h * M_DV:(h + 1) * M_DV]
            og = o_ref[pl.ds(r0, CHUNK), h * M_DV:(h + 1) * M_DV].astype(F32)
            out_ref[pl.ds(r0, CHUNK), h * M_DV:(h + 1) * M_DV] = (hn * jax.nn.sigmoid(og)).astype(BF16)

            g_tot = b_row[:, CHUNK - 1:CHUNK]
            a_row = g_tot - b_row + li_row
            m_loc = jnp.max(a_row, axis=1, keepdims=True)
            w_col = _row_to_col(jnp.exp(a_row - m_loc), eye)
            kw = k.astype(F32) * w_col
            ct_loc = lax.dot_general(kw.astype(BF16), v, (((0,), (0,)), ((), ())),
                                     preferred_element_type=F32)
            n_loc = jnp.sum(kw, axis=0, keepdims=True)
            m_new = jnp.maximum(g_tot + m_prev, m_loc)
            s_old = jnp.exp(g_tot + m_prev - m_new)
            s_loc = jnp.exp(m_loc - m_new)
            ct_scr[h] = s_old * ct_prev + s_loc * ct_loc
            n_scr[h] = jnp.broadcast_to(s_old * n_prev + s_loc * n_loc, (8, M_DQK))
            m_scr[h] = jnp.broadcast_to(m_new, (8, 128))
        return carry

    lax.fori_loop(0, n_chunks, chunk_body, 0)


def _mlstm(proj, gates3, conv_w, conv_b, m_norm_w):
    s = proj.shape[0]
    n_chunks = R_ML // CHUNK
    hb = R_ML // HALO
    return pl.pallas_call(
        _mlstm_kernel,
        grid=(s // R_ML,),
        in_specs=[
            pl.BlockSpec((R_ML, SEC), lambda i: (i, 0)),
            pl.BlockSpec((HALO, SEC), lambda i: (jnp.maximum(i * hb - 1, 0), 0)),
            pl.BlockSpec((R_ML, SEC), lambda i: (i, 1)),
            pl.BlockSpec((R_ML, SEC), lambda i: (i, 2)),
            pl.BlockSpec((N_GATES, n_chunks, CHUNK), lambda i: (0, i, 0)),
            pl.BlockSpec((CONV_W, SEC), lambda i: (0, 0)),
            pl.BlockSpec((1, SEC), lambda i: (0, 0)),
            pl.BlockSpec((1, SEC), lambda i: (0, 0)),
        ],
        out_specs=pl.BlockSpec((R_ML, SEC), lambda i: (i, 0)),
        out_shape=jax.ShapeDtypeStruct((s, SEC), BF16),
        scratch_shapes=[
            pltpu.VMEM((R_ML + HALO, SEC), F32),
            pltpu.VMEM((R_ML, SEC), BF16),
            pltpu.VMEM((M_HEADS * n_chunks, CHUNK), F32),
            pltpu.VMEM((M_HEADS * n_chunks, CHUNK), F32),
            pltpu.VMEM((M_HEADS, M_DQK, M_DV), F32),
            pltpu.VMEM((M_HEADS, 8, M_DQK), F32),
            pltpu.VMEM((M_HEADS, 8, 128), F32),
        ],
        compiler_params=_cparams(("arbitrary",)),
        name="mlstm",
    )(proj, proj, proj, proj, gates3, conv_w, conv_b, m_norm_w)


def _lane_partial_sum(p):
    acc = p[:, 0:128]
    for t in range(1, p.shape[1] // 128):
        acc = acc + p[:, t * 128:(t + 1) * 128]
    return acc


def _attn_kernel(par_ref, q_ref, k_ref, v_ref, lq1_ref, lk1_ref, lq2_ref, lk2_ref, nw_ref,
                 out_ref, m_scr, l_scr, acc_scr):
    i = pl.program_id(1)
    shift = par_ref[0]
    fixed = par_ref[1] > 0.5

    l_scr[...] = jnp.zeros_like(l_scr)
    acc_scr[...] = jnp.zeros_like(acc_scr)

    def chunk_mask():
        qc = lax.broadcasted_iota(jnp.int32, (TQ, TQ), 0) // CHUNK
        kc = lax.broadcasted_iota(jnp.int32, (TQ, TQ), 1) // CHUNK
        return kc <= qc

    def scores(j, c):
        r0 = pl.multiple_of(j * TQ, TQ)
        q = q_ref[:, c * A_DQK:(c + 1) * A_DQK]
        k = k_ref[pl.ds(r0, TQ), c * A_DQK:(c + 1) * A_DQK]
        return lax.dot_general(q, k, (((1,), (1,)), ((), ())), preferred_element_type=F32)

    def fixed_tile(j, masked):
        v = v_ref[pl.ds(pl.multiple_of(j * TQ, TQ), TQ), :]
        for c in range(2):
            p = jnp.exp2(scores(j, c) - shift)
            if masked:
                p = jnp.where(chunk_mask(), p, 0.0)
            l_scr[c] += _lane_partial_sum(p)
            acc_scr[c] += jnp.dot(p.astype(BF16), v, preferred_element_type=F32)

    def online_tile(j, masked):
        v = v_ref[pl.ds(pl.multiple_of(j * TQ, TQ), TQ), :]
        for c in range(2):
            s = scores(j, c)
            if masked:
                s = jnp.where(chunk_mask(), s, NEG)
            m_prev = m_scr[c]
            m_new = jnp.maximum(m_prev, jnp.max(s, axis=1, keepdims=True))
            alpha = jnp.exp2(m_prev - m_new)
            p = jnp.exp2(s - m_new[:, 0:1])
            l_scr[c] = alpha * l_scr[c] + _lane_partial_sum(p)
            acc_scr[c] = alpha[:, 0:1] * acc_scr[c] + jnp.dot(p.astype(BF16), v,
                                                              preferred_element_type=F32)
            m_scr[c] = m_new

    @pl.when(fixed)
    def _():
        def body(j, carry):
            fixed_tile(j, False)
            return carry
        lax.fori_loop(0, i, body, 0)
        fixed_tile(i, True)

    @pl.when(jnp.logical_not(fixed))
    def _():
        m_scr[...] = jnp.full_like(m_scr, NEG)

        def body(j, carry):
            online_tile(j, False)
            return carry
        lax.fori_loop(0, i, body, 0)
        online_tile(i, True)

    lam = (jnp.exp(jnp.sum(lq1_ref[...] * lk1_ref[...], axis=1, keepdims=True))
           - jnp.exp(jnp.sum(lq2_ref[...] * lk2_ref[...], axis=1, keepdims=True)) + LAM_INIT)
    l0 = jnp.sum(l_scr[0], axis=1, keepdims=True)
    l1 = jnp.sum(l_scr[1], axis=1, keepdims=True)
    o = acc_scr[0] / l0 - lam * (acc_scr[1] / l1)
    ms = jnp.mean(o * o, axis=1, keepdims=True)
    out_ref[...] = (o * lax.rsqrt(ms + RMS_EPS) * nw_ref[0] * (1.0 - LAM_INIT)).astype(BF16)


def _attn(proj, q_norm_w, k_norm_w, lq1, lk1, lq2, lk2, a_norm_w3):
    s = proj.shape[0]
    qb = 3 * SEC // A_DV
    kb = 4 * SEC // A_DV
    vb = 5 * SEC // A_DV
    bound = (A_DQK ** 0.5 * LOG2E) * jnp.max(jnp.abs(q_norm_w * k_norm_w))
    shift = jnp.ceil(bound * 1.02) + 1.0
    params = jnp.stack([shift, (shift <= MAX_FIXED_SHIFT).astype(F32)]).astype(F32)
    vec = pl.BlockSpec((1, A_DQK), lambda h, i: (0, 0))
    return pl.pallas_call(
        _attn_kernel,
        grid=(A_HEADS, s // TQ),
        in_specs=[
            pl.BlockSpec(memory_space=pltpu.SMEM),
            pl.BlockSpec((TQ, 2 * A_DQK), lambda h, i: (i, qb + h)),
            pl.BlockSpec((s, 2 * A_DQK), lambda h, i: (0, kb + h)),
            pl.BlockSpec((s, A_DV), lambda h, i: (0, vb + h)),
            vec, vec, vec, vec,
            pl.BlockSpec((1, 1, A_DV), lambda h, i: (h, 0, 0)),
        ],
        out_specs=pl.BlockSpec((TQ, A_DV), lambda h, i: (i, h)),
        out_shape=jax.ShapeDtypeStruct((s, A_HEADS * A_DV), BF16),
        scratch_shapes=[
            pltpu.VMEM((2, TQ, 128), F32),
            pltpu.VMEM((2, TQ, 128), F32),
            pltpu.VMEM((2, TQ, A_DV), F32),
        ],
        compiler_params=_cparams(("arbitrary", "arbitrary")),
        name="diffattn",
    )(params, proj, proj, proj, lq1, lk1, lq2, lk2, a_norm_w3)


def _outproj_kernel(hm_ref, ha_ref, wt_ref, wb_ref, x_ref, out_ref):
    out_ref[...] = (x_ref[...]
                    + jnp.dot(hm_ref[...], wt_ref[...], preferred_element_type=F32)
                    + jnp.dot(ha_ref[...], wb_ref[...], preferred_element_type=F32))


def _outproj(hm, ha, w_out, x2):
    s = x2.shape[0]
    return pl.pallas_call(
        _outproj_kernel,
        grid=(s // TM_OUT,),
        in_specs=[
            pl.BlockSpec((TM_OUT, SEC), lambda i: (i, 0)),
            pl.BlockSpec((TM_OUT, SEC), lambda i: (i, 0)),
            pl.BlockSpec((SEC, D_MODEL), lambda i: (0, 0)),
            pl.BlockSpec((SEC, D_MODEL), lambda i: (1, 0)),
            pl.BlockSpec((TM_OUT, D_MODEL), lambda i: (i, 0)),
        ],
        out_specs=pl.BlockSpec((TM_OUT, D_MODEL), lambda i: (i, 0)),
        out_shape=jax.ShapeDtypeStruct((s, D_MODEL), F32),
        compiler_params=_cparams(("arbitrary",)),
        name="outproj",
    )(hm, ha, w_out, w_out, x2)


def _ffn_kernel(x_ref, nw_ref, wg_ref, wu_ref, wd_ref, out_ref, h_scr):
    j = pl.program_id(1)

    @pl.when(j == 0)
    def _():
        xf = x_ref[...]
        ms = jnp.mean(xf * xf, axis=1, keepdims=True)
        h_scr[...] = (xf * lax.rsqrt(ms + RMS_EPS) * nw_ref[...]).astype(BF16)
        out_ref[...] = xf

    h = h_scr[...]
    g = jnp.dot(h, wg_ref[...], preferred_element_type=F32)
    u = jnp.dot(h, wu_ref[...], preferred_element_type=F32)
    a = (g * jax.nn.sigmoid(g) * u).astype(BF16)
    out_ref[...] += jnp.dot(a, wd_ref[...], preferred_element_type=F32)


def _ffn(x1, nw, w_gate, w_up, w_down):
    s = x1.shape[0]
    return pl.pallas_call(
        _ffn_kernel,
        grid=(s // TM_FF, D_FF // TF_FF),
        in_specs=[
            pl.BlockSpec((TM_FF, D_MODEL), lambda i, j: (i, 0)),
            pl.BlockSpec((1, D_MODEL), lambda i, j: (0, 0)),
            pl.BlockSpec((D_MODEL, TF_FF), lambda i, j: (0, j)),
            pl.BlockSpec((D_MODEL, TF_FF), lambda i, j: (0, j)),
            pl.BlockSpec((TF_FF, D_MODEL), lambda i, j: (j, 0)),
        ],
        out_specs=pl.BlockSpec((TM_FF, D_MODEL), lambda i, j: (i, 0)),
        out_shape=jax.ShapeDtypeStruct((s, D_MODEL), F32),
        scratch_shapes=[pltpu.VMEM((TM_FF, D_MODEL), BF16)],
        compiler_params=_cparams(("arbitrary", "arbitrary")),
        name="swiglu",
    )(x1, nw, w_gate, w_up, w_down)


def _layer(x2, norm1_w, w_in, conv_w, conv_b, i_bias, f_bias, m_norm_w, q_norm_w, k_norm_w,
           lambda_q1, lambda_k1, lambda_q2, lambda_k2, a_norm_w, w_out, norm2_w,
           w_gate, w_up, w_down):
    s = x2.shape[0]
    g0 = 3 * SEC
    w_lo = w_in[:, :g0].astype(BF16)
    w_hi = w_in[:, g0 + N_GATES:].astype(BF16)
    w_g = jnp.pad(w_in[:, g0:g0 + N_GATES], ((0, 0), (0, GATE_PAD - N_GATES))).astype(BF16)
    gbias = jnp.pad(jnp.concatenate([i_bias, f_bias]), (0, GATE_PAD - N_GATES))[None, :]

    proj, gates_t = _inproj(x2, norm1_w[None, :], w_lo, w_hi, w_g, gbias,
                            q_norm_w[None, :], k_norm_w[None, :])
    gates3 = gates_t.reshape(N_GATES, s // CHUNK, CHUNK)
    hm = _mlstm(proj, gates3, conv_w, conv_b[None, :], m_norm_w.reshape(1, SEC))
    ha = _attn(proj, q_norm_w, k_norm_w, lambda_q1[None, :], lambda_k1[None, :],
               lambda_q2[None, :], lambda_k2[None, :], a_norm_w[:, None, :])
    x1 = _outproj(hm, ha, w_out.astype(BF16), x2)
    return _ffn(x1, norm2_w[None, :], w_gate.astype(BF16), w_up.astype(BF16),
                w_down.astype(BF16))


def kernel(x, norm1_w, w_in, conv_w, conv_b, i_bias, f_bias, m_norm_w, q_norm_w, k_norm_w,
           lambda_q1, lambda_k1, lambda_q2, lambda_k2, a_norm_w, w_out, norm2_w,
           w_gate, w_up, w_down):
    b, s, d = x.shape
    assert d == D_MODEL and b == 1 and norm1_w.shape[0] == 1
    assert s % R_ML == 0 and s % TQ == 0 and s % TM_IN == 0 and s % TM_FF == 0
    y = _layer(x.reshape(s, d), norm1_w[0], w_in[0], conv_w[0], conv_b[0], i_bias[0], f_bias[0],
               m_norm_w[0], q_norm_w[0], k_norm_w[0], lambda_q1[0], lambda_k1[0],
               lambda_q2[0], lambda_k2[0], a_norm_w[0], w_out[0], norm2_w[0],
               w_gate[0], w_up[0], w_down[0])
    return y.reshape(b, s, d)
```

```python
import functools
import math

import jax
import jax.numpy as jnp
from jax import lax
from jax.experimental import pallas as pl
from jax.experimental.pallas import tpu as pltpu

F32 = jnp.float32
BF16 = jnp.bfloat16

D_MODEL = 2048
CHUNK = 64
SEC = 1024
N_SEC_LO = 3
M_HEADS = 4
M_DV = 256
M_DQK = 128
CONV_W = 4
A_HEADS = 4
A_DV = 256
A_DQK = 128
D_FF = 5632
RMS_EPS = 1e-6
NEG = -1e30
LAM_INIT = 0.8 - 0.6 * math.exp(-0.3 * 0)
LOG2E = 1.4426950408889634
MAX_FIXED_SHIFT = 60.0
GATE_PAD = 128
N_GATES = 2 * M_HEADS
HALO = 8

TM_IN = 1024
TM_OUT = 512
TM_FF = 1024
TF_FF = 512
R_ML = 512
TQ = 512

VMEM_LIMIT = 56 * 1024 * 1024


def _cparams(sem):
    return pltpu.CompilerParams(dimension_semantics=sem, vmem_limit_bytes=VMEM_LIMIT)


def _inproj_kernel(x_ref, nw_ref, wlo_ref, whi_ref, wg_ref, gb_ref, qw_ref, kw_ref,
                   out_ref, gt_ref, h_scr, acc_scr):
    j = pl.program_id(1)

    @pl.when(j == 0)
    def _():
        xf = x_ref[...]
        ms = jnp.mean(xf * xf, axis=1, keepdims=True)
        hb = (xf * lax.rsqrt(ms + RMS_EPS) * nw_ref[...]).astype(BF16)
        h_scr[...] = hb
        g = jnp.dot(hb, wg_ref[...], preferred_element_type=F32) + gb_ref[...]
        gt_ref[...] = g.T[0:N_GATES, :]

    @pl.when(j < N_SEC_LO)
    def _():
        acc_scr[...] = jnp.dot(h_scr[...], wlo_ref[...], preferred_element_type=F32)

    @pl.when(j >= N_SEC_LO)
    def _():
        acc_scr[...] = jnp.dot(h_scr[...], whi_ref[...], preferred_element_type=F32)

    is_q = j == N_SEC_LO
    is_k = j == N_SEC_LO + 1
    is_qk = jnp.logical_or(is_q, is_k)

    @pl.when(is_qk)
    def _():
        w = jnp.where(is_q, qw_ref[...] * (A_DQK ** -0.5 * LOG2E), kw_ref[...])
        for g in range(SEC // A_DQK):
            sl = slice(g * A_DQK, (g + 1) * A_DQK)
            a = acc_scr[:, sl]
            ms = jnp.mean(a * a, axis=1, keepdims=True)
            out_ref[:, sl] = (a * lax.rsqrt(ms + RMS_EPS) * w).astype(BF16)

    @pl.when(jnp.logical_not(is_qk))
    def _():
        out_ref[...] = acc_scr[...].astype(BF16)


def _inproj(x2, nw, w_lo, w_hi, w_gate, gbias, qw, kw):
    s = x2.shape[0]
    n_hi = w_hi.shape[1] // SEC
    assert w_lo.shape[1] == N_SEC_LO * SEC
    n_sec = N_SEC_LO + n_hi
    return pl.pallas_call(
        _inproj_kernel,
        grid=(s // TM_IN, n_sec),
        in_specs=[
            pl.BlockSpec((TM_IN, D_MODEL), lambda i, j: (i, 0)),
            pl.BlockSpec((1, D_MODEL), lambda i, j: (0, 0)),
            pl.BlockSpec((D_MODEL, SEC), lambda i, j: (0, jnp.minimum(j, N_SEC_LO - 1))),
            pl.BlockSpec((D_MODEL, SEC), lambda i, j: (0, jnp.maximum(j - N_SEC_LO, 0))),
            pl.BlockSpec((D_MODEL, GATE_PAD), lambda i, j: (0, 0)),
            pl.BlockSpec((1, GATE_PAD), lambda i, j: (0, 0)),
            pl.BlockSpec((1, A_DQK), lambda i, j: (0, 0)),
            pl.BlockSpec((1, A_DQK), lambda i, j: (0, 0)),
        ],
        out_specs=[
            pl.BlockSpec((TM_IN, SEC), lambda i, j: (i, j)),
            pl.BlockSpec((N_GATES, TM_IN), lambda i, j: (0, i)),
        ],
        out_shape=[
            jax.ShapeDtypeStruct((s, n_sec * SEC), BF16),
            jax.ShapeDtypeStruct((N_GATES, s), F32),
        ],
        scratch_shapes=[
            pltpu.VMEM((TM_IN, D_MODEL), BF16),
            pltpu.VMEM((TM_IN, SEC), F32),
        ],
        compiler_params=_cparams(("arbitrary", "arbitrary")),
        name="inproj",
    )(x2, nw, w_lo, w_hi, w_gate, gbias, qw, kw)


def _log_sigmoid(x):
    return jnp.minimum(x, 0.0) - jnp.log1p(jnp.exp(-jnp.abs(x)))


def _row_to_col(row, eye):
    return jnp.sum(jnp.where(eye, row, 0.0), axis=1, keepdims=True)


def _mlstm_kernel(qk_ref, halo_ref, v_ref, o_ref, g_ref, cw_ref, cb_ref, nw_ref,
                  out_ref, stage, act, b_scr, li_scr, ct_scr, n_scr, m_scr):
    i = pl.program_id(0)
    n_chunks = R_ML // CHUNK

    @pl.when(i == 0)
    def _():
        ct_scr[...] = jnp.zeros_like(ct_scr)
        n_scr[...] = jnp.zeros_like(n_scr)
        m_scr[...] = jnp.full_like(m_scr, NEG)

    halo = halo_ref[...].astype(F32)
    stage[0:HALO, :] = jnp.where(i == 0, 0.0, halo)
    stage[HALO:HALO + R_ML, :] = qk_ref[...].astype(F32)
    k_scale = M_DQK ** -0.5
    for cs in range(SEC // 128):
        sl = slice(cs * 128, (cs + 1) * 128)
        y = cb_ref[:, sl]
        for t in range(CONV_W):
            y = y + cw_ref[t:t + 1, sl] * stage[HALO - (CONV_W - 1) + t:HALO - (CONV_W - 1) + t + R_ML, sl]
        a = y * jax.nn.sigmoid(y)
        if cs >= M_HEADS:
            a = a * k_scale
        act[:, sl] = a.astype(BF16)

    li_scr[...] = g_ref[0:M_HEADS].reshape(M_HEADS * n_chunks, CHUNK)
    lf = _log_sigmoid(g_ref[M_HEADS:N_GATES].reshape(M_HEADS * n_chunks, CHUNK))
    r_i = lax.broadcasted_iota(jnp.int32, (CHUNK, CHUNK), 0)
    c_i = lax.broadcasted_iota(jnp.int32, (CHUNK, CHUNK), 1)
    upper = jnp.where(r_i <= c_i, 1.0, 0.0).astype(BF16)
    p0 = lf.astype(BF16)
    r1 = lf - p0.astype(F32)
    p1 = r1.astype(BF16)
    p2 = (r1 - p1.astype(F32)).astype(BF16)
    b_scr[...] = (jnp.dot(p0, upper, preferred_element_type=F32)
                  + jnp.dot(p1, upper, preferred_element_type=F32)
                  + jnp.dot(p2, upper, preferred_element_type=F32))

    eye = r_i == c_i
    tril = r_i >= c_i

    def chunk_body(c, carry):
        r0 = pl.multiple_of(c * CHUNK, CHUNK)
        for h in range(M_HEADS):
            q = act[pl.ds(r0, CHUNK), h * M_DQK:(h + 1) * M_DQK]
            k = act[pl.ds(r0, CHUNK), (M_HEADS + h) * M_DQK:(M_HEADS + h + 1) * M_DQK]
            v = v_ref[pl.ds(r0, CHUNK), h * M_DV:(h + 1) * M_DV]
            b_row = b_scr[pl.ds(h * n_chunks + c, 1), :]
            li_row = li_scr[pl.ds(h * n_chunks + c, 1), :]
            m_prev = m_scr[h, 0:1, 0:1]
            ct_prev = ct_scr[h]
            n_prev = n_scr[h, 0:1, :]

            b_col = _row_to_col(b_row, eye)
            d = jnp.where(tril, b_col - b_row + li_row, NEG)
            m_intra = jnp.max(d, axis=1, keepdims=True)
            inter_log = b_col + m_prev
            m_t = jnp.maximum(inter_log, m_intra)
            s = lax.dot_general(q, k, (((1,), (1,)), ((), ())), preferred_element_type=F32)
            w = jnp.exp(d - m_t) * s
            s_inter = jnp.exp(inter_log - m_t)
            qf = q.astype(F32)
            num = (jnp.dot(w.astype(BF16), v, preferred_element_type=F32)
                   + s_inter * jnp.dot(q, ct_prev.astype(BF16), preferred_element_type=F32))
            den = (jnp.sum(w, axis=1, keepdims=True)
                   + s_inter * jnp.sum(qf * n_prev, axis=1, keepdims=True))
            hout = num / jnp.maximum(jnp.abs(den), jnp.exp(-m_t))

            ms = jnp.mean(hout * hout, axis=1, keepdims=True)
            hn = hout * lax.rsqrt(ms + RMS_EPS) * nw_ref[:, h * M_DV:(h + 1) * M_DV]
            og = o_ref[pl.ds(r0, CHUNK), h * M_DV:(h + 1) * M_DV].astype(F32)
            out_ref[pl.ds(r0, CHUNK), h * M_DV:(h + 1) * M_DV] = (hn * jax.nn.sigmoid(og)).astype(BF16)

            g_tot = b_row[:, CHUNK - 1:CHUNK]
            a_row = g_tot - b_row + li_row
            m_loc = jnp.max(a_row, axis=1, keepdims=True)
            w_col = _row_to_col(jnp.exp(a_row - m_loc), eye)
            kw = k.astype(F32) * w_col
            ct_loc = lax.dot_general(kw.astype(BF16), v, (((0,), (0,)), ((), ())),
                                     preferred_element_type=F32)
            n_loc = jnp.sum(kw, axis=0, keepdims=True)
            m_new = jnp.maximum(g_tot + m_prev, m_loc)
            s_old = jnp.exp(g_tot + m_prev - m_new)
            s_loc = jnp.exp(m_loc - m_new)
            ct_scr[h] = s_old * ct_prev + s_loc * ct_loc
            n_scr[h] = jnp.broadcast_to(s_old * n_prev + s_loc * n_loc, (8, M_DQK))
            m_scr[h] = jnp.broadcast_to(m_new, (8, 128))
        return carry

    lax.fori_loop(0, n_chunks, chunk_body, 0)


def _mlstm(proj, gates3, conv_w, conv_b, m_norm_w):
    s = proj.shape[0]
    n_chunks = R_ML // CHUNK
    hb = R_ML // HALO
    return pl.pallas_call(
        _mlstm_kernel,
        grid=(s // R_ML,),
        in_specs=[
            pl.BlockSpec((R_ML, SEC), lambda i: (i, 0)),
            pl.BlockSpec((HALO, SEC), lambda i: (jnp.maximum(i * hb - 1, 0), 0)),
            pl.BlockSpec((R_ML, SEC), lambda i: (i, 1)),
            pl.BlockSpec((R_ML, SEC), lambda i: (i, 2)),
            pl.BlockSpec((N_GATES, n_chunks, CHUNK), lambda i: (0, i, 0)),
            pl.BlockSpec((CONV_W, SEC), lambda i: (0, 0)),
            pl.BlockSpec((1, SEC), lambda i: (0, 0)),
            pl.BlockSpec((1, SEC), lambda i: (0, 0)),
        ],
        out_specs=pl.BlockSpec((R_ML, SEC), lambda i: (i, 0)),
        out_shape=jax.ShapeDtypeStruct((s, SEC), BF16),
        scratch_shapes=[
            pltpu.VMEM((R_ML + HALO, SEC), F32),
            pltpu.VMEM((R_ML, SEC), BF16),
            pltpu.VMEM((M_HEADS * n_chunks, CHUNK), F32),
            pltpu.VMEM((M_HEADS * n_chunks, CHUNK), F32),
            pltpu.VMEM((M_HEADS, M_DQK, M_DV), F32),
            pltpu.VMEM((M_HEADS, 8, M_DQK), F32),
            pltpu.VMEM((M_HEADS, 8, 128), F32),
        ],
        compiler_params=_cparams(("arbitrary",)),
        name="mlstm",
    )(proj, proj, proj, proj, gates3, conv_w, conv_b, m_norm_w)


def _lane_partial_sum(p):
    acc = p[:, 0:128]
    for t in range(1, p.shape[1] // 128):
        acc = acc + p[:, t * 128:(t + 1) * 128]
    return acc


def _attn_kernel(par_ref, q_ref, k_ref, v_ref, lq1_ref, lk1_ref, lq2_ref, lk2_ref, nw_ref,
                 out_ref, m_scr, l_scr, acc_scr, sa_scr, sb_scr):
    i = pl.program_id(1)
    shift = par_ref[0]
    fixed = par_ref[1] > 0.5

    l_scr[...] = jnp.zeros_like(l_scr)
    acc_scr[...] = jnp.zeros_like(acc_scr)

    def chunk_mask():
        qc = lax.broadcasted_iota(jnp.int32, (TQ, TQ), 0) // CHUNK
        kc = lax.broadcasted_iota(jnp.int32, (TQ, TQ), 1) // CHUNK
        return kc <= qc

    def scores(j, c):
        r0 = pl.multiple_of(j * TQ, TQ)
        q = q_ref[:, c * A_DQK:(c + 1) * A_DQK]
        k = k_ref[pl.ds(r0, TQ), c * A_DQK:(c + 1) * A_DQK]
        return lax.dot_general(q, k, (((1,), (1,)), ((), ())), preferred_element_type=F32)

    def fixed_scores(j, s_buf):
        for c in range(2):
            s_buf[c] = scores(j, c)

    def fixed_pv(j, s_buf, masked):
        v = v_ref[pl.ds(pl.multiple_of(j * TQ, TQ), TQ), :]
        for c in range(2):
            p = jnp.exp2(s_buf[c] - shift)
            if masked:
                p = jnp.where(chunk_mask(), p, 0.0)
            l_scr[c] += _lane_partial_sum(p)
            acc_scr[c] += jnp.dot(p.astype(BF16), v, preferred_element_type=F32)

    def online_tile(j, masked):
        v = v_ref[pl.ds(pl.multiple_of(j * TQ, TQ), TQ), :]
        for c in range(2):
            s = scores(j, c)
            if masked:
                s = jnp.where(chunk_mask(), s, NEG)
            m_prev = m_scr[c]
            m_new = jnp.maximum(m_prev, jnp.max(s, axis=1, keepdims=True))
            alpha = jnp.exp2(m_prev - m_new)
            p = jnp.exp2(s - m_new[:, 0:1])
            l_scr[c] = alpha * l_scr[c] + _lane_partial_sum(p)
            acc_scr[c] = alpha[:, 0:1] * acc_scr[c] + jnp.dot(p.astype(BF16), v,
                                                              preferred_element_type=F32)
            m_scr[c] = m_new

    @pl.when(fixed)
    def _():
        fixed_scores(0, sa_scr)

        def pair(t, carry):
            j = 2 * t
            fixed_pv(j, sa_scr, False)
            fixed_scores(j + 1, sb_scr)
            fixed_pv(j + 1, sb_scr, False)
            fixed_scores(j + 2, sa_scr)
            return carry
        lax.fori_loop(0, i // 2, pair, 0)

        @pl.when(i % 2 == 1)
        def _():
            fixed_pv(i - 1, sa_scr, False)
            fixed_scores(i, sb_scr)
            fixed_pv(i, sb_scr, True)

        @pl.when(i % 2 == 0)
        def _():
            fixed_pv(i, sa_scr, True)

    @pl.when(jnp.logical_not(fixed))
    def _():
        m_scr[...] = jnp.full_like(m_scr, NEG)

        def body(j, carry):
            online_tile(j, False)
            return carry
        lax.fori_loop(0, i, body, 0)
        online_tile(i, True)

    lam = (jnp.exp(jnp.sum(lq1_ref[...] * lk1_ref[...], axis=1, keepdims=True))
           - jnp.exp(jnp.sum(lq2_ref[...] * lk2_ref[...], axis=1, keepdims=True)) + LAM_INIT)
    l0 = jnp.sum(l_scr[0], axis=1, keepdims=True)
    l1 = jnp.sum(l_scr[1], axis=1, keepdims=True)
    o = acc_scr[0] / l0 - lam * (acc_scr[1] / l1)
    ms = jnp.mean(o * o, axis=1, keepdims=True)
    out_ref[...] = (o * lax.rsqrt(ms + RMS_EPS) * nw_ref[0] * (1.0 - LAM_INIT)).astype(BF16)


def _attn(proj, q_norm_w, k_norm_w, lq1, lk1, lq2, lk2, a_norm_w3):
    s = proj.shape[0]
    qb = 3 * SEC // A_DV
    kb = 4 * SEC // A_DV
    vb = 5 * SEC // A_DV
    bound = (A_DQK ** 0.5 * LOG2E) * jnp.max(jnp.abs(q_norm_w * k_norm_w))
    shift = jnp.ceil(bound * 1.02) + 1.0
    params = jnp.stack([shift, (shift <= MAX_FIXED_SHIFT).astype(F32)]).astype(F32)
    vec = pl.BlockSpec((1, A_DQK), lambda h, i: (0, 0))
    return pl.pallas_call(
        _attn_kernel,
        grid=(A_HEADS, s // TQ),
        in_specs=[
            pl.BlockSpec(memory_space=pltpu.SMEM),
            pl.BlockSpec((TQ, 2 * A_DQK), lambda h, i: (i, qb + h)),
            pl.BlockSpec((s, 2 * A_DQK), lambda h, i: (0, kb + h)),
            pl.BlockSpec((s, A_DV), lambda h, i: (0, vb + h)),
            vec, vec, vec, vec,
            pl.BlockSpec((1, 1, A_DV), lambda h, i: (h, 0, 0)),
        ],
        out_specs=pl.BlockSpec((TQ, A_DV), lambda h, i: (i, h)),
        out_shape=jax.ShapeDtypeStruct((s, A_HEADS * A_DV), BF16),
        scratch_shapes=[
            pltpu.VMEM((2, TQ, 128), F32),
            pltpu.VMEM((2, TQ, 128), F32),
            pltpu.VMEM((2, TQ, A_DV), F32),
            pltpu.VMEM((2, TQ, TQ), F32),
            pltpu.VMEM((2, TQ, TQ), F32),
        ],
        compiler_params=_cparams(("arbitrary", "arbitrary")),
        name="diffattn",
    )(params, proj, proj, proj, lq1, lk1, lq2, lk2, a_norm_w3)


def _outproj_kernel(hm_ref, ha_ref, wt_ref, wb_ref, x_ref, out_ref):
    out_ref[...] = (x_ref[...]
                    + jnp.dot(hm_ref[...], wt_ref[...], preferred_element_type=F32)
                    + jnp.dot(ha_ref[...], wb_ref[...], preferred_element_type=F32))


def _outproj(hm, ha, w_out, x2):
    s = x2.shape[0]
    return pl.pallas_call(
        _outproj_kernel,
        grid=(s // TM_OUT,),
        in_specs=[
            pl.BlockSpec((TM_OUT, SEC), lambda i: (i, 0)),
            pl.BlockSpec((TM_OUT, SEC), lambda i: (i, 0)),
            pl.BlockSpec((SEC, D_MODEL), lambda i: (0, 0)),
            pl.BlockSpec((SEC, D_MODEL), lambda i: (1, 0)),
            pl.BlockSpec((TM_OUT, D_MODEL), lambda i: (i, 0)),
        ],
        out_specs=pl.BlockSpec((TM_OUT, D_MODEL), lambda i: (i, 0)),
        out_shape=jax.ShapeDtypeStruct((s, D_MODEL), F32),
        compiler_params=_cparams(("arbitrary",)),
        name="outproj",
    )(hm, ha, w_out, w_out, x2)


def _ffn_kernel(x_ref, nw_ref, wg_ref, wu_ref, wd_ref, out_ref, h_scr):
    j = pl.program_id(1)

    @pl.when(j == 0)
    def _():
        xf = x_ref[...]
        ms = jnp.mean(xf * xf, axis=1, keepdims=True)
        h_scr[...] = (xf * lax.rsqrt(ms + RMS_EPS) * nw_ref[...]).astype(BF16)
        out_ref[...] = xf

    h = h_scr[...]
    g = jnp.dot(h, wg_ref[...], preferred_element_type=F32)
    u = jnp.dot(h, wu_ref[...], preferred_element_type=F32)
    a = (g * jax.nn.sigmoid(g) * u).astype(BF16)
    out_ref[...] += jnp.dot(a, wd_ref[...], preferred_element_type=F32)


def _ffn(x1, nw, w_gate, w_up, w_down):
    s = x1.shape[0]
    return pl.pallas_call(
        _ffn_kernel,
        grid=(s // TM_FF, D_FF // TF_FF),
        in_specs=[
            pl.BlockSpec((TM_FF, D_MODEL), lambda i, j: (i, 0)),
            pl.BlockSpec((1, D_MODEL), lambda i, j: (0, 0)),
            pl.BlockSpec((D_MODEL, TF_FF), lambda i, j: (0, j)),
            pl.BlockSpec((D_MODEL, TF_FF), lambda i, j: (0, j)),
            pl.BlockSpec((TF_FF, D_MODEL), lambda i, j: (j, 0)),
        ],
        out_specs=pl.BlockSpec((TM_FF, D_MODEL), lambda i, j: (i, 0)),
        out_shape=jax.ShapeDtypeStruct((s, D_MODEL), F32),
        scratch_shapes=[pltpu.VMEM((TM_FF, D_MODEL), BF16)],
        compiler_params=_cparams(("arbitrary", "arbitrary")),
        name="swiglu",
    )(x1, nw, w_gate, w_up, w_down)


def _layer(x2, norm1_w, w_in, conv_w, conv_b, i_bias, f_bias, m_norm_w, q_norm_w, k_norm_w,
           lambda_q1, lambda_k1, lambda_q2, lambda_k2, a_norm_w, w_out, norm2_w,
           w_gate, w_up, w_down):
    s = x2.shape[0]
    g0 = 3 * SEC
    w_lo = w_in[:, :g0].astype(BF16)
    w_hi = w_in[:, g0 + N_GATES:].astype(BF16)
    w_g = jnp.pad(w_in[:, g0:g0 + N_GATES], ((0, 0), (0, GATE_PAD - N_GATES))).astype(BF16)
    gbias = jnp.pad(jnp.concatenate([i_bias, f_bias]), (0, GATE_PAD - N_GATES))[None, :]

    proj, gates_t = _inproj(x2, norm1_w[None, :], w_lo, w_hi, w_g, gbias,
                            q_norm_w[None, :], k_norm_w[None, :])
    gates3 = gates_t.reshape(N_GATES, s // CHUNK, CHUNK)
    hm = _mlstm(proj, gates3, conv_w, conv_b[None, :], m_norm_w.reshape(1, SEC))
    ha = _attn(proj, q_norm_w, k_norm_w, lambda_q1[None, :], lambda_k1[None, :],
               lambda_q2[None, :], lambda_k2[None, :], a_norm_w[:, None, :])
    x1 = _outproj(hm, ha, w_out.astype(BF16), x2)
    return _ffn(x1, norm2_w[None, :], w_gate.astype(BF16), w_up.astype(BF16),
                w_down.astype(BF16))


def kernel(x, norm1_w, w_in, conv_w, conv_b, i_bias, f_bias, m_norm_w, q_norm_w, k_norm_w,
           lambda_q1, lambda_k1, lambda_q2, lambda_k2, a_norm_w, w_out, norm2_w,
           w_gate, w_up, w_down):
    b, s, d = x.shape
    assert d == D_MODEL and b == 1 and norm1_w.shape[0] == 1
    assert s % R_ML == 0 and s % TQ == 0 and s % TM_IN == 0 and s % TM_FF == 0
    y = _layer(x.reshape(s, d), norm1_w[0], w_in[0], conv_w[0], conv_b[0], i_bias[0], f_bias[0],
               m_norm_w[0], q_norm_w[0], k_norm_w[0], lambda_q1[0], lambda_k1[0],
               lambda_q2[0], lambda_k2[0], a_norm_w[0], w_out[0], norm2_w[0],
               w_gate[0], w_up[0], w_down[0])
    return y.reshape(b, s, d)
```

```python
import functools
import math

import jax
import jax.numpy as jnp
from jax import lax
from jax.experimental import pallas as pl
from jax.experimental.pallas import tpu as pltpu

F32 = jnp.float32
BF16 = jnp.bfloat16

D_MODEL = 2048
CHUNK = 64
SEC = 1024
N_SEC_LO = 3
M_HEADS = 4
M_DV = 256
M_DQK = 128
CONV_W = 4
A_HEADS = 4
A_DV = 256
A_DQK = 128
D_FF = 5632
RMS_EPS = 1e-6
NEG = -1e30
LAM_INIT = 0.8 - 0.6 * math.exp(-0.3 * 0)
LOG2E = 1.4426950408889634
MAX_FIXED_SHIFT = 60.0
GATE_PAD = 128
N_GATES = 2 * M_HEADS
HALO = 8

TM_IN = 1024
TM_OUT = 512
TM_FF = 1024
TF_FF = 512
R_ML = 512
TQ = 512

VMEM_LIMIT = 56 * 1024 * 1024


def _cparams(sem):
    return pltpu.CompilerParams(dimension_semantics=sem, vmem_limit_bytes=VMEM_LIMIT)


def _inproj_kernel(x_ref, nw_ref, wlo_ref, whi_ref, wg_ref, gb_ref, qw_ref, kw_ref,
                   out_ref, gt_ref, h_scr, acc_scr):
    j = pl.program_id(1)

    @pl.when(j == 0)
    def _():
        xf = x_ref[...]
        ms = jnp.mean(xf * xf, axis=1, keepdims=True)
        hb = (xf * lax.rsqrt(ms + RMS_EPS) * nw_ref[...]).astype(BF16)
        h_scr[...] = hb
        g = jnp.dot(hb, wg_ref[...], preferred_element_type=F32) + gb_ref[...]
        gt_ref[...] = g.T[0:N_GATES, :]

    @pl.when(j < N_SEC_LO)
    def _():
        acc_scr[...] = jnp.dot(h_scr[...], wlo_ref[...], preferred_element_type=F32)

    @pl.when(j >= N_SEC_LO)
    def _():
        acc_scr[...] = jnp.dot(h_scr[...], whi_ref[...], preferred_element_type=F32)

    is_q = j == N_SEC_LO
    is_k = j == N_SEC_LO + 1
    is_qk = jnp.logical_or(is_q, is_k)

    @pl.when(is_qk)
    def _():
        w = jnp.where(is_q, qw_ref[...] * (A_DQK ** -0.5 * LOG2E), kw_ref[...])
        for g in range(SEC // A_DQK):
            sl = slice(g * A_DQK, (g + 1) * A_DQK)
            a = acc_scr[:, sl]
            ms = jnp.mean(a * a, axis=1, keepdims=True)
            out_ref[:, sl] = (a * lax.rsqrt(ms + RMS_EPS) * w).astype(BF16)

    @pl.when(jnp.logical_not(is_qk))
    def _():
        out_ref[...] = acc_scr[...].astype(BF16)


def _inproj(x2, nw, w_lo, w_hi, w_gate, gbias, qw, kw):
    s = x2.shape[0]
    n_hi = w_hi.shape[1] // SEC
    assert w_lo.shape[1] == N_SEC_LO * SEC
    n_sec = N_SEC_LO + n_hi
    return pl.pallas_call(
        _inproj_kernel,
        grid=(s // TM_IN, n_sec),
        in_specs=[
            pl.BlockSpec((TM_IN, D_MODEL), lambda i, j: (i, 0)),
            pl.BlockSpec((1, D_MODEL), lambda i, j: (0, 0)),
            pl.BlockSpec((D_MODEL, SEC), lambda i, j: (0, jnp.minimum(j, N_SEC_LO - 1))),
            pl.BlockSpec((D_MODEL, SEC), lambda i, j: (0, jnp.maximum(j - N_SEC_LO, 0))),
            pl.BlockSpec((D_MODEL, GATE_PAD), lambda i, j: (0, 0)),
            pl.BlockSpec((1, GATE_PAD), lambda i, j: (0, 0)),
            pl.BlockSpec((1, A_DQK), lambda i, j: (0, 0)),
            pl.BlockSpec((1, A_DQK), lambda i, j: (0, 0)),
        ],
        out_specs=[
            pl.BlockSpec((TM_IN, SEC), lambda i, j: (i, j)),
            pl.BlockSpec((N_GATES, TM_IN), lambda i, j: (0, i)),
        ],
        out_shape=[
            jax.ShapeDtypeStruct((s, n_sec * SEC), BF16),
            jax.ShapeDtypeStruct((N_GATES, s), F32),
        ],
        scratch_shapes=[
            pltpu.VMEM((TM_IN, D_MODEL), BF16),
            pltpu.VMEM((TM_IN, SEC), F32),
        ],
        compiler_params=_cparams(("arbitrary", "arbitrary")),
        name="inproj",
    )(x2, nw, w_lo, w_hi, w_gate, gbias, qw, kw)


def _log_sigmoid(x):
    return jnp.minimum(x, 0.0) - jnp.log1p(jnp.exp(-jnp.abs(x)))


def _row_to_col(row, eye):
    return jnp.sum(jnp.where(eye, row, 0.0), axis=1, keepdims=True)


def _mlstm_kernel(qk_ref, halo_ref, v_ref, o_ref, g_ref, cw_ref, cb_ref, nw_ref,
                  out_ref, stage, act, b_scr, li_scr, ct_scr, n_scr, m_scr):
    i = pl.program_id(0)
    n_chunks = R_ML // CHUNK

    @pl.when(i == 0)
    def _():
        ct_scr[...] = jnp.zeros_like(ct_scr)
        n_scr[...] = jnp.zeros_like(n_scr)
        m_scr[...] = jnp.full_like(m_scr, NEG)

    halo = halo_ref[...].astype(F32)
    stage[0:HALO, :] = jnp.where(i == 0, 0.0, halo)
    stage[HALO:HALO + R_ML, :] = qk_ref[...].astype(F32)
    k_scale = M_DQK ** -0.5
    for cs in range(SEC // 128):
        sl = slice(cs * 128, (cs + 1) * 128)
        y = cb_ref[:, sl]
        for t in range(CONV_W):
            y = y + cw_ref[t:t + 1, sl] * stage[HALO - (CONV_W - 1) + t:HALO - (CONV_W - 1) + t + R_ML, sl]
        a = y * jax.nn.sigmoid(y)
        if cs >= M_HEADS:
            a = a * k_scale
        act[:, sl] = a.astype(BF16)

    li_scr[...] = g_ref[0:M_HEADS].reshape(M_HEADS * n_chunks, CHUNK)
    lf = _log_sigmoid(g_ref[M_HEADS:N_GATES].reshape(M_HEADS * n_chunks, CHUNK))
    r_i = lax.broadcasted_iota(jnp.int32, (CHUNK, CHUNK), 0)
    c_i = lax.broadcasted_iota(jnp.int32, (CHUNK, CHUNK), 1)
    upper = jnp.where(r_i <= c_i, 1.0, 0.0).astype(BF16)
    p0 = lf.astype(BF16)
    r1 = lf - p0.astype(F32)
    p1 = r1.astype(BF16)
    p2 = (r1 - p1.astype(F32)).astype(BF16)
    b_scr[...] = (jnp.dot(p0, upper, preferred_element_type=F32)
                  + jnp.dot(p1, upper, preferred_element_type=F32)
                  + jnp.dot(p2, upper, preferred_element_type=F32))

    t_i = lax.broadcasted_iota(jnp.int32, (R_ML, CHUNK), 0) % CHUNK
    s_i = lax.broadcasted_iota(jnp.int32, (R_ML, CHUNK), 1)
    eye = t_i == s_i
    tril = t_i >= s_i

    def rows_of(x, c):
        return x[c * CHUNK:(c + 1) * CHUNK]

    def per_chunk_rows(x):
        return jnp.concatenate(
            [jnp.broadcast_to(x[c:c + 1, :], (CHUNK, x.shape[1])) for c in range(n_chunks)], axis=0)

    def to_col(rows):
        return jnp.sum(jnp.where(eye, rows, 0.0), axis=1, keepdims=True)

    for h in range(M_HEADS):
        qs = slice(h * M_DQK, (h + 1) * M_DQK)
        ks = slice((M_HEADS + h) * M_DQK, (M_HEADS + h + 1) * M_DQK)
        vs = slice(h * M_DV, (h + 1) * M_DV)
        q = act[:, qs]
        k = act[:, ks]
        v = v_ref[:, vs]
        b_h = b_scr[h * n_chunks:(h + 1) * n_chunks, :]
        li_h = li_scr[h * n_chunks:(h + 1) * n_chunks, :]

        b_rows = per_chunk_rows(b_h)
        b_col = to_col(b_rows)
        d = jnp.where(tril, b_col - b_rows + per_chunk_rows(li_h), NEG)
        m_intra = jnp.max(d, axis=1, keepdims=True)
        s = jnp.concatenate(
            [lax.dot_general(rows_of(q, c), rows_of(k, c), (((1,), (1,)), ((), ())),
                             preferred_element_type=F32) for c in range(n_chunks)], axis=0)
        p = jnp.exp(d - m_intra) * s
        row_sum = jnp.sum(p, axis=1, keepdims=True)
        pb = p.astype(BF16)
        n_intra = jnp.concatenate(
            [jnp.dot(rows_of(pb, c), rows_of(v, c), preferred_element_type=F32)
             for c in range(n_chunks)], axis=0)

        g_tot = b_h[:, CHUNK - 1:CHUNK]
        a = g_tot - b_h + li_h
        m_loc = jnp.max(a, axis=1, keepdims=True)
        w_col = to_col(per_chunk_rows(jnp.exp(a - m_loc)))
        kw = k.astype(F32) * w_col
        kwb = kw.astype(BF16)
        ct_locs = [lax.dot_general(rows_of(kwb, c), rows_of(v, c), (((0,), (0,)), ((), ())),
                                   preferred_element_type=F32) for c in range(n_chunks)]
        n_locs = [jnp.sum(rows_of(kw, c), axis=0, keepdims=True) for c in range(n_chunks)]

        m = m_scr[h, 0:1, 0:1]
        ct = ct_scr[h]
        n = n_scr[h, 0:1, :]
        m_prevs, ct_prevs, n_prevs = [], [], []
        for c in range(n_chunks):
            m_prevs.append(m)
            ct_prevs.append(ct.astype(BF16))
            n_prevs.append(n)
            g_c = g_tot[c:c + 1, :]
            m_loc_c = m_loc[c:c + 1, :]
            m_new = jnp.maximum(g_c + m, m_loc_c)
            s_old = jnp.exp(g_c + m - m_new)
            s_loc = jnp.exp(m_loc_c - m_new)
            ct = s_old * ct + s_loc * ct_locs[c]
            n = s_old * n + s_loc * n_locs[c]
            m = m_new
        ct_scr[h] = ct
        n_scr[h] = jnp.broadcast_to(n, (8, M_DQK))
        m_scr[h] = jnp.broadcast_to(m, (8, 128))

        m_prev = per_chunk_rows(jnp.concatenate(m_prevs, axis=0))
        n_prev = per_chunk_rows(jnp.concatenate(n_prevs, axis=0))
        inter = jnp.concatenate(
            [jnp.dot(rows_of(q, c), ct_prevs[c], preferred_element_type=F32)
             for c in range(n_chunks)], axis=0)
        inter_log = b_col + m_prev
        m_t = jnp.maximum(inter_log, m_intra)
        s_intra = jnp.exp(m_intra - m_t)
        s_inter = jnp.exp(inter_log - m_t)
        num = s_intra * n_intra + s_inter * inter
        den = (s_intra * row_sum
               + s_inter * jnp.sum(q.astype(F32) * n_prev, axis=1, keepdims=True))
        hout = num / jnp.maximum(jnp.abs(den), jnp.exp(-m_t))

        ms = jnp.mean(hout * hout, axis=1, keepdims=True)
        hn = hout * lax.rsqrt(ms + RMS_EPS) * nw_ref[:, vs]
        out_ref[:, vs] = (hn * jax.nn.sigmoid(o_ref[:, vs].astype(F32))).astype(BF16)


def _mlstm(proj, gates3, conv_w, conv_b, m_norm_w):
    s = proj.shape[0]
    n_chunks = R_ML // CHUNK
    hb = R_ML // HALO
    return pl.pallas_call(
        _mlstm_kernel,
        grid=(s // R_ML,),
        in_specs=[
            pl.BlockSpec((R_ML, SEC), lambda i: (i, 0)),
            pl.BlockSpec((HALO, SEC), lambda i: (jnp.maximum(i * hb - 1, 0), 0)),
            pl.BlockSpec((R_ML, SEC), lambda i: (i, 1)),
            pl.BlockSpec((R_ML, SEC), lambda i: (i, 2)),
            pl.BlockSpec((N_GATES, n_chunks, CHUNK), lambda i: (0, i, 0)),
            pl.BlockSpec((CONV_W, SEC), lambda i: (0, 0)),
            pl.BlockSpec((1, SEC), lambda i: (0, 0)),
            pl.BlockSpec((1, SEC), lambda i: (0, 0)),
        ],
        out_specs=pl.BlockSpec((R_ML, SEC), lambda i: (i, 0)),
        out_shape=jax.ShapeDtypeStruct((s, SEC), BF16),
        scratch_shapes=[
            pltpu.VMEM((R_ML + HALO, SEC), F32),
            pltpu.VMEM((R_ML, SEC), BF16),
            pltpu.VMEM((M_HEADS * n_chunks, CHUNK), F32),
            pltpu.VMEM((M_HEADS * n_chunks, CHUNK), F32),
            pltpu.VMEM((M_HEADS, M_DQK, M_DV), F32),
            pltpu.VMEM((M_HEADS, 8, M_DQK), F32),
            pltpu.VMEM((M_HEADS, 8, 128), F32),
        ],
        compiler_params=_cparams(("arbitrary",)),
        name="mlstm",
    )(proj, proj, proj, proj, gates3, conv_w, conv_b, m_norm_w)


def _lane_partial_sum(p):
    acc = p[:, 0:128]
    for t in range(1, p.shape[1] // 128):
        acc = acc + p[:, t * 128:(t + 1) * 128]
    return acc


def _attn_kernel(par_ref, q_ref, k_ref, v_ref, lq1_ref, lk1_ref, lq2_ref, lk2_ref, nw_ref,
                 out_ref, m_scr, l_scr, acc_scr, sa_scr, sb_scr):
    i = pl.program_id(1)
    shift = par_ref[0]
    fixed = par_ref[1] > 0.5

    l_scr[...] = jnp.zeros_like(l_scr)
    acc_scr[...] = jnp.zeros_like(acc_scr)

    def chunk_mask():
        qc = lax.broadcasted_iota(jnp.int32, (TQ, TQ), 0) // CHUNK
        kc = lax.broadcasted_iota(jnp.int32, (TQ, TQ), 1) // CHUNK
        return kc <= qc

    def scores(j, c):
        r0 = pl.multiple_of(j * TQ, TQ)
        q = q_ref[:, c * A_DQK:(c + 1) * A_DQK]
        k = k_ref[pl.ds(r0, TQ), c * A_DQK:(c + 1) * A_DQK]
        return lax.dot_general(q, k, (((1,), (1,)), ((), ())), preferred_element_type=F32)

    def fixed_scores(j, s_buf):
        for c in range(2):
            s_buf[c] = scores(j, c)

    def fixed_pv(j, s_buf, masked):
        v = v_ref[pl.ds(pl.multiple_of(j * TQ, TQ), TQ), :]
        for c in range(2):
            p = jnp.exp2(s_buf[c] - shift)
            if masked:
                p = jnp.where(chunk_mask(), p, 0.0)
            l_scr[c] += _lane_partial_sum(p)
            acc_scr[c] += jnp.dot(p.astype(BF16), v, preferred_element_type=F32)

    def online_tile(j, masked):
        v = v_ref[pl.ds(pl.multiple_of(j * TQ, TQ), TQ), :]
        for c in range(2):
            s = scores(j, c)
            if masked:
                s = jnp.where(chunk_mask(), s, NEG)
            m_prev = m_scr[c]
            m_new = jnp.maximum(m_prev, jnp.max(s, axis=1, keepdims=True))
            alpha = jnp.exp2(m_prev - m_new)
            p = jnp.exp2(s - m_new[:, 0:1])
            l_scr[c] = alpha * l_scr[c] + _lane_partial_sum(p)
            acc_scr[c] = alpha[:, 0:1] * acc_scr[c] + jnp.dot(p.astype(BF16), v,
                                                              preferred_element_type=F32)
            m_scr[c] = m_new

    @pl.when(fixed)
    def _():
        fixed_scores(0, sa_scr)

        def pair(t, carry):
            j = 2 * t
            fixed_pv(j, sa_scr, False)
            fixed_scores(j + 1, sb_scr)
            fixed_pv(j + 1, sb_scr, False)
            fixed_scores(j + 2, sa_scr)
            return carry
        lax.fori_loop(0, i // 2, pair, 0)

        @pl.when(i % 2 == 1)
        def _():
            fixed_pv(i - 1, sa_scr, False)
            fixed_scores(i, sb_scr)
            fixed_pv(i, sb_scr, True)

        @pl.when(i % 2 == 0)
        def _():
            fixed_pv(i, sa_scr, True)

    @pl.when(jnp.logical_not(fixed))
    def _():
        m_scr[...] = jnp.full_like(m_scr, NEG)

        def body(j, carry):
            online_tile(j, False)
            return carry
        lax.fori_loop(0, i, body, 0)
        online_tile(i, True)

    lam = (jnp.exp(jnp.sum(lq1_ref[...] * lk1_ref[...], axis=1, keepdims=True))
           - jnp.exp(jnp.sum(lq2_ref[...] * lk2_ref[...], axis=1, keepdims=True)) + LAM_INIT)
    l0 = jnp.sum(l_scr[0], axis=1, keepdims=True)
    l1 = jnp.sum(l_scr[1], axis=1, keepdims=True)
    o = acc_scr[0] / l0 - lam * (acc_scr[1] / l1)
    ms = jnp.mean(o * o, axis=1, keepdims=True)
    out_ref[...] = (o * lax.rsqrt(ms + RMS_EPS) * nw_ref[0] * (1.0 - LAM_INIT)).astype(BF16)


def _attn(proj, q_norm_w, k_norm_w, lq1, lk1, lq2, lk2, a_norm_w3):
    s = proj.shape[0]
    qb = 3 * SEC // A_DV
    kb = 4 * SEC // A_DV
    vb = 5 * SEC // A_DV
    bound = (A_DQK ** 0.5 * LOG2E) * jnp.max(jnp.abs(q_norm_w * k_norm_w))
    shift = jnp.ceil(bound * 1.02) + 1.0
    params = jnp.stack([shift, (shift <= MAX_FIXED_SHIFT).astype(F32)]).astype(F32)
    vec = pl.BlockSpec((1, A_DQK), lambda h, i: (0, 0))
    return pl.pallas_call(
        _attn_kernel,
        grid=(A_HEADS, s // TQ),
        in_specs=[
            pl.BlockSpec(memory_space=pltpu.SMEM),
            pl.BlockSpec((TQ, 2 * A_DQK), lambda h, i: (i, qb + h)),
            pl.BlockSpec((s, 2 * A_DQK), lambda h, i: (0, kb + h)),
            pl.BlockSpec((s, A_DV), lambda h, i: (0, vb + h)),
            vec, vec, vec, vec,
            pl.BlockSpec((1, 1, A_DV), lambda h, i: (h, 0, 0)),
        ],
        out_specs=pl.BlockSpec((TQ, A_DV), lambda h, i: (i, h)),
        out_shape=jax.ShapeDtypeStruct((s, A_HEADS * A_DV), BF16),
        scratch_shapes=[
            pltpu.VMEM((2, TQ, 128), F32),
            pltpu.VMEM((2, TQ, 128), F32),
            pltpu.VMEM((2, TQ, A_DV), F32),
            pltpu.VMEM((2, TQ, TQ), F32),
            pltpu.VMEM((2, TQ, TQ), F32),
        ],
        compiler_params=_cparams(("arbitrary", "arbitrary")),
        name="diffattn",
    )(params, proj, proj, proj, lq1, lk1, lq2, lk2, a_norm_w3)


def _outproj_kernel(hm_ref, ha_ref, wt_ref, wb_ref, x_ref, out_ref):
    out_ref[...] = (x_ref[...]
                    + jnp.dot(hm_ref[...], wt_ref[...], preferred_element_type=F32)
                    + jnp.dot(ha_ref[...], wb_ref[...], preferred_element_type=F32))


def _outproj(hm, ha, w_out, x2):
    s = x2.shape[0]
    return pl.pallas_call(
        _outproj_kernel,
        grid=(s // TM_OUT,),
        in_specs=[
            pl.BlockSpec((TM_OUT, SEC), lambda i: (i, 0)),
            pl.BlockSpec((TM_OUT, SEC), lambda i: (i, 0)),
            pl.BlockSpec((SEC, D_MODEL), lambda i: (0, 0)),
            pl.BlockSpec((SEC, D_MODEL), lambda i: (1, 0)),
            pl.BlockSpec((TM_OUT, D_MODEL), lambda i: (i, 0)),
        ],
        out_specs=pl.BlockSpec((TM_OUT, D_MODEL), lambda i: (i, 0)),
        out_shape=jax.ShapeDtypeStruct((s, D_MODEL), F32),
        compiler_params=_cparams(("arbitrary",)),
        name="outproj",
    )(hm, ha, w_out, w_out, x2)


def _ffn_kernel(x_ref, nw_ref, wg_ref, wu_ref, wd_ref, out_ref, h_scr):
    j = pl.program_id(1)

    @pl.when(j == 0)
    def _():
        xf = x_ref[...]
        ms = jnp.mean(xf * xf, axis=1, keepdims=True)
        h_scr[...] = (xf * lax.rsqrt(ms + RMS_EPS) * nw_ref[...]).astype(BF16)
        out_ref[...] = xf

    h = h_scr[...]
    g = jnp.dot(h, wg_ref[...], preferred_element_type=F32)
    u = jnp.dot(h, wu_ref[...], preferred_element_type=F32)
    a = (g * jax.nn.sigmoid(g) * u).astype(BF16)
    out_ref[...] += jnp.dot(a, wd_ref[...], preferred_element_type=F32)


def _ffn(x1, nw, w_gate, w_up, w_down):
    s = x1.shape[0]
    return pl.pallas_call(
        _ffn_kernel,
        grid=(s // TM_FF, D_FF // TF_FF),
        in_specs=[
            pl.BlockSpec((TM_FF, D_MODEL), lambda i, j: (i, 0)),
            pl.BlockSpec((1, D_MODEL), lambda i, j: (0, 0)),
            pl.BlockSpec((D_MODEL, TF_FF), lambda i, j: (0, j)),
            pl.BlockSpec((D_MODEL, TF_FF), lambda i, j: (0, j)),
            pl.BlockSpec((TF_FF, D_MODEL), lambda i, j: (j, 0)),
        ],
        out_specs=pl.BlockSpec((TM_FF, D_MODEL), lambda i, j: (i, 0)),
        out_shape=jax.ShapeDtypeStruct((s, D_MODEL), F32),
        scratch_shapes=[pltpu.VMEM((TM_FF, D_MODEL), BF16)],
        compiler_params=_cparams(("arbitrary", "arbitrary")),
        name="swiglu",
    )(x1, nw, w_gate, w_up, w_down)


def _layer(x2, norm1_w, w_in, conv_w, conv_b, i_bias, f_bias, m_norm_w, q_norm_w, k_norm_w,
           lambda_q1, lambda_k1, lambda_q2, lambda_k2, a_norm_w, w_out, norm2_w,
           w_gate, w_up, w_down):
    s = x2.shape[0]
    g0 = 3 * SEC
    w_lo = w_in[:, :g0].astype(BF16)
    w_hi = w_in[:, g0 + N_GATES:].astype(BF16)
    w_g = jnp.pad(w_in[:, g0:g0 + N_GATES], ((0, 0), (0, GATE_PAD - N_GATES))).astype(BF16)
    gbias = jnp.pad(jnp.concatenate([i_bias, f_bias]), (0, GATE_PAD - N_GATES))[None, :]

    proj, gates_t = _inproj(x2, norm1_w[None, :], w_lo, w_hi, w_g, gbias,
                            q_norm_w[None, :], k_norm_w[None, :])
    gates3 = gates_t.reshape(N_GATES, s // CHUNK, CHUNK)
    hm = _mlstm(proj, gates3, conv_w, conv_b[None, :], m_norm_w.reshape(1, SEC))
    ha = _attn(proj, q_norm_w, k_norm_w, lambda_q1[None, :], lambda_k1[None, :],
               lambda_q2[None, :], lambda_k2[None, :], a_norm_w[:, None, :])
    x1 = _outproj(hm, ha, w_out.astype(BF16), x2)
    return _ffn(x1, norm2_w[None, :], w_gate.astype(BF16), w_up.astype(BF16),
                w_down.astype(BF16))


def kernel(x, norm1_w, w_in, conv_w, conv_b, i_bias, f_bias, m_norm_w, q_norm_w, k_norm_w,
           lambda_q1, lambda_k1, lambda_q2, lambda_k2, a_norm_w, w_out, norm2_w,
           w_gate, w_up, w_down):
    b, s, d = x.shape
    assert d == D_MODEL and b == 1 and norm1_w.shape[0] == 1
    assert s % R_ML == 0 and s % TQ == 0 and s % TM_IN == 0 and s % TM_FF == 0
    y = _layer(x.reshape(s, d), norm1_w[0], w_in[0], conv_w[0], conv_b[0], i_bias[0], f_bias[0],
               m_norm_w[0], q_norm_w[0], k_norm_w[0], lambda_q1[0], lambda_k1[0],
               lambda_q2[0], lambda_k2[0], a_norm_w[0], w_out[0], norm2_w[0],
               w_gate[0], w_up[0], w_down[0])
    return y.reshape(b, s, d)
```

```python
import functools
import math

import jax
import jax.numpy as jnp
from jax import lax
from jax.experimental import pallas as pl
from jax.experimental.pallas import tpu as pltpu

F32 = jnp.float32
BF16 = jnp.bfloat16

D_MODEL = 2048
CHUNK = 64
SEC = 1024
N_SEC_LO = 3
M_HEADS = 4
M_DV = 256
M_DQK = 128
CONV_W = 4
A_HEADS = 4
A_DV = 256
A_DQK = 128
D_FF = 5632
RMS_EPS = 1e-6
NEG = -1e30
LAM_INIT = 0.8 - 0.6 * math.exp(-0.3 * 0)
LOG2E = 1.4426950408889634
MAX_FIXED_SHIFT = 60.0
GATE_PAD = 128
N_GATES = 2 * M_HEADS
HALO = 8
BF16_SUBLANES = 16

TR_PREP = 256
TM_IN = 1024
TM_OUT = 512
TM_FF = 1024
TF_FF = 512
R_ML = 512
TQ = 512

VMEM_LIMIT = 56 * 1024 * 1024


def _cparams(sem):
    return pltpu.CompilerParams(dimension_semantics=sem, vmem_limit_bytes=VMEM_LIMIT)


def _wprep_kernel(w_ref, lo_ref, hi_ref, g_ref):
    g0 = N_SEC_LO * SEC
    lo_ref[...] = w_ref[:, 0:g0].astype(BF16)
    g_ref[...] = w_ref[:, g0:g0 + GATE_PAD].astype(BF16)
    hi_ref[...] = w_ref[:, g0 + N_GATES:].astype(BF16)


def _wprep(w_in):
    d, n_in = w_in.shape
    g0 = N_SEC_LO * SEC
    n_hi = n_in - g0 - N_GATES
    return pl.pallas_call(
        _wprep_kernel,
        grid=(d // TR_PREP,),
        in_specs=[pl.BlockSpec((TR_PREP, n_in), lambda i: (i, 0))],
        out_specs=[
            pl.BlockSpec((TR_PREP, g0), lambda i: (i, 0)),
            pl.BlockSpec((TR_PREP, n_hi), lambda i: (i, 0)),
            pl.BlockSpec((TR_PREP, GATE_PAD), lambda i: (i, 0)),
        ],
        out_shape=[
            jax.ShapeDtypeStruct((d, g0), BF16),
            jax.ShapeDtypeStruct((d, n_hi), BF16),
            jax.ShapeDtypeStruct((d, GATE_PAD), BF16),
        ],
        compiler_params=_cparams(("arbitrary",)),
        name="wprep",
    )(w_in)


def _inproj_kernel(x_ref, nw_ref, wlo_ref, whi_ref, wg_ref, gb_ref, qw_ref, kw_ref,
                   out_ref, gt_ref, h_scr, acc_scr):
    j = pl.program_id(1)

    @pl.when(j == 0)
    def _():
        xf = x_ref[...]
        ms = jnp.mean(xf * xf, axis=1, keepdims=True)
        hb = (xf * lax.rsqrt(ms + RMS_EPS) * nw_ref[...]).astype(BF16)
        h_scr[...] = hb
        g = jnp.dot(hb, wg_ref[...], preferred_element_type=F32) + gb_ref[...]
        gt_ref[...] = g.T[0:N_GATES, :]

    @pl.when(j < N_SEC_LO)
    def _():
        acc_scr[...] = jnp.dot(h_scr[...], wlo_ref[...], preferred_element_type=F32)

    @pl.when(j >= N_SEC_LO)
    def _():
        acc_scr[...] = jnp.dot(h_scr[...], whi_ref[...], preferred_element_type=F32)

    is_q = j == N_SEC_LO
    is_k = j == N_SEC_LO + 1
    is_qk = jnp.logical_or(is_q, is_k)

    @pl.when(is_qk)
    def _():
        w = jnp.where(is_q, qw_ref[...] * (A_DQK ** -0.5 * LOG2E), kw_ref[...])
        for g in range(SEC // A_DQK):
            sl = slice(g * A_DQK, (g + 1) * A_DQK)
            a = acc_scr[:, sl]
            ms = jnp.mean(a * a, axis=1, keepdims=True)
            out_ref[:, sl] = (a * lax.rsqrt(ms + RMS_EPS) * w).astype(BF16)

    @pl.when(jnp.logical_not(is_qk))
    def _():
        out_ref[...] = acc_scr[...].astype(BF16)


def _inproj(x2, nw, w_lo, w_hi, w_gate, gbias, qw, kw):
    s = x2.shape[0]
    n_hi = w_hi.shape[1] // SEC
    assert w_lo.shape[1] == N_SEC_LO * SEC
    n_sec = N_SEC_LO + n_hi
    return pl.pallas_call(
        _inproj_kernel,
        grid=(s // TM_IN, n_sec),
        in_specs=[
            pl.BlockSpec((TM_IN, D_MODEL), lambda i, j: (i, 0)),
            pl.BlockSpec((1, D_MODEL), lambda i, j: (0, 0)),
            pl.BlockSpec((D_MODEL, SEC), lambda i, j: (0, jnp.minimum(j, N_SEC_LO - 1))),
            pl.BlockSpec((D_MODEL, SEC), lambda i, j: (0, jnp.maximum(j - N_SEC_LO, 0))),
            pl.BlockSpec((D_MODEL, GATE_PAD), lambda i, j: (0, 0)),
            pl.BlockSpec((1, GATE_PAD), lambda i, j: (0, 0)),
            pl.BlockSpec((1, A_DQK), lambda i, j: (0, 0)),
            pl.BlockSpec((1, A_DQK), lambda i, j: (0, 0)),
        ],
        out_specs=[
            pl.BlockSpec((TM_IN, SEC), lambda i, j: (i, j)),
            pl.BlockSpec((N_GATES, TM_IN), lambda i, j: (0, i)),
        ],
        out_shape=[
            jax.ShapeDtypeStruct((s, n_sec * SEC), BF16),
            jax.ShapeDtypeStruct((N_GATES, s), F32),
        ],
        scratch_shapes=[
            pltpu.VMEM((TM_IN, D_MODEL), BF16),
            pltpu.VMEM((TM_IN, SEC), F32),
        ],
        compiler_params=_cparams(("arbitrary", "arbitrary")),
        name="inproj",
    )(x2, nw, w_lo, w_hi, w_gate, gbias, qw, kw)


def _log_sigmoid(x):
    return jnp.minimum(x, 0.0) - jnp.log1p(jnp.exp(-jnp.abs(x)))


def _row_to_col(row, eye):
    return jnp.sum(jnp.where(eye, row, 0.0), axis=1, keepdims=True)


def _mlstm_kernel(qk_ref, halo_ref, v_ref, o_ref, g_ref, cw_ref, cb_ref, nw_ref,
                  wa_ref, wb_ref,
                  out_ref, wa_out_ref, wb_out_ref,
                  stage, act, b_scr, li_scr, ct_scr, n_scr, m_scr):
    i = pl.program_id(0)
    n_chunks = R_ML // CHUNK

    wa_out_ref[...] = wa_ref[...].astype(BF16)
    wb_out_ref[...] = wb_ref[...].astype(BF16)

    @pl.when(i == 0)
    def _():
        ct_scr[...] = jnp.zeros_like(ct_scr)
        n_scr[...] = jnp.zeros_like(n_scr)
        m_scr[...] = jnp.full_like(m_scr, NEG)

    halo = halo_ref[...].astype(F32)
    stage[0:HALO, :] = jnp.where(i == 0, 0.0, halo)
    stage[HALO:HALO + R_ML, :] = qk_ref[...].astype(F32)
    k_scale = M_DQK ** -0.5
    for cs in range(SEC // 128):
        sl = slice(cs * 128, (cs + 1) * 128)
        y = cb_ref[:, sl]
        for t in range(CONV_W):
            y = y + cw_ref[t:t + 1, sl] * stage[HALO - (CONV_W - 1) + t:HALO - (CONV_W - 1) + t + R_ML, sl]
        a = y * jax.nn.sigmoid(y)
        if cs >= M_HEADS:
            a = a * k_scale
        act[:, sl] = a.astype(BF16)

    li_scr[...] = g_ref[0:M_HEADS].reshape(M_HEADS * n_chunks, CHUNK)
    lf = _log_sigmoid(g_ref[M_HEADS:N_GATES].reshape(M_HEADS * n_chunks, CHUNK))
    r_i = lax.broadcasted_iota(jnp.int32, (CHUNK, CHUNK), 0)
    c_i = lax.broadcasted_iota(jnp.int32, (CHUNK, CHUNK), 1)
    upper = jnp.where(r_i <= c_i, 1.0, 0.0).astype(BF16)
    p0 = lf.astype(BF16)
    r1 = lf - p0.astype(F32)
    p1 = r1.astype(BF16)
    p2 = (r1 - p1.astype(F32)).astype(BF16)
    b_scr[...] = (jnp.dot(p0, upper, preferred_element_type=F32)
                  + jnp.dot(p1, upper, preferred_element_type=F32)
                  + jnp.dot(p2, upper, preferred_element_type=F32))

    t_i = lax.broadcasted_iota(jnp.int32, (R_ML, CHUNK), 0) % CHUNK
    s_i = lax.broadcasted_iota(jnp.int32, (R_ML, CHUNK), 1)
    eye = t_i == s_i
    tril = t_i >= s_i

    def rows_of(x, c):
        return x[c * CHUNK:(c + 1) * CHUNK]

    def per_chunk_rows(x):
        return jnp.concatenate(
            [jnp.broadcast_to(x[c:c + 1, :], (CHUNK, x.shape[1])) for c in range(n_chunks)], axis=0)

    def to_col(rows):
        return jnp.sum(jnp.where(eye, rows, 0.0), axis=1, keepdims=True)

    for h in range(M_HEADS):
        qs = slice(h * M_DQK, (h + 1) * M_DQK)
        ks = slice((M_HEADS + h) * M_DQK, (M_HEADS + h + 1) * M_DQK)
        vs = slice(h * M_DV, (h + 1) * M_DV)
        q = act[:, qs]
        k = act[:, ks]
        v = v_ref[:, vs]
        b_h = b_scr[h * n_chunks:(h + 1) * n_chunks, :]
        li_h = li_scr[h * n_chunks:(h + 1) * n_chunks, :]

        b_rows = per_chunk_rows(b_h)
        b_col = to_col(b_rows)
        d = jnp.where(tril, b_col - b_rows + per_chunk_rows(li_h), NEG)
        m_intra = jnp.max(d, axis=1, keepdims=True)
        s = jnp.concatenate(
            [lax.dot_general(rows_of(q, c), rows_of(k, c), (((1,), (1,)), ((), ())),
                             preferred_element_type=F32) for c in range(n_chunks)], axis=0)
        p = jnp.exp(d - m_intra) * s
        row_sum = jnp.sum(p, axis=1, keepdims=True)
        pb = p.astype(BF16)
        n_intra = jnp.concatenate(
            [jnp.dot(rows_of(pb, c), rows_of(v, c), preferred_element_type=F32)
             for c in range(n_chunks)], axis=0)

        g_tot = b_h[:, CHUNK - 1:CHUNK]
        a = g_tot - b_h + li_h
        m_loc = jnp.max(a, axis=1, keepdims=True)
        w_col = to_col(per_chunk_rows(jnp.exp(a - m_loc)))
        kw = k.astype(F32) * w_col
        kwb = kw.astype(BF16)
        ct_locs = [lax.dot_general(rows_of(kwb, c), rows_of(v, c), (((0,), (0,)), ((), ())),
                                   preferred_element_type=F32) for c in range(n_chunks)]
        n_locs = [jnp.sum(rows_of(kw, c), axis=0, keepdims=True) for c in range(n_chunks)]

        m = m_scr[h, 0:1, 0:1]
        ct = ct_scr[h]
        n = n_scr[h, 0:1, :]
        m_prevs, ct_prevs, n_prevs = [], [], []
        for c in range(n_chunks):
            m_prevs.append(m)
            ct_prevs.append(ct.astype(BF16))
            n_prevs.append(n)
            g_c = g_tot[c:c + 1, :]
            m_loc_c = m_loc[c:c + 1, :]
            m_new = jnp.maximum(g_c + m, m_loc_c)
            s_old = jnp.exp(g_c + m - m_new)
            s_loc = jnp.exp(m_loc_c - m_new)
            ct = s_old * ct + s_loc * ct_locs[c]
            n = s_old * n + s_loc * n_locs[c]
            m = m_new
        ct_scr[h] = ct
        n_scr[h] = jnp.broadcast_to(n, (8, M_DQK))
        m_scr[h] = jnp.broadcast_to(m, (8, 128))

        m_prev = per_chunk_rows(jnp.concatenate(m_prevs, axis=0))
        n_prev = per_chunk_rows(jnp.concatenate(n_prevs, axis=0))
        inter = jnp.concatenate(
            [jnp.dot(rows_of(q, c), ct_prevs[c], preferred_element_type=F32)
             for c in range(n_chunks)], axis=0)
        inter_log = b_col + m_prev
        m_t = jnp.maximum(inter_log, m_intra)
        s_intra = jnp.exp(m_intra - m_t)
        s_inter = jnp.exp(inter_log - m_t)
        num = s_intra * n_intra + s_inter * inter
        den = (s_intra * row_sum
               + s_inter * jnp.sum(q.astype(F32) * n_prev, axis=1, keepdims=True))
        hout = num / jnp.maximum(jnp.abs(den), jnp.exp(-m_t))

        ms = jnp.mean(hout * hout, axis=1, keepdims=True)
        hn = hout * lax.rsqrt(ms + RMS_EPS) * nw_ref[:, vs]
        out_ref[:, vs] = (hn * jax.nn.sigmoid(o_ref[:, vs].astype(F32))).astype(BF16)


def _row_slab(w, n_steps, step_of):
    rows, cols = w.shape
    assert rows % (n_steps * BF16_SUBLANES) == 0, (w.shape, n_steps)
    return pl.BlockSpec((rows // n_steps, cols), lambda *idx: (step_of(*idx), 0))


def _mlstm(proj, gates3, conv_w, conv_b, m_norm_w, w_a, w_b):
    s = proj.shape[0]
    n_chunks = R_ML // CHUNK
    hb = R_ML // HALO
    n_steps = s // R_ML
    return pl.pallas_call(
        _mlstm_kernel,
        grid=(n_steps,),
        in_specs=[
            pl.BlockSpec((R_ML, SEC), lambda i: (i, 0)),
            pl.BlockSpec((HALO, SEC), lambda i: (jnp.maximum(i * hb - 1, 0), 0)),
            pl.BlockSpec((R_ML, SEC), lambda i: (i, 1)),
            pl.BlockSpec((R_ML, SEC), lambda i: (i, 2)),
            pl.BlockSpec((N_GATES, n_chunks, CHUNK), lambda i: (0, i, 0)),
            pl.BlockSpec((CONV_W, SEC), lambda i: (0, 0)),
            pl.BlockSpec((1, SEC), lambda i: (0, 0)),
            pl.BlockSpec((1, SEC), lambda i: (0, 0)),
            _row_slab(w_a, n_steps, lambda i: i),
            _row_slab(w_b, n_steps, lambda i: i),
        ],
        out_specs=[
            pl.BlockSpec((R_ML, SEC), lambda i: (i, 0)),
            _row_slab(w_a, n_steps, lambda i: i),
            _row_slab(w_b, n_steps, lambda i: i),
        ],
        out_shape=[
            jax.ShapeDtypeStruct((s, SEC), BF16),
            jax.ShapeDtypeStruct(w_a.shape, BF16),
            jax.ShapeDtypeStruct(w_b.shape, BF16),
        ],
        scratch_shapes=[
            pltpu.VMEM((R_ML + HALO, SEC), F32),
            pltpu.VMEM((R_ML, SEC), BF16),
            pltpu.VMEM((M_HEADS * n_chunks, CHUNK), F32),
            pltpu.VMEM((M_HEADS * n_chunks, CHUNK), F32),
            pltpu.VMEM((M_HEADS, M_DQK, M_DV), F32),
            pltpu.VMEM((M_HEADS, 8, M_DQK), F32),
            pltpu.VMEM((M_HEADS, 8, 128), F32),
        ],
        compiler_params=_cparams(("arbitrary",)),
        name="mlstm",
    )(proj, proj, proj, proj, gates3, conv_w, conv_b, m_norm_w, w_a, w_b)


def _lane_partial_sum(p):
    acc = p[:, 0:128]
    for t in range(1, p.shape[1] // 128):
        acc = acc + p[:, t * 128:(t + 1) * 128]
    return acc


def _attn_kernel(par_ref, q_ref, k_ref, v_ref, lq1_ref, lk1_ref, lq2_ref, lk2_ref, nw_ref,
                 wa_ref, wb_ref,
                 out_ref, wa_out_ref, wb_out_ref,
                 m_scr, l_scr, acc_scr, sa_scr, sb_scr):
    wa_out_ref[...] = wa_ref[...].astype(BF16)
    wb_out_ref[...] = wb_ref[...].astype(BF16)

    i = pl.program_id(1)
    shift = par_ref[0]
    fixed = par_ref[1] > 0.5

    l_scr[...] = jnp.zeros_like(l_scr)
    acc_scr[...] = jnp.zeros_like(acc_scr)

    def chunk_mask():
        qc = lax.broadcasted_iota(jnp.int32, (TQ, TQ), 0) // CHUNK
        kc = lax.broadcasted_iota(jnp.int32, (TQ, TQ), 1) // CHUNK
        return kc <= qc

    def scores(j, c):
        r0 = pl.multiple_of(j * TQ, TQ)
        q = q_ref[:, c * A_DQK:(c + 1) * A_DQK]
        k = k_ref[pl.ds(r0, TQ), c * A_DQK:(c + 1) * A_DQK]
        return lax.dot_general(q, k, (((1,), (1,)), ((), ())), preferred_element_type=F32)

    def fixed_scores(j, s_buf):
        for c in range(2):
            s_buf[c] = scores(j, c)

    def fixed_pv(j, s_buf, masked):
        v = v_ref[pl.ds(pl.multiple_of(j * TQ, TQ), TQ), :]
        for c in range(2):
            p = jnp.exp2(s_buf[c] - shift)
            if masked:
                p = jnp.where(chunk_mask(), p, 0.0)
            l_scr[c] += _lane_partial_sum(p)
            acc_scr[c] += jnp.dot(p.astype(BF16), v, preferred_element_type=F32)

    def online_tile(j, masked):
        v = v_ref[pl.ds(pl.multiple_of(j * TQ, TQ), TQ), :]
        for c in range(2):
            s = scores(j, c)
            if masked:
                s = jnp.where(chunk_mask(), s, NEG)
            m_prev = m_scr[c]
            m_new = jnp.maximum(m_prev, jnp.max(s, axis=1, keepdims=True))
            alpha = jnp.exp2(m_prev - m_new)
            p = jnp.exp2(s - m_new[:, 0:1])
            l_scr[c] = alpha * l_scr[c] + _lane_partial_sum(p)
            acc_scr[c] = alpha[:, 0:1] * acc_scr[c] + jnp.dot(p.astype(BF16), v,
                                                              preferred_element_type=F32)
            m_scr[c] = m_new

    @pl.when(fixed)
    def _():
        fixed_scores(0, sa_scr)

        def pair(t, carry):
            j = 2 * t
            fixed_pv(j, sa_scr, False)
            fixed_scores(j + 1, sb_scr)
            fixed_pv(j + 1, sb_scr, False)
            fixed_scores(j + 2, sa_scr)
            return carry
        lax.fori_loop(0, i // 2, pair, 0)

        @pl.when(i % 2 == 1)
        def _():
            fixed_pv(i - 1, sa_scr, False)
            fixed_scores(i, sb_scr)
            fixed_pv(i, sb_scr, True)

        @pl.when(i % 2 == 0)
        def _():
            fixed_pv(i, sa_scr, True)

    @pl.when(jnp.logical_not(fixed))
    def _():
        m_scr[...] = jnp.full_like(m_scr, NEG)

        def body(j, carry):
            online_tile(j, False)
            return carry
        lax.fori_loop(0, i, body, 0)
        online_tile(i, True)

    lam = (jnp.exp(jnp.sum(lq1_ref[...] * lk1_ref[...], axis=1, keepdims=True))
           - jnp.exp(jnp.sum(lq2_ref[...] * lk2_ref[...], axis=1, keepdims=True)) + LAM_INIT)
    l0 = jnp.sum(l_scr[0], axis=1, keepdims=True)
    l1 = jnp.sum(l_scr[1], axis=1, keepdims=True)
    o = acc_scr[0] / l0 - lam * (acc_scr[1] / l1)
    ms = jnp.mean(o * o, axis=1, keepdims=True)
    out_ref[...] = (o * lax.rsqrt(ms + RMS_EPS) * nw_ref[0] * (1.0 - LAM_INIT)).astype(BF16)


def _attn(proj, q_norm_w, k_norm_w, lq1, lk1, lq2, lk2, a_norm_w3, w_a, w_b):
    s = proj.shape[0]
    qb = 3 * SEC // A_DV
    kb = 4 * SEC // A_DV
    vb = 5 * SEC // A_DV
    bound = (A_DQK ** 0.5 * LOG2E) * jnp.max(jnp.abs(q_norm_w * k_norm_w))
    shift = jnp.ceil(bound * 1.02) + 1.0
    params = jnp.stack([shift, (shift <= MAX_FIXED_SHIFT).astype(F32)]).astype(F32)
    vec = pl.BlockSpec((1, A_DQK), lambda h, i: (0, 0))
    nq = s // TQ
    n_steps = A_HEADS * nq
    return pl.pallas_call(
        _attn_kernel,
        grid=(A_HEADS, nq),
        in_specs=[
            pl.BlockSpec(memory_space=pltpu.SMEM),
            pl.BlockSpec((TQ, 2 * A_DQK), lambda h, i: (i, qb + h)),
            pl.BlockSpec((s, 2 * A_DQK), lambda h, i: (0, kb + h)),
            pl.BlockSpec((s, A_DV), lambda h, i: (0, vb + h)),
            vec, vec, vec, vec,
            pl.BlockSpec((1, 1, A_DV), lambda h, i: (h, 0, 0)),
            _row_slab(w_a, n_steps, lambda h, i: h * nq + i),
            _row_slab(w_b, n_steps, lambda h, i: h * nq + i),
        ],
        out_specs=[
            pl.BlockSpec((TQ, A_DV), lambda h, i: (i, h)),
            _row_slab(w_a, n_steps, lambda h, i: h * nq + i),
            _row_slab(w_b, n_steps, lambda h, i: h * nq + i),
        ],
        out_shape=[
            jax.ShapeDtypeStruct((s, A_HEADS * A_DV), BF16),
            jax.ShapeDtypeStruct(w_a.shape, BF16),
            jax.ShapeDtypeStruct(w_b.shape, BF16),
        ],
        scratch_shapes=[
            pltpu.VMEM((2, TQ, 128), F32),
            pltpu.VMEM((2, TQ, 128), F32),
            pltpu.VMEM((2, TQ, A_DV), F32),
            pltpu.VMEM((2, TQ, TQ), F32),
            pltpu.VMEM((2, TQ, TQ), F32),
        ],
        compiler_params=_cparams(("arbitrary", "arbitrary")),
        name="diffattn",
    )(params, proj, proj, proj, lq1, lk1, lq2, lk2, a_norm_w3, w_a, w_b)


def _outproj_kernel(hm_ref, ha_ref, wt_ref, wb_ref, x_ref, out_ref):
    out_ref[...] = (x_ref[...]
                    + jnp.dot(hm_ref[...], wt_ref[...], preferred_element_type=F32)
                    + jnp.dot(ha_ref[...], wb_ref[...], preferred_element_type=F32))


def _outproj(hm, ha, w_out, x2):
    s = x2.shape[0]
    return pl.pallas_call(
        _outproj_kernel,
        grid=(s // TM_OUT,),
        in_specs=[
            pl.BlockSpec((TM_OUT, SEC), lambda i: (i, 0)),
            pl.BlockSpec((TM_OUT, SEC), lambda i: (i, 0)),
            pl.BlockSpec((SEC, D_MODEL), lambda i: (0, 0)),
            pl.BlockSpec((SEC, D_MODEL), lambda i: (1, 0)),
            pl.BlockSpec((TM_OUT, D_MODEL), lambda i: (i, 0)),
        ],
        out_specs=pl.BlockSpec((TM_OUT, D_MODEL), lambda i: (i, 0)),
        out_shape=jax.ShapeDtypeStruct((s, D_MODEL), F32),
        compiler_params=_cparams(("arbitrary",)),
        name="outproj",
    )(hm, ha, w_out, w_out, x2)


def _ffn_kernel(x_ref, nw_ref, wg_ref, wu_ref, wd_ref, out_ref, h_scr):
    j = pl.program_id(1)

    @pl.when(j == 0)
    def _():
        xf = x_ref[...]
        ms = jnp.mean(xf * xf, axis=1, keepdims=True)
        h_scr[...] = (xf * lax.rsqrt(ms + RMS_EPS) * nw_ref[...]).astype(BF16)
        out_ref[...] = xf

    h = h_scr[...]
    g = jnp.dot(h, wg_ref[...], preferred_element_type=F32)
    u = jnp.dot(h, wu_ref[...], preferred_element_type=F32)
    a = (g * jax.nn.sigmoid(g) * u).astype(BF16)
    out_ref[...] += jnp.dot(a, wd_ref[...], preferred_element_type=F32)


def _ffn(x1, nw, w_gate, w_up, w_down):
    s = x1.shape[0]
    return pl.pallas_call(
        _ffn_kernel,
        grid=(s // TM_FF, D_FF // TF_FF),
        in_specs=[
            pl.BlockSpec((TM_FF, D_MODEL), lambda i, j: (i, 0)),
            pl.BlockSpec((1, D_MODEL), lambda i, j: (0, 0)),
            pl.BlockSpec((D_MODEL, TF_FF), lambda i, j: (0, j)),
            pl.BlockSpec((D_MODEL, TF_FF), lambda i, j: (0, j)),
            pl.BlockSpec((TF_FF, D_MODEL), lambda i, j: (j, 0)),
        ],
        out_specs=pl.BlockSpec((TM_FF, D_MODEL), lambda i, j: (i, 0)),
        out_shape=jax.ShapeDtypeStruct((s, D_MODEL), F32),
        scratch_shapes=[pltpu.VMEM((TM_FF, D_MODEL), BF16)],
        compiler_params=_cparams(("arbitrary", "arbitrary")),
        name="swiglu",
    )(x1, nw, w_gate, w_up, w_down)


def _layer(x2, norm1_w, w_in, conv_w, conv_b, i_bias, f_bias, m_norm_w, q_norm_w, k_norm_w,
           lambda_q1, lambda_k1, lambda_q2, lambda_k2, a_norm_w, w_out, norm2_w,
           w_gate, w_up, w_down):
    s = x2.shape[0]
    w_lo, w_hi, w_g = _wprep(w_in)
    gbias = jnp.pad(jnp.concatenate([i_bias, f_bias]), (0, GATE_PAD - N_GATES))[None, :]

    proj, gates_t = _inproj(x2, norm1_w[None, :], w_lo, w_hi, w_g, gbias,
                            q_norm_w[None, :], k_norm_w[None, :])
    gates3 = gates_t.reshape(N_GATES, s // CHUNK, CHUNK)
    hm, w_out_b, w_down_b = _mlstm(proj, gates3, conv_w, conv_b[None, :],
                                   m_norm_w.reshape(1, SEC), w_out, w_down)
    ha, w_gate_b, w_up_b = _attn(proj, q_norm_w, k_norm_w, lambda_q1[None, :],
                                 lambda_k1[None, :], lambda_q2[None, :], lambda_k2[None, :],
                                 a_norm_w[:, None, :], w_gate, w_up)
    x1 = _outproj(hm, ha, w_out_b, x2)
    return _ffn(x1, norm2_w[None, :], w_gate_b, w_up_b, w_down_b)


def kernel(x, norm1_w, w_in, conv_w, conv_b, i_bias, f_bias, m_norm_w, q_norm_w, k_norm_w,
           lambda_q1, lambda_k1, lambda_q2, lambda_k2, a_norm_w, w_out, norm2_w,
           w_gate, w_up, w_down):
    b, s, d = x.shape
    assert d == D_MODEL and b == 1 and norm1_w.shape[0] == 1
    assert s % R_ML == 0 and s % TQ == 0 and s % TM_IN == 0 and s % TM_FF == 0
    y = _layer(x.reshape(s, d), norm1_w[0], w_in[0], conv_w[0], conv_b[0], i_bias[0], f_bias[0],
               m_norm_w[0], q_norm_w[0], k_norm_w[0], lambda_q1[0], lambda_k1[0],
               lambda_q2[0], lambda_k2[0], a_norm_w[0], w_out[0], norm2_w[0],
               w_gate[0], w_up[0], w_down[0])
    return y.reshape(b, s, d)
```

```python
import functools
import math

import jax
import jax.numpy as jnp
from jax import lax
from jax.experimental import pallas as pl
from jax.experimental.pallas import tpu as pltpu

F32 = jnp.float32
BF16 = jnp.bfloat16

D_MODEL = 2048
CHUNK = 64
SEC = 1024
N_SEC_LO = 3
M_HEADS = 4
M_DV = 256
M_DQK = 128
CONV_W = 4
A_HEADS = 4
A_DV = 256
A_DQK = 128
D_FF = 5632
RMS_EPS = 1e-6
NEG = -1e30
LAM_INIT = 0.8 - 0.6 * math.exp(-0.3 * 0)
LOG2E = 1.4426950408889634
MAX_FIXED_SHIFT = 60.0
GATE_PAD = 128
N_GATES = 2 * M_HEADS
HALO = 8
BF16_SUBLANES = 16

TR_PREP = 256
TM_IN = 1024
TM_OUT = 512
TM_FF = 1024
TF_FF = 512
R_ML = 512
TQ = 512

VMEM_LIMIT = 56 * 1024 * 1024


def _cparams(sem):
    return pltpu.CompilerParams(dimension_semantics=sem, vmem_limit_bytes=VMEM_LIMIT)


def _wprep_kernel(w_ref, lo_ref, hi_ref, g_ref):
    g0 = N_SEC_LO * SEC
    lo_ref[...] = w_ref[:, 0:g0].astype(BF16)
    g_ref[...] = w_ref[:, g0:g0 + GATE_PAD].astype(BF16)
    hi_ref[...] = w_ref[:, g0 + N_GATES:].astype(BF16)


def _wprep(w_in):
    _, d, n_in = w_in.shape
    g0 = N_SEC_LO * SEC
    n_hi = n_in - g0 - N_GATES
    return pl.pallas_call(
        _wprep_kernel,
        grid=(d // TR_PREP,),
        in_specs=[pl.BlockSpec((None, TR_PREP, n_in), lambda i: (0, i, 0))],
        out_specs=[
            pl.BlockSpec((TR_PREP, g0), lambda i: (i, 0)),
            pl.BlockSpec((TR_PREP, n_hi), lambda i: (i, 0)),
            pl.BlockSpec((TR_PREP, GATE_PAD), lambda i: (i, 0)),
        ],
        out_shape=[
            jax.ShapeDtypeStruct((d, g0), BF16),
            jax.ShapeDtypeStruct((d, n_hi), BF16),
            jax.ShapeDtypeStruct((d, GATE_PAD), BF16),
        ],
        compiler_params=_cparams(("arbitrary",)),
        name="wprep",
    )(w_in)


def _inproj_kernel(x_ref, nw_ref, wlo_ref, whi_ref, wg_ref, gb_ref, qw_ref, kw_ref,
                   out_ref, gt_ref, h_scr):
    j = pl.program_id(1)

    @pl.when(j == 0)
    def _():
        xf = x_ref[...]
        ms = jnp.mean(xf * xf, axis=1, keepdims=True)
        hb = (xf * lax.rsqrt(ms + RMS_EPS) * nw_ref[...]).astype(BF16)
        h_scr[...] = hb
        g = jnp.dot(hb, wg_ref[...], preferred_element_type=F32) + gb_ref[...]
        gt_ref[...] = g.T[0:N_GATES, :]

    is_q = j == N_SEC_LO
    is_k = j == N_SEC_LO + 1
    is_qk = jnp.logical_or(is_q, is_k)

    @pl.when(j < N_SEC_LO)
    def _():
        out_ref[...] = jnp.dot(h_scr[...], wlo_ref[...],
                               preferred_element_type=F32).astype(BF16)

    @pl.when(j > N_SEC_LO + 1)
    def _():
        out_ref[...] = jnp.dot(h_scr[...], whi_ref[...],
                               preferred_element_type=F32).astype(BF16)

    @pl.when(is_qk)
    def _():
        acc = jnp.dot(h_scr[...], whi_ref[...], preferred_element_type=F32)
        w = jnp.where(is_q, qw_ref[...] * (A_DQK ** -0.5 * LOG2E), kw_ref[...])
        for g in range(SEC // A_DQK):
            sl = slice(g * A_DQK, (g + 1) * A_DQK)
            a = acc[:, sl]
            ms = jnp.mean(a * a, axis=1, keepdims=True)
            out_ref[:, sl] = (a * lax.rsqrt(ms + RMS_EPS) * w).astype(BF16)


def _inproj(x2, nw, w_lo, w_hi, w_gate, gbias, qw, kw):
    s = x2.shape[0]
    n_hi = w_hi.shape[1] // SEC
    assert w_lo.shape[1] == N_SEC_LO * SEC
    n_sec = N_SEC_LO + n_hi
    return pl.pallas_call(
        _inproj_kernel,
        grid=(s // TM_IN, n_sec),
        in_specs=[
            pl.BlockSpec((TM_IN, D_MODEL), lambda i, j: (i, 0)),
            pl.BlockSpec((1, D_MODEL), lambda i, j: (0, 0)),
            pl.BlockSpec((D_MODEL, SEC), lambda i, j: (0, jnp.minimum(j, N_SEC_LO - 1))),
            pl.BlockSpec((D_MODEL, SEC), lambda i, j: (0, jnp.maximum(j - N_SEC_LO, 0))),
            pl.BlockSpec((D_MODEL, GATE_PAD), lambda i, j: (0, 0)),
            pl.BlockSpec((1, GATE_PAD), lambda i, j: (0, 0)),
            pl.BlockSpec((1, A_DQK), lambda i, j: (0, 0)),
            pl.BlockSpec((1, A_DQK), lambda i, j: (0, 0)),
        ],
        out_specs=[
            pl.BlockSpec((TM_IN, SEC), lambda i, j: (i, j)),
            pl.BlockSpec((N_GATES, TM_IN), lambda i, j: (0, i)),
        ],
        out_shape=[
            jax.ShapeDtypeStruct((s, n_sec * SEC), BF16),
            jax.ShapeDtypeStruct((N_GATES, s), F32),
        ],
        scratch_shapes=[
            pltpu.VMEM((TM_IN, D_MODEL), BF16),
        ],
        compiler_params=_cparams(("arbitrary", "arbitrary")),
        name="inproj",
    )(x2, nw, w_lo, w_hi, w_gate, gbias, qw, kw)


def _log_sigmoid(x):
    return jnp.minimum(x, 0.0) - jnp.log1p(jnp.exp(-jnp.abs(x)))


def _row_to_col(row, eye):
    return jnp.sum(jnp.where(eye, row, 0.0), axis=1, keepdims=True)


def _mlstm_kernel(qk_ref, halo_ref, v_ref, o_ref, g_ref, cw_ref, cb_ref, nw_ref,
                  wa_ref, wb_ref,
                  out_ref, wa_out_ref, wb_out_ref,
                  stage, act, b_scr, li_scr, ct_scr, n_scr, m_scr):
    i = pl.program_id(0)
    n_chunks = R_ML // CHUNK

    wa_out_ref[...] = wa_ref[...].astype(BF16)
    wb_out_ref[...] = wb_ref[...].astype(BF16)

    @pl.when(i == 0)
    def _():
        ct_scr[...] = jnp.zeros_like(ct_scr)
        n_scr[...] = jnp.zeros_like(n_scr)
        m_scr[...] = jnp.full_like(m_scr, NEG)

    halo = halo_ref[...].astype(F32)
    stage[0:HALO, :] = jnp.where(i == 0, 0.0, halo)
    stage[HALO:HALO + R_ML, :] = qk_ref[...].astype(F32)
    k_scale = M_DQK ** -0.5
    for cs in range(SEC // 128):
        sl = slice(cs * 128, (cs + 1) * 128)
        y = cb_ref[:, sl]
        for t in range(CONV_W):
            y = y + cw_ref[t:t + 1, sl] * stage[HALO - (CONV_W - 1) + t:HALO - (CONV_W - 1) + t + R_ML, sl]
        a = y * jax.nn.sigmoid(y)
        if cs >= M_HEADS:
            a = a * k_scale
        act[:, sl] = a.astype(BF16)

    li_scr[...] = g_ref[0:M_HEADS].reshape(M_HEADS * n_chunks, CHUNK)
    lf = _log_sigmoid(g_ref[M_HEADS:N_GATES].reshape(M_HEADS * n_chunks, CHUNK))
    r_i = lax.broadcasted_iota(jnp.int32, (CHUNK, CHUNK), 0)
    c_i = lax.broadcasted_iota(jnp.int32, (CHUNK, CHUNK), 1)
    upper = jnp.where(r_i <= c_i, 1.0, 0.0).astype(BF16)
    p0 = lf.astype(BF16)
    r1 = lf - p0.astype(F32)
    p1 = r1.astype(BF16)
    p2 = (r1 - p1.astype(F32)).astype(BF16)
    b_scr[...] = (jnp.dot(p0, upper, preferred_element_type=F32)
                  + jnp.dot(p1, upper, preferred_element_type=F32)
                  + jnp.dot(p2, upper, preferred_element_type=F32))

    t_i = lax.broadcasted_iota(jnp.int32, (R_ML, CHUNK), 0) % CHUNK
    s_i = lax.broadcasted_iota(jnp.int32, (R_ML, CHUNK), 1)
    eye = t_i == s_i
    tril = t_i >= s_i

    def rows_of(x, c):
        return x[c * CHUNK:(c + 1) * CHUNK]

    def per_chunk_rows(x):
        return jnp.concatenate(
            [jnp.broadcast_to(x[c:c + 1, :], (CHUNK, x.shape[1])) for c in range(n_chunks)], axis=0)

    def to_col(rows):
        return jnp.sum(jnp.where(eye, rows, 0.0), axis=1, keepdims=True)

    for h in range(M_HEADS):
        qs = slice(h * M_DQK, (h + 1) * M_DQK)
        ks = slice((M_HEADS + h) * M_DQK, (M_HEADS + h + 1) * M_DQK)
        vs = slice(h * M_DV, (h + 1) * M_DV)
        q = act[:, qs]
        k = act[:, ks]
        v = v_ref[:, vs]
        b_h = b_scr[h * n_chunks:(h + 1) * n_chunks, :]
        li_h = li_scr[h * n_chunks:(h + 1) * n_chunks, :]

        b_rows = per_chunk_rows(b_h)
        b_col = to_col(b_rows)
        d = jnp.where(tril, b_col - b_rows + per_chunk_rows(li_h), NEG)
        m_intra = jnp.max(d, axis=1, keepdims=True)
        s = jnp.concatenate(
            [lax.dot_general(rows_of(q, c), rows_of(k, c), (((1,), (1,)), ((), ())),
                             preferred_element_type=F32) for c in range(n_chunks)], axis=0)
        p = jnp.exp(d - m_intra) * s
        row_sum = jnp.sum(p, axis=1, keepdims=True)
        pb = p.astype(BF16)
        n_intra = jnp.concatenate(
            [jnp.dot(rows_of(pb, c), rows_of(v, c), preferred_element_type=F32)
             for c in range(n_chunks)], axis=0)

        g_tot = b_h[:, CHUNK - 1:CHUNK]
        a = g_tot - b_h + li_h
        m_loc = jnp.max(a, axis=1, keepdims=True)
        w_col = to_col(per_chunk_rows(jnp.exp(a - m_loc)))
        kw = k.astype(F32) * w_col
        kwb = kw.astype(BF16)
        ct_locs = [lax.dot_general(rows_of(kwb, c), rows_of(v, c), (((0,), (0,)), ((), ())),
                                   preferred_element_type=F32) for c in range(n_chunks)]
        n_locs = [jnp.sum(rows_of(kw, c), axis=0, keepdims=True) for c in range(n_chunks)]

        m = m_scr[h, 0:1, 0:1]
        ct = ct_scr[h]
        n = n_scr[h, 0:1, :]
        m_prevs, ct_prevs, n_prevs = [], [], []
        for c in range(n_chunks):
            m_prevs.append(m)
            ct_prevs.append(ct.astype(BF16))
            n_prevs.append(n)
            g_c = g_tot[c:c + 1, :]
            m_loc_c = m_loc[c:c + 1, :]
            m_new = jnp.maximum(g_c + m, m_loc_c)
            s_old = jnp.exp(g_c + m - m_new)
            s_loc = jnp.exp(m_loc_c - m_new)
            ct = s_old * ct + s_loc * ct_locs[c]
            n = s_old * n + s_loc * n_locs[c]
            m = m_new
        ct_scr[h] = ct
        n_scr[h] = jnp.broadcast_to(n, (8, M_DQK))
        m_scr[h] = jnp.broadcast_to(m, (8, 128))

        m_prev = per_chunk_rows(jnp.concatenate(m_prevs, axis=0))
        n_prev = per_chunk_rows(jnp.concatenate(n_prevs, axis=0))
        inter = jnp.concatenate(
            [jnp.dot(rows_of(q, c), ct_prevs[c], preferred_element_type=F32)
             for c in range(n_chunks)], axis=0)
        inter_log = b_col + m_prev
        m_t = jnp.maximum(inter_log, m_intra)
        s_intra = jnp.exp(m_intra - m_t)
        s_inter = jnp.exp(inter_log - m_t)
        num = s_intra * n_intra + s_inter * inter
        den = (s_intra * row_sum
               + s_inter * jnp.sum(q.astype(F32) * n_prev, axis=1, keepdims=True))
        hout = num / jnp.maximum(jnp.abs(den), jnp.exp(-m_t))

        ms = jnp.mean(hout * hout, axis=1, keepdims=True)
        hn = hout * lax.rsqrt(ms + RMS_EPS) * nw_ref[:, vs]
        out_ref[:, vs] = (hn * jax.nn.sigmoid(o_ref[:, vs].astype(F32))).astype(BF16)


def _row_slab(w, n_steps, step_of):
    rows, cols = w.shape
    assert rows % (n_steps * BF16_SUBLANES) == 0, (w.shape, n_steps)
    return pl.BlockSpec((rows // n_steps, cols), lambda *idx: (step_of(*idx), 0))


def _mlstm(proj, gates3, conv_w, conv_b, m_norm_w, w_a, w_b):
    s = proj.shape[0]
    n_chunks = R_ML // CHUNK
    hb = R_ML // HALO
    n_steps = s // R_ML
    return pl.pallas_call(
        _mlstm_kernel,
        grid=(n_steps,),
        in_specs=[
            pl.BlockSpec((R_ML, SEC), lambda i: (i, 0)),
            pl.BlockSpec((HALO, SEC), lambda i: (jnp.maximum(i * hb - 1, 0), 0)),
            pl.BlockSpec((R_ML, SEC), lambda i: (i, 1)),
            pl.BlockSpec((R_ML, SEC), lambda i: (i, 2)),
            pl.BlockSpec((N_GATES, n_chunks, CHUNK), lambda i: (0, i, 0)),
            pl.BlockSpec((CONV_W, SEC), lambda i: (0, 0)),
            pl.BlockSpec((1, SEC), lambda i: (0, 0)),
            pl.BlockSpec((1, SEC), lambda i: (0, 0)),
            _row_slab(w_a, n_steps, lambda i: i),
            _row_slab(w_b, n_steps, lambda i: i),
        ],
        out_specs=[
            pl.BlockSpec((R_ML, SEC), lambda i: (i, 0)),
            _row_slab(w_a, n_steps, lambda i: i),
            _row_slab(w_b, n_steps, lambda i: i),
        ],
        out_shape=[
            jax.ShapeDtypeStruct((s, SEC), BF16),
            jax.ShapeDtypeStruct(w_a.shape, BF16),
            jax.ShapeDtypeStruct(w_b.shape, BF16),
        ],
        scratch_shapes=[
            pltpu.VMEM((R_ML + HALO, SEC), F32),
            pltpu.VMEM((R_ML, SEC), BF16),
            pltpu.VMEM((M_HEADS * n_chunks, CHUNK), F32),
            pltpu.VMEM((M_HEADS * n_chunks, CHUNK), F32),
            pltpu.VMEM((M_HEADS, M_DQK, M_DV), F32),
            pltpu.VMEM((M_HEADS, 8, M_DQK), F32),
            pltpu.VMEM((M_HEADS, 8, 128), F32),
        ],
        compiler_params=_cparams(("arbitrary",)),
        name="mlstm",
    )(proj, proj, proj, proj, gates3, conv_w, conv_b, m_norm_w, w_a, w_b)


def _lane_partial_sum(p):
    acc = p[:, 0:128]
    for t in range(1, p.shape[1] // 128):
        acc = acc + p[:, t * 128:(t + 1) * 128]
    return acc


def _attn_kernel(par_ref, q_ref, k_ref, v_ref, lq1_ref, lk1_ref, lq2_ref, lk2_ref, nw_ref,
                 wa_ref, wb_ref,
                 out_ref, wa_out_ref, wb_out_ref,
                 m_scr, l_scr, acc_scr, sa_scr, sb_scr):
    wa_out_ref[...] = wa_ref[...].astype(BF16)
    wb_out_ref[...] = wb_ref[...].astype(BF16)

    i = pl.program_id(1)
    shift = par_ref[0]
    fixed = par_ref[1] > 0.5

    l_scr[...] = jnp.zeros_like(l_scr)
    acc_scr[...] = jnp.zeros_like(acc_scr)

    def chunk_mask():
        qc = lax.broadcasted_iota(jnp.int32, (TQ, TQ), 0) // CHUNK
        kc = lax.broadcasted_iota(jnp.int32, (TQ, TQ), 1) // CHUNK
        return kc <= qc

    def scores(j, c):
        r0 = pl.multiple_of(j * TQ, TQ)
        q = q_ref[:, c * A_DQK:(c + 1) * A_DQK]
        k = k_ref[pl.ds(r0, TQ), c * A_DQK:(c + 1) * A_DQK]
        return lax.dot_general(q, k, (((1,), (1,)), ((), ())), preferred_element_type=F32)

    def fixed_scores(j, s_buf):
        for c in range(2):
            s_buf[c] = scores(j, c)

    def fixed_pv(j, s_buf, masked):
        v = v_ref[pl.ds(pl.multiple_of(j * TQ, TQ), TQ), :]
        for c in range(2):
            p = jnp.exp2(s_buf[c] - shift)
            if masked:
                p = jnp.where(chunk_mask(), p, 0.0)
            l_scr[c] += _lane_partial_sum(p)
            acc_scr[c] += jnp.dot(p.astype(BF16), v, preferred_element_type=F32)

    def online_tile(j, masked):
        v = v_ref[pl.ds(pl.multiple_of(j * TQ, TQ), TQ), :]
        for c in range(2):
            s = scores(j, c)
            if masked:
                s = jnp.where(chunk_mask(), s, NEG)
            m_prev = m_scr[c]
            m_new = jnp.maximum(m_prev, jnp.max(s, axis=1, keepdims=True))
            alpha = jnp.exp2(m_prev - m_new)
            p = jnp.exp2(s - m_new[:, 0:1])
            l_scr[c] = alpha * l_scr[c] + _lane_partial_sum(p)
            acc_scr[c] = alpha[:, 0:1] * acc_scr[c] + jnp.dot(p.astype(BF16), v,
                                                              preferred_element_type=F32)
            m_scr[c] = m_new

    @pl.when(fixed)
    def _():
        fixed_scores(0, sa_scr)

        def pair(t, carry):
            j = 2 * t
            fixed_pv(j, sa_scr, False)
            fixed_scores(j + 1, sb_scr)
            fixed_pv(j + 1, sb_scr, False)
            fixed_scores(j + 2, sa_scr)
            return carry
        lax.fori_loop(0, i // 2, pair, 0)

        @pl.when(i % 2 == 1)
        def _():
            fixed_pv(i - 1, sa_scr, False)
            fixed_scores(i, sb_scr)
            fixed_pv(i, sb_scr, True)

        @pl.when(i % 2 == 0)
        def _():
            fixed_pv(i, sa_scr, True)

    @pl.when(jnp.logical_not(fixed))
    def _():
        m_scr[...] = jnp.full_like(m_scr, NEG)

        def body(j, carry):
            online_tile(j, False)
            return carry
        lax.fori_loop(0, i, body, 0)
        online_tile(i, True)

    lam = (jnp.exp(jnp.sum(lq1_ref[...] * lk1_ref[...], axis=1, keepdims=True))
           - jnp.exp(jnp.sum(lq2_ref[...] * lk2_ref[...], axis=1, keepdims=True)) + LAM_INIT)
    l0 = jnp.sum(l_scr[0], axis=1, keepdims=True)
    l1 = jnp.sum(l_scr[1], axis=1, keepdims=True)
    o = acc_scr[0] / l0 - lam * (acc_scr[1] / l1)
    ms = jnp.mean(o * o, axis=1, keepdims=True)
    out_ref[...] = (o * lax.rsqrt(ms + RMS_EPS) * nw_ref[0] * (1.0 - LAM_INIT)).astype(BF16)


def _attn(proj, q_norm_w, k_norm_w, lq1, lk1, lq2, lk2, a_norm_w3, w_a, w_b):
    s = proj.shape[0]
    qb = 3 * SEC // A_DV
    kb = 4 * SEC // A_DV
    vb = 5 * SEC // A_DV
    bound = (A_DQK ** 0.5 * LOG2E) * jnp.max(jnp.abs(q_norm_w * k_norm_w))
    shift = jnp.ceil(bound * 1.02) + 1.0
    params = jnp.stack([shift, (shift <= MAX_FIXED_SHIFT).astype(F32)]).astype(F32)
    vec = pl.BlockSpec((1, A_DQK), lambda h, i: (0, 0))
    nq = s // TQ
    n_steps = A_HEADS * nq
    return pl.pallas_call(
        _attn_kernel,
        grid=(A_HEADS, nq),
        in_specs=[
            pl.BlockSpec(memory_space=pltpu.SMEM),
            pl.BlockSpec((TQ, 2 * A_DQK), lambda h, i: (i, qb + h)),
            pl.BlockSpec((s, 2 * A_DQK), lambda h, i: (0, kb + h)),
            pl.BlockSpec((s, A_DV), lambda h, i: (0, vb + h)),
            vec, vec, vec, vec,
            pl.BlockSpec((1, 1, A_DV), lambda h, i: (h, 0, 0)),
            _row_slab(w_a, n_steps, lambda h, i: h * nq + i),
            _row_slab(w_b, n_steps, lambda h, i: h * nq + i),
        ],
        out_specs=[
            pl.BlockSpec((TQ, A_DV), lambda h, i: (i, h)),
            _row_slab(w_a, n_steps, lambda h, i: h * nq + i),
            _row_slab(w_b, n_steps, lambda h, i: h * nq + i),
        ],
        out_shape=[
            jax.ShapeDtypeStruct((s, A_HEADS * A_DV), BF16),
            jax.ShapeDtypeStruct(w_a.shape, BF16),
            jax.ShapeDtypeStruct(w_b.shape, BF16),
        ],
        scratch_shapes=[
            pltpu.VMEM((2, TQ, 128), F32),
            pltpu.VMEM((2, TQ, 128), F32),
            pltpu.VMEM((2, TQ, A_DV), F32),
            pltpu.VMEM((2, TQ, TQ), F32),
            pltpu.VMEM((2, TQ, TQ), F32),
        ],
        compiler_params=_cparams(("arbitrary", "arbitrary")),
        name="diffattn",
    )(params, proj, proj, proj, lq1, lk1, lq2, lk2, a_norm_w3, w_a, w_b)


def _outproj_kernel(hm_ref, ha_ref, wt_ref, wb_ref, x_ref, out_ref):
    out_ref[...] = (x_ref[...]
                    + jnp.dot(hm_ref[...], wt_ref[...], preferred_element_type=F32)
                    + jnp.dot(ha_ref[...], wb_ref[...], preferred_element_type=F32))


def _outproj(hm, ha, w_out, x2):
    s = x2.shape[0]
    return pl.pallas_call(
        _outproj_kernel,
        grid=(s // TM_OUT,),
        in_specs=[
            pl.BlockSpec((TM_OUT, SEC), lambda i: (i, 0)),
            pl.BlockSpec((TM_OUT, SEC), lambda i: (i, 0)),
            pl.BlockSpec((SEC, D_MODEL), lambda i: (0, 0)),
            pl.BlockSpec((SEC, D_MODEL), lambda i: (1, 0)),
            pl.BlockSpec((TM_OUT, D_MODEL), lambda i: (i, 0)),
        ],
        out_specs=pl.BlockSpec((TM_OUT, D_MODEL), lambda i: (i, 0)),
        out_shape=jax.ShapeDtypeStruct((s, D_MODEL), F32),
        compiler_params=_cparams(("arbitrary",)),
        name="outproj",
    )(hm, ha, w_out, w_out, x2)


def _ffn_kernel(x_ref, nw_ref, wg_ref, wu_ref, wd_ref, out_ref, h_scr):
    j = pl.program_id(1)

    @pl.when(j == 0)
    def _():
        xf = x_ref[...]
        ms = jnp.mean(xf * xf, axis=1, keepdims=True)
        h_scr[...] = (xf * lax.rsqrt(ms + RMS_EPS) * nw_ref[...]).astype(BF16)
        out_ref[...] = xf

    h = h_scr[...]
    g = jnp.dot(h, wg_ref[...], preferred_element_type=F32)
    u = jnp.dot(h, wu_ref[...], preferred_element_type=F32)
    a = (g * jax.nn.sigmoid(g) * u).astype(BF16)
    out_ref[...] += jnp.dot(a, wd_ref[...], preferred_element_type=F32)


def _ffn(x1, nw, w_gate, w_up, w_down):
    s = x1.shape[0]
    return pl.pallas_call(
        _ffn_kernel,
        grid=(s // TM_FF, D_FF // TF_FF),
        in_specs=[
            pl.BlockSpec((TM_FF, D_MODEL), lambda i, j: (i, 0)),
            pl.BlockSpec((1, D_MODEL), lambda i, j: (0, 0)),
            pl.BlockSpec((D_MODEL, TF_FF), lambda i, j: (0, j)),
            pl.BlockSpec((D_MODEL, TF_FF), lambda i, j: (0, j)),
            pl.BlockSpec((TF_FF, D_MODEL), lambda i, j: (j, 0)),
        ],
        out_specs=pl.BlockSpec((TM_FF, D_MODEL), lambda i, j: (i, 0)),
        out_shape=jax.ShapeDtypeStruct((s, D_MODEL), F32),
        scratch_shapes=[pltpu.VMEM((TM_FF, D_MODEL), BF16)],
        compiler_params=_cparams(("arbitrary", "arbitrary")),
        name="swiglu",
    )(x1, nw, w_gate, w_up, w_down)


def _layer(x2, norm1_w, w_in, conv_w, conv_b, i_bias, f_bias, m_norm_w, q_norm_w, k_norm_w,
           lambda_q1, lambda_k1, lambda_q2, lambda_k2, a_norm_w, w_out, norm2_w,
           w_gate, w_up, w_down):
    s = x2.shape[0]
    w_lo, w_hi, w_g = _wprep(w_in)
    gbias = jnp.pad(jnp.concatenate([i_bias, f_bias]), (0, GATE_PAD - N_GATES))[None, :]

    proj, gates_t = _inproj(x2, norm1_w[None, :], w_lo, w_hi, w_g, gbias,
                            q_norm_w[None, :], k_norm_w[None, :])
    gates3 = gates_t.reshape(N_GATES, s // CHUNK, CHUNK)
    hm, w_out_b, w_down_b = _mlstm(proj, gates3, conv_w, conv_b[None, :],
                                   m_norm_w.reshape(1, SEC), w_out, w_down)
    ha, w_gate_b, w_up_b = _attn(proj, q_norm_w, k_norm_w, lambda_q1[None, :],
                                 lambda_k1[None, :], lambda_q2[None, :], lambda_k2[None, :],
                                 a_norm_w[:, None, :], w_gate, w_up)
    x1 = _outproj(hm, ha, w_out_b, x2)
    return _ffn(x1, norm2_w[None, :], w_gate_b, w_up_b, w_down_b)


def kernel(x, norm1_w, w_in, conv_w, conv_b, i_bias, f_bias, m_norm_w, q_norm_w, k_norm_w,
           lambda_q1, lambda_k1, lambda_q2, lambda_k2, a_norm_w, w_out, norm2_w,
           w_gate, w_up, w_down):
    b, s, d = x.shape
    assert d == D_MODEL and b == 1 and norm1_w.shape[0] == 1
    assert s % R_ML == 0 and s % TQ == 0 and s % TM_IN == 0 and s % TM_FF == 0
    y = _layer(x.reshape(s, d), norm1_w[0], w_in, conv_w[0], conv_b[0], i_bias[0], f_bias[0],
               m_norm_w[0], q_norm_w[0], k_norm_w[0], lambda_q1[0], lambda_k1[0],
               lambda_q2[0], lambda_k2[0], a_norm_w[0], w_out[0], norm2_w[0],
               w_gate[0], w_up[0], w_down[0])
    return y.reshape(b, s, d)
```

```python
import functools
import math

import jax
import jax.numpy as jnp
from jax import lax
from jax.experimental import pallas as pl
from jax.experimental.pallas import tpu as pltpu

F32 = jnp.float32
BF16 = jnp.bfloat16

D_MODEL = 2048
CHUNK = 64
SEC = 1024
N_SEC_LO = 3
M_HEADS = 4
M_DV = 256
M_DQK = 128
CONV_W = 4
A_HEADS = 4
A_DV = 256
A_DQK = 128
D_FF = 5632
RMS_EPS = 1e-6
NEG = -1e30
LAM_INIT = 0.8 - 0.6 * math.exp(-0.3 * 0)
LOG2E = 1.4426950408889634
MAX_FIXED_SHIFT = 60.0
GATE_PAD = 128
N_GATES = 2 * M_HEADS
HALO = 8
BF16_SUBLANES = 16

TM_IN = 1024
TM_OUT = 512
TM_FF = 1024
TF_FF = 512
R_ML = 512
TQ = 512

VMEM_LIMIT = 56 * 1024 * 1024


def _cparams(sem):
    return pltpu.CompilerParams(dimension_semantics=sem, vmem_limit_bytes=VMEM_LIMIT)


def _inproj_kernel(x_ref, nw_ref, wlo_ref, whi_ref, wg_ref, gb_ref, qw_ref, kw_ref,
                   out_ref, gt_ref, h_scr):
    j = pl.program_id(1)

    @pl.when(j == 0)
    def _():
        xf = x_ref[...]
        ms = jnp.mean(xf * xf, axis=1, keepdims=True)
        hb = (xf * lax.rsqrt(ms + RMS_EPS) * nw_ref[...]).astype(BF16)
        h_scr[...] = hb
        g = jnp.dot(hb, wg_ref[...], preferred_element_type=F32) + gb_ref[...]
        gt_ref[...] = g.T[0:N_GATES, :]

    is_q = j == N_SEC_LO
    is_k = j == N_SEC_LO + 1
    is_qk = jnp.logical_or(is_q, is_k)

    @pl.when(j < N_SEC_LO)
    def _():
        out_ref[...] = jnp.dot(h_scr[...], wlo_ref[...],
                               preferred_element_type=F32).astype(BF16)

    @pl.when(j > N_SEC_LO + 1)
    def _():
        out_ref[...] = jnp.dot(h_scr[...], whi_ref[...],
                               preferred_element_type=F32).astype(BF16)

    @pl.when(is_qk)
    def _():
        acc = jnp.dot(h_scr[...], whi_ref[...], preferred_element_type=F32)
        w = jnp.where(is_q, qw_ref[...] * (A_DQK ** -0.5 * LOG2E), kw_ref[...])
        for g in range(SEC // A_DQK):
            sl = slice(g * A_DQK, (g + 1) * A_DQK)
            a = acc[:, sl]
            ms = jnp.mean(a * a, axis=1, keepdims=True)
            out_ref[:, sl] = (a * lax.rsqrt(ms + RMS_EPS) * w).astype(BF16)


def _inproj(x2, nw, w_all, w_hi, gbias, qw, kw):
    s = x2.shape[0]
    n_hi = w_hi.shape[1] // SEC
    n_sec = N_SEC_LO + n_hi
    gate_blk = N_SEC_LO * SEC // GATE_PAD
    return pl.pallas_call(
        _inproj_kernel,
        grid=(s // TM_IN, n_sec),
        in_specs=[
            pl.BlockSpec((TM_IN, D_MODEL), lambda i, j: (i, 0)),
            pl.BlockSpec((1, D_MODEL), lambda i, j: (0, 0)),
            pl.BlockSpec((D_MODEL, SEC), lambda i, j: (0, jnp.minimum(j, N_SEC_LO - 1))),
            pl.BlockSpec((D_MODEL, SEC), lambda i, j: (0, jnp.maximum(j - N_SEC_LO, 0))),
            pl.BlockSpec((D_MODEL, GATE_PAD), lambda i, j: (0, gate_blk)),
            pl.BlockSpec((1, GATE_PAD), lambda i, j: (0, 0)),
            pl.BlockSpec((1, A_DQK), lambda i, j: (0, 0)),
            pl.BlockSpec((1, A_DQK), lambda i, j: (0, 0)),
        ],
        out_specs=[
            pl.BlockSpec((TM_IN, SEC), lambda i, j: (i, j)),
            pl.BlockSpec((N_GATES, TM_IN), lambda i, j: (0, i)),
        ],
        out_shape=[
            jax.ShapeDtypeStruct((s, n_sec * SEC), BF16),
            jax.ShapeDtypeStruct((N_GATES, s), F32),
        ],
        scratch_shapes=[
            pltpu.VMEM((TM_IN, D_MODEL), BF16),
        ],
        compiler_params=_cparams(("arbitrary", "arbitrary")),
        name="inproj",
    )(x2, nw, w_all, w_hi, w_all, gbias, qw, kw)


def _log_sigmoid(x):
    return jnp.minimum(x, 0.0) - jnp.log1p(jnp.exp(-jnp.abs(x)))


def _row_to_col(row, eye):
    return jnp.sum(jnp.where(eye, row, 0.0), axis=1, keepdims=True)


def _mlstm_kernel(qk_ref, halo_ref, v_ref, o_ref, g_ref, cw_ref, cb_ref, nw_ref,
                  wa_ref, wb_ref,
                  out_ref, wa_out_ref, wb_out_ref,
                  stage, act, b_scr, li_scr, ct_scr, n_scr, m_scr):
    i = pl.program_id(0)
    n_chunks = R_ML // CHUNK

    wa_out_ref[...] = wa_ref[...].astype(BF16)
    wb_out_ref[...] = wb_ref[...].astype(BF16)

    @pl.when(i == 0)
    def _():
        ct_scr[...] = jnp.zeros_like(ct_scr)
        n_scr[...] = jnp.zeros_like(n_scr)
        m_scr[...] = jnp.full_like(m_scr, NEG)

    halo = halo_ref[...].astype(F32)
    stage[0:HALO, :] = jnp.where(i == 0, 0.0, halo)
    stage[HALO:HALO + R_ML, :] = qk_ref[...].astype(F32)
    k_scale = M_DQK ** -0.5
    for cs in range(SEC // 128):
        sl = slice(cs * 128, (cs + 1) * 128)
        y = cb_ref[:, sl]
        for t in range(CONV_W):
            y = y + cw_ref[t:t + 1, sl] * stage[HALO - (CONV_W - 1) + t:HALO - (CONV_W - 1) + t + R_ML, sl]
        a = y * jax.nn.sigmoid(y)
        if cs >= M_HEADS:
            a = a * k_scale
        act[:, sl] = a.astype(BF16)

    li_scr[...] = g_ref[0:M_HEADS].reshape(M_HEADS * n_chunks, CHUNK)
    lf = _log_sigmoid(g_ref[M_HEADS:N_GATES].reshape(M_HEADS * n_chunks, CHUNK))
    r_i = lax.broadcasted_iota(jnp.int32, (CHUNK, CHUNK), 0)
    c_i = lax.broadcasted_iota(jnp.int32, (CHUNK, CHUNK), 1)
    upper = jnp.where(r_i <= c_i, 1.0, 0.0).astype(BF16)
    p0 = lf.astype(BF16)
    r1 = lf - p0.astype(F32)
    p1 = r1.astype(BF16)
    p2 = (r1 - p1.astype(F32)).astype(BF16)
    b_scr[...] = (jnp.dot(p0, upper, preferred_element_type=F32)
                  + jnp.dot(p1, upper, preferred_element_type=F32)
                  + jnp.dot(p2, upper, preferred_element_type=F32))

    t_i = lax.broadcasted_iota(jnp.int32, (R_ML, CHUNK), 0) % CHUNK
    s_i = lax.broadcasted_iota(jnp.int32, (R_ML, CHUNK), 1)
    eye = t_i == s_i
    tril = t_i >= s_i

    def rows_of(x, c):
        return x[c * CHUNK:(c + 1) * CHUNK]

    def per_chunk_rows(x):
        return jnp.concatenate(
            [jnp.broadcast_to(x[c:c + 1, :], (CHUNK, x.shape[1])) for c in range(n_chunks)], axis=0)

    def to_col(rows):
        return jnp.sum(jnp.where(eye, rows, 0.0), axis=1, keepdims=True)

    for h in range(M_HEADS):
        qs = slice(h * M_DQK, (h + 1) * M_DQK)
        ks = slice((M_HEADS + h) * M_DQK, (M_HEADS + h + 1) * M_DQK)
        vs = slice(h * M_DV, (h + 1) * M_DV)
        q = act[:, qs]
        k = act[:, ks]
        v = v_ref[:, vs]
        b_h = b_scr[h * n_chunks:(h + 1) * n_chunks, :]
        li_h = li_scr[h * n_chunks:(h + 1) * n_chunks, :]

        b_rows = per_chunk_rows(b_h)
        b_col = to_col(b_rows)
        d = jnp.where(tril, b_col - b_rows + per_chunk_rows(li_h), NEG)
        m_intra = jnp.max(d, axis=1, keepdims=True)
        s = jnp.concatenate(
            [lax.dot_general(rows_of(q, c), rows_of(k, c), (((1,), (1,)), ((), ())),
                             preferred_element_type=F32) for c in range(n_chunks)], axis=0)
        p = jnp.exp(d - m_intra) * s
        row_sum = jnp.sum(p, axis=1, keepdims=True)
        pb = p.astype(BF16)
        n_intra = jnp.concatenate(
            [jnp.dot(rows_of(pb, c), rows_of(v, c), preferred_element_type=F32)
             for c in range(n_chunks)], axis=0)

        g_tot = b_h[:, CHUNK - 1:CHUNK]
        a = g_tot - b_h + li_h
        m_loc = jnp.max(a, axis=1, keepdims=True)
        w_col = to_col(per_chunk_rows(jnp.exp(a - m_loc)))
        kw = k.astype(F32) * w_col
        kwb = kw.astype(BF16)
        ct_locs = [lax.dot_general(rows_of(kwb, c), rows_of(v, c), (((0,), (0,)), ((), ())),
                                   preferred_element_type=F32) for c in range(n_chunks)]
        n_locs = [jnp.sum(rows_of(kw, c), axis=0, keepdims=True) for c in range(n_chunks)]

        m = m_scr[h, 0:1, 0:1]
        ct = ct_scr[h]
        n = n_scr[h, 0:1, :]
        m_prevs, ct_prevs, n_prevs = [], [], []
        for c in range(n_chunks):
            m_prevs.append(m)
            ct_prevs.append(ct.astype(BF16))
            n_prevs.append(n)
            g_c = g_tot[c:c + 1, :]
            m_loc_c = m_loc[c:c + 1, :]
            m_new = jnp.maximum(g_c + m, m_loc_c)
            s_old = jnp.exp(g_c + m - m_new)
            s_loc = jnp.exp(m_loc_c - m_new)
            ct = s_old * ct + s_loc * ct_locs[c]
            n = s_old * n + s_loc * n_locs[c]
            m = m_new
        ct_scr[h] = ct
        n_scr[h] = jnp.broadcast_to(n, (8, M_DQK))
        m_scr[h] = jnp.broadcast_to(m, (8, 128))

        m_prev = per_chunk_rows(jnp.concatenate(m_prevs, axis=0))
        n_prev = per_chunk_rows(jnp.concatenate(n_prevs, axis=0))
        inter = jnp.concatenate(
            [jnp.dot(rows_of(q, c), ct_prevs[c], preferred_element_type=F32)
             for c in range(n_chunks)], axis=0)
        inter_log = b_col + m_prev
        m_t = jnp.maximum(inter_log, m_intra)
        s_intra = jnp.exp(m_intra - m_t)
        s_inter = jnp.exp(inter_log - m_t)
        num = s_intra * n_intra + s_inter * inter
        den = (s_intra * row_sum
               + s_inter * jnp.sum(q.astype(F32) * n_prev, axis=1, keepdims=True))
        hout = num / jnp.maximum(jnp.abs(den), jnp.exp(-m_t))

        ms = jnp.mean(hout * hout, axis=1, keepdims=True)
        hn = hout * lax.rsqrt(ms + RMS_EPS) * nw_ref[:, vs]
        out_ref[:, vs] = (hn * jax.nn.sigmoid(o_ref[:, vs].astype(F32))).astype(BF16)


def _row_slab(w, n_steps, step_of):
    rows, cols = w.shape
    assert rows % (n_steps * BF16_SUBLANES) == 0, (w.shape, n_steps)
    return pl.BlockSpec((rows // n_steps, cols), lambda *idx: (step_of(*idx), 0))


def _mlstm(proj, gates3, conv_w, conv_b, m_norm_w, w_a, w_b):
    s = proj.shape[0]
    n_chunks = R_ML // CHUNK
    hb = R_ML // HALO
    n_steps = s // R_ML
    return pl.pallas_call(
        _mlstm_kernel,
        grid=(n_steps,),
        in_specs=[
            pl.BlockSpec((R_ML, SEC), lambda i: (i, 0)),
            pl.BlockSpec((HALO, SEC), lambda i: (jnp.maximum(i * hb - 1, 0), 0)),
            pl.BlockSpec((R_ML, SEC), lambda i: (i, 1)),
            pl.BlockSpec((R_ML, SEC), lambda i: (i, 2)),
            pl.BlockSpec((N_GATES, n_chunks, CHUNK), lambda i: (0, i, 0)),
            pl.BlockSpec((CONV_W, SEC), lambda i: (0, 0)),
            pl.BlockSpec((1, SEC), lambda i: (0, 0)),
            pl.BlockSpec((1, SEC), lambda i: (0, 0)),
            _row_slab(w_a, n_steps, lambda i: i),
            _row_slab(w_b, n_steps, lambda i: i),
        ],
        out_specs=[
            pl.BlockSpec((R_ML, SEC), lambda i: (i, 0)),
            _row_slab(w_a, n_steps, lambda i: i),
            _row_slab(w_b, n_steps, lambda i: i),
        ],
        out_shape=[
            jax.ShapeDtypeStruct((s, SEC), BF16),
            jax.ShapeDtypeStruct(w_a.shape, BF16),
            jax.ShapeDtypeStruct(w_b.shape, BF16),
        ],
        scratch_shapes=[
            pltpu.VMEM((R_ML + HALO, SEC), F32),
            pltpu.VMEM((R_ML, SEC), BF16),
            pltpu.VMEM((M_HEADS * n_chunks, CHUNK), F32),
            pltpu.VMEM((M_HEADS * n_chunks, CHUNK), F32),
            pltpu.VMEM((M_HEADS, M_DQK, M_DV), F32),
            pltpu.VMEM((M_HEADS, 8, M_DQK), F32),
            pltpu.VMEM((M_HEADS, 8, 128), F32),
        ],
        compiler_params=_cparams(("arbitrary",)),
        name="mlstm",
    )(proj, proj, proj, proj, gates3, conv_w, conv_b, m_norm_w, w_a, w_b)


def _lane_partial_sum(p):
    acc = p[:, 0:128]
    for t in range(1, p.shape[1] // 128):
        acc = acc + p[:, t * 128:(t + 1) * 128]
    return acc


def _attn_kernel(par_ref, q_ref, k_ref, v_ref, lq1_ref, lk1_ref, lq2_ref, lk2_ref, nw_ref,
                 wa_ref, wb_ref,
                 out_ref, wa_out_ref, wb_out_ref,
                 m_scr, l_scr, acc_scr, sa_scr, sb_scr):
    wa_out_ref[...] = wa_ref[...].astype(BF16)
    wb_out_ref[...] = wb_ref[...].astype(BF16)

    i = pl.program_id(1)
    shift = par_ref[0]
    fixed = par_ref[1] > 0.5

    l_scr[...] = jnp.zeros_like(l_scr)
    acc_scr[...] = jnp.zeros_like(acc_scr)

    def chunk_mask():
        qc = lax.broadcasted_iota(jnp.int32, (TQ, TQ), 0) // CHUNK
        kc = lax.broadcasted_iota(jnp.int32, (TQ, TQ), 1) // CHUNK
        return kc <= qc

    def scores(j, c):
        r0 = pl.multiple_of(j * TQ, TQ)
        q = q_ref[:, c * A_DQK:(c + 1) * A_DQK]
        k = k_ref[pl.ds(r0, TQ), c * A_DQK:(c + 1) * A_DQK]
        return lax.dot_general(q, k, (((1,), (1,)), ((), ())), preferred_element_type=F32)

    def fixed_scores(j, s_buf):
        for c in range(2):
            s_buf[c] = scores(j, c)

    def fixed_pv(j, s_buf, masked):
        v = v_ref[pl.ds(pl.multiple_of(j * TQ, TQ), TQ), :]
        for c in range(2):
            p = jnp.exp2(s_buf[c] - shift)
            if masked:
                p = jnp.where(chunk_mask(), p, 0.0)
            l_scr[c] += _lane_partial_sum(p)
            acc_scr[c] += jnp.dot(p.astype(BF16), v, preferred_element_type=F32)

    def online_tile(j, masked):
        v = v_ref[pl.ds(pl.multiple_of(j * TQ, TQ), TQ), :]
        for c in range(2):
            s = scores(j, c)
            if masked:
                s = jnp.where(chunk_mask(), s, NEG)
            m_prev = m_scr[c]
            m_new = jnp.maximum(m_prev, jnp.max(s, axis=1, keepdims=True))
            alpha = jnp.exp2(m_prev - m_new)
            p = jnp.exp2(s - m_new[:, 0:1])
            l_scr[c] = alpha * l_scr[c] + _lane_partial_sum(p)
            acc_scr[c] = alpha[:, 0:1] * acc_scr[c] + jnp.dot(p.astype(BF16), v,
                                                              preferred_element_type=F32)
            m_scr[c] = m_new

    @pl.when(fixed)
    def _():
        fixed_scores(0, sa_scr)

        def pair(t, carry):
            j = 2 * t
            fixed_pv(j, sa_scr, False)
            fixed_scores(j + 1, sb_scr)
            fixed_pv(j + 1, sb_scr, False)
            fixed_scores(j + 2, sa_scr)
            return carry
        lax.fori_loop(0, i // 2, pair, 0)

        @pl.when(i % 2 == 1)
        def _():
            fixed_pv(i - 1, sa_scr, False)
            fixed_scores(i, sb_scr)
            fixed_pv(i, sb_scr, True)

        @pl.when(i % 2 == 0)
        def _():
            fixed_pv(i, sa_scr, True)

    @pl.when(jnp.logical_not(fixed))
    def _():
        m_scr[...] = jnp.full_like(m_scr, NEG)

        def body(j, carry):
            online_tile(j, False)
            return carry
        lax.fori_loop(0, i, body, 0)
        online_tile(i, True)

    lam = (jnp.exp(jnp.sum(lq1_ref[...] * lk1_ref[...], axis=1, keepdims=True))
           - jnp.exp(jnp.sum(lq2_ref[...] * lk2_ref[...], axis=1, keepdims=True)) + LAM_INIT)
    l0 = jnp.sum(l_scr[0], axis=1, keepdims=True)
    l1 = jnp.sum(l_scr[1], axis=1, keepdims=True)
    o = acc_scr[0] / l0 - lam * (acc_scr[1] / l1)
    ms = jnp.mean(o * o, axis=1, keepdims=True)
    out_ref[...] = (o * lax.rsqrt(ms + RMS_EPS) * nw_ref[0] * (1.0 - LAM_INIT)).astype(BF16)


def _attn(proj, q_norm_w, k_norm_w, lq1, lk1, lq2, lk2, a_norm_w3, w_a, w_b):
    s = proj.shape[0]
    qb = 3 * SEC // A_DV
    kb = 4 * SEC // A_DV
    vb = 5 * SEC // A_DV
    bound = (A_DQK ** 0.5 * LOG2E) * jnp.max(jnp.abs(q_norm_w * k_norm_w))
    shift = jnp.ceil(bound * 1.02) + 1.0
    params = jnp.stack([shift, (shift <= MAX_FIXED_SHIFT).astype(F32)]).astype(F32)
    vec = pl.BlockSpec((1, A_DQK), lambda h, i: (0, 0))
    nq = s // TQ
    n_steps = A_HEADS * nq
    return pl.pallas_call(
        _attn_kernel,
        grid=(A_HEADS, nq),
        in_specs=[
            pl.BlockSpec(memory_space=pltpu.SMEM),
            pl.BlockSpec((TQ, 2 * A_DQK), lambda h, i: (i, qb + h)),
            pl.BlockSpec((s, 2 * A_DQK), lambda h, i: (0, kb + h)),
            pl.BlockSpec((s, A_DV), lambda h, i: (0, vb + h)),
            vec, vec, vec, vec,
            pl.BlockSpec((1, 1, A_DV), lambda h, i: (h, 0, 0)),
            _row_slab(w_a, n_steps, lambda h, i: h * nq + i),
            _row_slab(w_b, n_steps, lambda h, i: h * nq + i),
        ],
        out_specs=[
            pl.BlockSpec((TQ, A_DV), lambda h, i: (i, h)),
            _row_slab(w_a, n_steps, lambda h, i: h * nq + i),
            _row_slab(w_b, n_steps, lambda h, i: h * nq + i),
        ],
        out_shape=[
            jax.ShapeDtypeStruct((s, A_HEADS * A_DV), BF16),
            jax.ShapeDtypeStruct(w_a.shape, BF16),
            jax.ShapeDtypeStruct(w_b.shape, BF16),
        ],
        scratch_shapes=[
            pltpu.VMEM((2, TQ, 128), F32),
            pltpu.VMEM((2, TQ, 128), F32),
            pltpu.VMEM((2, TQ, A_DV), F32),
            pltpu.VMEM((2, TQ, TQ), F32),
            pltpu.VMEM((2, TQ, TQ), F32),
        ],
        compiler_params=_cparams(("arbitrary", "arbitrary")),
        name="diffattn",
    )(params, proj, proj, proj, lq1, lk1, lq2, lk2, a_norm_w3, w_a, w_b)


def _outproj_kernel(hm_ref, ha_ref, wt_ref, wb_ref, x_ref, out_ref):
    out_ref[...] = (x_ref[...]
                    + jnp.dot(hm_ref[...], wt_ref[...], preferred_element_type=F32)
                    + jnp.dot(ha_ref[...], wb_ref[...], preferred_element_type=F32))


def _outproj(hm, ha, w_out, x2):
    s = x2.shape[0]
    return pl.pallas_call(
        _outproj_kernel,
        grid=(s // TM_OUT,),
        in_specs=[
            pl.BlockSpec((TM_OUT, SEC), lambda i: (i, 0)),
            pl.BlockSpec((TM_OUT, SEC), lambda i: (i, 0)),
            pl.BlockSpec((SEC, D_MODEL), lambda i: (0, 0)),
            pl.BlockSpec((SEC, D_MODEL), lambda i: (1, 0)),
            pl.BlockSpec((TM_OUT, D_MODEL), lambda i: (i, 0)),
        ],
        out_specs=pl.BlockSpec((TM_OUT, D_MODEL), lambda i: (i, 0)),
        out_shape=jax.ShapeDtypeStruct((s, D_MODEL), F32),
        compiler_params=_cparams(("arbitrary",)),
        name="outproj",
    )(hm, ha, w_out, w_out, x2)


def _ffn_kernel(x_ref, nw_ref, wg_ref, wu_ref, wd_ref, out_ref, h_scr):
    j = pl.program_id(1)

    @pl.when(j == 0)
    def _():
        xf = x_ref[...]
        ms = jnp.mean(xf * xf, axis=1, keepdims=True)
        h_scr[...] = (xf * lax.rsqrt(ms + RMS_EPS) * nw_ref[...]).astype(BF16)
        out_ref[...] = xf

    h = h_scr[...]
    g = jnp.dot(h, wg_ref[...], preferred_element_type=F32)
    u = jnp.dot(h, wu_ref[...], preferred_element_type=F32)
    a = (g * jax.nn.sigmoid(g) * u).astype(BF16)
    out_ref[...] += jnp.dot(a, wd_ref[...], preferred_element_type=F32)


def _ffn(x1, nw, w_gate, w_up, w_down):
    s = x1.shape[0]
    return pl.pallas_call(
        _ffn_kernel,
        grid=(s // TM_FF, D_FF // TF_FF),
        in_specs=[
            pl.BlockSpec((TM_FF, D_MODEL), lambda i, j: (i, 0)),
            pl.BlockSpec((1, D_MODEL), lambda i, j: (0, 0)),
            pl.BlockSpec((D_MODEL, TF_FF), lambda i, j: (0, j)),
            pl.BlockSpec((D_MODEL, TF_FF), lambda i, j: (0, j)),
            pl.BlockSpec((TF_FF, D_MODEL), lambda i, j: (j, 0)),
        ],
        out_specs=pl.BlockSpec((TM_FF, D_MODEL), lambda i, j: (i, 0)),
        out_shape=jax.ShapeDtypeStruct((s, D_MODEL), F32),
        scratch_shapes=[pltpu.VMEM((TM_FF, D_MODEL), BF16)],
        compiler_params=_cparams(("arbitrary", "arbitrary")),
        name="swiglu",
    )(x1, nw, w_gate, w_up, w_down)


def _layer(x2, norm1_w, w_in, conv_w, conv_b, i_bias, f_bias, m_norm_w, q_norm_w, k_norm_w,
           lambda_q1, lambda_k1, lambda_q2, lambda_k2, a_norm_w, w_out, norm2_w,
           w_gate, w_up, w_down):
    s = x2.shape[0]
    w_all = w_in.astype(BF16)
    w_hi = w_all[:, N_SEC_LO * SEC + N_GATES:]
    gbias = jnp.pad(jnp.concatenate([i_bias, f_bias]), (0, GATE_PAD - N_GATES))[None, :]

    proj, gates_t = _inproj(x2, norm1_w[None, :], w_all, w_hi, gbias,
                            q_norm_w[None, :], k_norm_w[None, :])
    gates3 = gates_t.reshape(N_GATES, s // CHUNK, CHUNK)
    hm, w_out_b, w_down_b = _mlstm(proj, gates3, conv_w, conv_b[None, :],
                                   m_norm_w.reshape(1, SEC), w_out, w_down)
    ha, w_gate_b, w_up_b = _attn(proj, q_norm_w, k_norm_w, lambda_q1[None, :],
                                 lambda_k1[None, :], lambda_q2[None, :], lambda_k2[None, :],
                                 a_norm_w[:, None, :], w_gate, w_up)
    x1 = _outproj(hm, ha, w_out_b, x2)
    return _ffn(x1, norm2_w[None, :], w_gate_b, w_up_b, w_down_b)


def kernel(x, norm1_w, w_in, conv_w, conv_b, i_bias, f_bias, m_norm_w, q_norm_w, k_norm_w,
           lambda_q1, lambda_k1, lambda_q2, lambda_k2, a_norm_w, w_out, norm2_w,
           w_gate, w_up, w_down):
    b, s, d = x.shape
    assert d == D_MODEL and b == 1 and norm1_w.shape[0] == 1
    assert s % R_ML == 0 and s % TQ == 0 and s % TM_IN == 0 and s % TM_FF == 0
    y = _layer(x.reshape(s, d), norm1_w[0], w_in[0], conv_w[0], conv_b[0], i_bias[0], f_bias[0],
               m_norm_w[0], q_norm_w[0], k_norm_w[0], lambda_q1[0], lambda_k1[0],
               lambda_q2[0], lambda_k2[0], a_norm_w[0], w_out[0], norm2_w[0],
               w_gate[0], w_up[0], w_down[0])
    return y.reshape(b, s, d)
```

```python
import math

import jax
import jax.numpy as jnp
from jax import lax
from jax.experimental import pallas as pl
from jax.experimental.pallas import tpu as pltpu

F32 = jnp.float32
BF16 = jnp.bfloat16

D_MODEL = 2048
CHUNK = 64
SEC = 1024
N_SEC_LO = 3
M_HEADS = 4
M_DV = 256
M_DQK = 128
CONV_W = 4
A_HEADS = 4
A_DV = 256
A_DQK = 128
D_FF = 5632
RMS_EPS = 1e-6
NEG = -1e30
LAM_INIT = 0.8 - 0.6 * math.exp(-0.3 * 0)
LOG2E = 1.4426950408889634
MAX_FIXED_SHIFT = 60.0
GATE_PAD = 128
N_GATES = 2 * M_HEADS
HALO = 8
BF16_SUBLANES = 16

TM_IN = 1024
TM_OUT = 512
TM_FF = 1024
TF_FF = 512
R_ML = 512
TQ = 512

VMEM_LIMIT = 56 * 1024 * 1024


def _cparams(sem):
    return pltpu.CompilerParams(dimension_semantics=sem, vmem_limit_bytes=VMEM_LIMIT)


def _inproj_kernel(x_ref, nw_ref, wlo_ref, whi_ref, wg_ref, gb_ref, qw_ref, kw_ref,
                   out_ref, gt_ref, h_scr):
    j = pl.program_id(1)

    @pl.when(j == 0)
    def _():
        xf = x_ref[...]
        ms = jnp.mean(xf * xf, axis=1, keepdims=True)
        hb = (xf * lax.rsqrt(ms + RMS_EPS) * nw_ref[...]).astype(BF16)
        h_scr[...] = hb
        g = jnp.dot(hb, wg_ref[...], preferred_element_type=F32) + gb_ref[...]
        gt_ref[...] = g.T[0:N_GATES, :]

    is_q = j == N_SEC_LO
    is_k = j == N_SEC_LO + 1
    is_qk = jnp.logical_or(is_q, is_k)

    @pl.when(j < N_SEC_LO)
    def _():
        out_ref[...] = jnp.dot(h_scr[...], wlo_ref[...],
                               preferred_element_type=F32).astype(BF16)

    @pl.when(j > N_SEC_LO + 1)
    def _():
        out_ref[...] = jnp.dot(h_scr[...], whi_ref[...],
                               preferred_element_type=F32).astype(BF16)

    @pl.when(is_qk)
    def _():
        acc = jnp.dot(h_scr[...], whi_ref[...], preferred_element_type=F32)
        w = jnp.where(is_q, qw_ref[...] * (A_DQK ** -0.5 * LOG2E), kw_ref[...])
        for g in range(SEC // A_DQK):
            sl = slice(g * A_DQK, (g + 1) * A_DQK)
            a = acc[:, sl]
            ms = jnp.mean(a * a, axis=1, keepdims=True)
            out_ref[:, sl] = (a * lax.rsqrt(ms + RMS_EPS) * w).astype(BF16)


def _inproj(x2, nw, w_all, w_hi, gbias, qw, kw):
    s = x2.shape[0]
    n_hi = w_hi.shape[1] // SEC
    n_sec = N_SEC_LO + n_hi
    gate_blk = N_SEC_LO * SEC // GATE_PAD
    return pl.pallas_call(
        _inproj_kernel,
        grid=(s // TM_IN, n_sec),
        in_specs=[
            pl.BlockSpec((TM_IN, D_MODEL), lambda i, j: (i, 0)),
            pl.BlockSpec((1, D_MODEL), lambda i, j: (0, 0)),
            pl.BlockSpec((D_MODEL, SEC), lambda i, j: (0, jnp.minimum(j, N_SEC_LO - 1))),
            pl.BlockSpec((D_MODEL, SEC), lambda i, j: (0, jnp.maximum(j - N_SEC_LO, 0))),
            pl.BlockSpec((D_MODEL, GATE_PAD), lambda i, j: (0, gate_blk)),
            pl.BlockSpec((1, GATE_PAD), lambda i, j: (0, 0)),
            pl.BlockSpec((1, A_DQK), lambda i, j: (0, 0)),
            pl.BlockSpec((1, A_DQK), lambda i, j: (0, 0)),
        ],
        out_specs=[
            pl.BlockSpec((TM_IN, SEC), lambda i, j: (i, j)),
            pl.BlockSpec((N_GATES, TM_IN), lambda i, j: (0, i)),
        ],
        out_shape=[
            jax.ShapeDtypeStruct((s, n_sec * SEC), BF16),
            jax.ShapeDtypeStruct((N_GATES, s), F32),
        ],
        scratch_shapes=[
            pltpu.VMEM((TM_IN, D_MODEL), BF16),
        ],
        compiler_params=_cparams(("arbitrary", "arbitrary")),
        name="inproj",
    )(x2, nw, w_all, w_hi, w_all, gbias, qw, kw)


def _log_sigmoid(x):
    return jnp.minimum(x, 0.0) - jnp.log1p(jnp.exp(-jnp.abs(x)))


def _mlstm_kernel(qk_ref, halo_ref, v_ref, o_ref, g_ref, cw_ref, cb_ref, nw_ref,
                  wa_ref, wb_ref,
                  out_ref, wa_out_ref, wb_out_ref,
                  stage, act, b_scr, li_scr, ct_scr, n_scr, m_scr):
    i = pl.program_id(0)
    n_chunks = R_ML // CHUNK

    wa_out_ref[...] = wa_ref[...].astype(BF16)
    wb_out_ref[...] = wb_ref[...].astype(BF16)

    @pl.when(i == 0)
    def _():
        ct_scr[...] = jnp.zeros_like(ct_scr)
        n_scr[...] = jnp.zeros_like(n_scr)
        m_scr[...] = jnp.full_like(m_scr, NEG)

    halo = halo_ref[...].astype(F32)
    stage[0:HALO, :] = jnp.where(i == 0, 0.0, halo)
    stage[HALO:HALO + R_ML, :] = qk_ref[...].astype(F32)
    k_scale = M_DQK ** -0.5
    for cs in range(SEC // 128):
        sl = slice(cs * 128, (cs + 1) * 128)
        y = cb_ref[:, sl]
        for t in range(CONV_W):
            y = y + cw_ref[t:t + 1, sl] * stage[HALO - (CONV_W - 1) + t:HALO - (CONV_W - 1) + t + R_ML, sl]
        a = y * jax.nn.sigmoid(y)
        if cs >= M_HEADS:
            a = a * k_scale
        act[:, sl] = a.astype(BF16)

    li_scr[...] = g_ref[0:M_HEADS].reshape(M_HEADS * n_chunks, CHUNK)
    lf = _log_sigmoid(g_ref[M_HEADS:N_GATES].reshape(M_HEADS * n_chunks, CHUNK))
    r_i = lax.broadcasted_iota(jnp.int32, (CHUNK, CHUNK), 0)
    c_i = lax.broadcasted_iota(jnp.int32, (CHUNK, CHUNK), 1)
    upper = jnp.where(r_i <= c_i, 1.0, 0.0).astype(BF16)
    p0 = lf.astype(BF16)
    r1 = lf - p0.astype(F32)
    p1 = r1.astype(BF16)
    p2 = (r1 - p1.astype(F32)).astype(BF16)
    b_scr[...] = (jnp.dot(p0, upper, preferred_element_type=F32)
                  + jnp.dot(p1, upper, preferred_element_type=F32)
                  + jnp.dot(p2, upper, preferred_element_type=F32))

    t_i = lax.broadcasted_iota(jnp.int32, (R_ML, CHUNK), 0) % CHUNK
    s_i = lax.broadcasted_iota(jnp.int32, (R_ML, CHUNK), 1)
    eye = t_i == s_i
    tril = t_i >= s_i

    def rows_of(x, c):
        return x[c * CHUNK:(c + 1) * CHUNK]

    def per_chunk_rows(x):
        return jnp.concatenate(
            [jnp.broadcast_to(x[c:c + 1, :], (CHUNK, x.shape[1])) for c in range(n_chunks)], axis=0)

    def to_col(rows):
        return jnp.sum(jnp.where(eye, rows, 0.0), axis=1, keepdims=True)

    for h in range(M_HEADS):
        qs = slice(h * M_DQK, (h + 1) * M_DQK)
        ks = slice((M_HEADS + h) * M_DQK, (M_HEADS + h + 1) * M_DQK)
        vs = slice(h * M_DV, (h + 1) * M_DV)
        q = act[:, qs]
        k = act[:, ks]
        v = v_ref[:, vs]
        b_h = b_scr[h * n_chunks:(h + 1) * n_chunks, :]
        li_h = li_scr[h * n_chunks:(h + 1) * n_chunks, :]

        b_rows = per_chunk_rows(b_h)
        b_col = to_col(b_rows)
        d = jnp.where(tril, b_col - b_rows + per_chunk_rows(li_h), NEG)
        m_intra = jnp.max(d, axis=1, keepdims=True)
        s = jnp.concatenate(
            [lax.dot_general(rows_of(q, c), rows_of(k, c), (((1,), (1,)), ((), ())),
                             preferred_element_type=F32) for c in range(n_chunks)], axis=0)
        p = jnp.exp(d - m_intra) * s
        row_sum = jnp.sum(p, axis=1, keepdims=True)
        pb = p.astype(BF16)
        n_intra = jnp.concatenate(
            [jnp.dot(rows_of(pb, c), rows_of(v, c), preferred_element_type=F32)
             for c in range(n_chunks)], axis=0)

        g_tot = b_h[:, CHUNK - 1:CHUNK]
        a = g_tot - b_h + li_h
        m_loc = jnp.max(a, axis=1, keepdims=True)
        w_col = to_col(per_chunk_rows(jnp.exp(a - m_loc)))
        kw = k.astype(F32) * w_col
        kwb = kw.astype(BF16)
        ct_locs = [lax.dot_general(rows_of(kwb, c), rows_of(v, c), (((0,), (0,)), ((), ())),
                                   preferred_element_type=F32) for c in range(n_chunks)]
        n_locs = [jnp.sum(rows_of(kw, c), axis=0, keepdims=True) for c in range(n_chunks)]

        m = m_scr[h, 0:1, 0:1]
        ct = ct_scr[h]
        n = n_scr[h, 0:1, :]
        m_prevs, ct_prevs, n_prevs = [], [], []
        for c in range(n_chunks):
            m_prevs.append(m)
            ct_prevs.append(ct.astype(BF16))
            n_prevs.append(n)
            g_c = g_tot[c:c + 1, :]
            m_loc_c = m_loc[c:c + 1, :]
            m_new = jnp.maximum(g_c + m, m_loc_c)
            s_old = jnp.exp(g_c + m - m_new)
            s_loc = jnp.exp(m_loc_c - m_new)
            ct = s_old * ct + s_loc * ct_locs[c]
            n = s_old * n + s_loc * n_locs[c]
            m = m_new
        ct_scr[h] = ct
        n_scr[h] = jnp.broadcast_to(n, (8, M_DQK))
        m_scr[h] = jnp.broadcast_to(m, (8, 128))

        m_prev = per_chunk_rows(jnp.concatenate(m_prevs, axis=0))
        n_prev = per_chunk_rows(jnp.concatenate(n_prevs, axis=0))
        inter = jnp.concatenate(
            [jnp.dot(rows_of(q, c), ct_prevs[c], preferred_element_type=F32)
             for c in range(n_chunks)], axis=0)
        inter_log = b_col + m_prev
        m_t = jnp.maximum(inter_log, m_intra)
        s_intra = jnp.exp(m_intra - m_t)
        s_inter = jnp.exp(inter_log - m_t)
        num = s_intra * n_intra + s_inter * inter
        den = (s_intra * row_sum
               + s_inter * jnp.sum(q.astype(F32) * n_prev, axis=1, keepdims=True))
        hout = num / jnp.maximum(jnp.abs(den), jnp.exp(-m_t))

        ms = jnp.mean(hout * hout, axis=1, keepdims=True)
        hn = hout * lax.rsqrt(ms + RMS_EPS) * nw_ref[:, vs]
        out_ref[:, vs] = (hn * jax.nn.sigmoid(o_ref[:, vs].astype(F32))).astype(BF16)


def _row_slab(w, n_steps, step_of):
    rows, cols = w.shape
    assert rows % (n_steps * BF16_SUBLANES) == 0, (w.shape, n_steps)
    return pl.BlockSpec((rows // n_steps, cols), lambda *idx: (step_of(*idx), 0))


def _mlstm(proj, gates3, conv_w, conv_b, m_norm_w, w_a, w_b):
    s = proj.shape[0]
    n_chunks = R_ML // CHUNK
    hb = R_ML // HALO
    n_steps = s // R_ML
    return pl.pallas_call(
        _mlstm_kernel,
        grid=(n_steps,),
        in_specs=[
            pl.BlockSpec((R_ML, SEC), lambda i: (i, 0)),
            pl.BlockSpec((HALO, SEC), lambda i: (jnp.maximum(i * hb - 1, 0), 0)),
            pl.BlockSpec((R_ML, SEC), lambda i: (i, 1)),
            pl.BlockSpec((R_ML, SEC), lambda i: (i, 2)),
            pl.BlockSpec((N_GATES, n_chunks, CHUNK), lambda i: (0, i, 0)),
            pl.BlockSpec((CONV_W, SEC), lambda i: (0, 0)),
            pl.BlockSpec((1, SEC), lambda i: (0, 0)),
            pl.BlockSpec((1, SEC), lambda i: (0, 0)),
            _row_slab(w_a, n_steps, lambda i: i),
            _row_slab(w_b, n_steps, lambda i: i),
        ],
        out_specs=[
            pl.BlockSpec((R_ML, SEC), lambda i: (i, 0)),
            _row_slab(w_a, n_steps, lambda i: i),
            _row_slab(w_b, n_steps, lambda i: i),
        ],
        out_shape=[
            jax.ShapeDtypeStruct((s, SEC), BF16),
            jax.ShapeDtypeStruct(w_a.shape, BF16),
            jax.ShapeDtypeStruct(w_b.shape, BF16),
        ],
        scratch_shapes=[
            pltpu.VMEM((R_ML + HALO, SEC), F32),
            pltpu.VMEM((R_ML, SEC), BF16),
            pltpu.VMEM((M_HEADS * n_chunks, CHUNK), F32),
            pltpu.VMEM((M_HEADS * n_chunks, CHUNK), F32),
            pltpu.VMEM((M_HEADS, M_DQK, M_DV), F32),
            pltpu.VMEM((M_HEADS, 8, M_DQK), F32),
            pltpu.VMEM((M_HEADS, 8, 128), F32),
        ],
        compiler_params=_cparams(("arbitrary",)),
        name="mlstm",
    )(proj, proj, proj, proj, gates3, conv_w, conv_b, m_norm_w, w_a, w_b)


def _lane_partial_sum(p):
    acc = p[:, 0:128]
    for t in range(1, p.shape[1] // 128):
        acc = acc + p[:, t * 128:(t + 1) * 128]
    return acc


def _attn_kernel(par_ref, q_ref, k_ref, v_ref, lq1_ref, lk1_ref, lq2_ref, lk2_ref, nw_ref,
                 wa_ref, wb_ref,
                 out_ref, wa_out_ref, wb_out_ref,
                 m_scr, l_scr, acc_scr, sa_scr, sb_scr):
    wa_out_ref[...] = wa_ref[...].astype(BF16)
    wb_out_ref[...] = wb_ref[...].astype(BF16)

    i = pl.program_id(1)
    nq = pl.num_programs(1)
    shift = par_ref[0]
    fixed = par_ref[1] > 0.5

    l_scr[...] = jnp.zeros_like(l_scr)
    acc_scr[...] = jnp.zeros_like(acc_scr)

    def chunk_mask():
        qc = lax.broadcasted_iota(jnp.int32, (TQ, TQ), 0) // CHUNK
        kc = lax.broadcasted_iota(jnp.int32, (TQ, TQ), 1) // CHUNK
        return kc <= qc

    def tile_rows(t):
        return pl.ds(pl.multiple_of(t * TQ, TQ), TQ)

    def scores(qi, j, c):
        q = q_ref[tile_rows(qi), c * A_DQK:(c + 1) * A_DQK]
        k = k_ref[tile_rows(j), c * A_DQK:(c + 1) * A_DQK]
        return lax.dot_general(q, k, (((1,), (1,)), ((), ())), preferred_element_type=F32)

    def fixed_scores(qi, j, s_buf):
        for c in range(2):
            s_buf[c] = scores(qi, j, c)

    def fixed_pv(j, s_buf, masked):
        v = v_ref[tile_rows(j), :]
        for c in range(2):
            p = jnp.exp2(s_buf[c] - shift)
            if masked:
                p = jnp.where(chunk_mask(), p, 0.0)
            l_scr[c] += _lane_partial_sum(p)
            acc_scr[c] += jnp.dot(p.astype(BF16), v, preferred_element_type=F32)

    def online_tile(j, masked):
        v = v_ref[tile_rows(j), :]
        for c in range(2):
            s = scores(i, j, c)
            if masked:
                s = jnp.where(chunk_mask(), s, NEG)
            m_prev = m_scr[c]
            m_new = jnp.maximum(m_prev, jnp.max(s, axis=1, keepdims=True))
            alpha = jnp.exp2(m_prev - m_new)
            p = jnp.exp2(s - m_new[:, 0:1])
            l_scr[c] = alpha * l_scr[c] + _lane_partial_sum(p)
            acc_scr[c] = alpha[:, 0:1] * acc_scr[c] + jnp.dot(p.astype(BF16), v,
                                                              preferred_element_type=F32)
            m_scr[c] = m_new

    @pl.when(fixed)
    def _():
        @pl.when(i == 0)
        def _():
            fixed_scores(0, 0, sa_scr)

        def pair(t, carry):
            j = 2 * t
            fixed_pv(j, sa_scr, False)
            fixed_scores(i, j + 1, sb_scr)
            fixed_pv(j + 1, sb_scr, False)
            fixed_scores(i, j + 2, sa_scr)
            return carry
        lax.fori_loop(0, i // 2, pair, 0)

        def diagonal(s_buf, more_steps):
            fixed_pv(i, s_buf, True)
            if more_steps:
                fixed_scores(i + 1, 0, sa_scr)

        odd = i % 2 == 1
        more = i + 1 < nq
        for is_odd in (False, True):
            for has_more in (False, True):
                @pl.when(jnp.logical_and(odd == is_odd, more == has_more))
                def _(is_odd=is_odd, has_more=has_more):
                    if is_odd:
                        fixed_pv(i - 1, sa_scr, False)
                        fixed_scores(i, i, sb_scr)
                        diagonal(sb_scr, has_more)
                    else:
                        diagonal(sa_scr, has_more)

    @pl.when(jnp.logical_not(fixed))
    def _():
        m_scr[...] = jnp.full_like(m_scr, NEG)

        def body(j, carry):
            online_tile(j, False)
            return carry
        lax.fori_loop(0, i, body, 0)
        online_tile(i, True)

    lam = (jnp.exp(jnp.sum(lq1_ref[...] * lk1_ref[...], axis=1, keepdims=True))
           - jnp.exp(jnp.sum(lq2_ref[...] * lk2_ref[...], axis=1, keepdims=True)) + LAM_INIT)
    l0 = jnp.sum(l_scr[0], axis=1, keepdims=True)
    l1 = jnp.sum(l_scr[1], axis=1, keepdims=True)
    o = acc_scr[0] / l0 - lam * (acc_scr[1] / l1)
    ms = jnp.mean(o * o, axis=1, keepdims=True)
    out_ref[...] = (o * lax.rsqrt(ms + RMS_EPS) * nw_ref[0] * (1.0 - LAM_INIT)).astype(BF16)


def _attn(proj, q_norm_w, k_norm_w, lq1, lk1, lq2, lk2, a_norm_w3, w_a, w_b):
    s = proj.shape[0]
    qb = 3 * SEC // A_DV
    kb = 4 * SEC // A_DV
    vb = 5 * SEC // A_DV
    bound = (A_DQK ** 0.5 * LOG2E) * jnp.max(jnp.abs(q_norm_w * k_norm_w))
    shift = jnp.ceil(bound * 1.02) + 1.0
    params = jnp.stack([shift, (shift <= MAX_FIXED_SHIFT).astype(F32)]).astype(F32)
    vec = pl.BlockSpec((1, A_DQK), lambda h, i: (0, 0))
    nq = s // TQ
    n_steps = A_HEADS * nq
    return pl.pallas_call(
        _attn_kernel,
        grid=(A_HEADS, nq),
        in_specs=[
            pl.BlockSpec(memory_space=pltpu.SMEM),
            pl.BlockSpec((s, 2 * A_DQK), lambda h, i: (0, qb + h)),
            pl.BlockSpec((s, 2 * A_DQK), lambda h, i: (0, kb + h)),
            pl.BlockSpec((s, A_DV), lambda h, i: (0, vb + h)),
            vec, vec, vec, vec,
            pl.BlockSpec((1, 1, A_DV), lambda h, i: (h, 0, 0)),
            _row_slab(w_a, n_steps, lambda h, i: h * nq + i),
            _row_slab(w_b, n_steps, lambda h, i: h * nq + i),
        ],
        out_specs=[
            pl.BlockSpec((TQ, A_DV), lambda h, i: (i, h)),
            _row_slab(w_a, n_steps, lambda h, i: h * nq + i),
            _row_slab(w_b, n_steps, lambda h, i: h * nq + i),
        ],
        out_shape=[
            jax.ShapeDtypeStruct((s, A_HEADS * A_DV), BF16),
            jax.ShapeDtypeStruct(w_a.shape, BF16),
            jax.ShapeDtypeStruct(w_b.shape, BF16),
        ],
        scratch_shapes=[
            pltpu.VMEM((2, TQ, 128), F32),
            pltpu.VMEM((2, TQ, 128), F32),
            pltpu.VMEM((2, TQ, A_DV), F32),
            pltpu.VMEM((2, TQ, TQ), F32),
            pltpu.VMEM((2, TQ, TQ), F32),
        ],
        compiler_params=_cparams(("arbitrary", "arbitrary")),
        name="diffattn",
    )(params, proj, proj, proj, lq1, lk1, lq2, lk2, a_norm_w3, w_a, w_b)


def _outproj_kernel(hm_ref, ha_ref, wt_ref, wb_ref, x_ref, out_ref):
    out_ref[...] = (x_ref[...]
                    + jnp.dot(hm_ref[...], wt_ref[...], preferred_element_type=F32)
                    + jnp.dot(ha_ref[...], wb_ref[...], preferred_element_type=F32))


def _outproj(hm, ha, w_out, x2):
    s = x2.shape[0]
    return pl.pallas_call(
        _outproj_kernel,
        grid=(s // TM_OUT,),
        in_specs=[
            pl.BlockSpec((TM_OUT, SEC), lambda i: (i, 0)),
            pl.BlockSpec((TM_OUT, SEC), lambda i: (i, 0)),
            pl.BlockSpec((SEC, D_MODEL), lambda i: (0, 0)),
            pl.BlockSpec((SEC, D_MODEL), lambda i: (1, 0)),
            pl.BlockSpec((TM_OUT, D_MODEL), lambda i: (i, 0)),
        ],
        out_specs=pl.BlockSpec((TM_OUT, D_MODEL), lambda i: (i, 0)),
        out_shape=jax.ShapeDtypeStruct((s, D_MODEL), F32),
        compiler_params=_cparams(("arbitrary",)),
        name="outproj",
    )(hm, ha, w_out, w_out, x2)


def _ffn_kernel(x_ref, nw_ref, wg_ref, wu_ref, wd_ref, out_ref, h_scr):
    j = pl.program_id(1)

    @pl.when(j == 0)
    def _():
        xf = x_ref[...]
        ms = jnp.mean(xf * xf, axis=1, keepdims=True)
        h_scr[...] = (xf * lax.rsqrt(ms + RMS_EPS) * nw_ref[...]).astype(BF16)
        out_ref[...] = xf

    h = h_scr[...]
    g = jnp.dot(h, wg_ref[...], preferred_element_type=F32)
    u = jnp.dot(h, wu_ref[...], preferred_element_type=F32)
    a = (g * jax.nn.sigmoid(g) * u).astype(BF16)
    out_ref[...] += jnp.dot(a, wd_ref[...], preferred_element_type=F32)


def _ffn(x1, nw, w_gate, w_up, w_down):
    s = x1.shape[0]
    return pl.pallas_call(
        _ffn_kernel,
        grid=(s // TM_FF, D_FF // TF_FF),
        in_specs=[
            pl.BlockSpec((TM_FF, D_MODEL), lambda i, j: (i, 0)),
            pl.BlockSpec((1, D_MODEL), lambda i, j: (0, 0)),
            pl.BlockSpec((D_MODEL, TF_FF), lambda i, j: (0, j)),
            pl.BlockSpec((D_MODEL, TF_FF), lambda i, j: (0, j)),
            pl.BlockSpec((TF_FF, D_MODEL), lambda i, j: (j, 0)),
        ],
        out_specs=pl.BlockSpec((TM_FF, D_MODEL), lambda i, j: (i, 0)),
        out_shape=jax.ShapeDtypeStruct((s, D_MODEL), F32),
        scratch_shapes=[pltpu.VMEM((TM_FF, D_MODEL), BF16)],
        compiler_params=_cparams(("arbitrary", "arbitrary")),
        name="swiglu",
    )(x1, nw, w_gate, w_up, w_down)


def _layer(x2, norm1_w, w_in, conv_w, conv_b, i_bias, f_bias, m_norm_w, q_norm_w, k_norm_w,
           lambda_q1, lambda_k1, lambda_q2, lambda_k2, a_norm_w, w_out, norm2_w,
           w_gate, w_up, w_down):
    s = x2.shape[0]
    w_all = w_in.astype(BF16)
    w_hi = w_all[:, N_SEC_LO * SEC + N_GATES:]
    gbias = jnp.pad(jnp.concatenate([i_bias, f_bias]), (0, GATE_PAD - N_GATES))[None, :]

    proj, gates_t = _inproj(x2, norm1_w[None, :], w_all, w_hi, gbias,
                            q_norm_w[None, :], k_norm_w[None, :])
    gates3 = gates_t.reshape(N_GATES, s // CHUNK, CHUNK)
    hm, w_out_b, w_down_b = _mlstm(proj, gates3, conv_w, conv_b[None, :],
                                   m_norm_w.reshape(1, SEC), w_out, w_down)
    ha, w_gate_b, w_up_b = _attn(proj, q_norm_w, k_norm_w, lambda_q1[None, :],
                                 lambda_k1[None, :], lambda_q2[None, :], lambda_k2[None, :],
                                 a_norm_w[:, None, :], w_gate, w_up)
    x1 = _outproj(hm, ha, w_out_b, x2)
    return _ffn(x1, norm2_w[None, :], w_gate_b, w_up_b, w_down_b)


def kernel(x, norm1_w, w_in, conv_w, conv_b, i_bias, f_bias, m_norm_w, q_norm_w, k_norm_w,
           lambda_q1, lambda_k1, lambda_q2, lambda_k2, a_norm_w, w_out, norm2_w,
           w_gate, w_up, w_down):
    b, s, d = x.shape
    assert d == D_MODEL and b == 1 and norm1_w.shape[0] == 1
    assert s % R_ML == 0 and s % TQ == 0 and s % TM_IN == 0 and s % TM_FF == 0
    y = _layer(x.reshape(s, d), norm1_w[0], w_in[0], conv_w[0], conv_b[0], i_bias[0], f_bias[0],
               m_norm_w[0], q_norm_w[0], k_norm_w[0], lambda_q1[0], lambda_k1[0],
               lambda_q2[0], lambda_k2[0], a_norm_w[0], w_out[0], norm2_w[0],
               w_gate[0], w_up[0], w_down[0])
    return y.reshape(b, s, d)
```

```python
import math

import jax
import jax.numpy as jnp
from jax import lax
from jax.experimental import pallas as pl
from jax.experimental.pallas import tpu as pltpu

F32 = jnp.float32
BF16 = jnp.bfloat16

D_MODEL = 2048
CHUNK = 64
SEC = 1024
N_SEC_LO = 3
M_HEADS = 4
M_DV = 256
M_DQK = 128
CONV_W = 4
A_HEADS = 4
A_DV = 256
A_DQK = 128
D_FF = 5632
RMS_EPS = 1e-6
NEG = -1e30
LAM_INIT = 0.8 - 0.6 * math.exp(-0.3 * 0)
LOG2E = 1.4426950408889634
MAX_FIXED_SHIFT = 60.0
GATE_PAD = 128
N_GATES = 2 * M_HEADS
HALO = 8
BF16_SUBLANES = 16

TM_IN = 1024
TM_OUT = 512
TM_FF = 1024
TF_FF = 512
R_ML = 1024
TQ = 512

VMEM_LIMIT = 56 * 1024 * 1024


def _cparams(sem):
    return pltpu.CompilerParams(dimension_semantics=sem, vmem_limit_bytes=VMEM_LIMIT)


def _inproj_kernel(x_ref, nw_ref, wlo_ref, whi_ref, wg_ref, gb_ref, qw_ref, kw_ref,
                   out_ref, gt_ref, h_scr):
    j = pl.program_id(1)

    @pl.when(j == 0)
    def _():
        xf = x_ref[...]
        ms = jnp.mean(xf * xf, axis=1, keepdims=True)
        hb = (xf * lax.rsqrt(ms + RMS_EPS) * nw_ref[...]).astype(BF16)
        h_scr[...] = hb
        g = jnp.dot(hb, wg_ref[...], preferred_element_type=F32) + gb_ref[...]
        gt_ref[...] = g.T[0:N_GATES, :]

    is_q = j == N_SEC_LO
    is_k = j == N_SEC_LO + 1
    is_qk = jnp.logical_or(is_q, is_k)

    @pl.when(j < N_SEC_LO)
    def _():
        out_ref[...] = jnp.dot(h_scr[...], wlo_ref[...],
                               preferred_element_type=F32).astype(BF16)

    @pl.when(j > N_SEC_LO + 1)
    def _():
        out_ref[...] = jnp.dot(h_scr[...], whi_ref[...],
                               preferred_element_type=F32).astype(BF16)

    @pl.when(is_qk)
    def _():
        acc = jnp.dot(h_scr[...], whi_ref[...], preferred_element_type=F32)
        w = jnp.where(is_q, qw_ref[...] * (A_DQK ** -0.5 * LOG2E), kw_ref[...])
        for g in range(SEC // A_DQK):
            sl = slice(g * A_DQK, (g + 1) * A_DQK)
            a = acc[:, sl]
            ms = jnp.mean(a * a, axis=1, keepdims=True)
            out_ref[:, sl] = (a * lax.rsqrt(ms + RMS_EPS) * w).astype(BF16)


def _inproj(x2, nw, w_all, w_hi, gbias, qw, kw):
    s = x2.shape[0]
    n_hi = w_hi.shape[1] // SEC
    n_sec = N_SEC_LO + n_hi
    gate_blk = N_SEC_LO * SEC // GATE_PAD
    return pl.pallas_call(
        _inproj_kernel,
        grid=(s // TM_IN, n_sec),
        in_specs=[
            pl.BlockSpec((TM_IN, D_MODEL), lambda i, j: (i, 0)),
            pl.BlockSpec((1, D_MODEL), lambda i, j: (0, 0)),
            pl.BlockSpec((D_MODEL, SEC), lambda i, j: (0, jnp.minimum(j, N_SEC_LO - 1))),
            pl.BlockSpec((D_MODEL, SEC), lambda i, j: (0, jnp.maximum(j - N_SEC_LO, 0))),
            pl.BlockSpec((D_MODEL, GATE_PAD), lambda i, j: (0, gate_blk)),
            pl.BlockSpec((1, GATE_PAD), lambda i, j: (0, 0)),
            pl.BlockSpec((1, A_DQK), lambda i, j: (0, 0)),
            pl.BlockSpec((1, A_DQK), lambda i, j: (0, 0)),
        ],
        out_specs=[
            pl.BlockSpec((TM_IN, SEC), lambda i, j: (i, j)),
            pl.BlockSpec((N_GATES, TM_IN), lambda i, j: (0, i)),
        ],
        out_shape=[
            jax.ShapeDtypeStruct((s, n_sec * SEC), BF16),
            jax.ShapeDtypeStruct((N_GATES, s), F32),
        ],
        scratch_shapes=[
            pltpu.VMEM((TM_IN, D_MODEL), BF16),
        ],
        compiler_params=_cparams(("arbitrary", "arbitrary")),
        name="inproj",
    )(x2, nw, w_all, w_hi, w_all, gbias, qw, kw)


def _log_sigmoid(x):
    return jnp.minimum(x, 0.0) - jnp.log1p(jnp.exp(-jnp.abs(x)))


def _mlstm_kernel(qk_ref, halo_ref, v_ref, o_ref, g_ref, cw_ref, cb_ref, nw_ref,
                  wa_ref, wb_ref,
                  out_ref, wa_out_ref, wb_out_ref,
                  stage, act, b_scr, li_scr, ct_scr, n_scr, m_scr):
    i = pl.program_id(0)
    n_chunks = R_ML // CHUNK

    wa_out_ref[...] = wa_ref[...].astype(BF16)
    wb_out_ref[...] = wb_ref[...].astype(BF16)

    @pl.when(i == 0)
    def _():
        ct_scr[...] = jnp.zeros_like(ct_scr)
        n_scr[...] = jnp.zeros_like(n_scr)
        m_scr[...] = jnp.full_like(m_scr, NEG)

    halo = halo_ref[...].astype(F32)
    stage[0:HALO, :] = jnp.where(i == 0, 0.0, halo)
    stage[HALO:HALO + R_ML, :] = qk_ref[...].astype(F32)
    k_scale = M_DQK ** -0.5
    for cs in range(SEC // 128):
        sl = slice(cs * 128, (cs + 1) * 128)
        y = cb_ref[:, sl]
        for t in range(CONV_W):
            y = y + cw_ref[t:t + 1, sl] * stage[HALO - (CONV_W - 1) + t:HALO - (CONV_W - 1) + t + R_ML, sl]
        a = y * jax.nn.sigmoid(y)
        if cs >= M_HEADS:
            a = a * k_scale
        act[:, sl] = a.astype(BF16)

    li_scr[...] = g_ref[0:M_HEADS].reshape(M_HEADS * n_chunks, CHUNK)
    lf = _log_sigmoid(g_ref[M_HEADS:N_GATES].reshape(M_HEADS * n_chunks, CHUNK))
    r_i = lax.broadcasted_iota(jnp.int32, (CHUNK, CHUNK), 0)
    c_i = lax.broadcasted_iota(jnp.int32, (CHUNK, CHUNK), 1)
    upper = jnp.where(r_i <= c_i, 1.0, 0.0).astype(BF16)
    p0 = lf.astype(BF16)
    r1 = lf - p0.astype(F32)
    p1 = r1.astype(BF16)
    p2 = (r1 - p1.astype(F32)).astype(BF16)
    b_scr[...] = (jnp.dot(p0, upper, preferred_element_type=F32)
                  + jnp.dot(p1, upper, preferred_element_type=F32)
                  + jnp.dot(p2, upper, preferred_element_type=F32))

    t_i = lax.broadcasted_iota(jnp.int32, (R_ML, CHUNK), 0) % CHUNK
    s_i = lax.broadcasted_iota(jnp.int32, (R_ML, CHUNK), 1)
    eye = t_i == s_i
    tril = t_i >= s_i

    def rows_of(x, c):
        return x[c * CHUNK:(c + 1) * CHUNK]

    def per_chunk_rows(x):
        return jnp.concatenate(
            [jnp.broadcast_to(x[c:c + 1, :], (CHUNK, x.shape[1])) for c in range(n_chunks)], axis=0)

    def to_col(rows):
        return jnp.sum(jnp.where(eye, rows, 0.0), axis=1, keepdims=True)

    for h in range(M_HEADS):
        qs = slice(h * M_DQK, (h + 1) * M_DQK)
        ks = slice((M_HEADS + h) * M_DQK, (M_HEADS + h + 1) * M_DQK)
        vs = slice(h * M_DV, (h + 1) * M_DV)
        q = act[:, qs]
        k = act[:, ks]
        v = v_ref[:, vs]
        b_h = b_scr[h * n_chunks:(h + 1) * n_chunks, :]
        li_h = li_scr[h * n_chunks:(h + 1) * n_chunks, :]

        b_rows = per_chunk_rows(b_h)
        b_col = to_col(b_rows)
        d = jnp.where(tril, b_col - b_rows + per_chunk_rows(li_h), NEG)
        m_intra = jnp.max(d, axis=1, keepdims=True)
        s = jnp.concatenate(
            [lax.dot_general(rows_of(q, c), rows_of(k, c), (((1,), (1,)), ((), ())),
                             preferred_element_type=F32) for c in range(n_chunks)], axis=0)
        p = jnp.exp(d - m_intra) * s
        row_sum = jnp.sum(p, axis=1, keepdims=True)
        pb = p.astype(BF16)
        n_intra = jnp.concatenate(
            [jnp.dot(rows_of(pb, c), rows_of(v, c), preferred_element_type=F32)
             for c in range(n_chunks)], axis=0)

        g_tot = b_h[:, CHUNK - 1:CHUNK]
        a = g_tot - b_h + li_h
        m_loc = jnp.max(a, axis=1, keepdims=True)
        w_col = to_col(per_chunk_rows(jnp.exp(a - m_loc)))
        kw = k.astype(F32) * w_col
        kwb = kw.astype(BF16)
        ct_locs = [lax.dot_general(rows_of(kwb, c), rows_of(v, c), (((0,), (0,)), ((), ())),
                                   preferred_element_type=F32) for c in range(n_chunks)]
        n_locs = [jnp.sum(rows_of(kw, c), axis=0, keepdims=True) for c in range(n_chunks)]

        m = m_scr[h, 0:1, 0:1]
        ct = ct_scr[h]
        n = n_scr[h, 0:1, :]
        m_prevs, ct_prevs, n_prevs = [], [], []
        for c in range(n_chunks):
            m_prevs.append(m)
            ct_prevs.append(ct.astype(BF16))
            n_prevs.append(n)
            g_c = g_tot[c:c + 1, :]
            m_loc_c = m_loc[c:c + 1, :]
            m_new = jnp.maximum(g_c + m, m_loc_c)
            s_old = jnp.exp(g_c + m - m_new)
            s_loc = jnp.exp(m_loc_c - m_new)
            ct = s_old * ct + s_loc * ct_locs[c]
            n = s_old * n + s_loc * n_locs[c]
            m = m_new
        ct_scr[h] = ct
        n_scr[h] = jnp.broadcast_to(n, (8, M_DQK))
        m_scr[h] = jnp.broadcast_to(m, (8, 128))

        m_prev = per_chunk_rows(jnp.concatenate(m_prevs, axis=0))
        n_prev = per_chunk_rows(jnp.concatenate(n_prevs, axis=0))
        inter = jnp.concatenate(
            [jnp.dot(rows_of(q, c), ct_prevs[c], preferred_element_type=F32)
             for c in range(n_chunks)], axis=0)
        inter_log = b_col + m_prev
        m_t = jnp.maximum(inter_log, m_intra)
        s_intra = jnp.exp(m_intra - m_t)
        s_inter = jnp.exp(inter_log - m_t)
        num = s_intra * n_intra + s_inter * inter
        den = (s_intra * row_sum
               + s_inter * jnp.sum(q.astype(F32) * n_prev, axis=1, keepdims=True))
        hout = num / jnp.maximum(jnp.abs(den), jnp.exp(-m_t))

        ms = jnp.mean(hout * hout, axis=1, keepdims=True)
        hn = hout * lax.rsqrt(ms + RMS_EPS) * nw_ref[:, vs]
        out_ref[:, vs] = (hn * jax.nn.sigmoid(o_ref[:, vs].astype(F32))).astype(BF16)


def _row_slab(w, n_steps, step_of):
    rows, cols = w.shape
    assert rows % (n_steps * BF16_SUBLANES) == 0, (w.shape, n_steps)
    return pl.BlockSpec((rows // n_steps, cols), lambda *idx: (step_of(*idx), 0))


def _mlstm(proj, gates3, conv_w, conv_b, m_norm_w, w_a, w_b):
    s = proj.shape[0]
    n_chunks = R_ML // CHUNK
    hb = R_ML // HALO
    n_steps = s // R_ML
    return pl.pallas_call(
        _mlstm_kernel,
        grid=(n_steps,),
        in_specs=[
            pl.BlockSpec((R_ML, SEC), lambda i: (i, 0)),
            pl.BlockSpec((HALO, SEC), lambda i: (jnp.maximum(i * hb - 1, 0), 0)),
            pl.BlockSpec((R_ML, SEC), lambda i: (i, 1)),
            pl.BlockSpec((R_ML, SEC), lambda i: (i, 2)),
            pl.BlockSpec((N_GATES, n_chunks, CHUNK), lambda i: (0, i, 0)),
            pl.BlockSpec((CONV_W, SEC), lambda i: (0, 0)),
            pl.BlockSpec((1, SEC), lambda i: (0, 0)),
            pl.BlockSpec((1, SEC), lambda i: (0, 0)),
            _row_slab(w_a, n_steps, lambda i: i),
            _row_slab(w_b, n_steps, lambda i: i),
        ],
        out_specs=[
            pl.BlockSpec((R_ML, SEC), lambda i: (i, 0)),
            _row_slab(w_a, n_steps, lambda i: i),
            _row_slab(w_b, n_steps, lambda i: i),
        ],
        out_shape=[
            jax.ShapeDtypeStruct((s, SEC), BF16),
            jax.ShapeDtypeStruct(w_a.shape, BF16),
            jax.ShapeDtypeStruct(w_b.shape, BF16),
        ],
        scratch_shapes=[
            pltpu.VMEM((R_ML + HALO, SEC), F32),
            pltpu.VMEM((R_ML, SEC), BF16),
            pltpu.VMEM((M_HEADS * n_chunks, CHUNK), F32),
            pltpu.VMEM((M_HEADS * n_chunks, CHUNK), F32),
            pltpu.VMEM((M_HEADS, M_DQK, M_DV), F32),
            pltpu.VMEM((M_HEADS, 8, M_DQK), F32),
            pltpu.VMEM((M_HEADS, 8, 128), F32),
        ],
        compiler_params=_cparams(("arbitrary",)),
        name="mlstm",
    )(proj, proj, proj, proj, gates3, conv_w, conv_b, m_norm_w, w_a, w_b)


def _lane_partial_sum(p):
    acc = p[:, 0:128]
    for t in range(1, p.shape[1] // 128):
        acc = acc + p[:, t * 128:(t + 1) * 128]
    return acc


def _attn_kernel(par_ref, q_ref, k_ref, v_ref, lq1_ref, lk1_ref, lq2_ref, lk2_ref, nw_ref,
                 wa_ref, wb_ref,
                 out_ref, wa_out_ref, wb_out_ref,
                 m_scr, l_scr, acc_scr, sa_scr, sb_scr):
    wa_out_ref[...] = wa_ref[...].astype(BF16)
    wb_out_ref[...] = wb_ref[...].astype(BF16)

    i = pl.program_id(1)
    nq = pl.num_programs(1)
    shift = par_ref[0]
    fixed = par_ref[1] > 0.5

    l_scr[...] = jnp.zeros_like(l_scr)
    acc_scr[...] = jnp.zeros_like(acc_scr)

    def chunk_mask():
        qc = lax.broadcasted_iota(jnp.int32, (TQ, TQ), 0) // CHUNK
        kc = lax.broadcasted_iota(jnp.int32, (TQ, TQ), 1) // CHUNK
        return kc <= qc

    def tile_rows(t):
        return pl.ds(pl.multiple_of(t * TQ, TQ), TQ)

    def scores(qi, j, c):
        q = q_ref[tile_rows(qi), c * A_DQK:(c + 1) * A_DQK]
        k = k_ref[tile_rows(j), c * A_DQK:(c + 1) * A_DQK]
        return lax.dot_general(q, k, (((1,), (1,)), ((), ())), preferred_element_type=F32)

    def fixed_scores(qi, j, s_buf):
        for c in range(2):
            s_buf[c] = scores(qi, j, c)

    def fixed_pv(j, s_buf, masked):
        v = v_ref[tile_rows(j), :]
        for c in range(2):
            p = jnp.exp2(s_buf[c] - shift)
            if masked:
                p = jnp.where(chunk_mask(), p, 0.0)
            l_scr[c] += _lane_partial_sum(p)
            acc_scr[c] += jnp.dot(p.astype(BF16), v, preferred_element_type=F32)

    def online_tile(j, masked):
        v = v_ref[tile_rows(j), :]
        for c in range(2):
            s = scores(i, j, c)
            if masked:
                s = jnp.where(chunk_mask(), s, NEG)
            m_prev = m_scr[c]
            m_new = jnp.maximum(m_prev, jnp.max(s, axis=1, keepdims=True))
            alpha = jnp.exp2(m_prev - m_new)
            p = jnp.exp2(s - m_new[:, 0:1])
            l_scr[c] = alpha * l_scr[c] + _lane_partial_sum(p)
            acc_scr[c] = alpha[:, 0:1] * acc_scr[c] + jnp.dot(p.astype(BF16), v,
                                                              preferred_element_type=F32)
            m_scr[c] = m_new

    @pl.when(fixed)
    def _():
        @pl.when(i == 0)
        def _():
            fixed_scores(0, 0, sa_scr)

        def pair(t, carry):
            j = 2 * t
            fixed_pv(j, sa_scr, False)
            fixed_scores(i, j + 1, sb_scr)
            fixed_pv(j + 1, sb_scr, False)
            fixed_scores(i, j + 2, sa_scr)
            return carry
        lax.fori_loop(0, i // 2, pair, 0)

        def diagonal(s_buf, more_steps):
            fixed_pv(i, s_buf, True)
            if more_steps:
                fixed_scores(i + 1, 0, sa_scr)

        odd = i % 2 == 1
        more = i + 1 < nq
        for is_odd in (False, True):
            for has_more in (False, True):
                @pl.when(jnp.logical_and(odd == is_odd, more == has_more))
                def _(is_odd=is_odd, has_more=has_more):
                    if is_odd:
                        fixed_pv(i - 1, sa_scr, False)
                        fixed_scores(i, i, sb_scr)
                        diagonal(sb_scr, has_more)
                    else:
                        diagonal(sa_scr, has_more)

    @pl.when(jnp.logical_not(fixed))
    def _():
        m_scr[...] = jnp.full_like(m_scr, NEG)

        def body(j, carry):
            online_tile(j, False)
            return carry
        lax.fori_loop(0, i, body, 0)
        online_tile(i, True)

    lam = (jnp.exp(jnp.sum(lq1_ref[...] * lk1_ref[...], axis=1, keepdims=True))
           - jnp.exp(jnp.sum(lq2_ref[...] * lk2_ref[...], axis=1, keepdims=True)) + LAM_INIT)
    l0 = jnp.sum(l_scr[0], axis=1, keepdims=True)
    l1 = jnp.sum(l_scr[1], axis=1, keepdims=True)
    o = acc_scr[0] / l0 - lam * (acc_scr[1] / l1)
    ms = jnp.mean(o * o, axis=1, keepdims=True)
    out_ref[...] = (o * lax.rsqrt(ms + RMS_EPS) * nw_ref[0] * (1.0 - LAM_INIT)).astype(BF16)


def _attn(proj, q_norm_w, k_norm_w, lq1, lk1, lq2, lk2, a_norm_w3, w_a, w_b):
    s = proj.shape[0]
    qb = 3 * SEC // A_DV
    kb = 4 * SEC // A_DV
    vb = 5 * SEC // A_DV
    bound = (A_DQK ** 0.5 * LOG2E) * jnp.max(jnp.abs(q_norm_w * k_norm_w))
    shift = jnp.ceil(bound * 1.02) + 1.0
    params = jnp.stack([shift, (shift <= MAX_FIXED_SHIFT).astype(F32)]).astype(F32)
    vec = pl.BlockSpec((1, A_DQK), lambda h, i: (0, 0))
    nq = s // TQ
    n_steps = A_HEADS * nq
    return pl.pallas_call(
        _attn_kernel,
        grid=(A_HEADS, nq),
        in_specs=[
            pl.BlockSpec(memory_space=pltpu.SMEM),
            pl.BlockSpec((s, 2 * A_DQK), lambda h, i: (0, qb + h)),
            pl.BlockSpec((s, 2 * A_DQK), lambda h, i: (0, kb + h)),
            pl.BlockSpec((s, A_DV), lambda h, i: (0, vb + h)),
            vec, vec, vec, vec,
            pl.BlockSpec((1, 1, A_DV), lambda h, i: (h, 0, 0)),
            _row_slab(w_a, n_steps, lambda h, i: h * nq + i),
            _row_slab(w_b, n_steps, lambda h, i: h * nq + i),
        ],
        out_specs=[
            pl.BlockSpec((TQ, A_DV), lambda h, i: (i, h)),
            _row_slab(w_a, n_steps, lambda h, i: h * nq + i),
            _row_slab(w_b, n_steps, lambda h, i: h * nq + i),
        ],
        out_shape=[
            jax.ShapeDtypeStruct((s, A_HEADS * A_DV), BF16),
            jax.ShapeDtypeStruct(w_a.shape, BF16),
            jax.ShapeDtypeStruct(w_b.shape, BF16),
        ],
        scratch_shapes=[
            pltpu.VMEM((2, TQ, 128), F32),
            pltpu.VMEM((2, TQ, 128), F32),
            pltpu.VMEM((2, TQ, A_DV), F32),
            pltpu.VMEM((2, TQ, TQ), F32),
            pltpu.VMEM((2, TQ, TQ), F32),
        ],
        compiler_params=_cparams(("arbitrary", "arbitrary")),
        name="diffattn",
    )(params, proj, proj, proj, lq1, lk1, lq2, lk2, a_norm_w3, w_a, w_b)


def _outproj_kernel(hm_ref, ha_ref, wt_ref, wb_ref, x_ref, out_ref):
    out_ref[...] = (x_ref[...]
                    + jnp.dot(hm_ref[...], wt_ref[...], preferred_element_type=F32)
                    + jnp.dot(ha_ref[...], wb_ref[...], preferred_element_type=F32))


def _outproj(hm, ha, w_out, x2):
    s = x2.shape[0]
    return pl.pallas_call(
        _outproj_kernel,
        grid=(s // TM_OUT,),
        in_specs=[
            pl.BlockSpec((TM_OUT, SEC), lambda i: (i, 0)),
            pl.BlockSpec((TM_OUT, SEC), lambda i: (i, 0)),
            pl.BlockSpec((SEC, D_MODEL), lambda i: (0, 0)),
            pl.BlockSpec((SEC, D_MODEL), lambda i: (1, 0)),
            pl.BlockSpec((TM_OUT, D_MODEL), lambda i: (i, 0)),
        ],
        out_specs=pl.BlockSpec((TM_OUT, D_MODEL), lambda i: (i, 0)),
        out_shape=jax.ShapeDtypeStruct((s, D_MODEL), F32),
        compiler_params=_cparams(("arbitrary",)),
        name="outproj",
    )(hm, ha, w_out, w_out, x2)


def _ffn_kernel(x_ref, nw_ref, wg_ref, wu_ref, wd_ref, out_ref, h_scr):
    j = pl.program_id(1)

    @pl.when(j == 0)
    def _():
        xf = x_ref[...]
        ms = jnp.mean(xf * xf, axis=1, keepdims=True)
        h_scr[...] = (xf * lax.rsqrt(ms + RMS_EPS) * nw_ref[...]).astype(BF16)
        out_ref[...] = xf

    h = h_scr[...]
    g = jnp.dot(h, wg_ref[...], preferred_element_type=F32)
    u = jnp.dot(h, wu_ref[...], preferred_element_type=F32)
    a = (g * jax.nn.sigmoid(g) * u).astype(BF16)
    out_ref[...] += jnp.dot(a, wd_ref[...], preferred_element_type=F32)


def _ffn(x1, nw, w_gate, w_up, w_down):
    s = x1.shape[0]
    return pl.pallas_call(
        _ffn_kernel,
        grid=(s // TM_FF, D_FF // TF_FF),
        in_specs=[
            pl.BlockSpec((TM_FF, D_MODEL), lambda i, j: (i, 0)),
            pl.BlockSpec((1, D_MODEL), lambda i, j: (0, 0)),
            pl.BlockSpec((D_MODEL, TF_FF), lambda i, j: (0, j)),
            pl.BlockSpec((D_MODEL, TF_FF), lambda i, j: (0, j)),
            pl.BlockSpec((TF_FF, D_MODEL), lambda i, j: (j, 0)),
        ],
        out_specs=pl.BlockSpec((TM_FF, D_MODEL), lambda i, j: (i, 0)),
        out_shape=jax.ShapeDtypeStruct((s, D_MODEL), F32),
        scratch_shapes=[pltpu.VMEM((TM_FF, D_MODEL), BF16)],
        compiler_params=_cparams(("arbitrary", "arbitrary")),
        name="swiglu",
    )(x1, nw, w_gate, w_up, w_down)


def _layer(x2, norm1_w, w_in, conv_w, conv_b, i_bias, f_bias, m_norm_w, q_norm_w, k_norm_w,
           lambda_q1, lambda_k1, lambda_q2, lambda_k2, a_norm_w, w_out, norm2_w,
           w_gate, w_up, w_down):
    s = x2.shape[0]
    w_all = w_in.astype(BF16)
    w_hi = w_all[:, N_SEC_LO * SEC + N_GATES:]
    gbias = jnp.pad(jnp.concatenate([i_bias, f_bias]), (0, GATE_PAD - N_GATES))[None, :]

    proj, gates_t = _inproj(x2, norm1_w[None, :], w_all, w_hi, gbias,
                            q_norm_w[None, :], k_norm_w[None, :])
    gates3 = gates_t.reshape(N_GATES, s // CHUNK, CHUNK)
    hm, w_out_b, w_down_b = _mlstm(proj, gates3, conv_w, conv_b[None, :],
                                   m_norm_w.reshape(1, SEC), w_out, w_down)
    ha, w_gate_b, w_up_b = _attn(proj, q_norm_w, k_norm_w, lambda_q1[None, :],
                                 lambda_k1[None, :], lambda_q2[None, :], lambda_k2[None, :],
                                 a_norm_w[:, None, :], w_gate, w_up)
    x1 = _outproj(hm, ha, w_out_b, x2)
    return _ffn(x1, norm2_w[None, :], w_gate_b, w_up_b, w_down_b)


def kernel(x, norm1_w, w_in, conv_w, conv_b, i_bias, f_bias, m_norm_w, q_norm_w, k_norm_w,
           lambda_q1, lambda_k1, lambda_q2, lambda_k2, a_norm_w, w_out, norm2_w,
           w_gate, w_up, w_down):
    b, s, d = x.shape
    assert d == D_MODEL and b == 1 and norm1_w.shape[0] == 1
    assert s % R_ML == 0 and s % TQ == 0 and s % TM_IN == 0 and s % TM_FF == 0
    y = _layer(x.reshape(s, d), norm1_w[0], w_in[0], conv_w[0], conv_b[0], i_bias[0], f_bias[0],
               m_norm_w[0], q_norm_w[0], k_norm_w[0], lambda_q1[0], lambda_k1[0],
               lambda_q2[0], lambda_k2[0], a_norm_w[0], w_out[0], norm2_w[0],
               w_gate[0], w_up[0], w_down[0])
    return y.reshape(b, s, d)
```

```python
import math

import jax
import jax.numpy as jnp
from jax import lax
from jax.experimental import pallas as pl
from jax.experimental.pallas import tpu as pltpu

F32 = jnp.float32
BF16 = jnp.bfloat16

D_MODEL = 2048
CHUNK = 64
SEC = 1024
N_SEC_LO = 3
M_HEADS = 4
M_DV = 256
M_DQK = 128
CONV_W = 4
A_HEADS = 4
A_DV = 256
A_DQK = 128
D_FF = 5632
RMS_EPS = 1e-6
NEG = -1e30
LAM_INIT = 0.8 - 0.6 * math.exp(-0.3 * 0)
LOG2E = 1.4426950408889634
MAX_FIXED_SHIFT = 60.0
GATE_PAD = 128
N_GATES = 2 * M_HEADS
HALO = 8
BF16_SUBLANES = 16

TM_IN = 1024
TM_OUT = 512
TM_FF = 1024
TF_FF = 512
R_ML = 1024
TQ = 1024
TK = TQ // 2

VMEM_LIMIT = 56 * 1024 * 1024


def _cparams(sem):
    return pltpu.CompilerParams(dimension_semantics=sem, vmem_limit_bytes=VMEM_LIMIT)


def _inproj_kernel(x_ref, nw_ref, wlo_ref, whi_ref, wg_ref, gb_ref, qw_ref, kw_ref,
                   out_ref, gt_ref, h_scr):
    j = pl.program_id(1)

    @pl.when(j == 0)
    def _():
        xf = x_ref[...]
        ms = jnp.mean(xf * xf, axis=1, keepdims=True)
        hb = (xf * lax.rsqrt(ms + RMS_EPS) * nw_ref[...]).astype(BF16)
        h_scr[...] = hb
        g = jnp.dot(hb, wg_ref[...], preferred_element_type=F32) + gb_ref[...]
        gt_ref[...] = g.T[0:N_GATES, :]

    is_q = j == N_SEC_LO
    is_k = j == N_SEC_LO + 1
    is_qk = jnp.logical_or(is_q, is_k)

    @pl.when(j < N_SEC_LO)
    def _():
        out_ref[...] = jnp.dot(h_scr[...], wlo_ref[...],
                               preferred_element_type=F32).astype(BF16)

    @pl.when(j > N_SEC_LO + 1)
    def _():
        out_ref[...] = jnp.dot(h_scr[...], whi_ref[...],
                               preferred_element_type=F32).astype(BF16)

    @pl.when(is_qk)
    def _():
        acc = jnp.dot(h_scr[...], whi_ref[...], preferred_element_type=F32)
        w = jnp.where(is_q, qw_ref[...] * (A_DQK ** -0.5 * LOG2E), kw_ref[...])
        for g in range(SEC // A_DQK):
            sl = slice(g * A_DQK, (g + 1) * A_DQK)
            a = acc[:, sl]
            ms = jnp.mean(a * a, axis=1, keepdims=True)
            out_ref[:, sl] = (a * lax.rsqrt(ms + RMS_EPS) * w).astype(BF16)


def _inproj(x2, nw, w_all, w_hi, gbias, qw, kw):
    s = x2.shape[0]
    n_hi = w_hi.shape[1] // SEC
    n_sec = N_SEC_LO + n_hi
    gate_blk = N_SEC_LO * SEC // GATE_PAD
    return pl.pallas_call(
        _inproj_kernel,
        grid=(s // TM_IN, n_sec),
        in_specs=[
            pl.BlockSpec((TM_IN, D_MODEL), lambda i, j: (i, 0)),
            pl.BlockSpec((1, D_MODEL), lambda i, j: (0, 0)),
            pl.BlockSpec((D_MODEL, SEC), lambda i, j: (0, jnp.minimum(j, N_SEC_LO - 1))),
            pl.BlockSpec((D_MODEL, SEC), lambda i, j: (0, jnp.maximum(j - N_SEC_LO, 0))),
            pl.BlockSpec((D_MODEL, GATE_PAD), lambda i, j: (0, gate_blk)),
            pl.BlockSpec((1, GATE_PAD), lambda i, j: (0, 0)),
            pl.BlockSpec((1, A_DQK), lambda i, j: (0, 0)),
            pl.BlockSpec((1, A_DQK), lambda i, j: (0, 0)),
        ],
        out_specs=[
            pl.BlockSpec((TM_IN, SEC), lambda i, j: (i, j)),
            pl.BlockSpec((N_GATES, TM_IN), lambda i, j: (0, i)),
        ],
        out_shape=[
            jax.ShapeDtypeStruct((s, n_sec * SEC), BF16),
            jax.ShapeDtypeStruct((N_GATES, s), F32),
        ],
        scratch_shapes=[
            pltpu.VMEM((TM_IN, D_MODEL), BF16),
        ],
        compiler_params=_cparams(("arbitrary", "arbitrary")),
        name="inproj",
    )(x2, nw, w_all, w_hi, w_all, gbias, qw, kw)


def _log_sigmoid(x):
    return jnp.minimum(x, 0.0) - jnp.log1p(jnp.exp(-jnp.abs(x)))


def _mlstm_kernel(qk_ref, halo_ref, v_ref, o_ref, g_ref, cw_ref, cb_ref, nw_ref,
                  wa_ref, wb_ref,
                  out_ref, wa_out_ref, wb_out_ref,
                  stage, act, b_scr, li_scr, ct_scr, n_scr, m_scr):
    i = pl.program_id(0)
    n_chunks = R_ML // CHUNK

    wa_out_ref[...] = wa_ref[...].astype(BF16)
    wb_out_ref[...] = wb_ref[...].astype(BF16)

    @pl.when(i == 0)
    def _():
        ct_scr[...] = jnp.zeros_like(ct_scr)
        n_scr[...] = jnp.zeros_like(n_scr)
        m_scr[...] = jnp.full_like(m_scr, NEG)

    halo = halo_ref[...].astype(F32)
    stage[0:HALO, :] = jnp.where(i == 0, 0.0, halo)
    stage[HALO:HALO + R_ML, :] = qk_ref[...].astype(F32)
    k_scale = M_DQK ** -0.5
    for cs in range(SEC // 128):
        sl = slice(cs * 128, (cs + 1) * 128)
        y = cb_ref[:, sl]
        for t in range(CONV_W):
            y = y + cw_ref[t:t + 1, sl] * stage[HALO - (CONV_W - 1) + t:HALO - (CONV_W - 1) + t + R_ML, sl]
        a = y * jax.nn.sigmoid(y)
        if cs >= M_HEADS:
            a = a * k_scale
        act[:, sl] = a.astype(BF16)

    li_scr[...] = g_ref[0:M_HEADS].reshape(M_HEADS * n_chunks, CHUNK)
    lf = _log_sigmoid(g_ref[M_HEADS:N_GATES].reshape(M_HEADS * n_chunks, CHUNK))
    r_i = lax.broadcasted_iota(jnp.int32, (CHUNK, CHUNK), 0)
    c_i = lax.broadcasted_iota(jnp.int32, (CHUNK, CHUNK), 1)
    upper = jnp.where(r_i <= c_i, 1.0, 0.0).astype(BF16)
    p0 = lf.astype(BF16)
    r1 = lf - p0.astype(F32)
    p1 = r1.astype(BF16)
    p2 = (r1 - p1.astype(F32)).astype(BF16)
    b_scr[...] = (jnp.dot(p0, upper, preferred_element_type=F32)
                  + jnp.dot(p1, upper, preferred_element_type=F32)
                  + jnp.dot(p2, upper, preferred_element_type=F32))

    t_i = lax.broadcasted_iota(jnp.int32, (R_ML, CHUNK), 0) % CHUNK
    s_i = lax.broadcasted_iota(jnp.int32, (R_ML, CHUNK), 1)
    eye = t_i == s_i
    tril = t_i >= s_i

    def rows_of(x, c):
        return x[c * CHUNK:(c + 1) * CHUNK]

    def per_chunk_rows(x):
        return jnp.concatenate(
            [jnp.broadcast_to(x[c:c + 1, :], (CHUNK, x.shape[1])) for c in range(n_chunks)], axis=0)

    def to_col(rows):
        return jnp.sum(jnp.where(eye, rows, 0.0), axis=1, keepdims=True)

    for h in range(M_HEADS):
        qs = slice(h * M_DQK, (h + 1) * M_DQK)
        ks = slice((M_HEADS + h) * M_DQK, (M_HEADS + h + 1) * M_DQK)
        vs = slice(h * M_DV, (h + 1) * M_DV)
        q = act[:, qs]
        k = act[:, ks]
        v = v_ref[:, vs]
        b_h = b_scr[h * n_chunks:(h + 1) * n_chunks, :]
        li_h = li_scr[h * n_chunks:(h + 1) * n_chunks, :]

        b_rows = per_chunk_rows(b_h)
        b_col = to_col(b_rows)
        d = jnp.where(tril, b_col - b_rows + per_chunk_rows(li_h), NEG)
        m_intra = jnp.max(d, axis=1, keepdims=True)
        s = jnp.concatenate(
            [lax.dot_general(rows_of(q, c), rows_of(k, c), (((1,), (1,)), ((), ())),
                             preferred_element_type=F32) for c in range(n_chunks)], axis=0)
        p = jnp.exp(d - m_intra) * s
        row_sum = jnp.sum(p, axis=1, keepdims=True)
        pb = p.astype(BF16)
        n_intra = jnp.concatenate(
            [jnp.dot(rows_of(pb, c), rows_of(v, c), preferred_element_type=F32)
             for c in range(n_chunks)], axis=0)

        g_tot = b_h[:, CHUNK - 1:CHUNK]
        a = g_tot - b_h + li_h
        m_loc = jnp.max(a, axis=1, keepdims=True)
        w_col = to_col(per_chunk_rows(jnp.exp(a - m_loc)))
        kw = k.astype(F32) * w_col
        kwb = kw.astype(BF16)
        ct_locs = [lax.dot_general(rows_of(kwb, c), rows_of(v, c), (((0,), (0,)), ((), ())),
                                   preferred_element_type=F32) for c in range(n_chunks)]
        n_locs = [jnp.sum(rows_of(kw, c), axis=0, keepdims=True) for c in range(n_chunks)]

        m = m_scr[h, 0:1, 0:1]
        ct = ct_scr[h]
        n = n_scr[h, 0:1, :]
        m_prevs, ct_prevs, n_prevs = [], [], []
        for c in range(n_chunks):
            m_prevs.append(m)
            ct_prevs.append(ct.astype(BF16))
            n_prevs.append(n)
            g_c = g_tot[c:c + 1, :]
            m_loc_c = m_loc[c:c + 1, :]
            m_new = jnp.maximum(g_c + m, m_loc_c)
            s_old = jnp.exp(g_c + m - m_new)
            s_loc = jnp.exp(m_loc_c - m_new)
            ct = s_old * ct + s_loc * ct_locs[c]
            n = s_old * n + s_loc * n_locs[c]
            m = m_new
        ct_scr[h] = ct
        n_scr[h] = jnp.broadcast_to(n, (8, M_DQK))
        m_scr[h] = jnp.broadcast_to(m, (8, 128))

        m_prev = per_chunk_rows(jnp.concatenate(m_prevs, axis=0))
        n_prev = per_chunk_rows(jnp.concatenate(n_prevs, axis=0))
        inter = jnp.concatenate(
            [jnp.dot(rows_of(q, c), ct_prevs[c], preferred_element_type=F32)
             for c in range(n_chunks)], axis=0)
        inter_log = b_col + m_prev
        m_t = jnp.maximum(inter_log, m_intra)
        s_intra = jnp.exp(m_intra - m_t)
        s_inter = jnp.exp(inter_log - m_t)
        num = s_intra * n_intra + s_inter * inter
        den = (s_intra * row_sum
               + s_inter * jnp.sum(q.astype(F32) * n_prev, axis=1, keepdims=True))
        hout = num / jnp.maximum(jnp.abs(den), jnp.exp(-m_t))

        ms = jnp.mean(hout * hout, axis=1, keepdims=True)
        hn = hout * lax.rsqrt(ms + RMS_EPS) * nw_ref[:, vs]
        out_ref[:, vs] = (hn * jax.nn.sigmoid(o_ref[:, vs].astype(F32))).astype(BF16)


def _row_slab(w, n_steps, step_of):
    rows, cols = w.shape
    assert rows % (n_steps * BF16_SUBLANES) == 0, (w.shape, n_steps)
    return pl.BlockSpec((rows // n_steps, cols), lambda *idx: (step_of(*idx), 0))


def _mlstm(proj, gates3, conv_w, conv_b, m_norm_w, w_a, w_b):
    s = proj.shape[0]
    n_chunks = R_ML // CHUNK
    hb = R_ML // HALO
    n_steps = s // R_ML
    return pl.pallas_call(
        _mlstm_kernel,
        grid=(n_steps,),
        in_specs=[
            pl.BlockSpec((R_ML, SEC), lambda i: (i, 0)),
            pl.BlockSpec((HALO, SEC), lambda i: (jnp.maximum(i * hb - 1, 0), 0)),
            pl.BlockSpec((R_ML, SEC), lambda i: (i, 1)),
            pl.BlockSpec((R_ML, SEC), lambda i: (i, 2)),
            pl.BlockSpec((N_GATES, n_chunks, CHUNK), lambda i: (0, i, 0)),
            pl.BlockSpec((CONV_W, SEC), lambda i: (0, 0)),
            pl.BlockSpec((1, SEC), lambda i: (0, 0)),
            pl.BlockSpec((1, SEC), lambda i: (0, 0)),
            _row_slab(w_a, n_steps, lambda i: i),
            _row_slab(w_b, n_steps, lambda i: i),
        ],
        out_specs=[
            pl.BlockSpec((R_ML, SEC), lambda i: (i, 0)),
            _row_slab(w_a, n_steps, lambda i: i),
            _row_slab(w_b, n_steps, lambda i: i),
        ],
        out_shape=[
            jax.ShapeDtypeStruct((s, SEC), BF16),
            jax.ShapeDtypeStruct(w_a.shape, BF16),
            jax.ShapeDtypeStruct(w_b.shape, BF16),
        ],
        scratch_shapes=[
            pltpu.VMEM((R_ML + HALO, SEC), F32),
            pltpu.VMEM((R_ML, SEC), BF16),
            pltpu.VMEM((M_HEADS * n_chunks, CHUNK), F32),
            pltpu.VMEM((M_HEADS * n_chunks, CHUNK), F32),
            pltpu.VMEM((M_HEADS, M_DQK, M_DV), F32),
            pltpu.VMEM((M_HEADS, 8, M_DQK), F32),
            pltpu.VMEM((M_HEADS, 8, 128), F32),
        ],
        compiler_params=_cparams(("arbitrary",)),
        name="mlstm",
    )(proj, proj, proj, proj, gates3, conv_w, conv_b, m_norm_w, w_a, w_b)


def _lane_partial_sum(p):
    acc = p[:, 0:128]
    for t in range(1, p.shape[1] // 128):
        acc = acc + p[:, t * 128:(t + 1) * 128]
    return acc


def _attn_kernel(par_ref, q_ref, k_ref, v_ref, lq1_ref, lk1_ref, lq2_ref, lk2_ref, nw_ref,
                 wa_ref, wb_ref,
                 out_ref, wa_out_ref, wb_out_ref,
                 m_scr, l_scr, acc_scr, sa_scr, sb_scr):
    wa_out_ref[...] = wa_ref[...].astype(BF16)
    wb_out_ref[...] = wb_ref[...].astype(BF16)

    i = pl.program_id(1)
    nq = pl.num_programs(1)
    shift = par_ref[0]
    fixed = par_ref[1] > 0.5

    l_scr[...] = jnp.zeros_like(l_scr)
    acc_scr[...] = jnp.zeros_like(acc_scr)

    ALL, LOWER = slice(0, TQ), slice(TK, TQ)

    def chunk_mask(n_rows):
        qc = lax.broadcasted_iota(jnp.int32, (n_rows, TK), 0) // CHUNK
        kc = lax.broadcasted_iota(jnp.int32, (n_rows, TK), 1) // CHUNK
        return kc <= qc

    def k_rows(j):
        return pl.ds(pl.multiple_of(j * TK, TK), TK)

    def scores(qi, j, c, rows):
        q0 = pl.multiple_of(qi * TQ + rows.start, TK)
        q = q_ref[pl.ds(q0, rows.stop - rows.start), c * A_DQK:(c + 1) * A_DQK]
        k = k_ref[k_rows(j), c * A_DQK:(c + 1) * A_DQK]
        return lax.dot_general(q, k, (((1,), (1,)), ((), ())), preferred_element_type=F32)

    def fixed_scores(qi, j, s_buf, rows=ALL):
        for c in range(2):
            s_buf[c, rows, :] = scores(qi, j, c, rows)

    def fixed_pv(j, s_buf, rows=ALL, masked=False):
        v = v_ref[k_rows(j), :]
        for c in range(2):
            p = jnp.exp2(s_buf[c, rows, :] - shift)
            if masked:
                p = jnp.where(chunk_mask(rows.stop - rows.start), p, 0.0)
            l_scr[c, rows, :] += _lane_partial_sum(p)
            acc_scr[c, rows, :] += jnp.dot(p.astype(BF16), v, preferred_element_type=F32)

    def online_tile(j, rows=ALL, masked=False):
        v = v_ref[k_rows(j), :]
        for c in range(2):
            s = scores(i, j, c, rows)
            if masked:
                s = jnp.where(chunk_mask(rows.stop - rows.start), s, NEG)
            m_prev = m_scr[c, rows, :]
            m_new = jnp.maximum(m_prev, jnp.max(s, axis=1, keepdims=True))
            alpha = jnp.exp2(m_prev - m_new)
            p = jnp.exp2(s - m_new[:, 0:1])
            l_scr[c, rows, :] = alpha * l_scr[c, rows, :] + _lane_partial_sum(p)
            acc_scr[c, rows, :] = (alpha[:, 0:1] * acc_scr[c, rows, :]
                                   + jnp.dot(p.astype(BF16), v, preferred_element_type=F32))
            m_scr[c, rows, :] = m_new

    @pl.when(fixed)
    def _():
        @pl.when(i == 0)
        def _():
            fixed_scores(0, 0, sa_scr)

        def pair(t, carry):
            j = 2 * t
            fixed_pv(j, sa_scr)
            fixed_scores(i, j + 1, sb_scr)
            fixed_pv(j + 1, sb_scr)
            fixed_scores(i, j + 2, sa_scr)
            return carry
        lax.fori_loop(0, i, pair, 0)

        fixed_pv(2 * i, sa_scr, masked=True)
        fixed_scores(i, 2 * i + 1, sb_scr, LOWER)

        @pl.when(i + 1 < nq)
        def _():
            fixed_pv(2 * i + 1, sb_scr, LOWER, masked=True)
            fixed_scores(i + 1, 0, sa_scr)

        @pl.when(i + 1 == nq)
        def _():
            fixed_pv(2 * i + 1, sb_scr, LOWER, masked=True)

    @pl.when(jnp.logical_not(fixed))
    def _():
        m_scr[...] = jnp.full_like(m_scr, NEG)

        def body(j, carry):
            online_tile(j)
            return carry
        lax.fori_loop(0, 2 * i, body, 0)
        online_tile(2 * i, masked=True)
        online_tile(2 * i + 1, LOWER, masked=True)

    lam = (jnp.exp(jnp.sum(lq1_ref[...] * lk1_ref[...], axis=1, keepdims=True))
           - jnp.exp(jnp.sum(lq2_ref[...] * lk2_ref[...], axis=1, keepdims=True)) + LAM_INIT)
    l0 = jnp.sum(l_scr[0], axis=1, keepdims=True)
    l1 = jnp.sum(l_scr[1], axis=1, keepdims=True)
    o = acc_scr[0] / l0 - lam * (acc_scr[1] / l1)
    ms = jnp.mean(o * o, axis=1, keepdims=True)
    out_ref[...] = (o * lax.rsqrt(ms + RMS_EPS) * nw_ref[0] * (1.0 - LAM_INIT)).astype(BF16)


def _attn(proj, q_norm_w, k_norm_w, lq1, lk1, lq2, lk2, a_norm_w3, w_a, w_b):
    s = proj.shape[0]
    qb = 3 * SEC // A_DV
    kb = 4 * SEC // A_DV
    vb = 5 * SEC // A_DV
    bound = (A_DQK ** 0.5 * LOG2E) * jnp.max(jnp.abs(q_norm_w * k_norm_w))
    shift = jnp.ceil(bound * 1.02) + 1.0
    params = jnp.stack([shift, (shift <= MAX_FIXED_SHIFT).astype(F32)]).astype(F32)
    vec = pl.BlockSpec((1, A_DQK), lambda h, i: (0, 0))
    nq = s // TQ
    n_steps = A_HEADS * nq
    return pl.pallas_call(
        _attn_kernel,
        grid=(A_HEADS, nq),
        in_specs=[
            pl.BlockSpec(memory_space=pltpu.SMEM),
            pl.BlockSpec((s, 2 * A_DQK), lambda h, i: (0, qb + h)),
            pl.BlockSpec((s, 2 * A_DQK), lambda h, i: (0, kb + h)),
            pl.BlockSpec((s, A_DV), lambda h, i: (0, vb + h)),
            vec, vec, vec, vec,
            pl.BlockSpec((1, 1, A_DV), lambda h, i: (h, 0, 0)),
            _row_slab(w_a, n_steps, lambda h, i: h * nq + i),
            _row_slab(w_b, n_steps, lambda h, i: h * nq + i),
        ],
        out_specs=[
            pl.BlockSpec((TQ, A_DV), lambda h, i: (i, h)),
            _row_slab(w_a, n_steps, lambda h, i: h * nq + i),
            _row_slab(w_b, n_steps, lambda h, i: h * nq + i),
        ],
        out_shape=[
            jax.ShapeDtypeStruct((s, A_HEADS * A_DV), BF16),
            jax.ShapeDtypeStruct(w_a.shape, BF16),
            jax.ShapeDtypeStruct(w_b.shape, BF16),
        ],
        scratch_shapes=[
            pltpu.VMEM((2, TQ, 128), F32),
            pltpu.VMEM((2, TQ, 128), F32),
            pltpu.VMEM((2, TQ, A_DV), F32),
            pltpu.VMEM((2, TQ, TK), F32),
            pltpu.VMEM((2, TQ, TK), F32),
        ],
        compiler_params=_cparams(("arbitrary", "arbitrary")),
        name="diffattn",
    )(params, proj, proj, proj, lq1, lk1, lq2, lk2, a_norm_w3, w_a, w_b)


def _outproj_kernel(hm_ref, ha_ref, wt_ref, wb_ref, x_ref, out_ref):
    out_ref[...] = (x_ref[...]
                    + jnp.dot(hm_ref[...], wt_ref[...], preferred_element_type=F32)
                    + jnp.dot(ha_ref[...], wb_ref[...], preferred_element_type=F32))


def _outproj(hm, ha, w_out, x2):
    s = x2.shape[0]
    return pl.pallas_call(
        _outproj_kernel,
        grid=(s // TM_OUT,),
        in_specs=[
            pl.BlockSpec((TM_OUT, SEC), lambda i: (i, 0)),
            pl.BlockSpec((TM_OUT, SEC), lambda i: (i, 0)),
            pl.BlockSpec((SEC, D_MODEL), lambda i: (0, 0)),
            pl.BlockSpec((SEC, D_MODEL), lambda i: (1, 0)),
            pl.BlockSpec((TM_OUT, D_MODEL), lambda i: (i, 0)),
        ],
        out_specs=pl.BlockSpec((TM_OUT, D_MODEL), lambda i: (i, 0)),
        out_shape=jax.ShapeDtypeStruct((s, D_MODEL), F32),
        compiler_params=_cparams(("arbitrary",)),
        name="outproj",
    )(hm, ha, w_out, w_out, x2)


def _ffn_kernel(x_ref, nw_ref, wg_ref, wu_ref, wd_ref, out_ref, h_scr):
    j = pl.program_id(1)

    @pl.when(j == 0)
    def _():
        xf = x_ref[...]
        ms = jnp.mean(xf * xf, axis=1, keepdims=True)
        h_scr[...] = (xf * lax.rsqrt(ms + RMS_EPS) * nw_ref[...]).astype(BF16)
        out_ref[...] = xf

    h = h_scr[...]
    g = jnp.dot(h, wg_ref[...], preferred_element_type=F32)
    u = jnp.dot(h, wu_ref[...], preferred_element_type=F32)
    a = (g * jax.nn.sigmoid(g) * u).astype(BF16)
    out_ref[...] += jnp.dot(a, wd_ref[...], preferred_element_type=F32)


def _ffn(x1, nw, w_gate, w_up, w_down):
    s = x1.shape[0]
    return pl.pallas_call(
        _ffn_kernel,
        grid=(s // TM_FF, D_FF // TF_FF),
        in_specs=[
            pl.BlockSpec((TM_FF, D_MODEL), lambda i, j: (i, 0)),
            pl.BlockSpec((1, D_MODEL), lambda i, j: (0, 0)),
            pl.BlockSpec((D_MODEL, TF_FF), lambda i, j: (0, j)),
            pl.BlockSpec((D_MODEL, TF_FF), lambda i, j: (0, j)),
            pl.BlockSpec((TF_FF, D_MODEL), lambda i, j: (j, 0)),
        ],
        out_specs=pl.BlockSpec((TM_FF, D_MODEL), lambda i, j: (i, 0)),
        out_shape=jax.ShapeDtypeStruct((s, D_MODEL), F32),
        scratch_shapes=[pltpu.VMEM((TM_FF, D_MODEL), BF16)],
        compiler_params=_cparams(("arbitrary", "arbitrary")),
        name="swiglu",
    )(x1, nw, w_gate, w_up, w_down)


def _layer(x2, norm1_w, w_in, conv_w, conv_b, i_bias, f_bias, m_norm_w, q_norm_w, k_norm_w,
           lambda_q1, lambda_k1, lambda_q2, lambda_k2, a_norm_w, w_out, norm2_w,
           w_gate, w_up, w_down):
    s = x2.shape[0]
    w_all = w_in.astype(BF16)
    w_hi = w_all[:, N_SEC_LO * SEC + N_GATES:]
    gbias = jnp.pad(jnp.concatenate([i_bias, f_bias]), (0, GATE_PAD - N_GATES))[None, :]

    proj, gates_t = _inproj(x2, norm1_w[None, :], w_all, w_hi, gbias,
                            q_norm_w[None, :], k_norm_w[None, :])
    gates3 = gates_t.reshape(N_GATES, s // CHUNK, CHUNK)
    hm, w_out_b, w_down_b = _mlstm(proj, gates3, conv_w, conv_b[None, :],
                                   m_norm_w.reshape(1, SEC), w_out, w_down)
    ha, w_gate_b, w_up_b = _attn(proj, q_norm_w, k_norm_w, lambda_q1[None, :],
                                 lambda_k1[None, :], lambda_q2[None, :], lambda_k2[None, :],
                                 a_norm_w[:, None, :], w_gate, w_up)
    x1 = _outproj(hm, ha, w_out_b, x2)
    return _ffn(x1, norm2_w[None, :], w_gate_b, w_up_b, w_down_b)


def kernel(x, norm1_w, w_in, conv_w, conv_b, i_bias, f_bias, m_norm_w, q_norm_w, k_norm_w,
           lambda_q1, lambda_k1, lambda_q2, lambda_k2, a_norm_w, w_out, norm2_w,
           w_gate, w_up, w_down):
    b, s, d = x.shape
    assert d == D_MODEL and b == 1 and norm1_w.shape[0] == 1
    assert s % R_ML == 0 and s % TQ == 0 and s % TM_IN == 0 and s % TM_FF == 0
    y = _layer(x.reshape(s, d), norm1_w[0], w_in[0], conv_w[0], conv_b[0], i_bias[0], f_bias[0],
               m_norm_w[0], q_norm_w[0], k_norm_w[0], lambda_q1[0], lambda_k1[0],
               lambda_q2[0], lambda_k2[0], a_norm_w[0], w_out[0], norm2_w[0],
               w_gate[0], w_up[0], w_down[0])
    return y.reshape(b, s, d)
```

```python
import math

import jax
import jax.numpy as jnp
from jax import lax
from jax.experimental import pallas as pl
from jax.experimental.pallas import tpu as pltpu

F32 = jnp.float32
BF16 = jnp.bfloat16

D_MODEL = 2048
CHUNK = 64
SEC = 1024
N_SEC_LO = 3
M_HEADS = 4
M_DV = 256
M_DQK = 128
CONV_W = 4
A_HEADS = 4
A_DV = 256
A_DQK = 128
D_FF = 5632
RMS_EPS = 1e-6
NEG = -1e30
LAM_INIT = 0.8 - 0.6 * math.exp(-0.3 * 0)
LOG2E = 1.4426950408889634
MAX_FIXED_SHIFT = 60.0
GATE_PAD = 128
N_GATES = 2 * M_HEADS
HALO = 8
BF16_SUBLANES = 16

TR_PREP = 256
TM_IN = 1024
TM_OUT = 512
TM_FF = 1024
TF_FF = 512
R_ML = 1024
TQ = 1024
TK = TQ // 2

VMEM_LIMIT = 56 * 1024 * 1024


def _cparams(sem):
    return pltpu.CompilerParams(dimension_semantics=sem, vmem_limit_bytes=VMEM_LIMIT)


def _wprep_kernel(lo_ref, hi_ref, hi_next_ref, lo_out, hi_out, g_out):
    lo_out[...] = lo_ref[...].astype(BF16)
    hi_out[...] = jnp.concatenate([hi_ref[N_GATES:, :], hi_next_ref[...]], axis=0).astype(BF16)

    @pl.when(pl.program_id(0) == 0)
    def _():
        g_out[...] = hi_ref[0:GATE_PAD, :].astype(BF16)


def _wprep(wt):
    n_in, d = wt.shape
    g0 = N_SEC_LO * SEC
    assert n_in == 2 * g0 + N_GATES and g0 % TR_PREP == 0 and TR_PREP % N_GATES == 0
    n_steps = g0 // TR_PREP
    return pl.pallas_call(
        _wprep_kernel,
        grid=(n_steps,),
        in_specs=[
            pl.BlockSpec((TR_PREP, d), lambda k: (k, 0)),
            pl.BlockSpec((TR_PREP, d), lambda k: (n_steps + k, 0)),
            pl.BlockSpec((N_GATES, d), lambda k: ((n_steps + k + 1) * (TR_PREP // N_GATES), 0)),
        ],
        out_specs=[
            pl.BlockSpec((TR_PREP, d), lambda k: (k, 0)),
            pl.BlockSpec((TR_PREP, d), lambda k: (k, 0)),
            pl.BlockSpec((GATE_PAD, d), lambda k: (0, 0)),
        ],
        out_shape=[
            jax.ShapeDtypeStruct((g0, d), BF16),
            jax.ShapeDtypeStruct((g0, d), BF16),
            jax.ShapeDtypeStruct((GATE_PAD, d), BF16),
        ],
        compiler_params=_cparams(("arbitrary",)),
        name="wprep",
    )(wt, wt, wt)


def _dot_nt(a, b_t):
    return lax.dot_general(a, b_t, (((1,), (1,)), ((), ())), preferred_element_type=F32)


def _inproj_kernel(x_ref, nw_ref, wlo_ref, whi_ref, wg_ref, gb_ref, qw_ref, kw_ref,
                   out_ref, gt_ref, h_scr):
    j = pl.program_id(1)

    @pl.when(j == 0)
    def _():
        xf = x_ref[...]
        ms = jnp.mean(xf * xf, axis=1, keepdims=True)
        hb = (xf * lax.rsqrt(ms + RMS_EPS) * nw_ref[...]).astype(BF16)
        h_scr[...] = hb
        g = _dot_nt(hb, wg_ref[...]) + gb_ref[...]
        gt_ref[...] = g.T[0:N_GATES, :]

    is_q = j == N_SEC_LO
    is_k = j == N_SEC_LO + 1
    is_qk = jnp.logical_or(is_q, is_k)

    @pl.when(j < N_SEC_LO)
    def _():
        out_ref[...] = _dot_nt(h_scr[...], wlo_ref[...]).astype(BF16)

    @pl.when(j > N_SEC_LO + 1)
    def _():
        out_ref[...] = _dot_nt(h_scr[...], whi_ref[...]).astype(BF16)

    @pl.when(is_qk)
    def _():
        acc = _dot_nt(h_scr[...], whi_ref[...])
        w = jnp.where(is_q, qw_ref[...] * (A_DQK ** -0.5 * LOG2E), kw_ref[...])
        for g in range(SEC // A_DQK):
            sl = slice(g * A_DQK, (g + 1) * A_DQK)
            a = acc[:, sl]
            ms = jnp.mean(a * a, axis=1, keepdims=True)
            out_ref[:, sl] = (a * lax.rsqrt(ms + RMS_EPS) * w).astype(BF16)


def _inproj(x2, nw, wt_lo, wt_hi, wt_g, gbias, qw, kw):
    s = x2.shape[0]
    n_hi = wt_hi.shape[0] // SEC
    n_sec = N_SEC_LO + n_hi
    return pl.pallas_call(
        _inproj_kernel,
        grid=(s // TM_IN, n_sec),
        in_specs=[
            pl.BlockSpec((TM_IN, D_MODEL), lambda i, j: (i, 0)),
            pl.BlockSpec((1, D_MODEL), lambda i, j: (0, 0)),
            pl.BlockSpec((SEC, D_MODEL), lambda i, j: (jnp.minimum(j, N_SEC_LO - 1), 0)),
            pl.BlockSpec((SEC, D_MODEL), lambda i, j: (jnp.maximum(j - N_SEC_LO, 0), 0)),
            pl.BlockSpec((GATE_PAD, D_MODEL), lambda i, j: (0, 0)),
            pl.BlockSpec((1, GATE_PAD), lambda i, j: (0, 0)),
            pl.BlockSpec((1, A_DQK), lambda i, j: (0, 0)),
            pl.BlockSpec((1, A_DQK), lambda i, j: (0, 0)),
        ],
        out_specs=[
            pl.BlockSpec((TM_IN, SEC), lambda i, j: (i, j)),
            pl.BlockSpec((N_GATES, TM_IN), lambda i, j: (0, i)),
        ],
        out_shape=[
            jax.ShapeDtypeStruct((s, n_sec * SEC), BF16),
            jax.ShapeDtypeStruct((N_GATES, s), F32),
        ],
        scratch_shapes=[
            pltpu.VMEM((TM_IN, D_MODEL), BF16),
        ],
        compiler_params=_cparams(("arbitrary", "arbitrary")),
        name="inproj",
    )(x2, nw, wt_lo, wt_hi, wt_g, gbias, qw, kw)


def _log_sigmoid(x):
    return jnp.minimum(x, 0.0) - jnp.log1p(jnp.exp(-jnp.abs(x)))


def _mlstm_kernel(qk_ref, halo_ref, v_ref, o_ref, g_ref, cw_ref, cb_ref, nw_ref,
                  wa_ref, wb_ref,
                  out_ref, wa_out_ref, wb_out_ref,
                  stage, act, b_scr, li_scr, ct_scr, n_scr, m_scr):
    i = pl.program_id(0)
    n_chunks = R_ML // CHUNK

    wa_out_ref[...] = wa_ref[...].astype(BF16)
    wb_out_ref[...] = wb_ref[...].astype(BF16)

    @pl.when(i == 0)
    def _():
        ct_scr[...] = jnp.zeros_like(ct_scr)
        n_scr[...] = jnp.zeros_like(n_scr)
        m_scr[...] = jnp.full_like(m_scr, NEG)

    halo = halo_ref[...].astype(F32)
    stage[0:HALO, :] = jnp.where(i == 0, 0.0, halo)
    stage[HALO:HALO + R_ML, :] = qk_ref[...].astype(F32)
    k_scale = M_DQK ** -0.5
    for cs in range(SEC // 128):
        sl = slice(cs * 128, (cs + 1) * 128)
        y = cb_ref[:, sl]
        for t in range(CONV_W):
            y = y + cw_ref[t:t + 1, sl] * stage[HALO - (CONV_W - 1) + t:HALO - (CONV_W - 1) + t + R_ML, sl]
        a = y * jax.nn.sigmoid(y)
        if cs >= M_HEADS:
            a = a * k_scale
        act[:, sl] = a.astype(BF16)

    li_scr[...] = g_ref[0:M_HEADS].reshape(M_HEADS * n_chunks, CHUNK)
    lf = _log_sigmoid(g_ref[M_HEADS:N_GATES].reshape(M_HEADS * n_chunks, CHUNK))
    r_i = lax.broadcasted_iota(jnp.int32, (CHUNK, CHUNK), 0)
    c_i = lax.broadcasted_iota(jnp.int32, (CHUNK, CHUNK), 1)
    upper = jnp.where(r_i <= c_i, 1.0, 0.0).astype(BF16)
    p0 = lf.astype(BF16)
    r1 = lf - p0.astype(F32)
    p1 = r1.astype(BF16)
    p2 = (r1 - p1.astype(F32)).astype(BF16)
    b_scr[...] = (jnp.dot(p0, upper, preferred_element_type=F32)
                  + jnp.dot(p1, upper, preferred_element_type=F32)
                  + jnp.dot(p2, upper, preferred_element_type=F32))

    t_i = lax.broadcasted_iota(jnp.int32, (R_ML, CHUNK), 0) % CHUNK
    s_i = lax.broadcasted_iota(jnp.int32, (R_ML, CHUNK), 1)
    eye = t_i == s_i
    tril = t_i >= s_i

    def rows_of(x, c):
        return x[c * CHUNK:(c + 1) * CHUNK]

    def per_chunk_rows(x):
        return jnp.concatenate(
            [jnp.broadcast_to(x[c:c + 1, :], (CHUNK, x.shape[1])) for c in range(n_chunks)], axis=0)

    def to_col(rows):
        return jnp.sum(jnp.where(eye, rows, 0.0), axis=1, keepdims=True)

    for h in range(M_HEADS):
        qs = slice(h * M_DQK, (h + 1) * M_DQK)
        ks = slice((M_HEADS + h) * M_DQK, (M_HEADS + h + 1) * M_DQK)
        vs = slice(h * M_DV, (h + 1) * M_DV)
        q = act[:, qs]
        k = act[:, ks]
        v = v_ref[:, vs]
        b_h = b_scr[h * n_chunks:(h + 1) * n_chunks, :]
        li_h = li_scr[h * n_chunks:(h + 1) * n_chunks, :]

        b_rows = per_chunk_rows(b_h)
        b_col = to_col(b_rows)
        d = jnp.where(tril, b_col - b_rows + per_chunk_rows(li_h), NEG)
        m_intra = jnp.max(d, axis=1, keepdims=True)
        s = jnp.concatenate(
            [lax.dot_general(rows_of(q, c), rows_of(k, c), (((1,), (1,)), ((), ())),
                             preferred_element_type=F32) for c in range(n_chunks)], axis=0)
        p = jnp.exp(d - m_intra) * s
        row_sum = jnp.sum(p, axis=1, keepdims=True)
        pb = p.astype(BF16)
        n_intra = jnp.concatenate(
            [jnp.dot(rows_of(pb, c), rows_of(v, c), preferred_element_type=F32)
             for c in range(n_chunks)], axis=0)

        g_tot = b_h[:, CHUNK - 1:CHUNK]
        a = g_tot - b_h + li_h
        m_loc = jnp.max(a, axis=1, keepdims=True)
        w_col = to_col(per_chunk_rows(jnp.exp(a - m_loc)))
        kw = k.astype(F32) * w_col
        kwb = kw.astype(BF16)
        ct_locs = [lax.dot_general(rows_of(kwb, c), rows_of(v, c), (((0,), (0,)), ((), ())),
                                   preferred_element_type=F32) for c in range(n_chunks)]
        n_locs = [jnp.sum(rows_of(kw, c), axis=0, keepdims=True) for c in range(n_chunks)]

        m = m_scr[h, 0:1, 0:1]
        ct = ct_scr[h]
        n = n_scr[h, 0:1, :]
        m_prevs, ct_prevs, n_prevs = [], [], []
        for c in range(n_chunks):
            m_prevs.append(m)
            ct_prevs.append(ct.astype(BF16))
            n_prevs.append(n)
            g_c = g_tot[c:c + 1, :]
            m_loc_c = m_loc[c:c + 1, :]
            m_new = jnp.maximum(g_c + m, m_loc_c)
            s_old = jnp.exp(g_c + m - m_new)
            s_loc = jnp.exp(m_loc_c - m_new)
            ct = s_old * ct + s_loc * ct_locs[c]
            n = s_old * n + s_loc * n_locs[c]
            m = m_new
        ct_scr[h] = ct
        n_scr[h] = jnp.broadcast_to(n, (8, M_DQK))
        m_scr[h] = jnp.broadcast_to(m, (8, 128))

        m_prev = per_chunk_rows(jnp.concatenate(m_prevs, axis=0))
        n_prev = per_chunk_rows(jnp.concatenate(n_prevs, axis=0))
        inter = jnp.concatenate(
            [jnp.dot(rows_of(q, c), ct_prevs[c], preferred_element_type=F32)
             for c in range(n_chunks)], axis=0)
        inter_log = b_col + m_prev
        m_t = jnp.maximum(inter_log, m_intra)
        s_intra = jnp.exp(m_intra - m_t)
        s_inter = jnp.exp(inter_log - m_t)
        num = s_intra * n_intra + s_inter * inter
        den = (s_intra * row_sum
               + s_inter * jnp.sum(q.astype(F32) * n_prev, axis=1, keepdims=True))
        hout = num / jnp.maximum(jnp.abs(den), jnp.exp(-m_t))

        ms = jnp.mean(hout * hout, axis=1, keepdims=True)
        hn = hout * lax.rsqrt(ms + RMS_EPS) * nw_ref[:, vs]
        out_ref[:, vs] = (hn * jax.nn.sigmoid(o_ref[:, vs].astype(F32))).astype(BF16)


def _row_slab(w, n_steps, step_of):
    rows, cols = w.shape
    assert rows % (n_steps * BF16_SUBLANES) == 0, (w.shape, n_steps)
    return pl.BlockSpec((rows // n_steps, cols), lambda *idx: (step_of(*idx), 0))


def _mlstm(proj, gates3, conv_w, conv_b, m_norm_w, w_a, w_b):
    s = proj.shape[0]
    n_chunks = R_ML // CHUNK
    hb = R_ML // HALO
    n_steps = s // R_ML
    return pl.pallas_call(
        _mlstm_kernel,
        grid=(n_steps,),
        in_specs=[
            pl.BlockSpec((R_ML, SEC), lambda i: (i, 0)),
            pl.BlockSpec((HALO, SEC), lambda i: (jnp.maximum(i * hb - 1, 0), 0)),
            pl.BlockSpec((R_ML, SEC), lambda i: (i, 1)),
            pl.BlockSpec((R_ML, SEC), lambda i: (i, 2)),
            pl.BlockSpec((N_GATES, n_chunks, CHUNK), lambda i: (0, i, 0)),
            pl.BlockSpec((CONV_W, SEC), lambda i: (0, 0)),
            pl.BlockSpec((1, SEC), lambda i: (0, 0)),
            pl.BlockSpec((1, SEC), lambda i: (0, 0)),
            _row_slab(w_a, n_steps, lambda i: i),
            _row_slab(w_b, n_steps, lambda i: i),
        ],
        out_specs=[
            pl.BlockSpec((R_ML, SEC), lambda i: (i, 0)),
            _row_slab(w_a, n_steps, lambda i: i),
            _row_slab(w_b, n_steps, lambda i: i),
        ],
        out_shape=[
            jax.ShapeDtypeStruct((s, SEC), BF16),
            jax.ShapeDtypeStruct(w_a.shape, BF16),
            jax.ShapeDtypeStruct(w_b.shape, BF16),
        ],
        scratch_shapes=[
            pltpu.VMEM((R_ML + HALO, SEC), F32),
            pltpu.VMEM((R_ML, SEC), BF16),
            pltpu.VMEM((M_HEADS * n_chunks, CHUNK), F32),
            pltpu.VMEM((M_HEADS * n_chunks, CHUNK), F32),
            pltpu.VMEM((M_HEADS, M_DQK, M_DV), F32),
            pltpu.VMEM((M_HEADS, 8, M_DQK), F32),
            pltpu.VMEM((M_HEADS, 8, 128), F32),
        ],
        compiler_params=_cparams(("arbitrary",)),
        name="mlstm",
    )(proj, proj, proj, proj, gates3, conv_w, conv_b, m_norm_w, w_a, w_b)


def _lane_partial_sum(p):
    acc = p[:, 0:128]
    for t in range(1, p.shape[1] // 128):
        acc = acc + p[:, t * 128:(t + 1) * 128]
    return acc


def _attn_kernel(par_ref, q_ref, k_ref, v_ref, lq1_ref, lk1_ref, lq2_ref, lk2_ref, nw_ref,
                 wa_ref, wb_ref,
                 out_ref, wa_out_ref, wb_out_ref,
                 m_scr, l_scr, acc_scr, sa_scr, sb_scr):
    wa_out_ref[...] = wa_ref[...].astype(BF16)
    wb_out_ref[...] = wb_ref[...].astype(BF16)

    i = pl.program_id(1)
    nq = pl.num_programs(1)
    shift = par_ref[0]
    fixed = par_ref[1] > 0.5

    l_scr[...] = jnp.zeros_like(l_scr)
    acc_scr[...] = jnp.zeros_like(acc_scr)

    ALL, LOWER = slice(0, TQ), slice(TK, TQ)

    def chunk_mask(n_rows):
        qc = lax.broadcasted_iota(jnp.int32, (n_rows, TK), 0) // CHUNK
        kc = lax.broadcasted_iota(jnp.int32, (n_rows, TK), 1) // CHUNK
        return kc <= qc

    def k_rows(j):
        return pl.ds(pl.multiple_of(j * TK, TK), TK)

    def scores(qi, j, c, rows):
        q0 = pl.multiple_of(qi * TQ + rows.start, TK)
        q = q_ref[pl.ds(q0, rows.stop - rows.start), c * A_DQK:(c + 1) * A_DQK]
        k = k_ref[k_rows(j), c * A_DQK:(c + 1) * A_DQK]
        return lax.dot_general(q, k, (((1,), (1,)), ((), ())), preferred_element_type=F32)

    def fixed_scores(qi, j, s_buf, rows=ALL):
        for c in range(2):
            s_buf[c, rows, :] = scores(qi, j, c, rows)

    def fixed_pv(j, s_buf, rows=ALL, masked=False):
        v = v_ref[k_rows(j), :]
        for c in range(2):
            p = jnp.exp2(s_buf[c, rows, :] - shift)
            if masked:
                p = jnp.where(chunk_mask(rows.stop - rows.start), p, 0.0)
            l_scr[c, rows, :] += _lane_partial_sum(p)
            acc_scr[c, rows, :] += jnp.dot(p.astype(BF16), v, preferred_element_type=F32)

    def online_tile(j, rows=ALL, masked=False):
        v = v_ref[k_rows(j), :]
        for c in range(2):
            s = scores(i, j, c, rows)
            if masked:
                s = jnp.where(chunk_mask(rows.stop - rows.start), s, NEG)
            m_prev = m_scr[c, rows, :]
            m_new = jnp.maximum(m_prev, jnp.max(s, axis=1, keepdims=True))
            alpha = jnp.exp2(m_prev - m_new)
            p = jnp.exp2(s - m_new[:, 0:1])
            l_scr[c, rows, :] = alpha * l_scr[c, rows, :] + _lane_partial_sum(p)
            acc_scr[c, rows, :] = (alpha[:, 0:1] * acc_scr[c, rows, :]
                                   + jnp.dot(p.astype(BF16), v, preferred_element_type=F32))
            m_scr[c, rows, :] = m_new

    @pl.when(fixed)
    def _():
        @pl.when(i == 0)
        def _():
            fixed_scores(0, 0, sa_scr)

        def pair(t, carry):
            j = 2 * t
            fixed_pv(j, sa_scr)
            fixed_scores(i, j + 1, sb_scr)
            fixed_pv(j + 1, sb_scr)
            fixed_scores(i, j + 2, sa_scr)
            return carry
        lax.fori_loop(0, i, pair, 0)

        fixed_pv(2 * i, sa_scr, masked=True)
        fixed_scores(i, 2 * i + 1, sb_scr, LOWER)

        @pl.when(i + 1 < nq)
        def _():
            fixed_pv(2 * i + 1, sb_scr, LOWER, masked=True)
            fixed_scores(i + 1, 0, sa_scr)

        @pl.when(i + 1 == nq)
        def _():
            fixed_pv(2 * i + 1, sb_scr, LOWER, masked=True)

    @pl.when(jnp.logical_not(fixed))
    def _():
        m_scr[...] = jnp.full_like(m_scr, NEG)

        def body(j, carry):
            online_tile(j)
            return carry
        lax.fori_loop(0, 2 * i, body, 0)
        online_tile(2 * i, masked=True)
        online_tile(2 * i + 1, LOWER, masked=True)

    lam = (jnp.exp(jnp.sum(lq1_ref[...] * lk1_ref[...], axis=1, keepdims=True))
           - jnp.exp(jnp.sum(lq2_ref[...] * lk2_ref[...], axis=1, keepdims=True)) + LAM_INIT)
    l0 = jnp.sum(l_scr[0], axis=1, keepdims=True)
    l1 = jnp.sum(l_scr[1], axis=1, keepdims=True)
    o = acc_scr[0] / l0 - lam * (acc_scr[1] / l1)
    ms = jnp.mean(o * o, axis=1, keepdims=True)
    out_ref[...] = (o * lax.rsqrt(ms + RMS_EPS) * nw_ref[0] * (1.0 - LAM_INIT)).astype(BF16)


def _attn(proj, q_norm_w, k_norm_w, lq1, lk1, lq2, lk2, a_norm_w3, w_a, w_b):
    s = proj.shape[0]
    qb = 3 * SEC // A_DV
    kb = 4 * SEC // A_DV
    vb = 5 * SEC // A_DV
    bound = (A_DQK ** 0.5 * LOG2E) * jnp.max(jnp.abs(q_norm_w * k_norm_w))
    shift = jnp.ceil(bound * 1.02) + 1.0
    params = jnp.stack([shift, (shift <= MAX_FIXED_SHIFT).astype(F32)]).astype(F32)
    vec = pl.BlockSpec((1, A_DQK), lambda h, i: (0, 0))
    nq = s // TQ
    n_steps = A_HEADS * nq
    return pl.pallas_call(
        _attn_kernel,
        grid=(A_HEADS, nq),
        in_specs=[
            pl.BlockSpec(memory_space=pltpu.SMEM),
            pl.BlockSpec((s, 2 * A_DQK), lambda h, i: (0, qb + h)),
            pl.BlockSpec((s, 2 * A_DQK), lambda h, i: (0, kb + h)),
            pl.BlockSpec((s, A_DV), lambda h, i: (0, vb + h)),
            vec, vec, vec, vec,
            pl.BlockSpec((1, 1, A_DV), lambda h, i: (h, 0, 0)),
            _row_slab(w_a, n_steps, lambda h, i: h * nq + i),
            _row_slab(w_b, n_steps, lambda h, i: h * nq + i),
        ],
        out_specs=[
            pl.BlockSpec((TQ, A_DV), lambda h, i: (i, h)),
            _row_slab(w_a, n_steps, lambda h, i: h * nq + i),
            _row_slab(w_b, n_steps, lambda h, i: h * nq + i),
        ],
        out_shape=[
            jax.ShapeDtypeStruct((s, A_HEADS * A_DV), BF16),
            jax.ShapeDtypeStruct(w_a.shape, BF16),
            jax.ShapeDtypeStruct(w_b.shape, BF16),
        ],
        scratch_shapes=[
            pltpu.VMEM((2, TQ, 128), F32),
            pltpu.VMEM((2, TQ, 128), F32),
            pltpu.VMEM((2, TQ, A_DV), F32),
            pltpu.VMEM((2, TQ, TK), F32),
            pltpu.VMEM((2, TQ, TK), F32),
        ],
        compiler_params=_cparams(("arbitrary", "arbitrary")),
        name="diffattn",
    )(params, proj, proj, proj, lq1, lk1, lq2, lk2, a_norm_w3, w_a, w_b)


def _outproj_kernel(hm_ref, ha_ref, wt_ref, wb_ref, x_ref, out_ref):
    out_ref[...] = (x_ref[...]
                    + jnp.dot(hm_ref[...], wt_ref[...], preferred_element_type=F32)
                    + jnp.dot(ha_ref[...], wb_ref[...], preferred_element_type=F32))


def _outproj(hm, ha, w_out, x2):
    s = x2.shape[0]
    return pl.pallas_call(
        _outproj_kernel,
        grid=(s // TM_OUT,),
        in_specs=[
            pl.BlockSpec((TM_OUT, SEC), lambda i: (i, 0)),
            pl.BlockSpec((TM_OUT, SEC), lambda i: (i, 0)),
            pl.BlockSpec((SEC, D_MODEL), lambda i: (0, 0)),
            pl.BlockSpec((SEC, D_MODEL), lambda i: (1, 0)),
            pl.BlockSpec((TM_OUT, D_MODEL), lambda i: (i, 0)),
        ],
        out_specs=pl.BlockSpec((TM_OUT, D_MODEL), lambda i: (i, 0)),
        out_shape=jax.ShapeDtypeStruct((s, D_MODEL), F32),
        compiler_params=_cparams(("arbitrary",)),
        name="outproj",
    )(hm, ha, w_out, w_out, x2)


def _ffn_kernel(x_ref, nw_ref, wg_ref, wu_ref, wd_ref, out_ref, h_scr):
    j = pl.program_id(1)

    @pl.when(j == 0)
    def _():
        xf = x_ref[...]
        ms = jnp.mean(xf * xf, axis=1, keepdims=True)
        h_scr[...] = (xf * lax.rsqrt(ms + RMS_EPS) * nw_ref[...]).astype(BF16)
        out_ref[...] = xf

    h = h_scr[...]
    g = jnp.dot(h, wg_ref[...], preferred_element_type=F32)
    u = jnp.dot(h, wu_ref[...], preferred_element_type=F32)
    a = (g * jax.nn.sigmoid(g) * u).astype(BF16)
    out_ref[...] += jnp.dot(a, wd_ref[...], preferred_element_type=F32)


def _ffn(x1, nw, w_gate, w_up, w_down):
    s = x1.shape[0]
    return pl.pallas_call(
        _ffn_kernel,
        grid=(s // TM_FF, D_FF // TF_FF),
        in_specs=[
            pl.BlockSpec((TM_FF, D_MODEL), lambda i, j: (i, 0)),
            pl.BlockSpec((1, D_MODEL), lambda i, j: (0, 0)),
            pl.BlockSpec((D_MODEL, TF_FF), lambda i, j: (0, j)),
            pl.BlockSpec((D_MODEL, TF_FF), lambda i, j: (0, j)),
            pl.BlockSpec((TF_FF, D_MODEL), lambda i, j: (j, 0)),
        ],
        out_specs=pl.BlockSpec((TM_FF, D_MODEL), lambda i, j: (i, 0)),
        out_shape=jax.ShapeDtypeStruct((s, D_MODEL), F32),
        scratch_shapes=[pltpu.VMEM((TM_FF, D_MODEL), BF16)],
        compiler_params=_cparams(("arbitrary", "arbitrary")),
        name="swiglu",
    )(x1, nw, w_gate, w_up, w_down)


def _layer(x2, norm1_w, w_in, conv_w, conv_b, i_bias, f_bias, m_norm_w, q_norm_w, k_norm_w,
           lambda_q1, lambda_k1, lambda_q2, lambda_k2, a_norm_w, w_out, norm2_w,
           w_gate, w_up, w_down):
    s = x2.shape[0]
    wt_lo, wt_hi, wt_g = _wprep(w_in.T)
    gbias = jnp.pad(jnp.concatenate([i_bias, f_bias]), (0, GATE_PAD - N_GATES))[None, :]

    proj, gates_t = _inproj(x2, norm1_w[None, :], wt_lo, wt_hi, wt_g, gbias,
                            q_norm_w[None, :], k_norm_w[None, :])
    gates3 = gates_t.reshape(N_GATES, s // CHUNK, CHUNK)
    hm, w_out_b, w_down_b = _mlstm(proj, gates3, conv_w, conv_b[None, :],
                                   m_norm_w.reshape(1, SEC), w_out, w_down)
    ha, w_gate_b, w_up_b = _attn(proj, q_norm_w, k_norm_w, lambda_q1[None, :],
                                 lambda_k1[None, :], lambda_q2[None, :], lambda_k2[None, :],
                                 a_norm_w[:, None, :], w_gate, w_up)
    x1 = _outproj(hm, ha, w_out_b, x2)
    return _ffn(x1, norm2_w[None, :], w_gate_b, w_up_b, w_down_b)


def kernel(x, norm1_w, w_in, conv_w, conv_b, i_bias, f_bias, m_norm_w, q_norm_w, k_norm_w,
           lambda_q1, lambda_k1, lambda_q2, lambda_k2, a_norm_w, w_out, norm2_w,
           w_gate, w_up, w_down):
    b, s, d = x.shape
    assert d == D_MODEL and b == 1 and norm1_w.shape[0] == 1
    assert s % R_ML == 0 and s % TQ == 0 and s % TM_IN == 0 and s % TM_FF == 0
    y = _layer(x.reshape(s, d), norm1_w[0], w_in[0], conv_w[0], conv_b[0], i_bias[0], f_bias[0],
               m_norm_w[0], q_norm_w[0], k_norm_w[0], lambda_q1[0], lambda_k1[0],
               lambda_q2[0], lambda_k2[0], a_norm_w[0], w_out[0], norm2_w[0],
               w_gate[0], w_up[0], w_down[0])
    return y.reshape(b, s, d)
```

```python
import math

import jax
import jax.numpy as jnp
from jax import lax
from jax.experimental import pallas as pl
from jax.experimental.pallas import tpu as pltpu

F32 = jnp.float32
BF16 = jnp.bfloat16

D_MODEL = 2048
CHUNK = 64
SEC = 1024
N_SEC_LO = 3
M_HEADS = 4
M_DV = 256
M_DQK = 128
CONV_W = 4
A_HEADS = 4
A_DV = 256
A_DQK = 128
D_FF = 5632
RMS_EPS = 1e-6
NEG = -1e30
LAM_INIT = 0.8 - 0.6 * math.exp(-0.3 * 0)
LOG2E = 1.4426950408889634
MAX_FIXED_SHIFT = 60.0
GATE_PAD = 128
N_GATES = 2 * M_HEADS
HALO = 8
BF16_SUBLANES = 16

TR_PREP = 256
TM_IN = 1024
TM_OUT = 512
TM_FF = 1024
TF_FF = 512
R_ML = 1024
TQ = 1024
TK = TQ // 2

VMEM_LIMIT = 56 * 1024 * 1024


def _cparams(sem):
    return pltpu.CompilerParams(dimension_semantics=sem, vmem_limit_bytes=VMEM_LIMIT)


def _wprep_kernel(lo_ref, hi_ref, hi_next_ref, lo_out, hi_out, g_out):
    lo_out[...] = lo_ref[...].astype(BF16)
    hi_out[...] = jnp.concatenate([hi_ref[N_GATES:, :], hi_next_ref[...]], axis=0).astype(BF16)

    @pl.when(pl.program_id(0) == 0)
    def _():
        g_out[...] = hi_ref[0:GATE_PAD, :].astype(BF16)


def _wprep(wt):
    n_in, d = wt.shape
    g0 = N_SEC_LO * SEC
    assert n_in == 2 * g0 + N_GATES and g0 % TR_PREP == 0 and TR_PREP % N_GATES == 0
    n_steps = g0 // TR_PREP
    return pl.pallas_call(
        _wprep_kernel,
        grid=(n_steps,),
        in_specs=[
            pl.BlockSpec((TR_PREP, d), lambda k: (k, 0)),
            pl.BlockSpec((TR_PREP, d), lambda k: (n_steps + k, 0)),
            pl.BlockSpec((N_GATES, d), lambda k: ((n_steps + k + 1) * (TR_PREP // N_GATES), 0)),
        ],
        out_specs=[
            pl.BlockSpec((TR_PREP, d), lambda k: (k, 0)),
            pl.BlockSpec((TR_PREP, d), lambda k: (k, 0)),
            pl.BlockSpec((GATE_PAD, d), lambda k: (0, 0)),
        ],
        out_shape=[
            jax.ShapeDtypeStruct((g0, d), BF16),
            jax.ShapeDtypeStruct((g0, d), BF16),
            jax.ShapeDtypeStruct((GATE_PAD, d), BF16),
        ],
        compiler_params=_cparams(("arbitrary",)),
        name="wprep",
    )(wt, wt, wt)


def _dot_nt(a, b_t):
    return lax.dot_general(a, b_t, (((1,), (1,)), ((), ())), preferred_element_type=F32)


def _inproj_kernel(x_ref, nw_ref, wlo_ref, whi_ref, wg_ref, gb_ref, qw_ref, kw_ref,
                   out_ref, gt_ref, h_scr):
    j = pl.program_id(1)

    @pl.when(j == 0)
    def _():
        xf = x_ref[...]
        ms = jnp.mean(xf * xf, axis=1, keepdims=True)
        hb = (xf * lax.rsqrt(ms + RMS_EPS) * nw_ref[...]).astype(BF16)
        h_scr[...] = hb
        g = _dot_nt(hb, wg_ref[...]) + gb_ref[...]
        gt_ref[...] = g.T[0:N_GATES, :]

    is_q = j == N_SEC_LO
    is_k = j == N_SEC_LO + 1
    is_qk = jnp.logical_or(is_q, is_k)

    @pl.when(j < N_SEC_LO)
    def _():
        out_ref[...] = _dot_nt(h_scr[...], wlo_ref[...]).astype(BF16)

    @pl.when(j > N_SEC_LO + 1)
    def _():
        out_ref[...] = _dot_nt(h_scr[...], whi_ref[...]).astype(BF16)

    @pl.when(is_qk)
    def _():
        acc = _dot_nt(h_scr[...], whi_ref[...])
        w = jnp.where(is_q, qw_ref[...] * (A_DQK ** -0.5 * LOG2E), kw_ref[...])
        for g in range(SEC // A_DQK):
            sl = slice(g * A_DQK, (g + 1) * A_DQK)
            a = acc[:, sl]
            ms = jnp.mean(a * a, axis=1, keepdims=True)
            out_ref[:, sl] = (a * lax.rsqrt(ms + RMS_EPS) * w).astype(BF16)


def _inproj(x2, nw, wt_lo, wt_hi, wt_g, gbias, qw, kw):
    s = x2.shape[0]
    n_hi = wt_hi.shape[0] // SEC
    n_sec = N_SEC_LO + n_hi
    return pl.pallas_call(
        _inproj_kernel,
        grid=(s // TM_IN, n_sec),
        in_specs=[
            pl.BlockSpec((TM_IN, D_MODEL), lambda i, j: (i, 0)),
            pl.BlockSpec((1, D_MODEL), lambda i, j: (0, 0)),
            pl.BlockSpec((SEC, D_MODEL), lambda i, j: (jnp.minimum(j, N_SEC_LO - 1), 0)),
            pl.BlockSpec((SEC, D_MODEL), lambda i, j: (jnp.maximum(j - N_SEC_LO, 0), 0)),
            pl.BlockSpec((GATE_PAD, D_MODEL), lambda i, j: (0, 0)),
            pl.BlockSpec((1, GATE_PAD), lambda i, j: (0, 0)),
            pl.BlockSpec((1, A_DQK), lambda i, j: (0, 0)),
            pl.BlockSpec((1, A_DQK), lambda i, j: (0, 0)),
        ],
        out_specs=[
            pl.BlockSpec((TM_IN, SEC), lambda i, j: (i, j)),
            pl.BlockSpec((N_GATES, TM_IN), lambda i, j: (0, i)),
        ],
        out_shape=[
            jax.ShapeDtypeStruct((s, n_sec * SEC), BF16),
            jax.ShapeDtypeStruct((N_GATES, s), F32),
        ],
        scratch_shapes=[
            pltpu.VMEM((TM_IN, D_MODEL), BF16),
        ],
        compiler_params=_cparams(("arbitrary", "arbitrary")),
        name="inproj",
    )(x2, nw, wt_lo, wt_hi, wt_g, gbias, qw, kw)


def _log_sigmoid(x):
    return jnp.minimum(x, 0.0) - jnp.log1p(jnp.exp(-jnp.abs(x)))


def _mlstm_kernel(qk_ref, halo_ref, v_ref, o_ref, g_ref, cw_ref, cb_ref, nw_ref,
                  wa_ref, wb_ref,
                  out_ref, wa_out_ref, wb_out_ref,
                  stage, act, b_scr, li_scr, ct_scr, n_scr, m_scr):
    i = pl.program_id(0)
    n_chunks = R_ML // CHUNK

    wa_out_ref[...] = wa_ref[...].astype(BF16)
    wb_out_ref[...] = wb_ref[...].astype(BF16)

    @pl.when(i == 0)
    def _():
        ct_scr[...] = jnp.zeros_like(ct_scr)
        n_scr[...] = jnp.zeros_like(n_scr)
        m_scr[...] = jnp.full_like(m_scr, NEG)

    halo = halo_ref[...].astype(F32)
    stage[0:HALO, :] = jnp.where(i == 0, 0.0, halo)
    stage[HALO:HALO + R_ML, :] = qk_ref[...].astype(F32)
    k_scale = M_DQK ** -0.5
    for cs in range(SEC // 128):
        sl = slice(cs * 128, (cs + 1) * 128)
        y = cb_ref[:, sl]
        for t in range(CONV_W):
            y = y + cw_ref[t:t + 1, sl] * stage[HALO - (CONV_W - 1) + t:HALO - (CONV_W - 1) + t + R_ML, sl]
        a = y * jax.nn.sigmoid(y)
        if cs >= M_HEADS:
            a = a * k_scale
        act[:, sl] = a.astype(BF16)

    li_scr[...] = g_ref[0:M_HEADS].reshape(M_HEADS * n_chunks, CHUNK)
    lf = _log_sigmoid(g_ref[M_HEADS:N_GATES].reshape(M_HEADS * n_chunks, CHUNK))
    r_i = lax.broadcasted_iota(jnp.int32, (CHUNK, CHUNK), 0)
    c_i = lax.broadcasted_iota(jnp.int32, (CHUNK, CHUNK), 1)
    upper = jnp.where(r_i <= c_i, 1.0, 0.0).astype(BF16)
    p0 = lf.astype(BF16)
    r1 = lf - p0.astype(F32)
    p1 = r1.astype(BF16)
    p2 = (r1 - p1.astype(F32)).astype(BF16)
    b_scr[...] = (jnp.dot(p0, upper, preferred_element_type=F32)
                  + jnp.dot(p1, upper, preferred_element_type=F32)
                  + jnp.dot(p2, upper, preferred_element_type=F32))

    t_i = lax.broadcasted_iota(jnp.int32, (R_ML, CHUNK), 0) % CHUNK
    s_i = lax.broadcasted_iota(jnp.int32, (R_ML, CHUNK), 1)
    eye = t_i == s_i
    tril = t_i >= s_i

    def rows_of(x, c):
        return x[c * CHUNK:(c + 1) * CHUNK]

    def per_chunk_rows(x):
        return jnp.concatenate(
            [jnp.broadcast_to(x[c:c + 1, :], (CHUNK, x.shape[1])) for c in range(n_chunks)], axis=0)

    def to_col(rows):
        return jnp.sum(jnp.where(eye, rows, 0.0), axis=1, keepdims=True)

    for h in range(M_HEADS):
        qs = slice(h * M_DQK, (h + 1) * M_DQK)
        ks = slice((M_HEADS + h) * M_DQK, (M_HEADS + h + 1) * M_DQK)
        vs = slice(h * M_DV, (h + 1) * M_DV)
        q = act[:, qs]
        k = act[:, ks]
        v = v_ref[:, vs]
        b_h = b_scr[h * n_chunks:(h + 1) * n_chunks, :]
        li_h = li_scr[h * n_chunks:(h + 1) * n_chunks, :]

        b_rows = per_chunk_rows(b_h)
        b_col = to_col(b_rows)
        d = jnp.where(tril, b_col - b_rows + per_chunk_rows(li_h), NEG)
        m_intra = jnp.max(d, axis=1, keepdims=True)
        s = jnp.concatenate(
            [lax.dot_general(rows_of(q, c), rows_of(k, c), (((1,), (1,)), ((), ())),
                             preferred_element_type=F32) for c in range(n_chunks)], axis=0)
        p = jnp.exp(d - m_intra) * s
        row_sum = jnp.sum(p, axis=1, keepdims=True)
        pb = p.astype(BF16)
        n_intra = jnp.concatenate(
            [jnp.dot(rows_of(pb, c), rows_of(v, c), preferred_element_type=F32)
             for c in range(n_chunks)], axis=0)

        g_tot = b_h[:, CHUNK - 1:CHUNK]
        a = g_tot - b_h + li_h
        m_loc = jnp.max(a, axis=1, keepdims=True)
        w_col = to_col(per_chunk_rows(jnp.exp(a - m_loc)))
        kw = k.astype(F32) * w_col
        kwb = kw.astype(BF16)
        ct_locs = [lax.dot_general(rows_of(kwb, c), rows_of(v, c), (((0,), (0,)), ((), ())),
                                   preferred_element_type=F32) for c in range(n_chunks)]
        n_locs = [jnp.sum(rows_of(kw, c), axis=0, keepdims=True) for c in range(n_chunks)]

        m = m_scr[h, 0:1, 0:1]
        ct = ct_scr[h]
        n = n_scr[h, 0:1, :]
        m_prevs, ct_prevs, n_prevs = [], [], []
        for c in range(n_chunks):
            m_prevs.append(m)
            ct_prevs.append(ct.astype(BF16))
            n_prevs.append(n)
            g_c = g_tot[c:c + 1, :]
            m_loc_c = m_loc[c:c + 1, :]
            m_new = jnp.maximum(g_c + m, m_loc_c)
            s_old = jnp.exp(g_c + m - m_new)
            s_loc = jnp.exp(m_loc_c - m_new)
            ct = s_old * ct + s_loc * ct_locs[c]
            n = s_old * n + s_loc * n_locs[c]
            m = m_new
        ct_scr[h] = ct
        n_scr[h] = jnp.broadcast_to(n, (8, M_DQK))
        m_scr[h] = jnp.broadcast_to(m, (8, 128))

        m_prev = per_chunk_rows(jnp.concatenate(m_prevs, axis=0))
        n_prev = per_chunk_rows(jnp.concatenate(n_prevs, axis=0))
        inter = jnp.concatenate(
            [jnp.dot(rows_of(q, c), ct_prevs[c], preferred_element_type=F32)
             for c in range(n_chunks)], axis=0)
        inter_log = b_col + m_prev
        m_t = jnp.maximum(inter_log, m_intra)
        s_intra = jnp.exp(m_intra - m_t)
        s_inter = jnp.exp(inter_log - m_t)
        num = s_intra * n_intra + s_inter * inter
        den = (s_intra * row_sum
               + s_inter * jnp.sum(q.astype(F32) * n_prev, axis=1, keepdims=True))
        hout = num / jnp.maximum(jnp.abs(den), jnp.exp(-m_t))

        ms = jnp.mean(hout * hout, axis=1, keepdims=True)
        hn = hout * lax.rsqrt(ms + RMS_EPS) * nw_ref[:, vs]
        out_ref[:, vs] = (hn * jax.nn.sigmoid(o_ref[:, vs].astype(F32))).astype(BF16)


def _row_slab(w, n_steps, step_of):
    rows, cols = w.shape
    assert rows % (n_steps * BF16_SUBLANES) == 0, (w.shape, n_steps)
    return pl.BlockSpec((rows // n_steps, cols), lambda *idx: (step_of(*idx), 0))


def _mlstm(proj, gates3, conv_w, conv_b, m_norm_w, w_a, w_b):
    s = proj.shape[0]
    n_chunks = R_ML // CHUNK
    hb = R_ML // HALO
    n_steps = s // R_ML
    return pl.pallas_call(
        _mlstm_kernel,
        grid=(n_steps,),
        in_specs=[
            pl.BlockSpec((R_ML, SEC), lambda i: (i, 0)),
            pl.BlockSpec((HALO, SEC), lambda i: (jnp.maximum(i * hb - 1, 0), 0)),
            pl.BlockSpec((R_ML, SEC), lambda i: (i, 1)),
            pl.BlockSpec((R_ML, SEC), lambda i: (i, 2)),
            pl.BlockSpec((N_GATES, n_chunks, CHUNK), lambda i: (0, i, 0)),
            pl.BlockSpec((CONV_W, SEC), lambda i: (0, 0)),
            pl.BlockSpec((1, SEC), lambda i: (0, 0)),
            pl.BlockSpec((1, SEC), lambda i: (0, 0)),
            _row_slab(w_a, n_steps, lambda i: i),
            _row_slab(w_b, n_steps, lambda i: i),
        ],
        out_specs=[
            pl.BlockSpec((R_ML, SEC), lambda i: (i, 0)),
            _row_slab(w_a, n_steps, lambda i: i),
            _row_slab(w_b, n_steps, lambda i: i),
        ],
        out_shape=[
            jax.ShapeDtypeStruct((s, SEC), BF16),
            jax.ShapeDtypeStruct(w_a.shape, BF16),
            jax.ShapeDtypeStruct(w_b.shape, BF16),
        ],
        scratch_shapes=[
            pltpu.VMEM((R_ML + HALO, SEC), F32),
            pltpu.VMEM((R_ML, SEC), BF16),
            pltpu.VMEM((M_HEADS * n_chunks, CHUNK), F32),
            pltpu.VMEM((M_HEADS * n_chunks, CHUNK), F32),
            pltpu.VMEM((M_HEADS, M_DQK, M_DV), F32),
            pltpu.VMEM((M_HEADS, 8, M_DQK), F32),
            pltpu.VMEM((M_HEADS, 8, 128), F32),
        ],
        compiler_params=_cparams(("arbitrary",)),
        name="mlstm",
    )(proj, proj, proj, proj, gates3, conv_w, conv_b, m_norm_w, w_a, w_b)


def _lane_partial_sum(p):
    acc = p[:, 0:128]
    for t in range(1, p.shape[1] // 128):
        acc = acc + p[:, t * 128:(t + 1) * 128]
    return acc


def _attn_kernel(par_ref, q_ref, k_ref, v_ref, lq1_ref, lk1_ref, lq2_ref, lk2_ref, nw_ref,
                 wa_ref, wb_ref,
                 out_ref, wa_out_ref, wb_out_ref,
                 m_scr, l_scr, acc_scr, sa_scr, sb_scr):
    wa_out_ref[...] = wa_ref[...].astype(BF16)
    wb_out_ref[...] = wb_ref[...].astype(BF16)

    i = pl.program_id(1)
    nq = pl.num_programs(1)
    shift = par_ref[0]
    fixed = par_ref[1] > 0.5

    l_scr[...] = jnp.zeros_like(l_scr)
    acc_scr[...] = jnp.zeros_like(acc_scr)

    ALL, LOWER = slice(0, TQ), slice(TK, TQ)

    def chunk_mask(n_rows):
        qc = lax.broadcasted_iota(jnp.int32, (n_rows, TK), 0) // CHUNK
        kc = lax.broadcasted_iota(jnp.int32, (n_rows, TK), 1) // CHUNK
        return kc <= qc

    def k_rows(j):
        return pl.ds(pl.multiple_of(j * TK, TK), TK)

    def scores(qi, j, c, rows):
        q0 = pl.multiple_of(qi * TQ + rows.start, TK)
        q = q_ref[pl.ds(q0, rows.stop - rows.start), c * A_DQK:(c + 1) * A_DQK]
        k = k_ref[k_rows(j), c * A_DQK:(c + 1) * A_DQK]
        return lax.dot_general(q, k, (((1,), (1,)), ((), ())), preferred_element_type=F32)

    def fixed_scores(qi, j, s_buf, rows=ALL):
        for c in range(2):
            s_buf[c, rows, :] = scores(qi, j, c, rows)

    def fixed_pv(j, s_buf, rows=ALL, masked=False):
        v = v_ref[k_rows(j), :]
        for c in range(2):
            p = jnp.exp2(s_buf[c, rows, :] - shift)
            if masked:
                p = jnp.where(chunk_mask(rows.stop - rows.start), p, 0.0)
            l_scr[c, rows, :] += _lane_partial_sum(p)
            acc_scr[c, rows, :] += jnp.dot(p.astype(BF16), v, preferred_element_type=F32)

    def online_tile(j, rows=ALL, masked=False):
        v = v_ref[k_rows(j), :]
        for c in range(2):
            s = scores(i, j, c, rows)
            if masked:
                s = jnp.where(chunk_mask(rows.stop - rows.start), s, NEG)
            m_prev = m_scr[c, rows, :]
            m_new = jnp.maximum(m_prev, jnp.max(s, axis=1, keepdims=True))
            alpha = jnp.exp2(m_prev - m_new)
            p = jnp.exp2(s - m_new[:, 0:1])
            l_scr[c, rows, :] = alpha * l_scr[c, rows, :] + _lane_partial_sum(p)
            acc_scr[c, rows, :] = (alpha[:, 0:1] * acc_scr[c, rows, :]
                                   + jnp.dot(p.astype(BF16), v, preferred_element_type=F32))
            m_scr[c, rows, :] = m_new

    @pl.when(fixed)
    def _():
        @pl.when(i == 0)
        def _():
            fixed_scores(0, 0, sa_scr)

        def pair(t, carry):
            j = 2 * t
            fixed_pv(j, sa_scr)
            fixed_scores(i, j + 1, sb_scr)
            fixed_pv(j + 1, sb_scr)
            fixed_scores(i, j + 2, sa_scr)
            return carry
        lax.fori_loop(0, i, pair, 0)

        def boundary_tiles(more_steps):
            fixed_pv(2 * i, sa_scr, masked=True)
            fixed_scores(i, 2 * i + 1, sb_scr, LOWER)
            fixed_pv(2 * i + 1, sb_scr, LOWER, masked=True)
            if more_steps:
                fixed_scores(i + 1, 0, sa_scr)

        @pl.when(i + 1 < nq)
        def _():
            boundary_tiles(True)

        @pl.when(i + 1 == nq)
        def _():
            boundary_tiles(False)

    @pl.when(jnp.logical_not(fixed))
    def _():
        m_scr[...] = jnp.full_like(m_scr, NEG)

        def body(j, carry):
            online_tile(j)
            return carry
        lax.fori_loop(0, 2 * i, body, 0)
        online_tile(2 * i, masked=True)
        online_tile(2 * i + 1, LOWER, masked=True)

    lam = (jnp.exp(jnp.sum(lq1_ref[...] * lk1_ref[...], axis=1, keepdims=True))
           - jnp.exp(jnp.sum(lq2_ref[...] * lk2_ref[...], axis=1, keepdims=True)) + LAM_INIT)
    inv0 = 1.0 / jnp.sum(l_scr[0], axis=1, keepdims=True)
    inv1 = lam / jnp.sum(l_scr[1], axis=1, keepdims=True)
    o = acc_scr[0] * inv0 - acc_scr[1] * inv1
    ms = jnp.mean(o * o, axis=1, keepdims=True)
    scale = lax.rsqrt(ms + RMS_EPS) * (1.0 - LAM_INIT)
    nw = nw_ref[pl.ds(pl.program_id(0), 1), :]
    out_ref[...] = (o * scale * nw).astype(BF16)


def _attn(proj, q_norm_w, k_norm_w, lq1, lk1, lq2, lk2, a_norm_w, w_a, w_b):
    s = proj.shape[0]
    qb = 3 * SEC // A_DV
    kb = 4 * SEC // A_DV
    vb = 5 * SEC // A_DV
    bound = (A_DQK ** 0.5 * LOG2E) * jnp.max(jnp.abs(q_norm_w * k_norm_w))
    shift = jnp.ceil(bound * 1.02) + 1.0
    params = jnp.stack([shift, (shift <= MAX_FIXED_SHIFT).astype(F32)]).astype(F32)
    vec = pl.BlockSpec((1, A_DQK), lambda h, i: (0, 0))
    nq = s // TQ
    n_steps = A_HEADS * nq
    return pl.pallas_call(
        _attn_kernel,
        grid=(A_HEADS, nq),
        in_specs=[
            pl.BlockSpec(memory_space=pltpu.SMEM),
            pl.BlockSpec((s, 2 * A_DQK), lambda h, i: (0, qb + h)),
            pl.BlockSpec((s, 2 * A_DQK), lambda h, i: (0, kb + h)),
            pl.BlockSpec((s, A_DV), lambda h, i: (0, vb + h)),
            vec, vec, vec, vec,
            pl.BlockSpec((A_HEADS, A_DV), lambda h, i: (0, 0)),
            _row_slab(w_a, n_steps, lambda h, i: h * nq + i),
            _row_slab(w_b, n_steps, lambda h, i: h * nq + i),
        ],
        out_specs=[
            pl.BlockSpec((TQ, A_DV), lambda h, i: (i, h)),
            _row_slab(w_a, n_steps, lambda h, i: h * nq + i),
            _row_slab(w_b, n_steps, lambda h, i: h * nq + i),
        ],
        out_shape=[
            jax.ShapeDtypeStruct((s, A_HEADS * A_DV), BF16),
            jax.ShapeDtypeStruct(w_a.shape, BF16),
            jax.ShapeDtypeStruct(w_b.shape, BF16),
        ],
        scratch_shapes=[
            pltpu.VMEM((2, TQ, 128), F32),
            pltpu.VMEM((2, TQ, 128), F32),
            pltpu.VMEM((2, TQ, A_DV), F32),
            pltpu.VMEM((2, TQ, TK), F32),
            pltpu.VMEM((2, TQ, TK), F32),
        ],
        compiler_params=_cparams(("arbitrary", "arbitrary")),
        name="diffattn",
    )(params, proj, proj, proj, lq1, lk1, lq2, lk2, a_norm_w, w_a, w_b)


def _outproj_kernel(hm_ref, ha_ref, wt_ref, wb_ref, x_ref, out_ref):
    out_ref[...] = (x_ref[...]
                    + jnp.dot(hm_ref[...], wt_ref[...], preferred_element_type=F32)
                    + jnp.dot(ha_ref[...], wb_ref[...], preferred_element_type=F32))


def _outproj(hm, ha, w_out, x2):
    s = x2.shape[0]
    return pl.pallas_call(
        _outproj_kernel,
        grid=(s // TM_OUT,),
        in_specs=[
            pl.BlockSpec((TM_OUT, SEC), lambda i: (i, 0)),
            pl.BlockSpec((TM_OUT, SEC), lambda i: (i, 0)),
            pl.BlockSpec((SEC, D_MODEL), lambda i: (0, 0)),
            pl.BlockSpec((SEC, D_MODEL), lambda i: (1, 0)),
            pl.BlockSpec((TM_OUT, D_MODEL), lambda i: (i, 0)),
        ],
        out_specs=pl.BlockSpec((TM_OUT, D_MODEL), lambda i: (i, 0)),
        out_shape=jax.ShapeDtypeStruct((s, D_MODEL), F32),
        compiler_params=_cparams(("arbitrary",)),
        name="outproj",
    )(hm, ha, w_out, w_out, x2)


def _ffn_kernel(x_ref, nw_ref, wg_ref, wu_ref, wd_ref, out_ref, h_scr):
    j = pl.program_id(1)

    @pl.when(j == 0)
    def _():
        xf = x_ref[...]
        ms = jnp.mean(xf * xf, axis=1, keepdims=True)
        h_scr[...] = (xf * lax.rsqrt(ms + RMS_EPS) * nw_ref[...]).astype(BF16)
        out_ref[...] = xf

    h = h_scr[...]
    g = jnp.dot(h, wg_ref[...], preferred_element_type=F32)
    u = jnp.dot(h, wu_ref[...], preferred_element_type=F32)
    a = (g * jax.nn.sigmoid(g) * u).astype(BF16)
    out_ref[...] += jnp.dot(a, wd_ref[...], preferred_element_type=F32)


def _ffn(x1, nw, w_gate, w_up, w_down):
    s = x1.shape[0]
    return pl.pallas_call(
        _ffn_kernel,
        grid=(s // TM_FF, D_FF // TF_FF),
        in_specs=[
            pl.BlockSpec((TM_FF, D_MODEL), lambda i, j: (i, 0)),
            pl.BlockSpec((1, D_MODEL), lambda i, j: (0, 0)),
            pl.BlockSpec((D_MODEL, TF_FF), lambda i, j: (0, j)),
            pl.BlockSpec((D_MODEL, TF_FF), lambda i, j: (0, j)),
            pl.BlockSpec((TF_FF, D_MODEL), lambda i, j: (j, 0)),
        ],
        out_specs=pl.BlockSpec((TM_FF, D_MODEL), lambda i, j: (i, 0)),
        out_shape=jax.ShapeDtypeStruct((s, D_MODEL), F32),
        scratch_shapes=[pltpu.VMEM((TM_FF, D_MODEL), BF16)],
        compiler_params=_cparams(("arbitrary", "arbitrary")),
        name="swiglu",
    )(x1, nw, w_gate, w_up, w_down)


def _layer(x2, norm1_w, w_in, conv_w, conv_b, i_bias, f_bias, m_norm_w, q_norm_w, k_norm_w,
           lambda_q1, lambda_k1, lambda_q2, lambda_k2, a_norm_w, w_out, norm2_w,
           w_gate, w_up, w_down):
    s = x2.shape[0]
    wt_lo, wt_hi, wt_g = _wprep(w_in.T)
    gbias = jnp.pad(jnp.concatenate([i_bias, f_bias]), (0, GATE_PAD - N_GATES))[None, :]

    proj, gates_t = _inproj(x2, norm1_w[None, :], wt_lo, wt_hi, wt_g, gbias,
                            q_norm_w[None, :], k_norm_w[None, :])
    gates3 = gates_t.reshape(N_GATES, s // CHUNK, CHUNK)
    hm, w_out_b, w_down_b = _mlstm(proj, gates3, conv_w, conv_b[None, :],
                                   m_norm_w.reshape(1, SEC), w_out, w_down)
    ha, w_gate_b, w_up_b = _attn(proj, q_norm_w, k_norm_w, lambda_q1[None, :],
                                 lambda_k1[None, :], lambda_q2[None, :], lambda_k2[None, :],
                                 a_norm_w, w_gate, w_up)
    x1 = _outproj(hm, ha, w_out_b, x2)
    return _ffn(x1, norm2_w[None, :], w_gate_b, w_up_b, w_down_b)


def kernel(x, norm1_w, w_in, conv_w, conv_b, i_bias, f_bias, m_norm_w, q_norm_w, k_norm_w,
           lambda_q1, lambda_k1, lambda_q2, lambda_k2, a_norm_w, w_out, norm2_w,
           w_gate, w_up, w_down):
    b, s, d = x.shape
    assert d == D_MODEL and b == 1 and norm1_w.shape[0] == 1
    assert s % R_ML == 0 and s % TQ == 0 and s % TM_IN == 0 and s % TM_FF == 0
    y = _layer(x.reshape(s, d), norm1_w[0], w_in[0], conv_w[0], conv_b[0], i_bias[0], f_bias[0],
               m_norm_w[0], q_norm_w[0], k_norm_w[0], lambda_q1[0], lambda_k1[0],
               lambda_q2[0], lambda_k2[0], a_norm_w[0], w_out[0], norm2_w[0],
               w_gate[0], w_up[0], w_down[0])
    return y.reshape(b, s, d)
```

```python
import math

import jax
import jax.numpy as jnp
from jax import lax
from jax.experimental import pallas as pl
from jax.experimental.pallas import tpu as pltpu

F32 = jnp.float32
BF16 = jnp.bfloat16

D_MODEL = 2048
CHUNK = 64
SEC = 1024
N_SEC_LO = 3
M_HEADS = 4
M_DV = 256
M_DQK = 128
CONV_W = 4
A_HEADS = 4
A_DV = 256
A_DQK = 128
D_FF = 5632
RMS_EPS = 1e-6
NEG = -1e30
LAM_INIT = 0.8 - 0.6 * math.exp(-0.3 * 0)
LOG2E = 1.4426950408889634
MAX_FIXED_SHIFT = 60.0
GATE_PAD = 128
N_GATES = 2 * M_HEADS
HALO = 8
BF16_SUBLANES = 16

TR_PREP = 256
TM_IN = 1024
TM_OUT = 512
TM_FF = 1024
TF_FF = 512
R_ML = 1024
TQ = 1024
TK = TQ // 2

VMEM_LIMIT = 56 * 1024 * 1024


def _cparams(sem):
    return pltpu.CompilerParams(dimension_semantics=sem, vmem_limit_bytes=VMEM_LIMIT)


def _wprep_kernel(lo_ref, hi_ref, hi_next_ref, lo_out, hi_out, g_out):
    lo_out[...] = lo_ref[...].astype(BF16)
    hi_out[...] = jnp.concatenate([hi_ref[N_GATES:, :], hi_next_ref[...]], axis=0).astype(BF16)

    @pl.when(pl.program_id(0) == 0)
    def _():
        g_out[...] = hi_ref[0:GATE_PAD, :].astype(BF16)


def _wprep(wt):
    n_in, d = wt.shape
    g0 = N_SEC_LO * SEC
    assert n_in == 2 * g0 + N_GATES and g0 % TR_PREP == 0 and TR_PREP % N_GATES == 0
    n_steps = g0 // TR_PREP
    return pl.pallas_call(
        _wprep_kernel,
        grid=(n_steps,),
        in_specs=[
            pl.BlockSpec((TR_PREP, d), lambda k: (k, 0)),
            pl.BlockSpec((TR_PREP, d), lambda k: (n_steps + k, 0)),
            pl.BlockSpec((N_GATES, d), lambda k: ((n_steps + k + 1) * (TR_PREP // N_GATES), 0)),
        ],
        out_specs=[
            pl.BlockSpec((TR_PREP, d), lambda k: (k, 0)),
            pl.BlockSpec((TR_PREP, d), lambda k: (k, 0)),
            pl.BlockSpec((GATE_PAD, d), lambda k: (0, 0)),
        ],
        out_shape=[
            jax.ShapeDtypeStruct((g0, d), BF16),
            jax.ShapeDtypeStruct((g0, d), BF16),
            jax.ShapeDtypeStruct((GATE_PAD, d), BF16),
        ],
        compiler_params=_cparams(("arbitrary",)),
        name="wprep",
    )(wt, wt, wt)


def _dot_nt(a, b_t):
    return lax.dot_general(a, b_t, (((1,), (1,)), ((), ())), preferred_element_type=F32)


def _inproj_kernel(x_ref, nw_ref, wlo_ref, whi_ref, wg_ref, ib_ref, fb_ref, qw_ref, kw_ref,
                   out_ref, gt_ref, h_scr):
    j = pl.program_id(1)

    @pl.when(j == 0)
    def _():
        xf = x_ref[...]
        ms = jnp.mean(xf * xf, axis=1, keepdims=True)
        hb = (xf * lax.rsqrt(ms + RMS_EPS) * nw_ref[...]).astype(BF16)
        h_scr[...] = hb
        row = lax.broadcasted_iota(jnp.int32, (N_GATES, 1), 0)
        bias = jnp.zeros((N_GATES, 1), F32)
        for r in range(N_GATES):
            b_r = ib_ref[r] if r < M_HEADS else fb_ref[r - M_HEADS]
            bias = jnp.where(row == r, b_r, bias)
        gt_ref[...] = _dot_nt(hb, wg_ref[...]).T[0:N_GATES, :] + bias

    is_q = j == N_SEC_LO
    is_k = j == N_SEC_LO + 1
    is_qk = jnp.logical_or(is_q, is_k)

    @pl.when(j < N_SEC_LO)
    def _():
        out_ref[...] = _dot_nt(h_scr[...], wlo_ref[...]).astype(BF16)

    @pl.when(j > N_SEC_LO + 1)
    def _():
        out_ref[...] = _dot_nt(h_scr[...], whi_ref[...]).astype(BF16)

    @pl.when(is_qk)
    def _():
        acc = _dot_nt(h_scr[...], whi_ref[...])
        w = jnp.where(is_q, qw_ref[...] * (A_DQK ** -0.5 * LOG2E), kw_ref[...])
        for g in range(SEC // A_DQK):
            sl = slice(g * A_DQK, (g + 1) * A_DQK)
            a = acc[:, sl]
            ms = jnp.mean(a * a, axis=1, keepdims=True)
            out_ref[:, sl] = (a * lax.rsqrt(ms + RMS_EPS) * w).astype(BF16)


def _inproj(x2, nw, wt_lo, wt_hi, wt_g, i_bias, f_bias, qw, kw):
    s = x2.shape[0]
    n_hi = wt_hi.shape[0] // SEC
    n_sec = N_SEC_LO + n_hi
    return pl.pallas_call(
        _inproj_kernel,
        grid=(s // TM_IN, n_sec),
        in_specs=[
            pl.BlockSpec((TM_IN, D_MODEL), lambda i, j: (i, 0)),
            pl.BlockSpec((1, D_MODEL), lambda i, j: (0, 0)),
            pl.BlockSpec((SEC, D_MODEL), lambda i, j: (jnp.minimum(j, N_SEC_LO - 1), 0)),
            pl.BlockSpec((SEC, D_MODEL), lambda i, j: (jnp.maximum(j - N_SEC_LO, 0), 0)),
            pl.BlockSpec((GATE_PAD, D_MODEL), lambda i, j: (0, 0)),
            pl.BlockSpec(memory_space=pltpu.SMEM),
            pl.BlockSpec(memory_space=pltpu.SMEM),
            pl.BlockSpec((1, A_DQK), lambda i, j: (0, 0)),
            pl.BlockSpec((1, A_DQK), lambda i, j: (0, 0)),
        ],
        out_specs=[
            pl.BlockSpec((TM_IN, SEC), lambda i, j: (i, j)),
            pl.BlockSpec((N_GATES, TM_IN), lambda i, j: (0, i)),
        ],
        out_shape=[
            jax.ShapeDtypeStruct((s, n_sec * SEC), BF16),
            jax.ShapeDtypeStruct((N_GATES, s), F32),
        ],
        scratch_shapes=[
            pltpu.VMEM((TM_IN, D_MODEL), BF16),
        ],
        compiler_params=_cparams(("arbitrary", "arbitrary")),
        name="inproj",
    )(x2, nw, wt_lo, wt_hi, wt_g, i_bias, f_bias, qw, kw)


def _log_sigmoid(x):
    return jnp.minimum(x, 0.0) - jnp.log1p(jnp.exp(-jnp.abs(x)))


def _mlstm_kernel(qk_ref, halo_ref, v_ref, o_ref, g_ref, cw_ref, cb_ref, nw_ref,
                  wa_ref, wb_ref,
                  out_ref, wa_out_ref, wb_out_ref,
                  stage, act, b_scr, li_scr, ct_scr, n_scr, m_scr):
    i = pl.program_id(0)
    n_chunks = R_ML // CHUNK

    wa_out_ref[...] = wa_ref[...].astype(BF16)
    wb_out_ref[...] = wb_ref[...].astype(BF16)

    @pl.when(i == 0)
    def _():
        ct_scr[...] = jnp.zeros_like(ct_scr)
        n_scr[...] = jnp.zeros_like(n_scr)
        m_scr[...] = jnp.full_like(m_scr, NEG)

    halo = halo_ref[...].astype(F32)
    stage[0:HALO, :] = jnp.where(i == 0, 0.0, halo)
    stage[HALO:HALO + R_ML, :] = qk_ref[...].astype(F32)
    k_scale = M_DQK ** -0.5
    for cs in range(SEC // 128):
        sl = slice(cs * 128, (cs + 1) * 128)
        y = cb_ref[:, sl]
        for t in range(CONV_W):
            y = y + cw_ref[t:t + 1, sl] * stage[HALO - (CONV_W - 1) + t:HALO - (CONV_W - 1) + t + R_ML, sl]
        a = y * jax.nn.sigmoid(y)
        if cs >= M_HEADS:
            a = a * k_scale
        act[:, sl] = a.astype(BF16)

    li_scr[...] = g_ref[0:M_HEADS].reshape(M_HEADS * n_chunks, CHUNK)
    lf = _log_sigmoid(g_ref[M_HEADS:N_GATES].reshape(M_HEADS * n_chunks, CHUNK))
    r_i = lax.broadcasted_iota(jnp.int32, (CHUNK, CHUNK), 0)
    c_i = lax.broadcasted_iota(jnp.int32, (CHUNK, CHUNK), 1)
    upper = jnp.where(r_i <= c_i, 1.0, 0.0).astype(BF16)
    p0 = lf.astype(BF16)
    r1 = lf - p0.astype(F32)
    p1 = r1.astype(BF16)
    p2 = (r1 - p1.astype(F32)).astype(BF16)
    b_scr[...] = (jnp.dot(p0, upper, preferred_element_type=F32)
                  + jnp.dot(p1, upper, preferred_element_type=F32)
                  + jnp.dot(p2, upper, preferred_element_type=F32))

    t_i = lax.broadcasted_iota(jnp.int32, (R_ML, CHUNK), 0) % CHUNK
    s_i = lax.broadcasted_iota(jnp.int32, (R_ML, CHUNK), 1)
    eye = t_i == s_i
    tril = t_i >= s_i

    def rows_of(x, c):
        return x[c * CHUNK:(c + 1) * CHUNK]

    def per_chunk_rows(x):
        return jnp.concatenate(
            [jnp.broadcast_to(x[c:c + 1, :], (CHUNK, x.shape[1])) for c in range(n_chunks)], axis=0)

    def to_col(rows):
        return jnp.sum(jnp.where(eye, rows, 0.0), axis=1, keepdims=True)

    for h in range(M_HEADS):
        qs = slice(h * M_DQK, (h + 1) * M_DQK)
        ks = slice((M_HEADS + h) * M_DQK, (M_HEADS + h + 1) * M_DQK)
        vs = slice(h * M_DV, (h + 1) * M_DV)
        q = act[:, qs]
        k = act[:, ks]
        v = v_ref[:, vs]
        b_h = b_scr[h * n_chunks:(h + 1) * n_chunks, :]
        li_h = li_scr[h * n_chunks:(h + 1) * n_chunks, :]

        b_rows = per_chunk_rows(b_h)
        b_col = to_col(b_rows)
        d = jnp.where(tril, b_col - b_rows + per_chunk_rows(li_h), NEG)
        m_intra = jnp.max(d, axis=1, keepdims=True)
        s = jnp.concatenate(
            [lax.dot_general(rows_of(q, c), rows_of(k, c), (((1,), (1,)), ((), ())),
                             preferred_element_type=F32) for c in range(n_chunks)], axis=0)
        p = jnp.exp(d - m_intra) * s
        row_sum = jnp.sum(p, axis=1, keepdims=True)
        pb = p.astype(BF16)
        n_intra = jnp.concatenate(
            [jnp.dot(rows_of(pb, c), rows_of(v, c), preferred_element_type=F32)
             for c in range(n_chunks)], axis=0)

        g_tot = b_h[:, CHUNK - 1:CHUNK]
        a = g_tot - b_h + li_h
        m_loc = jnp.max(a, axis=1, keepdims=True)
        w_col = to_col(per_chunk_rows(jnp.exp(a - m_loc)))
        kw = k.astype(F32) * w_col
        kwb = kw.astype(BF16)
        ct_locs = [lax.dot_general(rows_of(kwb, c), rows_of(v, c), (((0,), (0,)), ((), ())),
                                   preferred_element_type=F32) for c in range(n_chunks)]
        n_locs = [jnp.sum(rows_of(kw, c), axis=0, keepdims=True) for c in range(n_chunks)]

        m = m_scr[h, 0:1, 0:1]
        ct = ct_scr[h]
        n = n_scr[h, 0:1, :]
        m_prevs, ct_prevs, n_prevs = [], [], []
        for c in range(n_chunks):
            m_prevs.append(m)
            ct_prevs.append(ct.astype(BF16))
            n_prevs.append(n)
            g_c = g_tot[c:c + 1, :]
            m_loc_c = m_loc[c:c + 1, :]
            m_new = jnp.maximum(g_c + m, m_loc_c)
            s_old = jnp.exp(g_c + m - m_new)
            s_loc = jnp.exp(m_loc_c - m_new)
            ct = s_old * ct + s_loc * ct_locs[c]
            n = s_old * n + s_loc * n_locs[c]
            m = m_new
        ct_scr[h] = ct
        n_scr[h] = jnp.broadcast_to(n, (8, M_DQK))
        m_scr[h] = jnp.broadcast_to(m, (8, 128))

        m_prev = per_chunk_rows(jnp.concatenate(m_prevs, axis=0))
        n_prev = per_chunk_rows(jnp.concatenate(n_prevs, axis=0))
        inter = jnp.concatenate(
            [jnp.dot(rows_of(q, c), ct_prevs[c], preferred_element_type=F32)
             for c in range(n_chunks)], axis=0)
        inter_log = b_col + m_prev
        m_t = jnp.maximum(inter_log, m_intra)
        s_intra = jnp.exp(m_intra - m_t)
        s_inter = jnp.exp(inter_log - m_t)
        num = s_intra * n_intra + s_inter * inter
        den = (s_intra * row_sum
               + s_inter * jnp.sum(q.astype(F32) * n_prev, axis=1, keepdims=True))
        hout = num / jnp.maximum(jnp.abs(den), jnp.exp(-m_t))

        ms = jnp.mean(hout * hout, axis=1, keepdims=True)
        hn = hout * lax.rsqrt(ms + RMS_EPS) * nw_ref[h:h + 1, :]
        out_ref[:, vs] = (hn * jax.nn.sigmoid(o_ref[:, vs].astype(F32))).astype(BF16)


def _row_slab(w, n_steps, step_of):
    rows, cols = w.shape
    assert rows % (n_steps * BF16_SUBLANES) == 0, (w.shape, n_steps)
    return pl.BlockSpec((rows // n_steps, cols), lambda *idx: (step_of(*idx), 0))


def _mlstm(proj, gates3, conv_w, conv_b, m_norm_w, w_a, w_b):
    s = proj.shape[0]
    n_chunks = R_ML // CHUNK
    hb = R_ML // HALO
    n_steps = s // R_ML
    return pl.pallas_call(
        _mlstm_kernel,
        grid=(n_steps,),
        in_specs=[
            pl.BlockSpec((R_ML, SEC), lambda i: (i, 0)),
            pl.BlockSpec((HALO, SEC), lambda i: (jnp.maximum(i * hb - 1, 0), 0)),
            pl.BlockSpec((R_ML, SEC), lambda i: (i, 1)),
            pl.BlockSpec((R_ML, SEC), lambda i: (i, 2)),
            pl.BlockSpec((N_GATES, n_chunks, CHUNK), lambda i: (0, i, 0)),
            pl.BlockSpec((CONV_W, SEC), lambda i: (0, 0)),
            pl.BlockSpec((1, SEC), lambda i: (0, 0)),
            pl.BlockSpec((M_HEADS, M_DV), lambda i: (0, 0)),
            _row_slab(w_a, n_steps, lambda i: i),
            _row_slab(w_b, n_steps, lambda i: i),
        ],
        out_specs=[
            pl.BlockSpec((R_ML, SEC), lambda i: (i, 0)),
            _row_slab(w_a, n_steps, lambda i: i),
            _row_slab(w_b, n_steps, lambda i: i),
        ],
        out_shape=[
            jax.ShapeDtypeStruct((s, SEC), BF16),
            jax.ShapeDtypeStruct(w_a.shape, BF16),
            jax.ShapeDtypeStruct(w_b.shape, BF16),
        ],
        scratch_shapes=[
            pltpu.VMEM((R_ML + HALO, SEC), F32),
            pltpu.VMEM((R_ML, SEC), BF16),
            pltpu.VMEM((M_HEADS * n_chunks, CHUNK), F32),
            pltpu.VMEM((M_HEADS * n_chunks, CHUNK), F32),
            pltpu.VMEM((M_HEADS, M_DQK, M_DV), F32),
            pltpu.VMEM((M_HEADS, 8, M_DQK), F32),
            pltpu.VMEM((M_HEADS, 8, 128), F32),
        ],
        compiler_params=_cparams(("arbitrary",)),
        name="mlstm",
    )(proj, proj, proj, proj, gates3, conv_w, conv_b, m_norm_w, w_a, w_b)


def _lane_partial_sum(p):
    acc = p[:, 0:128]
    for t in range(1, p.shape[1] // 128):
        acc = acc + p[:, t * 128:(t + 1) * 128]
    return acc


def _attn_kernel(par_ref, q_ref, k_ref, v_ref, lq1_ref, lk1_ref, lq2_ref, lk2_ref, nw_ref,
                 wa_ref, wb_ref,
                 out_ref, wa_out_ref, wb_out_ref,
                 m_scr, l_scr, acc_scr, sa_scr, sb_scr):
    wa_out_ref[...] = wa_ref[...].astype(BF16)
    wb_out_ref[...] = wb_ref[...].astype(BF16)

    i = pl.program_id(1)
    nq = pl.num_programs(1)
    shift = par_ref[0]
    fixed = par_ref[1] > 0.5

    l_scr[...] = jnp.zeros_like(l_scr)
    acc_scr[...] = jnp.zeros_like(acc_scr)

    ALL, LOWER = slice(0, TQ), slice(TK, TQ)

    def chunk_mask(n_rows):
        qc = lax.broadcasted_iota(jnp.int32, (n_rows, TK), 0) // CHUNK
        kc = lax.broadcasted_iota(jnp.int32, (n_rows, TK), 1) // CHUNK
        return kc <= qc

    def k_rows(j):
        return pl.ds(pl.multiple_of(j * TK, TK), TK)

    def scores(qi, j, c, rows):
        q0 = pl.multiple_of(qi * TQ + rows.start, TK)
        q = q_ref[pl.ds(q0, rows.stop - rows.start), c * A_DQK:(c + 1) * A_DQK]
        k = k_ref[k_rows(j), c * A_DQK:(c + 1) * A_DQK]
        return lax.dot_general(q, k, (((1,), (1,)), ((), ())), preferred_element_type=F32)

    def fixed_scores(qi, j, s_buf, rows=ALL):
        for c in range(2):
            s_buf[c, rows, :] = scores(qi, j, c, rows)

    def fixed_pv(j, s_buf, rows=ALL, masked=False):
        v = v_ref[k_rows(j), :]
        for c in range(2):
            p = jnp.exp2(s_buf[c, rows, :] - shift)
            if masked:
                p = jnp.where(chunk_mask(rows.stop - rows.start), p, 0.0)
            l_scr[c, rows, :] += _lane_partial_sum(p)
            acc_scr[c, rows, :] += jnp.dot(p.astype(BF16), v, preferred_element_type=F32)

    def online_tile(j, rows=ALL, masked=False):
        v = v_ref[k_rows(j), :]
        for c in range(2):
            s = scores(i, j, c, rows)
            if masked:
                s = jnp.where(chunk_mask(rows.stop - rows.start), s, NEG)
            m_prev = m_scr[c, rows, :]
            m_new = jnp.maximum(m_prev, jnp.max(s, axis=1, keepdims=True))
            alpha = jnp.exp2(m_prev - m_new)
            p = jnp.exp2(s - m_new[:, 0:1])
            l_scr[c, rows, :] = alpha * l_scr[c, rows, :] + _lane_partial_sum(p)
            acc_scr[c, rows, :] = (alpha[:, 0:1] * acc_scr[c, rows, :]
                                   + jnp.dot(p.astype(BF16), v, preferred_element_type=F32))
            m_scr[c, rows, :] = m_new

    @pl.when(fixed)
    def _():
        @pl.when(i == 0)
        def _():
            fixed_scores(0, 0, sa_scr)

        def pair(t, carry):
            j = 2 * t
            fixed_pv(j, sa_scr)
            fixed_scores(i, j + 1, sb_scr)
            fixed_pv(j + 1, sb_scr)
            fixed_scores(i, j + 2, sa_scr)
            return carry
        lax.fori_loop(0, i, pair, 0)

        def boundary_tiles(more_steps):
            fixed_pv(2 * i, sa_scr, masked=True)
            fixed_scores(i, 2 * i + 1, sb_scr, LOWER)
            fixed_pv(2 * i + 1, sb_scr, LOWER, masked=True)
            if more_steps:
                fixed_scores(i + 1, 0, sa_scr)

        @pl.when(i + 1 < nq)
        def _():
            boundary_tiles(True)

        @pl.when(i + 1 == nq)
        def _():
            boundary_tiles(False)

    @pl.when(jnp.logical_not(fixed))
    def _():
        m_scr[...] = jnp.full_like(m_scr, NEG)

        def body(j, carry):
            online_tile(j)
            return carry
        lax.fori_loop(0, 2 * i, body, 0)
        online_tile(2 * i, masked=True)
        online_tile(2 * i + 1, LOWER, masked=True)

    lam = (jnp.exp(jnp.sum(lq1_ref[...] * lk1_ref[...], axis=1, keepdims=True))
           - jnp.exp(jnp.sum(lq2_ref[...] * lk2_ref[...], axis=1, keepdims=True)) + LAM_INIT)
    inv0 = 1.0 / jnp.sum(l_scr[0], axis=1, keepdims=True)
    inv1 = lam / jnp.sum(l_scr[1], axis=1, keepdims=True)
    o = acc_scr[0] * inv0 - acc_scr[1] * inv1
    ms = jnp.mean(o * o, axis=1, keepdims=True)
    scale = lax.rsqrt(ms + RMS_EPS) * (1.0 - LAM_INIT)
    nw = nw_ref[pl.ds(pl.program_id(0), 1), :]
    out_ref[...] = (o * scale * nw).astype(BF16)


def _attn(proj, q_norm_w, k_norm_w, lq1, lk1, lq2, lk2, a_norm_w, w_a, w_b):
    s = proj.shape[0]
    qb = 3 * SEC // A_DV
    kb = 4 * SEC // A_DV
    vb = 5 * SEC // A_DV
    bound = (A_DQK ** 0.5 * LOG2E) * jnp.max(jnp.abs(q_norm_w * k_norm_w))
    shift = jnp.ceil(bound * 1.02) + 1.0
    params = jnp.stack([shift, (shift <= MAX_FIXED_SHIFT).astype(F32)]).astype(F32)
    vec = pl.BlockSpec((1, A_DQK), lambda h, i: (0, 0))
    nq = s // TQ
    n_steps = A_HEADS * nq
    return pl.pallas_call(
        _attn_kernel,
        grid=(A_HEADS, nq),
        in_specs=[
            pl.BlockSpec(memory_space=pltpu.SMEM),
            pl.BlockSpec((s, 2 * A_DQK), lambda h, i: (0, qb + h)),
            pl.BlockSpec((s, 2 * A_DQK), lambda h, i: (0, kb + h)),
            pl.BlockSpec((s, A_DV), lambda h, i: (0, vb + h)),
            vec, vec, vec, vec,
            pl.BlockSpec((A_HEADS, A_DV), lambda h, i: (0, 0)),
            _row_slab(w_a, n_steps, lambda h, i: h * nq + i),
            _row_slab(w_b, n_steps, lambda h, i: h * nq + i),
        ],
        out_specs=[
            pl.BlockSpec((TQ, A_DV), lambda h, i: (i, h)),
            _row_slab(w_a, n_steps, lambda h, i: h * nq + i),
            _row_slab(w_b, n_steps, lambda h, i: h * nq + i),
        ],
        out_shape=[
            jax.ShapeDtypeStruct((s, A_HEADS * A_DV), BF16),
            jax.ShapeDtypeStruct(w_a.shape, BF16),
            jax.ShapeDtypeStruct(w_b.shape, BF16),
        ],
        scratch_shapes=[
            pltpu.VMEM((2, TQ, 128), F32),
            pltpu.VMEM((2, TQ, 128), F32),
            pltpu.VMEM((2, TQ, A_DV), F32),
            pltpu.VMEM((2, TQ, TK), F32),
            pltpu.VMEM((2, TQ, TK), F32),
        ],
        compiler_params=_cparams(("arbitrary", "arbitrary")),
        name="diffattn",
    )(params, proj, proj, proj, lq1, lk1, lq2, lk2, a_norm_w, w_a, w_b)


def _outproj_kernel(hm_ref, ha_ref, wt_ref, wb_ref, x_ref, out_ref):
    out_ref[...] = (x_ref[...]
                    + jnp.dot(hm_ref[...], wt_ref[...], preferred_element_type=F32)
                    + jnp.dot(ha_ref[...], wb_ref[...], preferred_element_type=F32))


def _outproj(hm, ha, w_out, x2):
    s = x2.shape[0]
    return pl.pallas_call(
        _outproj_kernel,
        grid=(s // TM_OUT,),
        in_specs=[
            pl.BlockSpec((TM_OUT, SEC), lambda i: (i, 0)),
            pl.BlockSpec((TM_OUT, SEC), lambda i: (i, 0)),
            pl.BlockSpec((SEC, D_MODEL), lambda i: (0, 0)),
            pl.BlockSpec((SEC, D_MODEL), lambda i: (1, 0)),
            pl.BlockSpec((TM_OUT, D_MODEL), lambda i: (i, 0)),
        ],
        out_specs=pl.BlockSpec((TM_OUT, D_MODEL), lambda i: (i, 0)),
        out_shape=jax.ShapeDtypeStruct((s, D_MODEL), F32),
        compiler_params=_cparams(("arbitrary",)),
        name="outproj",
    )(hm, ha, w_out, w_out, x2)


def _ffn_kernel(x_ref, nw_ref, wg_ref, wu_ref, wd_ref, out_ref, h_scr):
    j = pl.program_id(1)

    @pl.when(j == 0)
    def _():
        xf = x_ref[...]
        ms = jnp.mean(xf * xf, axis=1, keepdims=True)
        h_scr[...] = (xf * lax.rsqrt(ms + RMS_EPS) * nw_ref[...]).astype(BF16)
        out_ref[...] = xf

    h = h_scr[...]
    g = jnp.dot(h, wg_ref[...], preferred_element_type=F32)
    u = jnp.dot(h, wu_ref[...], preferred_element_type=F32)
    a = (g * jax.nn.sigmoid(g) * u).astype(BF16)
    out_ref[...] += jnp.dot(a, wd_ref[...], preferred_element_type=F32)


def _ffn(x1, nw, w_gate, w_up, w_down):
    s = x1.shape[0]
    return pl.pallas_call(
        _ffn_kernel,
        grid=(s // TM_FF, D_FF // TF_FF),
        in_specs=[
            pl.BlockSpec((TM_FF, D_MODEL), lambda i, j: (i, 0)),
            pl.BlockSpec((1, D_MODEL), lambda i, j: (0, 0)),
            pl.BlockSpec((D_MODEL, TF_FF), lambda i, j: (0, j)),
            pl.BlockSpec((D_MODEL, TF_FF), lambda i, j: (0, j)),
            pl.BlockSpec((TF_FF, D_MODEL), lambda i, j: (j, 0)),
        ],
        out_specs=pl.BlockSpec((TM_FF, D_MODEL), lambda i, j: (i, 0)),
        out_shape=jax.ShapeDtypeStruct((s, D_MODEL), F32),
        scratch_shapes=[pltpu.VMEM((TM_FF, D_MODEL), BF16)],
        compiler_params=_cparams(("arbitrary", "arbitrary")),
        name="swiglu",
    )(x1, nw, w_gate, w_up, w_down)


def _layer(x2, norm1_w, w_in, conv_w, conv_b, i_bias, f_bias, m_norm_w, q_norm_w, k_norm_w,
           lambda_q1, lambda_k1, lambda_q2, lambda_k2, a_norm_w, w_out, norm2_w,
           w_gate, w_up, w_down):
    s = x2.shape[0]
    wt_lo, wt_hi, wt_g = _wprep(w_in.T)
    proj, gates_t = _inproj(x2, norm1_w[None, :], wt_lo, wt_hi, wt_g, i_bias, f_bias,
                            q_norm_w[None, :], k_norm_w[None, :])
    gates3 = gates_t.reshape(N_GATES, s // CHUNK, CHUNK)
    hm, w_out_b, w_down_b = _mlstm(proj, gates3, conv_w, conv_b[None, :],
                                   m_norm_w, w_out, w_down)
    ha, w_gate_b, w_up_b = _attn(proj, q_norm_w, k_norm_w, lambda_q1[None, :],
                                 lambda_k1[None, :], lambda_q2[None, :], lambda_k2[None, :],
                                 a_norm_w, w_gate, w_up)
    x1 = _outproj(hm, ha, w_out_b, x2)
    return _ffn(x1, norm2_w[None, :], w_gate_b, w_up_b, w_down_b)


def kernel(x, norm1_w, w_in, conv_w, conv_b, i_bias, f_bias, m_norm_w, q_norm_w, k_norm_w,
           lambda_q1, lambda_k1, lambda_q2, lambda_k2, a_norm_w, w_out, norm2_w,
           w_gate, w_up, w_down):
    b, s, d = x.shape
    assert d == D_MODEL and b == 1 and norm1_w.shape[0] == 1
    assert s % R_ML == 0 and s % TQ == 0 and s % TM_IN == 0 and s % TM_FF == 0
    y = _layer(x.reshape(s, d), norm1_w[0], w_in[0], conv_w[0], conv_b[0], i_bias[0], f_bias[0],
               m_norm_w[0], q_norm_w[0], k_norm_w[0], lambda_q1[0], lambda_k1[0],
               lambda_q2[0], lambda_k2[0], a_norm_w[0], w_out[0], norm2_w[0],
               w_gate[0], w_up[0], w_down[0])
    return y.reshape(b, s, d)
```

```python
import math

import jax
import jax.numpy as jnp
from jax import lax
from jax.experimental import pallas as pl
from jax.experimental.pallas import tpu as pltpu

F32 = jnp.float32
BF16 = jnp.bfloat16

D_MODEL = 2048
CHUNK = 64
SEC = 1024
N_SEC_LO = 3
M_HEADS = 4
M_DV = 256
M_DQK = 128
CONV_W = 4
A_HEADS = 4
A_DV = 256
A_DQK = 128
D_FF = 5632
RMS_EPS = 1e-6
NEG = -1e30
LAM_INIT = 0.8 - 0.6 * math.exp(-0.3 * 0)
LOG2E = 1.4426950408889634
MAX_FIXED_SHIFT = 60.0
GATE_PAD = 128
N_GATES = 2 * M_HEADS
HALO = 8
BF16_SUBLANES = 16

TR_PREP = 256
TM_IN = 1024
TM_OUT = 512
TM_FF = 1024
TF_FF = 512
R_ML = 1024
TQ = 1024
TK = TQ // 2

VMEM_LIMIT = 56 * 1024 * 1024


def _cparams(sem):
    return pltpu.CompilerParams(dimension_semantics=sem, vmem_limit_bytes=VMEM_LIMIT)


def _wprep_kernel(lo_ref, hi_ref, hi_next_ref, w_out, g_out):
    k = pl.program_id(0)
    n = pl.num_programs(0) // 2

    @pl.when(k < n)
    def _():
        w_out[...] = lo_ref[...].astype(BF16)

    @pl.when(k >= n)
    def _():
        w_out[...] = jnp.concatenate([hi_ref[N_GATES:, :], hi_next_ref[...]],
                                     axis=0).astype(BF16)

    @pl.when(k == n)
    def _():
        g_out[...] = hi_ref[0:GATE_PAD, :].astype(BF16)


def _wprep(wt):
    n_in, d = wt.shape
    g0 = N_SEC_LO * SEC
    assert n_in == 2 * g0 + N_GATES and g0 % TR_PREP == 0 and TR_PREP % N_GATES == 0
    n = g0 // TR_PREP
    per_blk = TR_PREP // N_GATES

    def hi_blk(k):
        return n + jnp.maximum(k - n, 0)

    return pl.pallas_call(
        _wprep_kernel,
        grid=(2 * n,),
        in_specs=[
            pl.BlockSpec((TR_PREP, d), lambda k: (jnp.minimum(k, n - 1), 0)),
            pl.BlockSpec((TR_PREP, d), lambda k: (hi_blk(k), 0)),
            pl.BlockSpec((N_GATES, d), lambda k: ((hi_blk(k) + 1) * per_blk, 0)),
        ],
        out_specs=[
            pl.BlockSpec((TR_PREP, d), lambda k: (k, 0)),
            pl.BlockSpec((GATE_PAD, d), lambda k: (0, 0)),
        ],
        out_shape=[
            jax.ShapeDtypeStruct((2 * g0, d), BF16),
            jax.ShapeDtypeStruct((GATE_PAD, d), BF16),
        ],
        compiler_params=_cparams(("arbitrary",)),
        name="wprep",
    )(wt, wt, wt)


def _dot_nt(a, b_t):
    return lax.dot_general(a, b_t, (((1,), (1,)), ((), ())), preferred_element_type=F32)


def _inproj_kernel(x_ref, nw_ref, w_ref, wg_ref, ib_ref, fb_ref, qw_ref, kw_ref,
                   out_ref, gt_ref, h_scr):
    j = pl.program_id(1)
    halves = (slice(0, SEC), slice(SEC, 2 * SEC))

    def plain(half):
        out_ref[:, half] = _dot_nt(h_scr[...], w_ref[half, :]).astype(BF16)

    def qk_normed(half, w):
        acc = _dot_nt(h_scr[...], w_ref[half, :])
        for g in range(SEC // A_DQK):
            a = acc[:, g * A_DQK:(g + 1) * A_DQK]
            ms = jnp.mean(a * a, axis=1, keepdims=True)
            cols = slice(half.start + g * A_DQK, half.start + (g + 1) * A_DQK)
            out_ref[:, cols] = (a * lax.rsqrt(ms + RMS_EPS) * w).astype(BF16)

    @pl.when(j == 0)
    def _():
        xf = x_ref[...]
        ms = jnp.mean(xf * xf, axis=1, keepdims=True)
        hb = (xf * lax.rsqrt(ms + RMS_EPS) * nw_ref[...]).astype(BF16)
        h_scr[...] = hb
        row = lax.broadcasted_iota(jnp.int32, (N_GATES, 1), 0)
        bias = jnp.zeros((N_GATES, 1), F32)
        for r in range(N_GATES):
            b_r = ib_ref[r] if r < M_HEADS else fb_ref[r - M_HEADS]
            bias = jnp.where(row == r, b_r, bias)
        gt_ref[...] = _dot_nt(hb, wg_ref[...]).T[0:N_GATES, :] + bias
        plain(halves[0])
        plain(halves[1])

    @pl.when(j == 1)
    def _():
        plain(halves[0])
        qk_normed(halves[1], qw_ref[...] * (A_DQK ** -0.5 * LOG2E))

    @pl.when(j == 2)
    def _():
        qk_normed(halves[0], kw_ref[...])
        plain(halves[1])


def _inproj(x2, nw, wt, wt_g, i_bias, f_bias, qw, kw):
    s = x2.shape[0]
    n_sec = wt.shape[0] // SEC
    assert n_sec == 2 * N_SEC_LO == 6
    return pl.pallas_call(
        _inproj_kernel,
        grid=(s // TM_IN, n_sec // 2),
        in_specs=[
            pl.BlockSpec((TM_IN, D_MODEL), lambda i, j: (i, 0)),
            pl.BlockSpec((1, D_MODEL), lambda i, j: (0, 0)),
            pl.BlockSpec((2 * SEC, D_MODEL), lambda i, j: (j, 0)),
            pl.BlockSpec((GATE_PAD, D_MODEL), lambda i, j: (0, 0)),
            pl.BlockSpec(memory_space=pltpu.SMEM),
            pl.BlockSpec(memory_space=pltpu.SMEM),
            pl.BlockSpec((1, A_DQK), lambda i, j: (0, 0)),
            pl.BlockSpec((1, A_DQK), lambda i, j: (0, 0)),
        ],
        out_specs=[
            pl.BlockSpec((TM_IN, 2 * SEC), lambda i, j: (i, j)),
            pl.BlockSpec((N_GATES, TM_IN), lambda i, j: (0, i)),
        ],
        out_shape=[
            jax.ShapeDtypeStruct((s, n_sec * SEC), BF16),
            jax.ShapeDtypeStruct((N_GATES, s), F32),
        ],
        scratch_shapes=[
            pltpu.VMEM((TM_IN, D_MODEL), BF16),
        ],
        compiler_params=_cparams(("arbitrary", "arbitrary")),
        name="inproj",
    )(x2, nw, wt, wt_g, i_bias, f_bias, qw, kw)


def _log_sigmoid(x):
    return jnp.minimum(x, 0.0) - jnp.log1p(jnp.exp(-jnp.abs(x)))


def _mlstm_kernel(qk_ref, halo_ref, v_ref, o_ref, g_ref, cw_ref, cb_ref, nw_ref,
                  wa_ref, wb_ref,
                  out_ref, wa_out_ref, wb_out_ref,
                  stage, act, b_scr, li_scr, ct_scr, n_scr, m_scr):
    i = pl.program_id(0)
    n_chunks = R_ML // CHUNK

    wa_out_ref[...] = wa_ref[...].astype(BF16)
    wb_out_ref[...] = wb_ref[...].astype(BF16)

    @pl.when(i == 0)
    def _():
        ct_scr[...] = jnp.zeros_like(ct_scr)
        n_scr[...] = jnp.zeros_like(n_scr)
        m_scr[...] = jnp.full_like(m_scr, NEG)

    halo = halo_ref[...].astype(F32)
    stage[0:HALO, :] = jnp.where(i == 0, 0.0, halo)
    stage[HALO:HALO + R_ML, :] = qk_ref[...].astype(F32)
    k_scale = M_DQK ** -0.5
    for cs in range(SEC // 128):
        sl = slice(cs * 128, (cs + 1) * 128)
        y = cb_ref[:, sl]
        for t in range(CONV_W):
            y = y + cw_ref[t:t + 1, sl] * stage[HALO - (CONV_W - 1) + t:HALO - (CONV_W - 1) + t + R_ML, sl]
        a = y * jax.nn.sigmoid(y)
        if cs >= M_HEADS:
            a = a * k_scale
        act[:, sl] = a.astype(BF16)

    li_scr[...] = g_ref[0:M_HEADS].reshape(M_HEADS * n_chunks, CHUNK)
    lf = _log_sigmoid(g_ref[M_HEADS:N_GATES].reshape(M_HEADS * n_chunks, CHUNK))
    r_i = lax.broadcasted_iota(jnp.int32, (CHUNK, CHUNK), 0)
    c_i = lax.broadcasted_iota(jnp.int32, (CHUNK, CHUNK), 1)
    upper = jnp.where(r_i <= c_i, 1.0, 0.0).astype(BF16)
    p0 = lf.astype(BF16)
    r1 = lf - p0.astype(F32)
    p1 = r1.astype(BF16)
    p2 = (r1 - p1.astype(F32)).astype(BF16)
    b_scr[...] = (jnp.dot(p0, upper, preferred_element_type=F32)
                  + jnp.dot(p1, upper, preferred_element_type=F32)
                  + jnp.dot(p2, upper, preferred_element_type=F32))

    t_i = lax.broadcasted_iota(jnp.int32, (R_ML, CHUNK), 0) % CHUNK
    s_i = lax.broadcasted_iota(jnp.int32, (R_ML, CHUNK), 1)
    eye = t_i == s_i
    tril = t_i >= s_i

    def rows_of(x, c):
        return x[c * CHUNK:(c + 1) * CHUNK]

    def per_chunk_rows(x):
        return jnp.concatenate(
            [jnp.broadcast_to(x[c:c + 1, :], (CHUNK, x.shape[1])) for c in range(n_chunks)], axis=0)

    def to_col(rows):
        return jnp.sum(jnp.where(eye, rows, 0.0), axis=1, keepdims=True)

    for h in range(M_HEADS):
        qs = slice(h * M_DQK, (h + 1) * M_DQK)
        ks = slice((M_HEADS + h) * M_DQK, (M_HEADS + h + 1) * M_DQK)
        vs = slice(h * M_DV, (h + 1) * M_DV)
        q = act[:, qs]
        k = act[:, ks]
        v = v_ref[:, vs]
        b_h = b_scr[h * n_chunks:(h + 1) * n_chunks, :]
        li_h = li_scr[h * n_chunks:(h + 1) * n_chunks, :]

        b_rows = per_chunk_rows(b_h)
        b_col = to_col(b_rows)
        d = jnp.where(tril, b_col - b_rows + per_chunk_rows(li_h), NEG)
        m_intra = jnp.max(d, axis=1, keepdims=True)
        s = jnp.concatenate(
            [lax.dot_general(rows_of(q, c), rows_of(k, c), (((1,), (1,)), ((), ())),
                             preferred_element_type=F32) for c in range(n_chunks)], axis=0)
        p = jnp.exp(d - m_intra) * s
        row_sum = jnp.sum(p, axis=1, keepdims=True)
        pb = p.astype(BF16)
        n_intra = jnp.concatenate(
            [jnp.dot(rows_of(pb, c), rows_of(v, c), preferred_element_type=F32)
             for c in range(n_chunks)], axis=0)

        g_tot = b_h[:, CHUNK - 1:CHUNK]
        a = g_tot - b_h + li_h
        m_loc = jnp.max(a, axis=1, keepdims=True)
        w_col = to_col(per_chunk_rows(jnp.exp(a - m_loc)))
        kw = k.astype(F32) * w_col
        kwb = kw.astype(BF16)
        ct_locs = [lax.dot_general(rows_of(kwb, c), rows_of(v, c), (((0,), (0,)), ((), ())),
                                   preferred_element_type=F32) for c in range(n_chunks)]
        n_locs = [jnp.sum(rows_of(kw, c), axis=0, keepdims=True) for c in range(n_chunks)]

        m = m_scr[h, 0:1, 0:1]
        ct = ct_scr[h]
        n = n_scr[h, 0:1, :]
        m_prevs, ct_prevs, n_prevs = [], [], []
        for c in range(n_chunks):
            m_prevs.append(m)
            ct_prevs.append(ct.astype(BF16))
            n_prevs.append(n)
            g_c = g_tot[c:c + 1, :]
            m_loc_c = m_loc[c:c + 1, :]
            m_new = jnp.maximum(g_c + m, m_loc_c)
            s_old = jnp.exp(g_c + m - m_new)
            s_loc = jnp.exp(m_loc_c - m_new)
            ct = s_old * ct + s_loc * ct_locs[c]
            n = s_old * n + s_loc * n_locs[c]
            m = m_new
        ct_scr[h] = ct
        n_scr[h] = jnp.broadcast_to(n, (8, M_DQK))
        m_scr[h] = jnp.broadcast_to(m, (8, 128))

        m_prev = per_chunk_rows(jnp.concatenate(m_prevs, axis=0))
        n_prev = per_chunk_rows(jnp.concatenate(n_prevs, axis=0))
        inter = jnp.concatenate(
            [jnp.dot(rows_of(q, c), ct_prevs[c], preferred_element_type=F32)
             for c in range(n_chunks)], axis=0)
        inter_log = b_col + m_prev
        m_t = jnp.maximum(inter_log, m_intra)
        s_intra = jnp.exp(m_intra - m_t)
        s_inter = jnp.exp(inter_log - m_t)
        num = s_intra * n_intra + s_inter * inter
        den = (s_intra * row_sum
               + s_inter * jnp.sum(q.astype(F32) * n_prev, axis=1, keepdims=True))
        hout = num / jnp.maximum(jnp.abs(den), jnp.exp(-m_t))

        ms = jnp.mean(hout * hout, axis=1, keepdims=True)
        hn = hout * lax.rsqrt(ms + RMS_EPS) * nw_ref[h:h + 1, :]
        out_ref[:, vs] = (hn * jax.nn.sigmoid(o_ref[:, vs].astype(F32))).astype(BF16)


def _row_slab(w, n_steps, step_of):
    rows, cols = w.shape
    assert rows % (n_steps * BF16_SUBLANES) == 0, (w.shape, n_steps)
    return pl.BlockSpec((rows // n_steps, cols), lambda *idx: (step_of(*idx), 0))


def _mlstm(proj, gates3, conv_w, conv_b, m_norm_w, w_a, w_b):
    s = proj.shape[0]
    n_chunks = R_ML // CHUNK
    hb = R_ML // HALO
    n_steps = s // R_ML
    return pl.pallas_call(
        _mlstm_kernel,
        grid=(n_steps,),
        in_specs=[
            pl.BlockSpec((R_ML, SEC), lambda i: (i, 0)),
            pl.BlockSpec((HALO, SEC), lambda i: (jnp.maximum(i * hb - 1, 0), 0)),
            pl.BlockSpec((R_ML, SEC), lambda i: (i, 1)),
            pl.BlockSpec((R_ML, SEC), lambda i: (i, 2)),
            pl.BlockSpec((N_GATES, n_chunks, CHUNK), lambda i: (0, i, 0)),
            pl.BlockSpec((CONV_W, SEC), lambda i: (0, 0)),
            pl.BlockSpec((1, SEC), lambda i: (0, 0)),
            pl.BlockSpec((M_HEADS, M_DV), lambda i: (0, 0)),
            _row_slab(w_a, n_steps, lambda i: i),
            _row_slab(w_b, n_steps, lambda i: i),
        ],
        out_specs=[
            pl.BlockSpec((R_ML, SEC), lambda i: (i, 0)),
            _row_slab(w_a, n_steps, lambda i: i),
            _row_slab(w_b, n_steps, lambda i: i),
        ],
        out_shape=[
            jax.ShapeDtypeStruct((s, SEC), BF16),
            jax.ShapeDtypeStruct(w_a.shape, BF16),
            jax.ShapeDtypeStruct(w_b.shape, BF16),
        ],
        scratch_shapes=[
            pltpu.VMEM((R_ML + HALO, SEC), F32),
            pltpu.VMEM((R_ML, SEC), BF16),
            pltpu.VMEM((M_HEADS * n_chunks, CHUNK), F32),
            pltpu.VMEM((M_HEADS * n_chunks, CHUNK), F32),
            pltpu.VMEM((M_HEADS, M_DQK, M_DV), F32),
            pltpu.VMEM((M_HEADS, 8, M_DQK), F32),
            pltpu.VMEM((M_HEADS, 8, 128), F32),
        ],
        compiler_params=_cparams(("arbitrary",)),
        name="mlstm",
    )(proj, proj, proj, proj, gates3, conv_w, conv_b, m_norm_w, w_a, w_b)


def _lane_partial_sum(p):
    acc = p[:, 0:128]
    for t in range(1, p.shape[1] // 128):
        acc = acc + p[:, t * 128:(t + 1) * 128]
    return acc


def _attn_kernel(par_ref, q_ref, k_ref, v_ref, lq1_ref, lk1_ref, lq2_ref, lk2_ref, nw_ref,
                 wa_ref, wb_ref,
                 out_ref, wa_out_ref, wb_out_ref,
                 m_scr, l_scr, acc_scr, sa_scr, sb_scr):
    wa_out_ref[...] = wa_ref[...].astype(BF16)
    wb_out_ref[...] = wb_ref[...].astype(BF16)

    i = pl.program_id(1)
    nq = pl.num_programs(1)
    shift = par_ref[0]
    fixed = par_ref[1] > 0.5

    l_scr[...] = jnp.zeros_like(l_scr)
    acc_scr[...] = jnp.zeros_like(acc_scr)

    ALL, LOWER = slice(0, TQ), slice(TK, TQ)

    def chunk_mask(n_rows):
        qc = lax.broadcasted_iota(jnp.int32, (n_rows, TK), 0) // CHUNK
        kc = lax.broadcasted_iota(jnp.int32, (n_rows, TK), 1) // CHUNK
        return kc <= qc

    def k_rows(j):
        return pl.ds(pl.multiple_of(j * TK, TK), TK)

    def scores(qi, j, c, rows):
        q0 = pl.multiple_of(qi * TQ + rows.start, TK)
        q = q_ref[pl.ds(q0, rows.stop - rows.start), c * A_DQK:(c + 1) * A_DQK]
        k = k_ref[k_rows(j), c * A_DQK:(c + 1) * A_DQK]
        return lax.dot_general(q, k, (((1,), (1,)), ((), ())), preferred_element_type=F32)

    def fixed_scores(qi, j, s_buf, rows=ALL):
        for c in range(2):
            s_buf[c, rows, :] = scores(qi, j, c, rows)

    def fixed_pv(j, s_buf, rows=ALL, masked=False):
        v = v_ref[k_rows(j), :]
        for c in range(2):
            p = jnp.exp2(s_buf[c, rows, :] - shift)
            if masked:
                p = jnp.where(chunk_mask(rows.stop - rows.start), p, 0.0)
            l_scr[c, rows, :] += _lane_partial_sum(p)
            acc_scr[c, rows, :] += jnp.dot(p.astype(BF16), v, preferred_element_type=F32)

    def online_tile(j, rows=ALL, masked=False):
        v = v_ref[k_rows(j), :]
        for c in range(2):
            s = scores(i, j, c, rows)
            if masked:
                s = jnp.where(chunk_mask(rows.stop - rows.start), s, NEG)
            m_prev = m_scr[c, rows, :]
            m_new = jnp.maximum(m_prev, jnp.max(s, axis=1, keepdims=True))
            alpha = jnp.exp2(m_prev - m_new)
            p = jnp.exp2(s - m_new[:, 0:1])
            l_scr[c, rows, :] = alpha * l_scr[c, rows, :] + _lane_partial_sum(p)
            acc_scr[c, rows, :] = (alpha[:, 0:1] * acc_scr[c, rows, :]
                                   + jnp.dot(p.astype(BF16), v, preferred_element_type=F32))
            m_scr[c, rows, :] = m_new

    @pl.when(fixed)
    def _():
        @pl.when(i == 0)
        def _():
            fixed_scores(0, 0, sa_scr)

        def pair(t, carry):
            j = 2 * t
            fixed_pv(j, sa_scr)
            fixed_scores(i, j + 1, sb_scr)
            fixed_pv(j + 1, sb_scr)
            fixed_scores(i, j + 2, sa_scr)
            return carry
        lax.fori_loop(0, i, pair, 0)

        def boundary_tiles(more_steps):
            fixed_pv(2 * i, sa_scr, masked=True)
            fixed_scores(i, 2 * i + 1, sb_scr, LOWER)
            fixed_pv(2 * i + 1, sb_scr, LOWER, masked=True)
            if more_steps:
                fixed_scores(i + 1, 0, sa_scr)

        @pl.when(i + 1 < nq)
        def _():
            boundary_tiles(True)

        @pl.when(i + 1 == nq)
        def _():
            boundary_tiles(False)

    @pl.when(jnp.logical_not(fixed))
    def _():
        m_scr[...] = jnp.full_like(m_scr, NEG)

        def body(j, carry):
            online_tile(j)
            return carry
        lax.fori_loop(0, 2 * i, body, 0)
        online_tile(2 * i, masked=True)
        online_tile(2 * i + 1, LOWER, masked=True)

    lam = (jnp.exp(jnp.sum(lq1_ref[...] * lk1_ref[...], axis=1, keepdims=True))
           - jnp.exp(jnp.sum(lq2_ref[...] * lk2_ref[...], axis=1, keepdims=True)) + LAM_INIT)
    inv0 = 1.0 / jnp.sum(l_scr[0], axis=1, keepdims=True)
    inv1 = lam / jnp.sum(l_scr[1], axis=1, keepdims=True)
    o = acc_scr[0] * inv0 - acc_scr[1] * inv1
    ms = jnp.mean(o * o, axis=1, keepdims=True)
    scale = lax.rsqrt(ms + RMS_EPS) * (1.0 - LAM_INIT)
    nw = nw_ref[pl.ds(pl.program_id(0), 1), :]
    out_ref[...] = (o * scale * nw).astype(BF16)


def _attn(proj, q_norm_w, k_norm_w, lq1, lk1, lq2, lk2, a_norm_w, w_a, w_b):
    s = proj.shape[0]
    qb = 3 * SEC // A_DV
    kb = 4 * SEC // A_DV
    vb = 5 * SEC // A_DV
    bound = (A_DQK ** 0.5 * LOG2E) * jnp.max(jnp.abs(q_norm_w * k_norm_w))
    shift = jnp.ceil(bound * 1.02) + 1.0
    params = jnp.stack([shift, (shift <= MAX_FIXED_SHIFT).astype(F32)]).astype(F32)
    vec = pl.BlockSpec((1, A_DQK), lambda h, i: (0, 0))
    nq = s // TQ
    n_steps = A_HEADS * nq
    return pl.pallas_call(
        _attn_kernel,
        grid=(A_HEADS, nq),
        in_specs=[
            pl.BlockSpec(memory_space=pltpu.SMEM),
            pl.BlockSpec((s, 2 * A_DQK), lambda h, i: (0, qb + h)),
            pl.BlockSpec((s, 2 * A_DQK), lambda h, i: (0, kb + h)),
            pl.BlockSpec((s, A_DV), lambda h, i: (0, vb + h)),
            vec, vec, vec, vec,
            pl.BlockSpec((A_HEADS, A_DV), lambda h, i: (0, 0)),
            _row_slab(w_a, n_steps, lambda h, i: h * nq + i),
            _row_slab(w_b, n_steps, lambda h, i: h * nq + i),
        ],
        out_specs=[
            pl.BlockSpec((TQ, A_DV), lambda h, i: (i, h)),
            _row_slab(w_a, n_steps, lambda h, i: h * nq + i),
            _row_slab(w_b, n_steps, lambda h, i: h * nq + i),
        ],
        out_shape=[
            jax.ShapeDtypeStruct((s, A_HEADS * A_DV), BF16),
            jax.ShapeDtypeStruct(w_a.shape, BF16),
            jax.ShapeDtypeStruct(w_b.shape, BF16),
        ],
        scratch_shapes=[
            pltpu.VMEM((2, TQ, 128), F32),
            pltpu.VMEM((2, TQ, 128), F32),
            pltpu.VMEM((2, TQ, A_DV), F32),
            pltpu.VMEM((2, TQ, TK), F32),
            pltpu.VMEM((2, TQ, TK), F32),
        ],
        compiler_params=_cparams(("arbitrary", "arbitrary")),
        name="diffattn",
    )(params, proj, proj, proj, lq1, lk1, lq2, lk2, a_norm_w, w_a, w_b)


def _outproj_kernel(hm_ref, ha_ref, wt_ref, wb_ref, x_ref, out_ref):
    out_ref[...] = (x_ref[...]
                    + jnp.dot(hm_ref[...], wt_ref[...], preferred_element_type=F32)
                    + jnp.dot(ha_ref[...], wb_ref[...], preferred_element_type=F32))


def _outproj(hm, ha, w_out, x2):
    s = x2.shape[0]
    return pl.pallas_call(
        _outproj_kernel,
        grid=(s // TM_OUT,),
        in_specs=[
            pl.BlockSpec((TM_OUT, SEC), lambda i: (i, 0)),
            pl.BlockSpec((TM_OUT, SEC), lambda i: (i, 0)),
            pl.BlockSpec((SEC, D_MODEL), lambda i: (0, 0)),
            pl.BlockSpec((SEC, D_MODEL), lambda i: (1, 0)),
            pl.BlockSpec((TM_OUT, D_MODEL), lambda i: (i, 0)),
        ],
        out_specs=pl.BlockSpec((TM_OUT, D_MODEL), lambda i: (i, 0)),
        out_shape=jax.ShapeDtypeStruct((s, D_MODEL), F32),
        compiler_params=_cparams(("arbitrary",)),
        name="outproj",
    )(hm, ha, w_out, w_out, x2)


def _ffn_kernel(x_ref, nw_ref, wg_ref, wu_ref, wd_ref, out_ref, h_scr):
    j = pl.program_id(1)

    @pl.when(j == 0)
    def _():
        xf = x_ref[...]
        ms = jnp.mean(xf * xf, axis=1, keepdims=True)
        h_scr[...] = (xf * lax.rsqrt(ms + RMS_EPS) * nw_ref[...]).astype(BF16)
        out_ref[...] = xf

    h = h_scr[...]
    g = jnp.dot(h, wg_ref[...], preferred_element_type=F32)
    u = jnp.dot(h, wu_ref[...], preferred_element_type=F32)
    a = (g * jax.nn.sigmoid(g) * u).astype(BF16)
    out_ref[...] += jnp.dot(a, wd_ref[...], preferred_element_type=F32)


def _ffn(x1, nw, w_gate, w_up, w_down):
    s = x1.shape[0]
    return pl.pallas_call(
        _ffn_kernel,
        grid=(s // TM_FF, D_FF // TF_FF),
        in_specs=[
            pl.BlockSpec((TM_FF, D_MODEL), lambda i, j: (i, 0)),
            pl.BlockSpec((1, D_MODEL), lambda i, j: (0, 0)),
            pl.BlockSpec((D_MODEL, TF_FF), lambda i, j: (0, j)),
            pl.BlockSpec((D_MODEL, TF_FF), lambda i, j: (0, j)),
            pl.BlockSpec((TF_FF, D_MODEL), lambda i, j: (j, 0)),
        ],
        out_specs=pl.BlockSpec((TM_FF, D_MODEL), lambda i, j: (i, 0)),
        out_shape=jax.ShapeDtypeStruct((s, D_MODEL), F32),
        scratch_shapes=[pltpu.VMEM((TM_FF, D_MODEL), BF16)],
        compiler_params=_cparams(("arbitrary", "arbitrary")),
        name="swiglu",
    )(x1, nw, w_gate, w_up, w_down)


def _layer(x2, norm1_w, w_in, conv_w, conv_b, i_bias, f_bias, m_norm_w, q_norm_w, k_norm_w,
           lambda_q1, lambda_k1, lambda_q2, lambda_k2, a_norm_w, w_out, norm2_w,
           w_gate, w_up, w_down):
    s = x2.shape[0]
    wt, wt_g = _wprep(w_in.T)
    proj, gates_t = _inproj(x2, norm1_w[None, :], wt, wt_g, i_bias, f_bias,
                            q_norm_w[None, :], k_norm_w[None, :])
    gates3 = gates_t.reshape(N_GATES, s // CHUNK, CHUNK)
    hm, w_out_b, w_down_b = _mlstm(proj, gates3, conv_w, conv_b[None, :],
                                   m_norm_w, w_out, w_down)
    ha, w_gate_b, w_up_b = _attn(proj, q_norm_w, k_norm_w, lambda_q1[None, :],
                                 lambda_k1[None, :], lambda_q2[None, :], lambda_k2[None, :],
                                 a_norm_w, w_gate, w_up)
    x1 = _outproj(hm, ha, w_out_b, x2)
    return _ffn(x1, norm2_w[None, :], w_gate_b, w_up_b, w_down_b)


def kernel(x, norm1_w, w_in, conv_w, conv_b, i_bias, f_bias, m_norm_w, q_norm_w, k_norm_w,
           lambda_q1, lambda_k1, lambda_q2, lambda_k2, a_norm_w, w_out, norm2_w,
           w_gate, w_up, w_down):
    b, s, d = x.shape
    assert d == D_MODEL and b == 1 and norm1_w.shape[0] == 1
    assert s % R_ML == 0 and s % TQ == 0 and s % TM_IN == 0 and s % TM_FF == 0
    y = _layer(x.reshape(s, d), norm1_w[0], w_in[0], conv_w[0], conv_b[0], i_bias[0], f_bias[0],
               m_norm_w[0], q_norm_w[0], k_norm_w[0], lambda_q1[0], lambda_k1[0],
               lambda_q2[0], lambda_k2[0], a_norm_w[0], w_out[0], norm2_w[0],
               w_gate[0], w_up[0], w_down[0])
    return y.reshape(b, s, d)
```

```python
import math

import jax
import jax.numpy as jnp
from jax import lax
from jax.experimental import pallas as pl
from jax.experimental.pallas import tpu as pltpu

F32 = jnp.float32
BF16 = jnp.bfloat16

D_MODEL = 2048
CHUNK = 64
SEC = 1024
N_SEC_LO = 3
M_HEADS = 4
M_DV = 256
M_DQK = 128
CONV_W = 4
A_HEADS = 4
A_DV = 256
A_DQK = 128
D_FF = 5632
RMS_EPS = 1e-6
NEG = -1e30
LAM_INIT = 0.8 - 0.6 * math.exp(-0.3 * 0)
LOG2E = 1.4426950408889634
MAX_FIXED_SHIFT = 60.0
GATE_PAD = 128
N_GATES = 2 * M_HEADS
HALO = 8
BF16_SUBLANES = 16

TR_PREP = 512
TM_IN = 1024
TM_OUT = 1024
TM_FF = 1024
TF_FF = 512
R_ML = 1024
TQ = 1024
TK = TQ // 2

VMEM_LIMIT = 56 * 1024 * 1024


def _cparams(sem):
    return pltpu.CompilerParams(dimension_semantics=sem, vmem_limit_bytes=VMEM_LIMIT)


def _wprep_kernel(lo_ref, hi_ref, hi_next_ref, w_out, g_out):
    k = pl.program_id(0)
    n = pl.num_programs(0) // 2

    @pl.when(k < n)
    def _():
        w_out[...] = lo_ref[...].astype(BF16)

    @pl.when(k >= n)
    def _():
        w_out[...] = jnp.concatenate([hi_ref[N_GATES:, :], hi_next_ref[...]],
                                     axis=0).astype(BF16)

    @pl.when(k == n)
    def _():
        g_out[...] = hi_ref[0:GATE_PAD, :].astype(BF16)


def _wprep(wt):
    n_in, d = wt.shape
    g0 = N_SEC_LO * SEC
    assert n_in == 2 * g0 + N_GATES and g0 % TR_PREP == 0 and TR_PREP % N_GATES == 0
    n = g0 // TR_PREP
    per_blk = TR_PREP // N_GATES

    def hi_blk(k):
        return n + jnp.maximum(k - n, 0)

    return pl.pallas_call(
        _wprep_kernel,
        grid=(2 * n,),
        in_specs=[
            pl.BlockSpec((TR_PREP, d), lambda k: (jnp.minimum(k, n - 1), 0)),
            pl.BlockSpec((TR_PREP, d), lambda k: (hi_blk(k), 0)),
            pl.BlockSpec((N_GATES, d), lambda k: ((hi_blk(k) + 1) * per_blk, 0)),
        ],
        out_specs=[
            pl.BlockSpec((TR_PREP, d), lambda k: (k, 0)),
            pl.BlockSpec((GATE_PAD, d), lambda k: (0, 0)),
        ],
        out_shape=[
            jax.ShapeDtypeStruct((2 * g0, d), BF16),
            jax.ShapeDtypeStruct((GATE_PAD, d), BF16),
        ],
        compiler_params=_cparams(("arbitrary",)),
        name="wprep",
    )(wt, wt, wt)


def _dot_nt(a, b_t):
    return lax.dot_general(a, b_t, (((1,), (1,)), ((), ())), preferred_element_type=F32)


def _inproj_kernel(x_ref, nw_ref, w_ref, wg_ref, ib_ref, fb_ref, qw_ref, kw_ref,
                   out_ref, gt_ref, h_scr):
    j = pl.program_id(1)
    halves = (slice(0, SEC), slice(SEC, 2 * SEC))

    def plain(half):
        out_ref[:, half] = _dot_nt(h_scr[...], w_ref[half, :]).astype(BF16)

    def qk_normed(half, w):
        acc = _dot_nt(h_scr[...], w_ref[half, :])
        for g in range(SEC // A_DQK):
            a = acc[:, g * A_DQK:(g + 1) * A_DQK]
            ms = jnp.mean(a * a, axis=1, keepdims=True)
            cols = slice(half.start + g * A_DQK, half.start + (g + 1) * A_DQK)
            out_ref[:, cols] = (a * lax.rsqrt(ms + RMS_EPS) * w).astype(BF16)

    @pl.when(j == 0)
    def _():
        xf = x_ref[...]
        ms = jnp.mean(xf * xf, axis=1, keepdims=True)
        hb = (xf * lax.rsqrt(ms + RMS_EPS) * nw_ref[...]).astype(BF16)
        h_scr[...] = hb
        row = lax.broadcasted_iota(jnp.int32, (N_GATES, 1), 0)
        bias = jnp.zeros((N_GATES, 1), F32)
        for r in range(N_GATES):
            b_r = ib_ref[r] if r < M_HEADS else fb_ref[r - M_HEADS]
            bias = jnp.where(row == r, b_r, bias)
        gt_ref[...] = _dot_nt(hb, wg_ref[...]).T[0:N_GATES, :] + bias
        plain(halves[0])
        plain(halves[1])

    @pl.when(j == 1)
    def _():
        plain(halves[0])
        qk_normed(halves[1], qw_ref[...] * (A_DQK ** -0.5 * LOG2E))

    @pl.when(j == 2)
    def _():
        qk_normed(halves[0], kw_ref[...])
        plain(halves[1])


def _inproj(x2, nw, wt, wt_g, i_bias, f_bias, qw, kw):
    s = x2.shape[0]
    n_sec = wt.shape[0] // SEC
    assert n_sec == 2 * N_SEC_LO == 6
    return pl.pallas_call(
        _inproj_kernel,
        grid=(s // TM_IN, n_sec // 2),
        in_specs=[
            pl.BlockSpec((TM_IN, D_MODEL), lambda i, j: (i, 0)),
            pl.BlockSpec((1, D_MODEL), lambda i, j: (0, 0)),
            pl.BlockSpec((2 * SEC, D_MODEL), lambda i, j: (j, 0)),
            pl.BlockSpec((GATE_PAD, D_MODEL), lambda i, j: (0, 0)),
            pl.BlockSpec(memory_space=pltpu.SMEM),
            pl.BlockSpec(memory_space=pltpu.SMEM),
            pl.BlockSpec((1, A_DQK), lambda i, j: (0, 0)),
            pl.BlockSpec((1, A_DQK), lambda i, j: (0, 0)),
        ],
        out_specs=[
            pl.BlockSpec((TM_IN, 2 * SEC), lambda i, j: (i, j)),
            pl.BlockSpec((N_GATES, TM_IN), lambda i, j: (0, i)),
        ],
        out_shape=[
            jax.ShapeDtypeStruct((s, n_sec * SEC), BF16),
            jax.ShapeDtypeStruct((N_GATES, s), F32),
        ],
        scratch_shapes=[
            pltpu.VMEM((TM_IN, D_MODEL), BF16),
        ],
        compiler_params=_cparams(("arbitrary", "arbitrary")),
        name="inproj",
    )(x2, nw, wt, wt_g, i_bias, f_bias, qw, kw)


def _log_sigmoid(x):
    return jnp.minimum(x, 0.0) - jnp.log1p(jnp.exp(-jnp.abs(x)))


def _mlstm_kernel(qk_ref, halo_ref, v_ref, o_ref, g_ref, cw_ref, cb_ref, nw_ref,
                  wa_ref, wb_ref,
                  out_ref, wa_out_ref, wb_out_ref,
                  stage, act, b_scr, li_scr, ct_scr, n_scr, m_scr):
    i = pl.program_id(0)
    n_chunks = R_ML // CHUNK

    wa_out_ref[...] = wa_ref[...].astype(BF16)
    wb_out_ref[...] = wb_ref[...].astype(BF16)

    @pl.when(i == 0)
    def _():
        ct_scr[...] = jnp.zeros_like(ct_scr)
        n_scr[...] = jnp.zeros_like(n_scr)
        m_scr[...] = jnp.full_like(m_scr, NEG)

    halo = halo_ref[...].astype(F32)
    stage[0:HALO, :] = jnp.where(i == 0, 0.0, halo)
    stage[HALO:HALO + R_ML, :] = qk_ref[...].astype(F32)
    k_scale = M_DQK ** -0.5
    for cs in range(SEC // 128):
        sl = slice(cs * 128, (cs + 1) * 128)
        y = cb_ref[:, sl]
        for t in range(CONV_W):
            y = y + cw_ref[t:t + 1, sl] * stage[HALO - (CONV_W - 1) + t:HALO - (CONV_W - 1) + t + R_ML, sl]
        a = y * jax.nn.sigmoid(y)
        if cs >= M_HEADS:
            a = a * k_scale
        act[:, sl] = a.astype(BF16)

    li_scr[...] = g_ref[0:M_HEADS].reshape(M_HEADS * n_chunks, CHUNK)
    lf = _log_sigmoid(g_ref[M_HEADS:N_GATES].reshape(M_HEADS * n_chunks, CHUNK))
    r_i = lax.broadcasted_iota(jnp.int32, (CHUNK, CHUNK), 0)
    c_i = lax.broadcasted_iota(jnp.int32, (CHUNK, CHUNK), 1)
    upper = jnp.where(r_i <= c_i, 1.0, 0.0).astype(BF16)
    p0 = lf.astype(BF16)
    r1 = lf - p0.astype(F32)
    p1 = r1.astype(BF16)
    p2 = (r1 - p1.astype(F32)).astype(BF16)
    b_scr[...] = (jnp.dot(p0, upper, preferred_element_type=F32)
                  + jnp.dot(p1, upper, preferred_element_type=F32)
                  + jnp.dot(p2, upper, preferred_element_type=F32))

    t_i = lax.broadcasted_iota(jnp.int32, (R_ML, CHUNK), 0) % CHUNK
    s_i = lax.broadcasted_iota(jnp.int32, (R_ML, CHUNK), 1)
    eye = t_i == s_i
    tril = t_i >= s_i

    def rows_of(x, c):
        return x[c * CHUNK:(c + 1) * CHUNK]

    def per_chunk_rows(x):
        return jnp.concatenate(
            [jnp.broadcast_to(x[c:c + 1, :], (CHUNK, x.shape[1])) for c in range(n_chunks)], axis=0)

    def to_col(rows):
        return jnp.sum(jnp.where(eye, rows, 0.0), axis=1, keepdims=True)

    for h in range(M_HEADS):
        qs = slice(h * M_DQK, (h + 1) * M_DQK)
        ks = slice((M_HEADS + h) * M_DQK, (M_HEADS + h + 1) * M_DQK)
        vs = slice(h * M_DV, (h + 1) * M_DV)
        q = act[:, qs]
        k = act[:, ks]
        v = v_ref[:, vs]
        b_h = b_scr[h * n_chunks:(h + 1) * n_chunks, :]
        li_h = li_scr[h * n_chunks:(h + 1) * n_chunks, :]

        b_rows = per_chunk_rows(b_h)
        b_col = to_col(b_rows)
        d = jnp.where(tril, b_col - b_rows + per_chunk_rows(li_h), NEG)
        m_intra = jnp.max(d, axis=1, keepdims=True)
        s = jnp.concatenate(
            [lax.dot_general(rows_of(q, c), rows_of(k, c), (((1,), (1,)), ((), ())),
                             preferred_element_type=F32) for c in range(n_chunks)], axis=0)
        p = jnp.exp(d - m_intra) * s
        row_sum = jnp.sum(p, axis=1, keepdims=True)
        pb = p.astype(BF16)
        n_intra = jnp.concatenate(
            [jnp.dot(rows_of(pb, c), rows_of(v, c), preferred_element_type=F32)
             for c in range(n_chunks)], axis=0)

        g_tot = b_h[:, CHUNK - 1:CHUNK]
        a = g_tot - b_h + li_h
        m_loc = jnp.max(a, axis=1, keepdims=True)
        w_col = to_col(per_chunk_rows(jnp.exp(a - m_loc)))
        kw = k.astype(F32) * w_col
        kwb = kw.astype(BF16)
        ct_locs = [lax.dot_general(rows_of(kwb, c), rows_of(v, c), (((0,), (0,)), ((), ())),
                                   preferred_element_type=F32) for c in range(n_chunks)]
        n_locs = [jnp.sum(rows_of(kw, c), axis=0, keepdims=True) for c in range(n_chunks)]

        m = m_scr[h, 0:1, 0:1]
        ct = ct_scr[h]
        n = n_scr[h, 0:1, :]
        m_prevs, ct_prevs, n_prevs = [], [], []
        for c in range(n_chunks):
            m_prevs.append(m)
            ct_prevs.append(ct.astype(BF16))
            n_prevs.append(n)
            g_c = g_tot[c:c + 1, :]
            m_loc_c = m_loc[c:c + 1, :]
            m_new = jnp.maximum(g_c + m, m_loc_c)
            s_old = jnp.exp(g_c + m - m_new)
            s_loc = jnp.exp(m_loc_c - m_new)
            ct = s_old * ct + s_loc * ct_locs[c]
            n = s_old * n + s_loc * n_locs[c]
            m = m_new
        ct_scr[h] = ct
        n_scr[h] = jnp.broadcast_to(n, (8, M_DQK))
        m_scr[h] = jnp.broadcast_to(m, (8, 128))

        m_prev = per_chunk_rows(jnp.concatenate(m_prevs, axis=0))
        n_prev = per_chunk_rows(jnp.concatenate(n_prevs, axis=0))
        inter = jnp.concatenate(
            [jnp.dot(rows_of(q, c), ct_prevs[c], preferred_element_type=F32)
             for c in range(n_chunks)], axis=0)
        inter_log = b_col + m_prev
        m_t = jnp.maximum(inter_log, m_intra)
        s_intra = jnp.exp(m_intra - m_t)
        s_inter = jnp.exp(inter_log - m_t)
        num = s_intra * n_intra + s_inter * inter
        den = (s_intra * row_sum
               + s_inter * jnp.sum(q.astype(F32) * n_prev, axis=1, keepdims=True))
        hout = num / jnp.maximum(jnp.abs(den), jnp.exp(-m_t))

        ms = jnp.mean(hout * hout, axis=1, keepdims=True)
        hn = hout * lax.rsqrt(ms + RMS_EPS) * nw_ref[h:h + 1, :]
        out_ref[:, vs] = (hn * jax.nn.sigmoid(o_ref[:, vs].astype(F32))).astype(BF16)


def _row_slab(w, n_steps, step_of):
    rows, cols = w.shape
    assert rows % (n_steps * BF16_SUBLANES) == 0, (w.shape, n_steps)
    return pl.BlockSpec((rows // n_steps, cols), lambda *idx: (step_of(*idx), 0))


def _mlstm(proj, gates3, conv_w, conv_b, m_norm_w, w_a, w_b):
    s = proj.shape[0]
    n_chunks = R_ML // CHUNK
    hb = R_ML // HALO
    n_steps = s // R_ML
    return pl.pallas_call(
        _mlstm_kernel,
        grid=(n_steps,),
        in_specs=[
            pl.BlockSpec((R_ML, SEC), lambda i: (i, 0)),
            pl.BlockSpec((HALO, SEC), lambda i: (jnp.maximum(i * hb - 1, 0), 0)),
            pl.BlockSpec((R_ML, SEC), lambda i: (i, 1)),
            pl.BlockSpec((R_ML, SEC), lambda i: (i, 2)),
            pl.BlockSpec((N_GATES, n_chunks, CHUNK), lambda i: (0, i, 0)),
            pl.BlockSpec((CONV_W, SEC), lambda i: (0, 0)),
            pl.BlockSpec((1, SEC), lambda i: (0, 0)),
            pl.BlockSpec((M_HEADS, M_DV), lambda i: (0, 0)),
            _row_slab(w_a, n_steps, lambda i: i),
            _row_slab(w_b, n_steps, lambda i: i),
        ],
        out_specs=[
            pl.BlockSpec((R_ML, SEC), lambda i: (i, 0)),
            _row_slab(w_a, n_steps, lambda i: i),
            _row_slab(w_b, n_steps, lambda i: i),
        ],
        out_shape=[
            jax.ShapeDtypeStruct((s, SEC), BF16),
            jax.ShapeDtypeStruct(w_a.shape, BF16),
            jax.ShapeDtypeStruct(w_b.shape, BF16),
        ],
        scratch_shapes=[
            pltpu.VMEM((R_ML + HALO, SEC), F32),
            pltpu.VMEM((R_ML, SEC), BF16),
            pltpu.VMEM((M_HEADS * n_chunks, CHUNK), F32),
            pltpu.VMEM((M_HEADS * n_chunks, CHUNK), F32),
            pltpu.VMEM((M_HEADS, M_DQK, M_DV), F32),
            pltpu.VMEM((M_HEADS, 8, M_DQK), F32),
            pltpu.VMEM((M_HEADS, 8, 128), F32),
        ],
        compiler_params=_cparams(("arbitrary",)),
        name="mlstm",
    )(proj, proj, proj, proj, gates3, conv_w, conv_b, m_norm_w, w_a, w_b)


def _lane_partial_sum(p):
    acc = p[:, 0:128]
    for t in range(1, p.shape[1] // 128):
        acc = acc + p[:, t * 128:(t + 1) * 128]
    return acc


def _attn_kernel(par_ref, q_ref, k_ref, v_ref, lq1_ref, lk1_ref, lq2_ref, lk2_ref, nw_ref,
                 wa_ref, wb_ref,
                 out_ref, wa_out_ref, wb_out_ref,
                 m_scr, l_scr, acc_scr, sa_scr, sb_scr):
    wa_out_ref[...] = wa_ref[...].astype(BF16)
    wb_out_ref[...] = wb_ref[...].astype(BF16)

    i = pl.program_id(1)
    nq = pl.num_programs(1)
    shift = par_ref[0]
    fixed = par_ref[1] > 0.5

    l_scr[...] = jnp.zeros_like(l_scr)
    acc_scr[...] = jnp.zeros_like(acc_scr)

    ALL, LOWER = slice(0, TQ), slice(TK, TQ)

    def chunk_mask(n_rows):
        qc = lax.broadcasted_iota(jnp.int32, (n_rows, TK), 0) // CHUNK
        kc = lax.broadcasted_iota(jnp.int32, (n_rows, TK), 1) // CHUNK
        return kc <= qc

    def k_rows(j):
        return pl.ds(pl.multiple_of(j * TK, TK), TK)

    def scores(qi, j, c, rows):
        q0 = pl.multiple_of(qi * TQ + rows.start, TK)
        q = q_ref[pl.ds(q0, rows.stop - rows.start), c * A_DQK:(c + 1) * A_DQK]
        k = k_ref[k_rows(j), c * A_DQK:(c + 1) * A_DQK]
        return lax.dot_general(q, k, (((1,), (1,)), ((), ())), preferred_element_type=F32)

    def fixed_scores(qi, j, s_buf, rows=ALL):
        for c in range(2):
            s_buf[c, rows, :] = scores(qi, j, c, rows)

    def fixed_pv(j, s_buf, rows=ALL, masked=False):
        v = v_ref[k_rows(j), :]
        for c in range(2):
            p = jnp.exp2(s_buf[c, rows, :] - shift)
            if masked:
                p = jnp.where(chunk_mask(rows.stop - rows.start), p, 0.0)
            l_scr[c, rows, :] += _lane_partial_sum(p)
            acc_scr[c, rows, :] += jnp.dot(p.astype(BF16), v, preferred_element_type=F32)

    def online_tile(j, rows=ALL, masked=False):
        v = v_ref[k_rows(j), :]
        for c in range(2):
            s = scores(i, j, c, rows)
            if masked:
                s = jnp.where(chunk_mask(rows.stop - rows.start), s, NEG)
            m_prev = m_scr[c, rows, :]
            m_new = jnp.maximum(m_prev, jnp.max(s, axis=1, keepdims=True))
            alpha = jnp.exp2(m_prev - m_new)
            p = jnp.exp2(s - m_new[:, 0:1])
            l_scr[c, rows, :] = alpha * l_scr[c, rows, :] + _lane_partial_sum(p)
            acc_scr[c, rows, :] = (alpha[:, 0:1] * acc_scr[c, rows, :]
                                   + jnp.dot(p.astype(BF16), v, preferred_element_type=F32))
            m_scr[c, rows, :] = m_new

    @pl.when(fixed)
    def _():
        @pl.when(i == 0)
        def _():
            fixed_scores(0, 0, sa_scr)

        def pair(t, carry):
            j = 2 * t
            fixed_pv(j, sa_scr)
            fixed_scores(i, j + 1, sb_scr)
            fixed_pv(j + 1, sb_scr)
            fixed_scores(i, j + 2, sa_scr)
            return carry
        lax.fori_loop(0, i, pair, 0)

        def boundary_tiles(more_steps):
            fixed_pv(2 * i, sa_scr, masked=True)
            fixed_scores(i, 2 * i + 1, sb_scr, LOWER)
            fixed_pv(2 * i + 1, sb_scr, LOWER, masked=True)
            if more_steps:
                fixed_scores(i + 1, 0, sa_scr)

        @pl.when(i + 1 < nq)
        def _():
            boundary_tiles(True)

        @pl.when(i + 1 == nq)
        def _():
            boundary_tiles(False)

    @pl.when(jnp.logical_not(fixed))
    def _():
        m_scr[...] = jnp.full_like(m_scr, NEG)

        def body(j, carry):
            online_tile(j)
            return carry
        lax.fori_loop(0, 2 * i, body, 0)
        online_tile(2 * i, masked=True)
        online_tile(2 * i + 1, LOWER, masked=True)

    lam = (jnp.exp(jnp.sum(lq1_ref[...] * lk1_ref[...], axis=1, keepdims=True))
           - jnp.exp(jnp.sum(lq2_ref[...] * lk2_ref[...], axis=1, keepdims=True)) + LAM_INIT)
    inv0 = 1.0 / jnp.sum(l_scr[0], axis=1, keepdims=True)
    inv1 = lam / jnp.sum(l_scr[1], axis=1, keepdims=True)
    o = acc_scr[0] * inv0 - acc_scr[1] * inv1
    ms = jnp.mean(o * o, axis=1, keepdims=True)
    scale = lax.rsqrt(ms + RMS_EPS) * (1.0 - LAM_INIT)
    nw = nw_ref[pl.ds(pl.program_id(0), 1), :]
    out_ref[...] = (o * scale * nw).astype(BF16)


def _attn(proj, q_norm_w, k_norm_w, lq1, lk1, lq2, lk2, a_norm_w, w_a, w_b):
    s = proj.shape[0]
    qb = 3 * SEC // A_DV
    kb = 4 * SEC // A_DV
    vb = 5 * SEC // A_DV
    bound = (A_DQK ** 0.5 * LOG2E) * jnp.max(jnp.abs(q_norm_w * k_norm_w))
    shift = jnp.ceil(bound * 1.02) + 1.0
    params = jnp.stack([shift, (shift <= MAX_FIXED_SHIFT).astype(F32)]).astype(F32)
    vec = pl.BlockSpec((1, A_DQK), lambda h, i: (0, 0))
    nq = s // TQ
    n_steps = A_HEADS * nq
    return pl.pallas_call(
        _attn_kernel,
        grid=(A_HEADS, nq),
        in_specs=[
            pl.BlockSpec(memory_space=pltpu.SMEM),
            pl.BlockSpec((s, 2 * A_DQK), lambda h, i: (0, qb + h)),
            pl.BlockSpec((s, 2 * A_DQK), lambda h, i: (0, kb + h)),
            pl.BlockSpec((s, A_DV), lambda h, i: (0, vb + h)),
            vec, vec, vec, vec,
            pl.BlockSpec((A_HEADS, A_DV), lambda h, i: (0, 0)),
            _row_slab(w_a, n_steps, lambda h, i: h * nq + i),
            _row_slab(w_b, n_steps, lambda h, i: h * nq + i),
        ],
        out_specs=[
            pl.BlockSpec((TQ, A_DV), lambda h, i: (i, h)),
            _row_slab(w_a, n_steps, lambda h, i: h * nq + i),
            _row_slab(w_b, n_steps, lambda h, i: h * nq + i),
        ],
        out_shape=[
            jax.ShapeDtypeStruct((s, A_HEADS * A_DV), BF16),
            jax.ShapeDtypeStruct(w_a.shape, BF16),
            jax.ShapeDtypeStruct(w_b.shape, BF16),
        ],
        scratch_shapes=[
            pltpu.VMEM((2, TQ, 128), F32),
            pltpu.VMEM((2, TQ, 128), F32),
            pltpu.VMEM((2, TQ, A_DV), F32),
            pltpu.VMEM((2, TQ, TK), F32),
            pltpu.VMEM((2, TQ, TK), F32),
        ],
        compiler_params=_cparams(("arbitrary", "arbitrary")),
        name="diffattn",
    )(params, proj, proj, proj, lq1, lk1, lq2, lk2, a_norm_w, w_a, w_b)


def _outproj_kernel(hm_ref, ha_ref, wt_ref, wb_ref, x_ref, out_ref):
    for c0 in range(0, D_MODEL, SEC):
        cols = slice(c0, c0 + SEC)
        out_ref[:, cols] = (x_ref[:, cols]
                            + jnp.dot(hm_ref[...], wt_ref[:, cols], preferred_element_type=F32)
                            + jnp.dot(ha_ref[...], wb_ref[:, cols], preferred_element_type=F32))


def _outproj(hm, ha, w_out, x2):
    s = x2.shape[0]
    return pl.pallas_call(
        _outproj_kernel,
        grid=(s // TM_OUT,),
        in_specs=[
            pl.BlockSpec((TM_OUT, SEC), lambda i: (i, 0)),
            pl.BlockSpec((TM_OUT, SEC), lambda i: (i, 0)),
            pl.BlockSpec((SEC, D_MODEL), lambda i: (0, 0), pipeline_mode=pl.Buffered(1)),
            pl.BlockSpec((SEC, D_MODEL), lambda i: (1, 0), pipeline_mode=pl.Buffered(1)),
            pl.BlockSpec((TM_OUT, D_MODEL), lambda i: (i, 0)),
        ],
        out_specs=pl.BlockSpec((TM_OUT, D_MODEL), lambda i: (i, 0)),
        out_shape=jax.ShapeDtypeStruct((s, D_MODEL), F32),
        compiler_params=_cparams(("arbitrary",)),
        name="outproj",
    )(hm, ha, w_out, w_out, x2)


def _ffn_kernel(x_ref, nw_ref, wg_ref, wu_ref, wd_ref, out_ref, h_scr):
    j = pl.program_id(1)

    @pl.when(j == 0)
    def _():
        xf = x_ref[...]
        ms = jnp.mean(xf * xf, axis=1, keepdims=True)
        h_scr[...] = (xf * lax.rsqrt(ms + RMS_EPS) * nw_ref[...]).astype(BF16)
        out_ref[...] = xf

    h = h_scr[...]
    g = jnp.dot(h, wg_ref[...], preferred_element_type=F32)
    u = jnp.dot(h, wu_ref[...], preferred_element_type=F32)
    a = (g * jax.nn.sigmoid(g) * u).astype(BF16)
    out_ref[...] += jnp.dot(a, wd_ref[...], preferred_element_type=F32)


def _ffn(x1, nw, w_gate, w_up, w_down):
    s = x1.shape[0]
    return pl.pallas_call(
        _ffn_kernel,
        grid=(s // TM_FF, D_FF // TF_FF),
        in_specs=[
            pl.BlockSpec((TM_FF, D_MODEL), lambda i, j: (i, 0)),
            pl.BlockSpec((1, D_MODEL), lambda i, j: (0, 0)),
            pl.BlockSpec((D_MODEL, TF_FF), lambda i, j: (0, j)),
            pl.BlockSpec((D_MODEL, TF_FF), lambda i, j: (0, j)),
            pl.BlockSpec((TF_FF, D_MODEL), lambda i, j: (j, 0)),
        ],
        out_specs=pl.BlockSpec((TM_FF, D_MODEL), lambda i, j: (i, 0)),
        out_shape=jax.ShapeDtypeStruct((s, D_MODEL), F32),
        scratch_shapes=[pltpu.VMEM((TM_FF, D_MODEL), BF16)],
        compiler_params=_cparams(("arbitrary", "arbitrary")),
        name="swiglu",
    )(x1, nw, w_gate, w_up, w_down)


def _layer(x2, norm1_w, w_in, conv_w, conv_b, i_bias, f_bias, m_norm_w, q_norm_w, k_norm_w,
           lambda_q1, lambda_k1, lambda_q2, lambda_k2, a_norm_w, w_out, norm2_w,
           w_gate, w_up, w_down):
    s = x2.shape[0]
    wt, wt_g = _wprep(w_in.T)
    proj, gates_t = _inproj(x2, norm1_w[None, :], wt, wt_g, i_bias, f_bias,
                            q_norm_w[None, :], k_norm_w[None, :])
    gates3 = gates_t.reshape(N_GATES, s // CHUNK, CHUNK)
    hm, w_out_b, w_down_b = _mlstm(proj, gates3, conv_w, conv_b[None, :],
                                   m_norm_w, w_out, w_down)
    ha, w_gate_b, w_up_b = _attn(proj, q_norm_w, k_norm_w, lambda_q1[None, :],
                                 lambda_k1[None, :], lambda_q2[None, :], lambda_k2[None, :],
                                 a_norm_w, w_gate, w_up)
    x1 = _outproj(hm, ha, w_out_b, x2)
    return _ffn(x1, norm2_w[None, :], w_gate_b, w_up_b, w_down_b)


def kernel(x, norm1_w, w_in, conv_w, conv_b, i_bias, f_bias, m_norm_w, q_norm_w, k_norm_w,
           lambda_q1, lambda_k1, lambda_q2, lambda_k2, a_norm_w, w_out, norm2_w,
           w_gate, w_up, w_down):
    b, s, d = x.shape
    assert d == D_MODEL and b == 1 and norm1_w.shape[0] == 1
    assert s % R_ML == 0 and s % TQ == 0 and s % TM_IN == 0 and s % TM_FF == 0
    y = _layer(x.reshape(s, d), norm1_w[0], w_in[0], conv_w[0], conv_b[0], i_bias[0], f_bias[0],
               m_norm_w[0], q_norm_w[0], k_norm_w[0], lambda_q1[0], lambda_k1[0],
               lambda_q2[0], lambda_k2[0], a_norm_w[0], w_out[0], norm2_w[0],
               w_gate[0], w_up[0], w_down[0])
    return y.reshape(b, s, d)
```

```python
import math

import jax
import jax.numpy as jnp
from jax import lax
from jax.experimental import pallas as pl
from jax.experimental.pallas import tpu as pltpu

F32 = jnp.float32
BF16 = jnp.bfloat16

D_MODEL = 2048
CHUNK = 64
SEC = 1024
N_SEC_LO = 3
M_HEADS = 4
M_DV = 256
M_DQK = 128
CONV_W = 4
A_HEADS = 4
A_DV = 256
A_DQK = 128
D_FF = 5632
RMS_EPS = 1e-6
NEG = -1e30
LAM_INIT = 0.8 - 0.6 * math.exp(-0.3 * 0)
LOG2E = 1.4426950408889634
MAX_FIXED_SHIFT = 60.0
GATE_PAD = 128
N_GATES = 2 * M_HEADS
HALO = 8
BF16_SUBLANES = 16

TR_PREP = 512
TM_IN = 1024
TM_OUT = 512
TM_FF = 1024
TF_FF = 512
R_ML = 1024
TQ = 1024
TK = TQ // 2

VMEM_LIMIT = 56 * 1024 * 1024


def _cparams(sem):
    return pltpu.CompilerParams(dimension_semantics=sem, vmem_limit_bytes=VMEM_LIMIT)


def _wprep_kernel(lo_ref, hi_ref, hi_next_ref, w_out, g_out):
    k = pl.program_id(0)
    n = pl.num_programs(0) // 2

    @pl.when(k < n)
    def _():
        w_out[...] = lo_ref[...].astype(BF16)

    @pl.when(k >= n)
    def _():
        w_out[...] = jnp.concatenate([hi_ref[N_GATES:, :], hi_next_ref[...]],
                                     axis=0).astype(BF16)

    @pl.when(k == n)
    def _():
        g_out[...] = hi_ref[0:GATE_PAD, :].astype(BF16)


def _wprep(wt):
    n_in, d = wt.shape
    g0 = N_SEC_LO * SEC
    assert n_in == 2 * g0 + N_GATES and g0 % TR_PREP == 0 and TR_PREP % N_GATES == 0
    n = g0 // TR_PREP
    per_blk = TR_PREP // N_GATES

    def hi_blk(k):
        return n + jnp.maximum(k - n, 0)

    return pl.pallas_call(
        _wprep_kernel,
        grid=(2 * n,),
        in_specs=[
            pl.BlockSpec((TR_PREP, d), lambda k: (jnp.minimum(k, n - 1), 0)),
            pl.BlockSpec((TR_PREP, d), lambda k: (hi_blk(k), 0)),
            pl.BlockSpec((N_GATES, d), lambda k: ((hi_blk(k) + 1) * per_blk, 0)),
        ],
        out_specs=[
            pl.BlockSpec((TR_PREP, d), lambda k: (k, 0)),
            pl.BlockSpec((GATE_PAD, d), lambda k: (0, 0)),
        ],
        out_shape=[
            jax.ShapeDtypeStruct((2 * g0, d), BF16),
            jax.ShapeDtypeStruct((GATE_PAD, d), BF16),
        ],
        compiler_params=_cparams(("arbitrary",)),
        name="wprep",
    )(wt, wt, wt)


def _dot_nt(a, b_t):
    return lax.dot_general(a, b_t, (((1,), (1,)), ((), ())), preferred_element_type=F32)


def _inproj_kernel(x_ref, nw_ref, w_ref, wg_ref, ib_ref, fb_ref, qw_ref, kw_ref,
                   out_ref, gt_ref, h_scr):
    j = pl.program_id(1)
    halves = (slice(0, SEC), slice(SEC, 2 * SEC))

    def plain(half):
        out_ref[:, half] = _dot_nt(h_scr[...], w_ref[half, :]).astype(BF16)

    def qk_normed(half, w):
        acc = _dot_nt(h_scr[...], w_ref[half, :])
        for g in range(SEC // A_DQK):
            a = acc[:, g * A_DQK:(g + 1) * A_DQK]
            ms = jnp.mean(a * a, axis=1, keepdims=True)
            cols = slice(half.start + g * A_DQK, half.start + (g + 1) * A_DQK)
            out_ref[:, cols] = (a * lax.rsqrt(ms + RMS_EPS) * w).astype(BF16)

    @pl.when(j == 0)
    def _():
        xf = x_ref[...]
        ms = jnp.mean(xf * xf, axis=1, keepdims=True)
        hb = (xf * lax.rsqrt(ms + RMS_EPS) * nw_ref[...]).astype(BF16)
        h_scr[...] = hb
        row = lax.broadcasted_iota(jnp.int32, (N_GATES, 1), 0)
        bias = jnp.zeros((N_GATES, 1), F32)
        for r in range(N_GATES):
            b_r = ib_ref[r] if r < M_HEADS else fb_ref[r - M_HEADS]
            bias = jnp.where(row == r, b_r, bias)
        gt_ref[...] = _dot_nt(hb, wg_ref[...]).T[0:N_GATES, :] + bias
        plain(halves[0])
        plain(halves[1])

    @pl.when(j == 1)
    def _():
        plain(halves[0])
        qk_normed(halves[1], qw_ref[...] * (A_DQK ** -0.5 * LOG2E))

    @pl.when(j == 2)
    def _():
        qk_normed(halves[0], kw_ref[...])
        plain(halves[1])


def _inproj(x2, nw, wt, wt_g, i_bias, f_bias, qw, kw):
    s = x2.shape[0]
    n_sec = wt.shape[0] // SEC
    assert n_sec == 2 * N_SEC_LO == 6
    return pl.pallas_call(
        _inproj_kernel,
        grid=(s // TM_IN, n_sec // 2),
        in_specs=[
            pl.BlockSpec((TM_IN, D_MODEL), lambda i, j: (i, 0)),
            pl.BlockSpec((1, D_MODEL), lambda i, j: (0, 0)),
            pl.BlockSpec((2 * SEC, D_MODEL), lambda i, j: (j, 0)),
            pl.BlockSpec((GATE_PAD, D_MODEL), lambda i, j: (0, 0)),
            pl.BlockSpec(memory_space=pltpu.SMEM),
            pl.BlockSpec(memory_space=pltpu.SMEM),
            pl.BlockSpec((1, A_DQK), lambda i, j: (0, 0)),
            pl.BlockSpec((1, A_DQK), lambda i, j: (0, 0)),
        ],
        out_specs=[
            pl.BlockSpec((TM_IN, 2 * SEC), lambda i, j: (i, j)),
            pl.BlockSpec((N_GATES, TM_IN), lambda i, j: (0, i)),
        ],
        out_shape=[
            jax.ShapeDtypeStruct((s, n_sec * SEC), BF16),
            jax.ShapeDtypeStruct((N_GATES, s), F32),
        ],
        scratch_shapes=[
            pltpu.VMEM((TM_IN, D_MODEL), BF16),
        ],
        compiler_params=_cparams(("arbitrary", "arbitrary")),
        name="inproj",
    )(x2, nw, wt, wt_g, i_bias, f_bias, qw, kw)


def _log_sigmoid(x):
    return jnp.minimum(x, 0.0) - jnp.log1p(jnp.exp(-jnp.abs(x)))


def _mlstm_kernel(qk_ref, halo_ref, v_ref, o_ref, g_ref, cw_ref, cb_ref, nw_ref,
                  wa_ref, wb_ref,
                  out_ref, wa_out_ref, wb_out_ref,
                  stage, act, b_scr, li_scr, ct_scr, n_scr, m_scr):
    i = pl.program_id(0)
    n_chunks = R_ML // CHUNK

    wa_out_ref[...] = wa_ref[...].astype(BF16)
    wb_out_ref[...] = wb_ref[...].astype(BF16)

    @pl.when(i == 0)
    def _():
        ct_scr[...] = jnp.zeros_like(ct_scr)
        n_scr[...] = jnp.zeros_like(n_scr)
        m_scr[...] = jnp.full_like(m_scr, NEG)

    halo = halo_ref[...].astype(F32)
    stage[0:HALO, :] = jnp.where(i == 0, 0.0, halo)
    stage[HALO:HALO + R_ML, :] = qk_ref[...].astype(F32)
    k_scale = M_DQK ** -0.5
    for cs in range(SEC // 128):
        sl = slice(cs * 128, (cs + 1) * 128)
        y = cb_ref[:, sl]
        for t in range(CONV_W):
            y = y + cw_ref[t:t + 1, sl] * stage[HALO - (CONV_W - 1) + t:HALO - (CONV_W - 1) + t + R_ML, sl]
        a = y * jax.nn.sigmoid(y)
        if cs >= M_HEADS:
            a = a * k_scale
        act[:, sl] = a.astype(BF16)

    li_scr[...] = g_ref[0:M_HEADS].reshape(M_HEADS * n_chunks, CHUNK)
    lf = _log_sigmoid(g_ref[M_HEADS:N_GATES].reshape(M_HEADS * n_chunks, CHUNK))
    r_i = lax.broadcasted_iota(jnp.int32, (CHUNK, CHUNK), 0)
    c_i = lax.broadcasted_iota(jnp.int32, (CHUNK, CHUNK), 1)
    upper = jnp.where(r_i <= c_i, 1.0, 0.0).astype(BF16)
    p0 = lf.astype(BF16)
    r1 = lf - p0.astype(F32)
    p1 = r1.astype(BF16)
    p2 = (r1 - p1.astype(F32)).astype(BF16)
    b_scr[...] = (jnp.dot(p0, upper, preferred_element_type=F32)
                  + jnp.dot(p1, upper, preferred_element_type=F32)
                  + jnp.dot(p2, upper, preferred_element_type=F32))

    t_i = lax.broadcasted_iota(jnp.int32, (R_ML, CHUNK), 0) % CHUNK
    s_i = lax.broadcasted_iota(jnp.int32, (R_ML, CHUNK), 1)
    eye = t_i == s_i
    tril = t_i >= s_i

    def rows_of(x, c):
        return x[c * CHUNK:(c + 1) * CHUNK]

    def per_chunk_rows(x):
        return jnp.concatenate(
            [jnp.broadcast_to(x[c:c + 1, :], (CHUNK, x.shape[1])) for c in range(n_chunks)], axis=0)

    def to_col(rows):
        return jnp.sum(jnp.where(eye, rows, 0.0), axis=1, keepdims=True)

    for h in range(M_HEADS):
        qs = slice(h * M_DQK, (h + 1) * M_DQK)
        ks = slice((M_HEADS + h) * M_DQK, (M_HEADS + h + 1) * M_DQK)
        vs = slice(h * M_DV, (h + 1) * M_DV)
        q = act[:, qs]
        k = act[:, ks]
        v = v_ref[:, vs]
        b_h = b_scr[h * n_chunks:(h + 1) * n_chunks, :]
        li_h = li_scr[h * n_chunks:(h + 1) * n_chunks, :]

        b_rows = per_chunk_rows(b_h)
        b_col = to_col(b_rows)
        d = jnp.where(tril, b_col - b_rows + per_chunk_rows(li_h), NEG)
        m_intra = jnp.max(d, axis=1, keepdims=True)
        s = jnp.concatenate(
            [lax.dot_general(rows_of(q, c), rows_of(k, c), (((1,), (1,)), ((), ())),
                             preferred_element_type=F32) for c in range(n_chunks)], axis=0)
        p = jnp.exp(d - m_intra) * s
        row_sum = jnp.sum(p, axis=1, keepdims=True)
        pb = p.astype(BF16)
        n_intra = jnp.concatenate(
            [jnp.dot(rows_of(pb, c), rows_of(v, c), preferred_element_type=F32)
             for c in range(n_chunks)], axis=0)

        g_tot = b_h[:, CHUNK - 1:CHUNK]
        a = g_tot - b_h + li_h
        m_loc = jnp.max(a, axis=1, keepdims=True)
        w_col = to_col(per_chunk_rows(jnp.exp(a - m_loc)))
        kw = k.astype(F32) * w_col
        kwb = kw.astype(BF16)
        ct_locs = [lax.dot_general(rows_of(kwb, c), rows_of(v, c), (((0,), (0,)), ((), ())),
                                   preferred_element_type=F32) for c in range(n_chunks)]
        n_locs = [jnp.sum(rows_of(kw, c), axis=0, keepdims=True) for c in range(n_chunks)]

        m = m_scr[h, 0:1, 0:1]
        ct = ct_scr[h]
        n = n_scr[h, 0:1, :]
        m_prevs, ct_prevs, n_prevs = [], [], []
        for c in range(n_chunks):
            m_prevs.append(m)
            ct_prevs.append(ct.astype(BF16))
            n_prevs.append(n)
            g_c = g_tot[c:c + 1, :]
            m_loc_c = m_loc[c:c + 1, :]
            m_new = jnp.maximum(g_c + m, m_loc_c)
            s_old = jnp.exp(g_c + m - m_new)
            s_loc = jnp.exp(m_loc_c - m_new)
            ct = s_old * ct + s_loc * ct_locs[c]
            n = s_old * n + s_loc * n_locs[c]
            m = m_new
        ct_scr[h] = ct
        n_scr[h] = jnp.broadcast_to(n, (8, M_DQK))
        m_scr[h] = jnp.broadcast_to(m, (8, 128))

        m_prev = per_chunk_rows(jnp.concatenate(m_prevs, axis=0))
        n_prev = per_chunk_rows(jnp.concatenate(n_prevs, axis=0))
        inter = jnp.concatenate(
            [jnp.dot(rows_of(q, c), ct_prevs[c], preferred_element_type=F32)
             for c in range(n_chunks)], axis=0)
        inter_log = b_col + m_prev
        m_t = jnp.maximum(inter_log, m_intra)
        s_intra = jnp.exp(m_intra - m_t)
        s_inter = jnp.exp(inter_log - m_t)
        num = s_intra * n_intra + s_inter * inter
        den = (s_intra * row_sum
               + s_inter * jnp.sum(q.astype(F32) * n_prev, axis=1, keepdims=True))
        hout = num / jnp.maximum(jnp.abs(den), jnp.exp(-m_t))

        ms = jnp.mean(hout * hout, axis=1, keepdims=True)
        hn = hout * lax.rsqrt(ms + RMS_EPS) * nw_ref[h:h + 1, :]
        out_ref[:, vs] = (hn * jax.nn.sigmoid(o_ref[:, vs].astype(F32))).astype(BF16)


def _row_slab(w, n_steps, step_of):
    rows, cols = w.shape
    assert rows % (n_steps * BF16_SUBLANES) == 0, (w.shape, n_steps)
    return pl.BlockSpec((rows // n_steps, cols), lambda *idx: (step_of(*idx), 0))


def _mlstm(proj, gates3, conv_w, conv_b, m_norm_w, w_a, w_b):
    s = proj.shape[0]
    n_chunks = R_ML // CHUNK
    hb = R_ML // HALO
    n_steps = s // R_ML
    return pl.pallas_call(
        _mlstm_kernel,
        grid=(n_steps,),
        in_specs=[
            pl.BlockSpec((R_ML, SEC), lambda i: (i, 0)),
            pl.BlockSpec((HALO, SEC), lambda i: (jnp.maximum(i * hb - 1, 0), 0)),
            pl.BlockSpec((R_ML, SEC), lambda i: (i, 1)),
            pl.BlockSpec((R_ML, SEC), lambda i: (i, 2)),
            pl.BlockSpec((N_GATES, n_chunks, CHUNK), lambda i: (0, i, 0)),
            pl.BlockSpec((CONV_W, SEC), lambda i: (0, 0)),
            pl.BlockSpec((1, SEC), lambda i: (0, 0)),
            pl.BlockSpec((M_HEADS, M_DV), lambda i: (0, 0)),
            _row_slab(w_a, n_steps, lambda i: i),
            _row_slab(w_b, n_steps, lambda i: i),
        ],
        out_specs=[
            pl.BlockSpec((R_ML, SEC), lambda i: (i, 0)),
            _row_slab(w_a, n_steps, lambda i: i),
            _row_slab(w_b, n_steps, lambda i: i),
        ],
        out_shape=[
            jax.ShapeDtypeStruct((s, SEC), BF16),
            jax.ShapeDtypeStruct(w_a.shape, BF16),
            jax.ShapeDtypeStruct(w_b.shape, BF16),
        ],
        scratch_shapes=[
            pltpu.VMEM((R_ML + HALO, SEC), F32),
            pltpu.VMEM((R_ML, SEC), BF16),
            pltpu.VMEM((M_HEADS * n_chunks, CHUNK), F32),
            pltpu.VMEM((M_HEADS * n_chunks, CHUNK), F32),
            pltpu.VMEM((M_HEADS, M_DQK, M_DV), F32),
            pltpu.VMEM((M_HEADS, 8, M_DQK), F32),
            pltpu.VMEM((M_HEADS, 8, 128), F32),
        ],
        compiler_params=_cparams(("arbitrary",)),
        name="mlstm",
    )(proj, proj, proj, proj, gates3, conv_w, conv_b, m_norm_w, w_a, w_b)


def _lane_partial_sum(p):
    acc = p[:, 0:128]
    for t in range(1, p.shape[1] // 128):
        acc = acc + p[:, t * 128:(t + 1) * 128]
    return acc


def _attn_kernel(par_ref, q_ref, k_ref, v_ref, lq1_ref, lk1_ref, lq2_ref, lk2_ref, nw_ref,
                 wa_ref, wb_ref,
                 out_ref, wa_out_ref, wb_out_ref,
                 m_scr, l_scr, acc_scr, sa_scr, sb_scr):
    wa_out_ref[...] = wa_ref[...].astype(BF16)
    wb_out_ref[...] = wb_ref[...].astype(BF16)

    i = pl.program_id(1)
    nq = pl.num_programs(1)
    shift = par_ref[0]
    fixed = par_ref[1] > 0.5

    l_scr[...] = jnp.zeros_like(l_scr)
    acc_scr[...] = jnp.zeros_like(acc_scr)

    ALL, LOWER = slice(0, TQ), slice(TK, TQ)

    def chunk_mask(n_rows):
        qc = lax.broadcasted_iota(jnp.int32, (n_rows, TK), 0) // CHUNK
        kc = lax.broadcasted_iota(jnp.int32, (n_rows, TK), 1) // CHUNK
        return kc <= qc

    def k_rows(j):
        return pl.ds(pl.multiple_of(j * TK, TK), TK)

    def scores(qi, j, c, rows):
        q0 = pl.multiple_of(qi * TQ + rows.start, TK)
        q = q_ref[pl.ds(q0, rows.stop - rows.start), c * A_DQK:(c + 1) * A_DQK]
        k = k_ref[k_rows(j), c * A_DQK:(c + 1) * A_DQK]
        return lax.dot_general(q, k, (((1,), (1,)), ((), ())), preferred_element_type=F32)

    def fixed_scores(qi, j, s_buf, rows=ALL):
        for c in range(2):
            s_buf[c, rows, :] = scores(qi, j, c, rows)

    def fixed_pv(j, s_buf, rows=ALL, masked=False):
        v = v_ref[k_rows(j), :]
        for c in range(2):
            p = jnp.exp2(s_buf[c, rows, :] - shift)
            if masked:
                p = jnp.where(chunk_mask(rows.stop - rows.start), p, 0.0)
            l_scr[c, rows, :] += _lane_partial_sum(p)
            acc_scr[c, rows, :] += jnp.dot(p.astype(BF16), v, preferred_element_type=F32)

    def online_tile(j, rows=ALL, masked=False):
        v = v_ref[k_rows(j), :]
        for c in range(2):
            s = scores(i, j, c, rows)
            if masked:
                s = jnp.where(chunk_mask(rows.stop - rows.start), s, NEG)
            m_prev = m_scr[c, rows, :]
            m_new = jnp.maximum(m_prev, jnp.max(s, axis=1, keepdims=True))
            alpha = jnp.exp2(m_prev - m_new)
            p = jnp.exp2(s - m_new[:, 0:1])
            l_scr[c, rows, :] = alpha * l_scr[c, rows, :] + _lane_partial_sum(p)
            acc_scr[c, rows, :] = (alpha[:, 0:1] * acc_scr[c, rows, :]
                                   + jnp.dot(p.astype(BF16), v, preferred_element_type=F32))
            m_scr[c, rows, :] = m_new

    @pl.when(fixed)
    def _():
        @pl.when(i == 0)
        def _():
            fixed_scores(0, 0, sa_scr)

        def pair(t, carry):
            j = 2 * t
            fixed_pv(j, sa_scr)
            fixed_scores(i, j + 1, sb_scr)
            fixed_pv(j + 1, sb_scr)
            fixed_scores(i, j + 2, sa_scr)
            return carry
        lax.fori_loop(0, i, pair, 0)

        def boundary_tiles(more_steps):
            fixed_pv(2 * i, sa_scr, masked=True)
            fixed_scores(i, 2 * i + 1, sb_scr, LOWER)
            fixed_pv(2 * i + 1, sb_scr, LOWER, masked=True)
            if more_steps:
                fixed_scores(i + 1, 0, sa_scr)

        @pl.when(i + 1 < nq)
        def _():
            boundary_tiles(True)

        @pl.when(i + 1 == nq)
        def _():
            boundary_tiles(False)

    @pl.when(jnp.logical_not(fixed))
    def _():
        m_scr[...] = jnp.full_like(m_scr, NEG)

        def body(j, carry):
            online_tile(j)
            return carry
        lax.fori_loop(0, 2 * i, body, 0)
        online_tile(2 * i, masked=True)
        online_tile(2 * i + 1, LOWER, masked=True)

    lam = (jnp.exp(jnp.sum(lq1_ref[...] * lk1_ref[...], axis=1, keepdims=True))
           - jnp.exp(jnp.sum(lq2_ref[...] * lk2_ref[...], axis=1, keepdims=True)) + LAM_INIT)
    inv0 = 1.0 / jnp.sum(l_scr[0], axis=1, keepdims=True)
    inv1 = lam / jnp.sum(l_scr[1], axis=1, keepdims=True)
    o = acc_scr[0] * inv0 - acc_scr[1] * inv1
    ms = jnp.mean(o * o, axis=1, keepdims=True)
    scale = lax.rsqrt(ms + RMS_EPS) * (1.0 - LAM_INIT)
    nw = nw_ref[pl.ds(pl.program_id(0), 1), :]
    out_ref[...] = (o * scale * nw).astype(BF16)


def _attn(proj, q_norm_w, k_norm_w, lq1, lk1, lq2, lk2, a_norm_w, w_a, w_b):
    s = proj.shape[0]
    qb = 3 * SEC // A_DV
    kb = 4 * SEC // A_DV
    vb = 5 * SEC // A_DV
    bound = (A_DQK ** 0.5 * LOG2E) * jnp.max(jnp.abs(q_norm_w * k_norm_w))
    shift = jnp.ceil(bound * 1.02) + 1.0
    params = jnp.stack([shift, (shift <= MAX_FIXED_SHIFT).astype(F32)]).astype(F32)
    vec = pl.BlockSpec((1, A_DQK), lambda h, i: (0, 0))
    nq = s // TQ
    n_steps = A_HEADS * nq
    return pl.pallas_call(
        _attn_kernel,
        grid=(A_HEADS, nq),
        in_specs=[
            pl.BlockSpec(memory_space=pltpu.SMEM),
            pl.BlockSpec((s, 2 * A_DQK), lambda h, i: (0, qb + h)),
            pl.BlockSpec((s, 2 * A_DQK), lambda h, i: (0, kb + h)),
            pl.BlockSpec((s, A_DV), lambda h, i: (0, vb + h)),
            vec, vec, vec, vec,
            pl.BlockSpec((A_HEADS, A_DV), lambda h, i: (0, 0)),
            _row_slab(w_a, n_steps, lambda h, i: h * nq + i),
            _row_slab(w_b, n_steps, lambda h, i: h * nq + i),
        ],
        out_specs=[
            pl.BlockSpec((TQ, A_DV), lambda h, i: (i, h)),
            _row_slab(w_a, n_steps, lambda h, i: h * nq + i),
            _row_slab(w_b, n_steps, lambda h, i: h * nq + i),
        ],
        out_shape=[
            jax.ShapeDtypeStruct((s, A_HEADS * A_DV), BF16),
            jax.ShapeDtypeStruct(w_a.shape, BF16),
            jax.ShapeDtypeStruct(w_b.shape, BF16),
        ],
        scratch_shapes=[
            pltpu.VMEM((2, TQ, 128), F32),
            pltpu.VMEM((2, TQ, 128), F32),
            pltpu.VMEM((2, TQ, A_DV), F32),
            pltpu.VMEM((2, TQ, TK), F32),
            pltpu.VMEM((2, TQ, TK), F32),
        ],
        compiler_params=_cparams(("arbitrary", "arbitrary")),
        name="diffattn",
    )(params, proj, proj, proj, lq1, lk1, lq2, lk2, a_norm_w, w_a, w_b)


def _outproj_kernel(hm_ref, ha_ref, wt_ref, wb_ref, x_ref, out_ref):
    out_ref[...] = (x_ref[...]
                    + jnp.dot(hm_ref[...], wt_ref[...], preferred_element_type=F32)
                    + jnp.dot(ha_ref[...], wb_ref[...], preferred_element_type=F32))


def _outproj(hm, ha, w_out, x2):
    s = x2.shape[0]
    return pl.pallas_call(
        _outproj_kernel,
        grid=(s // TM_OUT,),
        in_specs=[
            pl.BlockSpec((TM_OUT, SEC), lambda i: (i, 0)),
            pl.BlockSpec((TM_OUT, SEC), lambda i: (i, 0)),
            pl.BlockSpec((SEC, D_MODEL), lambda i: (0, 0)),
            pl.BlockSpec((SEC, D_MODEL), lambda i: (1, 0)),
            pl.BlockSpec((TM_OUT, D_MODEL), lambda i: (i, 0)),
        ],
        out_specs=pl.BlockSpec((TM_OUT, D_MODEL), lambda i: (i, 0)),
        out_shape=jax.ShapeDtypeStruct((s, D_MODEL), F32),
        compiler_params=_cparams(("arbitrary",)),
        name="outproj",
    )(hm, ha, w_out, w_out, x2)


def _ffn_kernel(x_ref, nw_ref, wg_ref, wu_ref, wd_ref, out_ref, h_scr):
    j = pl.program_id(1)

    @pl.when(j == 0)
    def _():
        xf = x_ref[...]
        ms = jnp.mean(xf * xf, axis=1, keepdims=True)
        h_scr[...] = (xf * lax.rsqrt(ms + RMS_EPS) * nw_ref[...]).astype(BF16)
        out_ref[...] = xf

    h = h_scr[...]
    g = jnp.dot(h, wg_ref[...], preferred_element_type=F32)
    u = jnp.dot(h, wu_ref[...], preferred_element_type=F32)
    a = (g * jax.nn.sigmoid(g) * u).astype(BF16)
    out_ref[...] += jnp.dot(a, wd_ref[...], preferred_element_type=F32)


def _ffn(x1, nw, w_gate, w_up, w_down):
    s = x1.shape[0]
    return pl.pallas_call(
        _ffn_kernel,
        grid=(s // TM_FF, D_FF // TF_FF),
        in_specs=[
            pl.BlockSpec((TM_FF, D_MODEL), lambda i, j: (i, 0)),
            pl.BlockSpec((1, D_MODEL), lambda i, j: (0, 0)),
            pl.BlockSpec((D_MODEL, TF_FF), lambda i, j: (0, j)),
            pl.BlockSpec((D_MODEL, TF_FF), lambda i, j: (0, j)),
            pl.BlockSpec((TF_FF, D_MODEL), lambda i, j: (j, 0)),
        ],
        out_specs=pl.BlockSpec((TM_FF, D_MODEL), lambda i, j: (i, 0)),
        out_shape=jax.ShapeDtypeStruct((s, D_MODEL), F32),
        scratch_shapes=[pltpu.VMEM((TM_FF, D_MODEL), BF16)],
        compiler_params=_cparams(("arbitrary", "arbitrary")),
        name="swiglu",
    )(x1, nw, w_gate, w_up, w_down)


def _layer(x2, norm1_w, w_in, conv_w, conv_b, i_bias, f_bias, m_norm_w, q_norm_w, k_norm_w,
           lambda_q1, lambda_k1, lambda_q2, lambda_k2, a_norm_w, w_out, norm2_w,
           w_gate, w_up, w_down):
    s = x2.shape[0]
    wt, wt_g = _wprep(w_in.T)
    proj, gates_t = _inproj(x2, norm1_w[None, :], wt, wt_g, i_bias, f_bias,
                            q_norm_w[None, :], k_norm_w[None, :])
    gates3 = gates_t.reshape(N_GATES, s // CHUNK, CHUNK)
    hm, w_out_b, w_down_b = _mlstm(proj, gates3, conv_w, conv_b[None, :],
                                   m_norm_w, w_out, w_down)
    ha, w_gate_b, w_up_b = _attn(proj, q_norm_w, k_norm_w, lambda_q1[None, :],
                                 lambda_k1[None, :], lambda_q2[None, :], lambda_k2[None, :],
                                 a_norm_w, w_gate, w_up)
    x1 = _outproj(hm, ha, w_out_b, x2)
    return _ffn(x1, norm2_w[None, :], w_gate_b, w_up_b, w_down_b)


def kernel(x, norm1_w, w_in, conv_w, conv_b, i_bias, f_bias, m_norm_w, q_norm_w, k_norm_w,
           lambda_q1, lambda_k1, lambda_q2, lambda_k2, a_norm_w, w_out, norm2_w,
           w_gate, w_up, w_down):
    b, s, d = x.shape
    assert d == D_MODEL and b == 1 and norm1_w.shape[0] == 1
    assert s % R_ML == 0 and s % TQ == 0 and s % TM_IN == 0 and s % TM_FF == 0
    y = _layer(x.reshape(s, d), norm1_w[0], w_in[0], conv_w[0], conv_b[0], i_bias[0], f_bias[0],
               m_norm_w[0], q_norm_w[0], k_norm_w[0], lambda_q1[0], lambda_k1[0],
               lambda_q2[0], lambda_k2[0], a_norm_w[0], w_out[0], norm2_w[0],
               w_gate[0], w_up[0], w_down[0])
    return y.reshape(b, s, d)
```

```python
import math

import jax
import jax.numpy as jnp
from jax import lax
from jax.experimental import pallas as pl
from jax.experimental.pallas import tpu as pltpu

F32 = jnp.float32
BF16 = jnp.bfloat16

D_MODEL = 2048
CHUNK = 64
SEC = 1024
N_SEC_LO = 3
M_HEADS = 4
M_DV = 256
M_DQK = 128
CONV_W = 4
A_HEADS = 4
A_DV = 256
A_DQK = 128
D_FF = 5632
RMS_EPS = 1e-6
NEG = -1e30
LAM_INIT = 0.8 - 0.6 * math.exp(-0.3 * 0)
LOG2E = 1.4426950408889634
MAX_FIXED_SHIFT = 60.0
GATE_PAD = 128
N_GATES = 2 * M_HEADS
HALO = 8
BF16_SUBLANES = 16

TR_PREP = 512
TM_IN = 1024
TM_OUT = 512
TM_FF = 1024
TF_FF = 512
R_ML = 1024
TQ = 1024
TK = TQ // 2

VMEM_LIMIT = 56 * 1024 * 1024


def _cparams(sem):
    return pltpu.CompilerParams(dimension_semantics=sem, vmem_limit_bytes=VMEM_LIMIT)


def _wprep_kernel(lo_ref, hi_ref, hi_next_ref, w_out, g_out):
    k = pl.program_id(0)
    n = pl.num_programs(0) // 2

    @pl.when(k < n)
    def _():
        w_out[...] = lo_ref[...].astype(BF16)

    @pl.when(k >= n)
    def _():
        w_out[...] = jnp.concatenate([hi_ref[N_GATES:, :], hi_next_ref[...]],
                                     axis=0).astype(BF16)

    @pl.when(k == n)
    def _():
        g_out[...] = hi_ref[0:GATE_PAD, :].astype(BF16)


def _wprep(wt):
    n_in, d = wt.shape
    g0 = N_SEC_LO * SEC
    assert n_in == 2 * g0 + N_GATES and g0 % TR_PREP == 0 and TR_PREP % N_GATES == 0
    n = g0 // TR_PREP
    per_blk = TR_PREP // N_GATES

    def hi_blk(k):
        return n + jnp.maximum(k - n, 0)

    return pl.pallas_call(
        _wprep_kernel,
        grid=(2 * n,),
        in_specs=[
            pl.BlockSpec((TR_PREP, d), lambda k: (jnp.minimum(k, n - 1), 0)),
            pl.BlockSpec((TR_PREP, d), lambda k: (hi_blk(k), 0)),
            pl.BlockSpec((N_GATES, d), lambda k: ((hi_blk(k) + 1) * per_blk, 0)),
        ],
        out_specs=[
            pl.BlockSpec((TR_PREP, d), lambda k: (k, 0)),
            pl.BlockSpec((GATE_PAD, d), lambda k: (0, 0)),
        ],
        out_shape=[
            jax.ShapeDtypeStruct((2 * g0, d), BF16),
            jax.ShapeDtypeStruct((GATE_PAD, d), BF16),
        ],
        compiler_params=_cparams(("arbitrary",)),
        name="wprep",
    )(wt, wt, wt)


def _dot_nt(a, b_t):
    return lax.dot_general(a, b_t, (((1,), (1,)), ((), ())), preferred_element_type=F32)


def _inproj_kernel(x_ref, nw_ref, w_ref, wg_ref, ib_ref, fb_ref, qw_ref, kw_ref,
                   out_ref, gt_ref, h_scr):
    j = pl.program_id(1)
    halves = (slice(0, SEC), slice(SEC, 2 * SEC))

    def plain(half):
        out_ref[:, half] = _dot_nt(h_scr[...], w_ref[half, :]).astype(BF16)

    def qk_normed(half, w):
        acc = _dot_nt(h_scr[...], w_ref[half, :])
        for g in range(SEC // A_DQK):
            a = acc[:, g * A_DQK:(g + 1) * A_DQK]
            ms = jnp.mean(a * a, axis=1, keepdims=True)
            cols = slice(half.start + g * A_DQK, half.start + (g + 1) * A_DQK)
            out_ref[:, cols] = (a * lax.rsqrt(ms + RMS_EPS) * w).astype(BF16)

    @pl.when(j == 0)
    def _():
        xf = x_ref[...]
        ms = jnp.mean(xf * xf, axis=1, keepdims=True)
        hb = (xf * lax.rsqrt(ms + RMS_EPS) * nw_ref[...]).astype(BF16)
        h_scr[...] = hb
        row = lax.broadcasted_iota(jnp.int32, (N_GATES, 1), 0)
        bias = jnp.zeros((N_GATES, 1), F32)
        for r in range(N_GATES):
            b_r = ib_ref[r] if r < M_HEADS else fb_ref[r - M_HEADS]
            bias = jnp.where(row == r, b_r, bias)
        gt_ref[...] = _dot_nt(hb, wg_ref[...]).T[0:N_GATES, :] + bias
        plain(halves[0])
        plain(halves[1])

    @pl.when(j == 1)
    def _():
        plain(halves[0])
        qk_normed(halves[1], qw_ref[...] * (A_DQK ** -0.5 * LOG2E))

    @pl.when(j == 2)
    def _():
        qk_normed(halves[0], kw_ref[...])
        plain(halves[1])


def _inproj(x2, nw, wt, wt_g, i_bias, f_bias, qw, kw):
    s = x2.shape[0]
    n_sec = wt.shape[0] // SEC
    assert n_sec == 2 * N_SEC_LO == 6
    return pl.pallas_call(
        _inproj_kernel,
        grid=(s // TM_IN, n_sec // 2),
        in_specs=[
            pl.BlockSpec((TM_IN, D_MODEL), lambda i, j: (i, 0)),
            pl.BlockSpec((1, D_MODEL), lambda i, j: (0, 0)),
            pl.BlockSpec((2 * SEC, D_MODEL), lambda i, j: (j, 0)),
            pl.BlockSpec((GATE_PAD, D_MODEL), lambda i, j: (0, 0)),
            pl.BlockSpec(memory_space=pltpu.SMEM),
            pl.BlockSpec(memory_space=pltpu.SMEM),
            pl.BlockSpec((1, A_DQK), lambda i, j: (0, 0)),
            pl.BlockSpec((1, A_DQK), lambda i, j: (0, 0)),
        ],
        out_specs=[
            pl.BlockSpec((TM_IN, 2 * SEC), lambda i, j: (i, j)),
            pl.BlockSpec((N_GATES, TM_IN), lambda i, j: (0, i)),
        ],
        out_shape=[
            jax.ShapeDtypeStruct((s, n_sec * SEC), BF16),
            jax.ShapeDtypeStruct((N_GATES, s), F32),
        ],
        scratch_shapes=[
            pltpu.VMEM((TM_IN, D_MODEL), BF16),
        ],
        compiler_params=_cparams(("arbitrary", "arbitrary")),
        name="inproj",
    )(x2, nw, wt, wt_g, i_bias, f_bias, qw, kw)


def _log_sigmoid(x):
    return jnp.minimum(x, 0.0) - jnp.log1p(jnp.exp(-jnp.abs(x)))


def _mlstm_kernel(qk_ref, halo_ref, v_ref, o_ref, g_ref, cw_ref, cb_ref, nw_ref,
                  wa_ref, wb_ref,
                  out_ref, wa_out_ref, wb_out_ref,
                  stage, act, b_scr, li_scr, ct_scr, n_scr, m_scr):
    i = pl.program_id(0)
    n_chunks = R_ML // CHUNK

    wa_out_ref[...] = wa_ref[...].astype(BF16)
    wb_out_ref[...] = wb_ref[...].astype(BF16)

    @pl.when(i == 0)
    def _():
        ct_scr[...] = jnp.zeros_like(ct_scr)
        n_scr[...] = jnp.zeros_like(n_scr)
        m_scr[...] = jnp.full_like(m_scr, NEG)

    halo = halo_ref[...].astype(F32)
    stage[0:HALO, :] = jnp.where(i == 0, 0.0, halo)
    stage[HALO:HALO + R_ML, :] = qk_ref[...].astype(F32)
    k_scale = M_DQK ** -0.5
    for cs in range(SEC // 128):
        sl = slice(cs * 128, (cs + 1) * 128)
        y = cb_ref[:, sl]
        for t in range(CONV_W):
            y = y + cw_ref[t:t + 1, sl] * stage[HALO - (CONV_W - 1) + t:HALO - (CONV_W - 1) + t + R_ML, sl]
        a = y * jax.nn.sigmoid(y)
        if cs >= M_HEADS:
            a = a * k_scale
        act[:, sl] = a.astype(BF16)

    li_scr[...] = g_ref[0:M_HEADS].reshape(M_HEADS * n_chunks, CHUNK)
    lf = _log_sigmoid(g_ref[M_HEADS:N_GATES].reshape(M_HEADS * n_chunks, CHUNK))
    r_i = lax.broadcasted_iota(jnp.int32, (CHUNK, CHUNK), 0)
    c_i = lax.broadcasted_iota(jnp.int32, (CHUNK, CHUNK), 1)
    upper = jnp.where(r_i <= c_i, 1.0, 0.0).astype(BF16)
    p0 = lf.astype(BF16)
    r1 = lf - p0.astype(F32)
    p1 = r1.astype(BF16)
    p2 = (r1 - p1.astype(F32)).astype(BF16)
    b_scr[...] = (jnp.dot(p0, upper, preferred_element_type=F32)
                  + jnp.dot(p1, upper, preferred_element_type=F32)
                  + jnp.dot(p2, upper, preferred_element_type=F32))

    t_i = lax.broadcasted_iota(jnp.int32, (R_ML, CHUNK), 0) % CHUNK
    s_i = lax.broadcasted_iota(jnp.int32, (R_ML, CHUNK), 1)
    eye = t_i == s_i
    tril = t_i >= s_i

    def rows_of(x, c):
        return x[c * CHUNK:(c + 1) * CHUNK]

    def per_chunk_rows(x):
        return jnp.concatenate(
            [jnp.broadcast_to(x[c:c + 1, :], (CHUNK, x.shape[1])) for c in range(n_chunks)], axis=0)

    def to_col(rows):
        return jnp.sum(jnp.where(eye, rows, 0.0), axis=1, keepdims=True)

    for h in range(M_HEADS):
        qs = slice(h * M_DQK, (h + 1) * M_DQK)
        ks = slice((M_HEADS + h) * M_DQK, (M_HEADS + h + 1) * M_DQK)
        vs = slice(h * M_DV, (h + 1) * M_DV)
        q = act[:, qs]
        k = act[:, ks]
        v = v_ref[:, vs]
        b_h = b_scr[h * n_chunks:(h + 1) * n_chunks, :]
        li_h = li_scr[h * n_chunks:(h + 1) * n_chunks, :]

        b_rows = per_chunk_rows(b_h)
        b_col = to_col(b_rows)
        d = jnp.where(tril, b_col - b_rows + per_chunk_rows(li_h), NEG)
        m_intra = jnp.max(d, axis=1, keepdims=True)
        s = jnp.concatenate(
            [lax.dot_general(rows_of(q, c), rows_of(k, c), (((1,), (1,)), ((), ())),
                             preferred_element_type=F32) for c in range(n_chunks)], axis=0)
        p = jnp.exp(d - m_intra) * s
        row_sum = jnp.sum(p, axis=1, keepdims=True)
        pb = p.astype(BF16)
        n_intra = jnp.concatenate(
            [jnp.dot(rows_of(pb, c), rows_of(v, c), preferred_element_type=F32)
             for c in range(n_chunks)], axis=0)

        g_tot = b_h[:, CHUNK - 1:CHUNK]
        a = g_tot - b_h + li_h
        m_loc = jnp.max(a, axis=1, keepdims=True)
        w_col = to_col(per_chunk_rows(jnp.exp(a - m_loc)))
        kw = k.astype(F32) * w_col
        kwb = kw.astype(BF16)
        ct_locs = [lax.dot_general(rows_of(kwb, c), rows_of(v, c), (((0,), (0,)), ((), ())),
                                   preferred_element_type=F32) for c in range(n_chunks)]
        n_locs = [jnp.sum(rows_of(kw, c), axis=0, keepdims=True) for c in range(n_chunks)]

        m = m_scr[h, 0:1, 0:1]
        ct = ct_scr[h]
        n = n_scr[h, 0:1, :]
        m_prevs, ct_prevs, n_prevs = [], [], []
        for c in range(n_chunks):
            m_prevs.append(m)
            ct_prevs.append(ct.astype(BF16))
            n_prevs.append(n)
            g_c = g_tot[c:c + 1, :]
            m_loc_c = m_loc[c:c + 1, :]
            m_new = jnp.maximum(g_c + m, m_loc_c)
            s_old = jnp.exp(g_c + m - m_new)
            s_loc = jnp.exp(m_loc_c - m_new)
            ct = s_old * ct + s_loc * ct_locs[c]
            n = s_old * n + s_loc * n_locs[c]
            m = m_new
        ct_scr[h] = ct
        n_scr[h] = jnp.broadcast_to(n, (8, M_DQK))
        m_scr[h] = jnp.broadcast_to(m, (8, 128))

        m_prev = per_chunk_rows(jnp.concatenate(m_prevs, axis=0))
        n_prev = per_chunk_rows(jnp.concatenate(n_prevs, axis=0))
        inter = jnp.concatenate(
            [jnp.dot(rows_of(q, c), ct_prevs[c], preferred_element_type=F32)
             for c in range(n_chunks)], axis=0)
        inter_log = b_col + m_prev
        m_t = jnp.maximum(inter_log, m_intra)
        s_intra = jnp.exp(m_intra - m_t)
        s_inter = jnp.exp(inter_log - m_t)
        num = s_intra * n_intra + s_inter * inter
        den = (s_intra * row_sum
               + s_inter * jnp.sum(q.astype(F32) * n_prev, axis=1, keepdims=True))
        hout = num / jnp.maximum(jnp.abs(den), jnp.exp(-m_t))

        ms = jnp.mean(hout * hout, axis=1, keepdims=True)
        hn = hout * lax.rsqrt(ms + RMS_EPS) * nw_ref[h:h + 1, :]
        out_ref[:, vs] = (hn * jax.nn.sigmoid(o_ref[:, vs].astype(F32))).astype(BF16)


def _row_slab(w, n_steps, step_of):
    rows, cols = w.shape
    assert rows % (n_steps * BF16_SUBLANES) == 0, (w.shape, n_steps)
    return pl.BlockSpec((rows // n_steps, cols), lambda *idx: (step_of(*idx), 0))


def _mlstm(proj, gates3, conv_w, conv_b, m_norm_w, w_a, w_b):
    s = proj.shape[0]
    n_chunks = R_ML // CHUNK
    hb = R_ML // HALO
    n_steps = s // R_ML
    return pl.pallas_call(
        _mlstm_kernel,
        grid=(n_steps,),
        in_specs=[
            pl.BlockSpec((R_ML, SEC), lambda i: (i, 0)),
            pl.BlockSpec((HALO, SEC), lambda i: (jnp.maximum(i * hb - 1, 0), 0)),
            pl.BlockSpec((R_ML, SEC), lambda i: (i, 1)),
            pl.BlockSpec((R_ML, SEC), lambda i: (i, 2)),
            pl.BlockSpec((N_GATES, n_chunks, CHUNK), lambda i: (0, i, 0)),
            pl.BlockSpec((CONV_W, SEC), lambda i: (0, 0)),
            pl.BlockSpec((1, SEC), lambda i: (0, 0)),
            pl.BlockSpec((M_HEADS, M_DV), lambda i: (0, 0)),
            _row_slab(w_a, n_steps, lambda i: i),
            _row_slab(w_b, n_steps, lambda i: i),
        ],
        out_specs=[
            pl.BlockSpec((R_ML, SEC), lambda i: (i, 0)),
            _row_slab(w_a, n_steps, lambda i: i),
            _row_slab(w_b, n_steps, lambda i: i),
        ],
        out_shape=[
            jax.ShapeDtypeStruct((s, SEC), BF16),
            jax.ShapeDtypeStruct(w_a.shape, BF16),
            jax.ShapeDtypeStruct(w_b.shape, BF16),
        ],
        scratch_shapes=[
            pltpu.VMEM((R_ML + HALO, SEC), F32),
            pltpu.VMEM((R_ML, SEC), BF16),
            pltpu.VMEM((M_HEADS * n_chunks, CHUNK), F32),
            pltpu.VMEM((M_HEADS * n_chunks, CHUNK), F32),
            pltpu.VMEM((M_HEADS, M_DQK, M_DV), F32),
            pltpu.VMEM((M_HEADS, 8, M_DQK), F32),
            pltpu.VMEM((M_HEADS, 8, 128), F32),
        ],
        compiler_params=_cparams(("arbitrary",)),
        name="mlstm",
    )(proj, proj, proj, proj, gates3, conv_w, conv_b, m_norm_w, w_a, w_b)


def _lane_partial_sum(p):
    acc = p[:, 0:128]
    for t in range(1, p.shape[1] // 128):
        acc = acc + p[:, t * 128:(t + 1) * 128]
    return acc


def _attn_kernel(par_ref, q_ref, k_ref, v_ref, lq1_ref, lk1_ref, lq2_ref, lk2_ref, nw_ref,
                 wa_ref, wb_ref,
                 out_ref, wa_out_ref, wb_out_ref,
                 m_scr, l_scr, acc_scr, sa_scr, sb_scr):
    wa_out_ref[...] = wa_ref[...].astype(BF16)
    wb_out_ref[...] = wb_ref[...].astype(BF16)

    i = pl.program_id(1)
    nq = pl.num_programs(1)
    shift = par_ref[0]
    fixed = par_ref[1] > 0.5

    l_scr[...] = jnp.zeros_like(l_scr)
    acc_scr[...] = jnp.zeros_like(acc_scr)

    ALL, LOWER = slice(0, TQ), slice(TK, TQ)

    def chunk_mask(n_rows):
        qc = lax.broadcasted_iota(jnp.int32, (n_rows, TK), 0) // CHUNK
        kc = lax.broadcasted_iota(jnp.int32, (n_rows, TK), 1) // CHUNK
        return kc <= qc

    def k_rows(j):
        return pl.ds(pl.multiple_of(j * TK, TK), TK)

    def scores(qi, j, c, rows):
        q0 = pl.multiple_of(qi * TQ + rows.start, TK)
        q = q_ref[pl.ds(q0, rows.stop - rows.start), c * A_DQK:(c + 1) * A_DQK]
        k = k_ref[k_rows(j), c * A_DQK:(c + 1) * A_DQK]
        return lax.dot_general(q, k, (((1,), (1,)), ((), ())), preferred_element_type=F32)

    def fixed_scores(qi, j, s_buf, rows=ALL):
        for c in range(2):
            s_buf[c, rows, :] = scores(qi, j, c, rows)

    def fixed_pv(j, s_buf, rows=ALL, masked=False):
        v = v_ref[k_rows(j), :]
        for c in range(2):
            p = jnp.exp2(s_buf[c, rows, :] - shift)
            if masked:
                p = jnp.where(chunk_mask(rows.stop - rows.start), p, 0.0)
            l_scr[c, rows, :] += _lane_partial_sum(p)
            acc_scr[c, rows, :] += jnp.dot(p.astype(BF16), v, preferred_element_type=F32)

    def online_tile(j, rows=ALL, masked=False):
        v = v_ref[k_rows(j), :]
        for c in range(2):
            s = scores(i, j, c, rows)
            if masked:
                s = jnp.where(chunk_mask(rows.stop - rows.start), s, NEG)
            m_prev = m_scr[c, rows, :]
            m_new = jnp.maximum(m_prev, jnp.max(s, axis=1, keepdims=True))
            alpha = jnp.exp2(m_prev - m_new)
            p = jnp.exp2(s - m_new[:, 0:1])
            l_scr[c, rows, :] = alpha * l_scr[c, rows, :] + _lane_partial_sum(p)
            acc_scr[c, rows, :] = (alpha[:, 0:1] * acc_scr[c, rows, :]
                                   + jnp.dot(p.astype(BF16), v, preferred_element_type=F32))
            m_scr[c, rows, :] = m_new

    def finalize():
        lam = (jnp.exp(jnp.sum(lq1_ref[...] * lk1_ref[...], axis=1, keepdims=True))
               - jnp.exp(jnp.sum(lq2_ref[...] * lk2_ref[...], axis=1, keepdims=True)) + LAM_INIT)
        inv0 = 1.0 / jnp.sum(l_scr[0], axis=1, keepdims=True)
        inv1 = lam / jnp.sum(l_scr[1], axis=1, keepdims=True)
        o = acc_scr[0] * inv0 - acc_scr[1] * inv1
        ms = jnp.mean(o * o, axis=1, keepdims=True)
        scale = lax.rsqrt(ms + RMS_EPS) * (1.0 - LAM_INIT)
        nw = nw_ref[pl.ds(pl.program_id(0), 1), :]
        out_ref[...] = (o * scale * nw).astype(BF16)

    @pl.when(fixed)
    def _():
        @pl.when(i == 0)
        def _():
            fixed_scores(0, 0, sa_scr)

        def pair(t, carry):
            j = 2 * t
            fixed_pv(j, sa_scr)
            fixed_scores(i, j + 1, sb_scr)
            fixed_pv(j + 1, sb_scr)
            fixed_scores(i, j + 2, sa_scr)
            return carry
        lax.fori_loop(0, i, pair, 0)

        def boundary_tiles(more_steps):
            fixed_pv(2 * i, sa_scr, masked=True)
            fixed_scores(i, 2 * i + 1, sb_scr, LOWER)
            fixed_pv(2 * i + 1, sb_scr, LOWER, masked=True)
            finalize()
            if more_steps:
                fixed_scores(i + 1, 0, sa_scr)

        @pl.when(i + 1 < nq)
        def _():
            boundary_tiles(True)

        @pl.when(i + 1 == nq)
        def _():
            boundary_tiles(False)

    @pl.when(jnp.logical_not(fixed))
    def _():
        m_scr[...] = jnp.full_like(m_scr, NEG)

        def body(j, carry):
            online_tile(j)
            return carry
        lax.fori_loop(0, 2 * i, body, 0)
        online_tile(2 * i, masked=True)
        online_tile(2 * i + 1, LOWER, masked=True)
        finalize()


def _attn(proj, q_norm_w, k_norm_w, lq1, lk1, lq2, lk2, a_norm_w, w_a, w_b):
    s = proj.shape[0]
    qb = 3 * SEC // A_DV
    kb = 4 * SEC // A_DV
    vb = 5 * SEC // A_DV
    bound = (A_DQK ** 0.5 * LOG2E) * jnp.max(jnp.abs(q_norm_w * k_norm_w))
    shift = jnp.ceil(bound * 1.02) + 1.0
    params = jnp.stack([shift, (shift <= MAX_FIXED_SHIFT).astype(F32)]).astype(F32)
    vec = pl.BlockSpec((1, A_DQK), lambda h, i: (0, 0))
    nq = s // TQ
    n_steps = A_HEADS * nq
    return pl.pallas_call(
        _attn_kernel,
        grid=(A_HEADS, nq),
        in_specs=[
            pl.BlockSpec(memory_space=pltpu.SMEM),
            pl.BlockSpec((s, 2 * A_DQK), lambda h, i: (0, qb + h)),
            pl.BlockSpec((s, 2 * A_DQK), lambda h, i: (0, kb + h)),
            pl.BlockSpec((s, A_DV), lambda h, i: (0, vb + h)),
            vec, vec, vec, vec,
            pl.BlockSpec((A_HEADS, A_DV), lambda h, i: (0, 0)),
            _row_slab(w_a, n_steps, lambda h, i: h * nq + i),
            _row_slab(w_b, n_steps, lambda h, i: h * nq + i),
        ],
        out_specs=[
            pl.BlockSpec((TQ, A_DV), lambda h, i: (i, h)),
            _row_slab(w_a, n_steps, lambda h, i: h * nq + i),
            _row_slab(w_b, n_steps, lambda h, i: h * nq + i),
        ],
        out_shape=[
            jax.ShapeDtypeStruct((s, A_HEADS * A_DV), BF16),
            jax.ShapeDtypeStruct(w_a.shape, BF16),
            jax.ShapeDtypeStruct(w_b.shape, BF16),
        ],
        scratch_shapes=[
            pltpu.VMEM((2, TQ, 128), F32),
            pltpu.VMEM((2, TQ, 128), F32),
            pltpu.VMEM((2, TQ, A_DV), F32),
            pltpu.VMEM((2, TQ, TK), F32),
            pltpu.VMEM((2, TQ, TK), F32),
        ],
        compiler_params=_cparams(("arbitrary", "arbitrary")),
        name="diffattn",
    )(params, proj, proj, proj, lq1, lk1, lq2, lk2, a_norm_w, w_a, w_b)


def _outproj_kernel(hm_ref, ha_ref, wt_ref, wb_ref, x_ref, out_ref):
    out_ref[...] = (x_ref[...]
                    + jnp.dot(hm_ref[...], wt_ref[...], preferred_element_type=F32)
                    + jnp.dot(ha_ref[...], wb_ref[...], preferred_element_type=F32))


def _outproj(hm, ha, w_out, x2):
    s = x2.shape[0]
    return pl.pallas_call(
        _outproj_kernel,
        grid=(s // TM_OUT,),
        in_specs=[
            pl.BlockSpec((TM_OUT, SEC), lambda i: (i, 0)),
            pl.BlockSpec((TM_OUT, SEC), lambda i: (i, 0)),
            pl.BlockSpec((SEC, D_MODEL), lambda i: (0, 0)),
            pl.BlockSpec((SEC, D_MODEL), lambda i: (1, 0)),
            pl.BlockSpec((TM_OUT, D_MODEL), lambda i: (i, 0)),
        ],
        out_specs=pl.BlockSpec((TM_OUT, D_MODEL), lambda i: (i, 0)),
        out_shape=jax.ShapeDtypeStruct((s, D_MODEL), F32),
        compiler_params=_cparams(("arbitrary",)),
        name="outproj",
    )(hm, ha, w_out, w_out, x2)


def _ffn_kernel(x_ref, nw_ref, wg_ref, wu_ref, wd_ref, out_ref, h_scr):
    j = pl.program_id(1)

    @pl.when(j == 0)
    def _():
        xf = x_ref[...]
        ms = jnp.mean(xf * xf, axis=1, keepdims=True)
        h_scr[...] = (xf * lax.rsqrt(ms + RMS_EPS) * nw_ref[...]).astype(BF16)
        out_ref[...] = xf

    h = h_scr[...]
    g = jnp.dot(h, wg_ref[...], preferred_element_type=F32)
    u = jnp.dot(h, wu_ref[...], preferred_element_type=F32)
    a = (g * jax.nn.sigmoid(g) * u).astype(BF16)
    out_ref[...] += jnp.dot(a, wd_ref[...], preferred_element_type=F32)


def _ffn(x1, nw, w_gate, w_up, w_down):
    s = x1.shape[0]
    return pl.pallas_call(
        _ffn_kernel,
        grid=(s // TM_FF, D_FF // TF_FF),
        in_specs=[
            pl.BlockSpec((TM_FF, D_MODEL), lambda i, j: (i, 0)),
            pl.BlockSpec((1, D_MODEL), lambda i, j: (0, 0)),
            pl.BlockSpec((D_MODEL, TF_FF), lambda i, j: (0, j)),
            pl.BlockSpec((D_MODEL, TF_FF), lambda i, j: (0, j)),
            pl.BlockSpec((TF_FF, D_MODEL), lambda i, j: (j, 0)),
        ],
        out_specs=pl.BlockSpec((TM_FF, D_MODEL), lambda i, j: (i, 0)),
        out_shape=jax.ShapeDtypeStruct((s, D_MODEL), F32),
        scratch_shapes=[pltpu.VMEM((TM_FF, D_MODEL), BF16)],
        compiler_params=_cparams(("arbitrary", "arbitrary")),
        name="swiglu",
    )(x1, nw, w_gate, w_up, w_down)


def _layer(x2, norm1_w, w_in, conv_w, conv_b, i_bias, f_bias, m_norm_w, q_norm_w, k_norm_w,
           lambda_q1, lambda_k1, lambda_q2, lambda_k2, a_norm_w, w_out, norm2_w,
           w_gate, w_up, w_down):
    s = x2.shape[0]
    wt, wt_g = _wprep(w_in.T)
    proj, gates_t = _inproj(x2, norm1_w[None, :], wt, wt_g, i_bias, f_bias,
                            q_norm_w[None, :], k_norm_w[None, :])
    gates3 = gates_t.reshape(N_GATES, s // CHUNK, CHUNK)
    hm, w_out_b, w_down_b = _mlstm(proj, gates3, conv_w, conv_b[None, :],
                                   m_norm_w, w_out, w_down)
    ha, w_gate_b, w_up_b = _attn(proj, q_norm_w, k_norm_w, lambda_q1[None, :],
                                 lambda_k1[None, :], lambda_q2[None, :], lambda_k2[None, :],
                                 a_norm_w, w_gate, w_up)
    x1 = _outproj(hm, ha, w_out_b, x2)
    return _ffn(x1, norm2_w[None, :], w_gate_b, w_up_b, w_down_b)


def kernel(x, norm1_w, w_in, conv_w, conv_b, i_bias, f_bias, m_norm_w, q_norm_w, k_norm_w,
           lambda_q1, lambda_k1, lambda_q2, lambda_k2, a_norm_w, w_out, norm2_w,
           w_gate, w_up, w_down):
    b, s, d = x.shape
    assert d == D_MODEL and b == 1 and norm1_w.shape[0] == 1
    assert s % R_ML == 0 and s % TQ == 0 and s % TM_IN == 0 and s % TM_FF == 0
    y = _layer(x.reshape(s, d), norm1_w[0], w_in[0], conv_w[0], conv_b[0], i_bias[0], f_bias[0],
               m_norm_w[0], q_norm_w[0], k_norm_w[0], lambda_q1[0], lambda_k1[0],
               lambda_q2[0], lambda_k2[0], a_norm_w[0], w_out[0], norm2_w[0],
               w_gate[0], w_up[0], w_down[0])
    return y.reshape(b, s, d)
```

```python
import math

import jax
import jax.numpy as jnp
from jax import lax
from jax.experimental import pallas as pl
from jax.experimental.pallas import tpu as pltpu

F32 = jnp.float32
BF16 = jnp.bfloat16

D_MODEL = 2048
CHUNK = 64
SEC = 1024
N_SEC_LO = 3
M_HEADS = 4
M_DV = 256
M_DQK = 128
CONV_W = 4
A_HEADS = 4
A_DV = 256
A_DQK = 128
D_FF = 5632
RMS_EPS = 1e-6
NEG = -1e30
LAM_INIT = 0.8 - 0.6 * math.exp(-0.3 * 0)
LOG2E = 1.4426950408889634
MAX_FIXED_SHIFT = 60.0
GATE_PAD = 128
N_GATES = 2 * M_HEADS
HALO = 8
BF16_SUBLANES = 16

TR_PREP = 512
TM_IN = 1024
TM_OUT = 512
TM_FF = 1024
TF_FF = 512
R_ML = 1024
TQ = 1024
TK = TQ // 2

VMEM_LIMIT = 56 * 1024 * 1024


def _cparams(sem):
    return pltpu.CompilerParams(dimension_semantics=sem, vmem_limit_bytes=VMEM_LIMIT)


def _wprep_kernel(lo_ref, hi_ref, hi_next_ref, w_out, g_out):
    k = pl.program_id(0)
    n = pl.num_programs(0) // 2

    @pl.when(k < n)
    def _():
        w_out[...] = lo_ref[...].astype(BF16)

    @pl.when(k >= n)
    def _():
        w_out[...] = jnp.concatenate([hi_ref[N_GATES:, :], hi_next_ref[...]],
                                     axis=0).astype(BF16)

    @pl.when(k == n)
    def _():
        g_out[...] = hi_ref[0:GATE_PAD, :].astype(BF16)


def _wprep(wt):
    n_in, d = wt.shape
    g0 = N_SEC_LO * SEC
    assert n_in == 2 * g0 + N_GATES and g0 % TR_PREP == 0 and TR_PREP % N_GATES == 0
    n = g0 // TR_PREP
    per_blk = TR_PREP // N_GATES

    def hi_blk(k):
        return n + jnp.maximum(k - n, 0)

    return pl.pallas_call(
        _wprep_kernel,
        grid=(2 * n,),
        in_specs=[
            pl.BlockSpec((TR_PREP, d), lambda k: (jnp.minimum(k, n - 1), 0)),
            pl.BlockSpec((TR_PREP, d), lambda k: (hi_blk(k), 0)),
            pl.BlockSpec((N_GATES, d), lambda k: ((hi_blk(k) + 1) * per_blk, 0)),
        ],
        out_specs=[
            pl.BlockSpec((TR_PREP, d), lambda k: (k, 0)),
            pl.BlockSpec((GATE_PAD, d), lambda k: (0, 0)),
        ],
        out_shape=[
            jax.ShapeDtypeStruct((2 * g0, d), BF16),
            jax.ShapeDtypeStruct((GATE_PAD, d), BF16),
        ],
        compiler_params=_cparams(("arbitrary",)),
        name="wprep",
    )(wt, wt, wt)


def _dot_nt(a, b_t):
    return lax.dot_general(a, b_t, (((1,), (1,)), ((), ())), preferred_element_type=F32)


def _inproj_kernel(x_ref, nw_ref, w_ref, wg_ref, ib_ref, fb_ref, qw_ref, kw_ref,
                   out_ref, gt_ref, h_scr):
    j = pl.program_id(1)
    halves = (slice(0, SEC), slice(SEC, 2 * SEC))

    def plain(half):
        out_ref[:, half] = _dot_nt(h_scr[...], w_ref[half, :]).astype(BF16)

    def qk_normed(half, w):
        acc = _dot_nt(h_scr[...], w_ref[half, :])
        for g in range(SEC // A_DQK):
            a = acc[:, g * A_DQK:(g + 1) * A_DQK]
            ms = jnp.mean(a * a, axis=1, keepdims=True)
            cols = slice(half.start + g * A_DQK, half.start + (g + 1) * A_DQK)
            out_ref[:, cols] = (a * lax.rsqrt(ms + RMS_EPS) * w).astype(BF16)

    @pl.when(j == 0)
    def _():
        xf = x_ref[...]
        ms = jnp.mean(xf * xf, axis=1, keepdims=True)
        hb = (xf * lax.rsqrt(ms + RMS_EPS) * nw_ref[...]).astype(BF16)
        h_scr[...] = hb
        row = lax.broadcasted_iota(jnp.int32, (N_GATES, 1), 0)
        bias = jnp.zeros((N_GATES, 1), F32)
        for r in range(N_GATES):
            b_r = ib_ref[r] if r < M_HEADS else fb_ref[r - M_HEADS]
            bias = jnp.where(row == r, b_r, bias)
        gt_ref[...] = _dot_nt(hb, wg_ref[...]).T[0:N_GATES, :] + bias
        plain(halves[0])
        plain(halves[1])

    @pl.when(j == 1)
    def _():
        plain(halves[0])
        qk_normed(halves[1], qw_ref[...] * (A_DQK ** -0.5 * LOG2E))

    @pl.when(j == 2)
    def _():
        qk_normed(halves[0], kw_ref[...])
        plain(halves[1])


def _inproj(x2, nw, wt, wt_g, i_bias, f_bias, qw, kw):
    s = x2.shape[0]
    n_sec = wt.shape[0] // SEC
    assert n_sec == 2 * N_SEC_LO == 6
    return pl.pallas_call(
        _inproj_kernel,
        grid=(s // TM_IN, n_sec // 2),
        in_specs=[
            pl.BlockSpec((TM_IN, D_MODEL), lambda i, j: (i, 0)),
            pl.BlockSpec((1, D_MODEL), lambda i, j: (0, 0)),
            pl.BlockSpec((2 * SEC, D_MODEL), lambda i, j: (j, 0)),
            pl.BlockSpec((GATE_PAD, D_MODEL), lambda i, j: (0, 0)),
            pl.BlockSpec(memory_space=pltpu.SMEM),
            pl.BlockSpec(memory_space=pltpu.SMEM),
            pl.BlockSpec((1, A_DQK), lambda i, j: (0, 0)),
            pl.BlockSpec((1, A_DQK), lambda i, j: (0, 0)),
        ],
        out_specs=[
            pl.BlockSpec((TM_IN, 2 * SEC), lambda i, j: (i, j)),
            pl.BlockSpec((N_GATES, TM_IN), lambda i, j: (0, i)),
        ],
        out_shape=[
            jax.ShapeDtypeStruct((s, n_sec * SEC), BF16),
            jax.ShapeDtypeStruct((N_GATES, s), F32),
        ],
        scratch_shapes=[
            pltpu.VMEM((TM_IN, D_MODEL), BF16),
        ],
        compiler_params=_cparams(("arbitrary", "arbitrary")),
        name="inproj",
    )(x2, nw, wt, wt_g, i_bias, f_bias, qw, kw)


def _log_sigmoid(x):
    return jnp.minimum(x, 0.0) - jnp.log1p(jnp.exp(-jnp.abs(x)))


def _mlstm_kernel(qk_ref, halo_ref, v_ref, o_ref, g_ref, cw_ref, cb_ref, nw_ref,
                  wa_ref, wb_ref,
                  out_ref, wa_out_ref, wb_out_ref,
                  stage, act, b_scr, li_scr, ct_scr, n_scr, m_scr):
    i = pl.program_id(0)
    n_chunks = R_ML // CHUNK

    wa_out_ref[...] = wa_ref[...].astype(BF16)
    wb_out_ref[...] = wb_ref[...].astype(BF16)

    @pl.when(i == 0)
    def _():
        ct_scr[...] = jnp.zeros_like(ct_scr)
        n_scr[...] = jnp.zeros_like(n_scr)
        m_scr[...] = jnp.full_like(m_scr, NEG)

    halo = halo_ref[...].astype(F32)
    stage[0:HALO, :] = jnp.where(i == 0, 0.0, halo)
    stage[HALO:HALO + R_ML, :] = qk_ref[...].astype(F32)
    k_scale = M_DQK ** -0.5
    for cs in range(SEC // 128):
        sl = slice(cs * 128, (cs + 1) * 128)
        y = cb_ref[:, sl]
        for t in range(CONV_W):
            y = y + cw_ref[t:t + 1, sl] * stage[HALO - (CONV_W - 1) + t:HALO - (CONV_W - 1) + t + R_ML, sl]
        a = y * jax.nn.sigmoid(y)
        if cs >= M_HEADS:
            a = a * k_scale
        act[:, sl] = a.astype(BF16)

    li_scr[...] = g_ref[0:M_HEADS].reshape(M_HEADS * n_chunks, CHUNK)
    lf = _log_sigmoid(g_ref[M_HEADS:N_GATES].reshape(M_HEADS * n_chunks, CHUNK))
    r_i = lax.broadcasted_iota(jnp.int32, (CHUNK, CHUNK), 0)
    c_i = lax.broadcasted_iota(jnp.int32, (CHUNK, CHUNK), 1)
    upper = jnp.where(r_i <= c_i, 1.0, 0.0).astype(BF16)
    p0 = lf.astype(BF16)
    r1 = lf - p0.astype(F32)
    p1 = r1.astype(BF16)
    p2 = (r1 - p1.astype(F32)).astype(BF16)
    b_scr[...] = (jnp.dot(p0, upper, preferred_element_type=F32)
                  + jnp.dot(p1, upper, preferred_element_type=F32)
                  + jnp.dot(p2, upper, preferred_element_type=F32))

    t_i = lax.broadcasted_iota(jnp.int32, (R_ML, CHUNK), 0) % CHUNK
    s_i = lax.broadcasted_iota(jnp.int32, (R_ML, CHUNK), 1)
    eye = t_i == s_i
    tril = t_i >= s_i

    def rows_of(x, c):
        return x[c * CHUNK:(c + 1) * CHUNK]

    def per_chunk_rows(x):
        return jnp.concatenate(
            [jnp.broadcast_to(x[c:c + 1, :], (CHUNK, x.shape[1])) for c in range(n_chunks)], axis=0)

    def to_col(rows):
        return jnp.sum(jnp.where(eye, rows, 0.0), axis=1, keepdims=True)

    for h in range(M_HEADS):
        qs = slice(h * M_DQK, (h + 1) * M_DQK)
        ks = slice((M_HEADS + h) * M_DQK, (M_HEADS + h + 1) * M_DQK)
        vs = slice(h * M_DV, (h + 1) * M_DV)
        q = act[:, qs]
        k = act[:, ks]
        v = v_ref[:, vs]
        b_h = b_scr[h * n_chunks:(h + 1) * n_chunks, :]
        li_h = li_scr[h * n_chunks:(h + 1) * n_chunks, :]

        b_rows = per_chunk_rows(b_h)
        b_col = to_col(b_rows)
        d = jnp.where(tril, b_col - b_rows + per_chunk_rows(li_h), NEG)
        m_intra = jnp.max(d, axis=1, keepdims=True)
        s = jnp.concatenate(
            [lax.dot_general(rows_of(q, c), rows_of(k, c), (((1,), (1,)), ((), ())),
                             preferred_element_type=F32) for c in range(n_chunks)], axis=0)
        p = jnp.exp(d - m_intra) * s
        row_sum = jnp.sum(p, axis=1, keepdims=True)
        pb = p.astype(BF16)
        n_intra = jnp.concatenate(
            [jnp.dot(rows_of(pb, c), rows_of(v, c), preferred_element_type=F32)
             for c in range(n_chunks)], axis=0)

        g_tot = b_h[:, CHUNK - 1:CHUNK]
        a = g_tot - b_h + li_h
        m_loc = jnp.max(a, axis=1, keepdims=True)
        w_col = to_col(per_chunk_rows(jnp.exp(a - m_loc)))
        kw = k.astype(F32) * w_col
        kwb = kw.astype(BF16)
        ct_locs = [lax.dot_general(rows_of(kwb, c), rows_of(v, c), (((0,), (0,)), ((), ())),
                                   preferred_element_type=F32) for c in range(n_chunks)]
        n_locs = [jnp.sum(rows_of(kw, c), axis=0, keepdims=True) for c in range(n_chunks)]

        m = m_scr[h, 0:1, 0:1]
        ct = ct_scr[h]
        n = n_scr[h, 0:1, :]
        m_prevs, ct_prevs, n_prevs = [], [], []
        for c in range(n_chunks):
            m_prevs.append(m)
            ct_prevs.append(ct.astype(BF16))
            n_prevs.append(n)
            g_c = g_tot[c:c + 1, :]
            m_loc_c = m_loc[c:c + 1, :]
            m_new = jnp.maximum(g_c + m, m_loc_c)
            s_old = jnp.exp(g_c + m - m_new)
            s_loc = jnp.exp(m_loc_c - m_new)
            ct = s_old * ct + s_loc * ct_locs[c]
            n = s_old * n + s_loc * n_locs[c]
            m = m_new
        ct_scr[h] = ct
        n_scr[h] = jnp.broadcast_to(n, (8, M_DQK))
        m_scr[h] = jnp.broadcast_to(m, (8, 128))

        m_prev = per_chunk_rows(jnp.concatenate(m_prevs, axis=0))
        n_prev = per_chunk_rows(jnp.concatenate(n_prevs, axis=0))
        inter = jnp.concatenate(
            [jnp.dot(rows_of(q, c), ct_prevs[c], preferred_element_type=F32)
             for c in range(n_chunks)], axis=0)
        inter_log = b_col + m_prev
        m_t = jnp.maximum(inter_log, m_intra)
        s_intra = jnp.exp(m_intra - m_t)
        s_inter = jnp.exp(inter_log - m_t)
        num = s_intra * n_intra + s_inter * inter
        den = (s_intra * row_sum
               + s_inter * jnp.sum(q.astype(F32) * n_prev, axis=1, keepdims=True))
        hout = num / jnp.maximum(jnp.abs(den), jnp.exp(-m_t))

        ms = jnp.mean(hout * hout, axis=1, keepdims=True)
        hn = hout * lax.rsqrt(ms + RMS_EPS) * nw_ref[h:h + 1, :]
        out_ref[:, vs] = (hn * jax.nn.sigmoid(o_ref[:, vs].astype(F32))).astype(BF16)


def _row_slab(w, n_steps, step_of):
    rows, cols = w.shape
    assert rows % (n_steps * BF16_SUBLANES) == 0, (w.shape, n_steps)
    return pl.BlockSpec((rows // n_steps, cols), lambda *idx: (step_of(*idx), 0))


def _mlstm(proj, gates3, conv_w, conv_b, m_norm_w, w_a, w_b):
    s = proj.shape[0]
    n_chunks = R_ML // CHUNK
    hb = R_ML // HALO
    n_steps = s // R_ML
    return pl.pallas_call(
        _mlstm_kernel,
        grid=(n_steps,),
        in_specs=[
            pl.BlockSpec((R_ML, SEC), lambda i: (i, 0)),
            pl.BlockSpec((HALO, SEC), lambda i: (jnp.maximum(i * hb - 1, 0), 0)),
            pl.BlockSpec((R_ML, SEC), lambda i: (i, 1)),
            pl.BlockSpec((R_ML, SEC), lambda i: (i, 2)),
            pl.BlockSpec((N_GATES, n_chunks, CHUNK), lambda i: (0, i, 0)),
            pl.BlockSpec((CONV_W, SEC), lambda i: (0, 0)),
            pl.BlockSpec((1, SEC), lambda i: (0, 0)),
            pl.BlockSpec((M_HEADS, M_DV), lambda i: (0, 0)),
            _row_slab(w_a, n_steps, lambda i: i),
            _row_slab(w_b, n_steps, lambda i: i),
        ],
        out_specs=[
            pl.BlockSpec((R_ML, SEC), lambda i: (i, 0)),
            _row_slab(w_a, n_steps, lambda i: i),
            _row_slab(w_b, n_steps, lambda i: i),
        ],
        out_shape=[
            jax.ShapeDtypeStruct((s, SEC), BF16),
            jax.ShapeDtypeStruct(w_a.shape, BF16),
            jax.ShapeDtypeStruct(w_b.shape, BF16),
        ],
        scratch_shapes=[
            pltpu.VMEM((R_ML + HALO, SEC), F32),
            pltpu.VMEM((R_ML, SEC), BF16),
            pltpu.VMEM((M_HEADS * n_chunks, CHUNK), F32),
            pltpu.VMEM((M_HEADS * n_chunks, CHUNK), F32),
            pltpu.VMEM((M_HEADS, M_DQK, M_DV), F32),
            pltpu.VMEM((M_HEADS, 8, M_DQK), F32),
            pltpu.VMEM((M_HEADS, 8, 128), F32),
        ],
        compiler_params=_cparams(("arbitrary",)),
        name="mlstm",
    )(proj, proj, proj, proj, gates3, conv_w, conv_b, m_norm_w, w_a, w_b)


def _lane_partial_sum(p):
    acc = p[:, 0:128]
    for t in range(1, p.shape[1] // 128):
        acc = acc + p[:, t * 128:(t + 1) * 128]
    return acc


def _attn_kernel(par_ref, q_ref, k_ref, v_ref, lq1_ref, lk1_ref, lq2_ref, lk2_ref, nw_ref,
                 wa_ref, wb_ref,
                 out_ref, wa_out_ref, wb_out_ref,
                 m_scr, l_scr, acc_scr, sa_scr, sb_scr):
    i = pl.program_id(1)
    nq = pl.num_programs(1)
    shift = par_ref[0]
    fixed = par_ref[1] > 0.5

    def cast_weight_slabs():
        wa_out_ref[...] = wa_ref[...].astype(BF16)
        wb_out_ref[...] = wb_ref[...].astype(BF16)

    def reset_sums():
        l_scr[...] = jnp.zeros_like(l_scr)
        acc_scr[...] = jnp.zeros_like(acc_scr)

    ALL, LOWER = slice(0, TQ), slice(TK, TQ)

    def chunk_mask(n_rows):
        qc = lax.broadcasted_iota(jnp.int32, (n_rows, TK), 0) // CHUNK
        kc = lax.broadcasted_iota(jnp.int32, (n_rows, TK), 1) // CHUNK
        return kc <= qc

    def k_rows(j):
        return pl.ds(pl.multiple_of(j * TK, TK), TK)

    def scores(qi, j, c, rows):
        q0 = pl.multiple_of(qi * TQ + rows.start, TK)
        q = q_ref[pl.ds(q0, rows.stop - rows.start), c * A_DQK:(c + 1) * A_DQK]
        k = k_ref[k_rows(j), c * A_DQK:(c + 1) * A_DQK]
        return lax.dot_general(q, k, (((1,), (1,)), ((), ())), preferred_element_type=F32)

    def fixed_scores(qi, j, s_buf, rows=ALL):
        for c in range(2):
            s_buf[c, rows, :] = scores(qi, j, c, rows)

    def fixed_pv(j, s_buf, rows=ALL, masked=False):
        v = v_ref[k_rows(j), :]
        for c in range(2):
            p = jnp.exp2(s_buf[c, rows, :] - shift)
            if masked:
                p = jnp.where(chunk_mask(rows.stop - rows.start), p, 0.0)
            l_scr[c, rows, :] += _lane_partial_sum(p)
            acc_scr[c, rows, :] += jnp.dot(p.astype(BF16), v, preferred_element_type=F32)

    def online_tile(j, rows=ALL, masked=False):
        v = v_ref[k_rows(j), :]
        for c in range(2):
            s = scores(i, j, c, rows)
            if masked:
                s = jnp.where(chunk_mask(rows.stop - rows.start), s, NEG)
            m_prev = m_scr[c, rows, :]
            m_new = jnp.maximum(m_prev, jnp.max(s, axis=1, keepdims=True))
            alpha = jnp.exp2(m_prev - m_new)
            p = jnp.exp2(s - m_new[:, 0:1])
            l_scr[c, rows, :] = alpha * l_scr[c, rows, :] + _lane_partial_sum(p)
            acc_scr[c, rows, :] = (alpha[:, 0:1] * acc_scr[c, rows, :]
                                   + jnp.dot(p.astype(BF16), v, preferred_element_type=F32))
            m_scr[c, rows, :] = m_new

    def finalize():
        lam = (jnp.exp(jnp.sum(lq1_ref[...] * lk1_ref[...], axis=1, keepdims=True))
               - jnp.exp(jnp.sum(lq2_ref[...] * lk2_ref[...], axis=1, keepdims=True)) + LAM_INIT)
        inv0 = 1.0 / jnp.sum(l_scr[0], axis=1, keepdims=True)
        inv1 = lam / jnp.sum(l_scr[1], axis=1, keepdims=True)
        o = acc_scr[0] * inv0 - acc_scr[1] * inv1
        ms = jnp.mean(o * o, axis=1, keepdims=True)
        scale = lax.rsqrt(ms + RMS_EPS) * (1.0 - LAM_INIT)
        nw = nw_ref[pl.ds(pl.program_id(0), 1), :]
        out_ref[...] = (o * scale * nw).astype(BF16)

    @pl.when(fixed)
    def _():
        @pl.when(i == 0)
        def _():
            reset_sums()
            fixed_scores(0, 0, sa_scr)

        def pair(t, carry):
            j = 2 * t
            fixed_pv(j, sa_scr)
            fixed_scores(i, j + 1, sb_scr)
            fixed_pv(j + 1, sb_scr)
            fixed_scores(i, j + 2, sa_scr)
            return carry
        lax.fori_loop(0, i, pair, 0)

        def boundary_tiles(more_steps):
            cast_weight_slabs()
            fixed_pv(2 * i, sa_scr, masked=True)
            fixed_scores(i, 2 * i + 1, sb_scr, LOWER)
            fixed_pv(2 * i + 1, sb_scr, LOWER, masked=True)
            finalize()
            if more_steps:
                reset_sums()
                fixed_scores(i + 1, 0, sa_scr)

        @pl.when(i + 1 < nq)
        def _():
            boundary_tiles(True)

        @pl.when(i + 1 == nq)
        def _():
            boundary_tiles(False)

    @pl.when(jnp.logical_not(fixed))
    def _():
        cast_weight_slabs()
        reset_sums()
        m_scr[...] = jnp.full_like(m_scr, NEG)

        def body(j, carry):
            online_tile(j)
            return carry
        lax.fori_loop(0, 2 * i, body, 0)
        online_tile(2 * i, masked=True)
        online_tile(2 * i + 1, LOWER, masked=True)
        finalize()


def _attn(proj, q_norm_w, k_norm_w, lq1, lk1, lq2, lk2, a_norm_w, w_a, w_b):
    s = proj.shape[0]
    qb = 3 * SEC // A_DV
    kb = 4 * SEC // A_DV
    vb = 5 * SEC // A_DV
    bound = (A_DQK ** 0.5 * LOG2E) * jnp.max(jnp.abs(q_norm_w * k_norm_w))
    shift = jnp.ceil(bound * 1.02) + 1.0
    params = jnp.stack([shift, (shift <= MAX_FIXED_SHIFT).astype(F32)]).astype(F32)
    vec = pl.BlockSpec((1, A_DQK), lambda h, i: (0, 0))
    nq = s // TQ
    n_steps = A_HEADS * nq
    return pl.pallas_call(
        _attn_kernel,
        grid=(A_HEADS, nq),
        in_specs=[
            pl.BlockSpec(memory_space=pltpu.SMEM),
            pl.BlockSpec((s, 2 * A_DQK), lambda h, i: (0, qb + h)),
            pl.BlockSpec((s, 2 * A_DQK), lambda h, i: (0, kb + h)),
            pl.BlockSpec((s, A_DV), lambda h, i: (0, vb + h)),
            vec, vec, vec, vec,
            pl.BlockSpec((A_HEADS, A_DV), lambda h, i: (0, 0)),
            _row_slab(w_a, n_steps, lambda h, i: h * nq + i),
            _row_slab(w_b, n_steps, lambda h, i: h * nq + i),
        ],
        out_specs=[
            pl.BlockSpec((TQ, A_DV), lambda h, i: (i, h)),
            _row_slab(w_a, n_steps, lambda h, i: h * nq + i),
            _row_slab(w_b, n_steps, lambda h, i: h * nq + i),
        ],
        out_shape=[
            jax.ShapeDtypeStruct((s, A_HEADS * A_DV), BF16),
            jax.ShapeDtypeStruct(w_a.shape, BF16),
            jax.ShapeDtypeStruct(w_b.shape, BF16),
        ],
        scratch_shapes=[
            pltpu.VMEM((2, TQ, 128), F32),
            pltpu.VMEM((2, TQ, 128), F32),
            pltpu.VMEM((2, TQ, A_DV), F32),
            pltpu.VMEM((2, TQ, TK), F32),
            pltpu.VMEM((2, TQ, TK), F32),
        ],
        compiler_params=_cparams(("arbitrary", "arbitrary")),
        name="diffattn",
    )(params, proj, proj, proj, lq1, lk1, lq2, lk2, a_norm_w, w_a, w_b)


def _outproj_kernel(hm_ref, ha_ref, wt_ref, wb_ref, x_ref, out_ref):
    out_ref[...] = (x_ref[...]
                    + jnp.dot(hm_ref[...], wt_ref[...], preferred_element_type=F32)
                    + jnp.dot(ha_ref[...], wb_ref[...], preferred_element_type=F32))


def _outproj(hm, ha, w_out, x2):
    s = x2.shape[0]
    return pl.pallas_call(
        _outproj_kernel,
        grid=(s // TM_OUT,),
        in_specs=[
            pl.BlockSpec((TM_OUT, SEC), lambda i: (i, 0)),
            pl.BlockSpec((TM_OUT, SEC), lambda i: (i, 0)),
            pl.BlockSpec((SEC, D_MODEL), lambda i: (0, 0)),
            pl.BlockSpec((SEC, D_MODEL), lambda i: (1, 0)),
            pl.BlockSpec((TM_OUT, D_MODEL), lambda i: (i, 0)),
        ],
        out_specs=pl.BlockSpec((TM_OUT, D_MODEL), lambda i: (i, 0)),
        out_shape=jax.ShapeDtypeStruct((s, D_MODEL), F32),
        compiler_params=_cparams(("arbitrary",)),
        name="outproj",
    )(hm, ha, w_out, w_out, x2)


def _ffn_kernel(x_ref, nw_ref, wg_ref, wu_ref, wd_ref, out_ref, h_scr):
    j = pl.program_id(1)

    @pl.when(j == 0)
    def _():
        xf = x_ref[...]
        ms = jnp.mean(xf * xf, axis=1, keepdims=True)
        h_scr[...] = (xf * lax.rsqrt(ms + RMS_EPS) * nw_ref[...]).astype(BF16)
        out_ref[...] = xf

    h = h_scr[...]
    g = jnp.dot(h, wg_ref[...], preferred_element_type=F32)
    u = jnp.dot(h, wu_ref[...], preferred_element_type=F32)
    a = (g * jax.nn.sigmoid(g) * u).astype(BF16)
    out_ref[...] += jnp.dot(a, wd_ref[...], preferred_element_type=F32)


def _ffn(x1, nw, w_gate, w_up, w_down):
    s = x1.shape[0]
    return pl.pallas_call(
        _ffn_kernel,
        grid=(s // TM_FF, D_FF // TF_FF),
        in_specs=[
            pl.BlockSpec((TM_FF, D_MODEL), lambda i, j: (i, 0)),
            pl.BlockSpec((1, D_MODEL), lambda i, j: (0, 0)),
            pl.BlockSpec((D_MODEL, TF_FF), lambda i, j: (0, j)),
            pl.BlockSpec((D_MODEL, TF_FF), lambda i, j: (0, j)),
            pl.BlockSpec((TF_FF, D_MODEL), lambda i, j: (j, 0)),
        ],
        out_specs=pl.BlockSpec((TM_FF, D_MODEL), lambda i, j: (i, 0)),
        out_shape=jax.ShapeDtypeStruct((s, D_MODEL), F32),
        scratch_shapes=[pltpu.VMEM((TM_FF, D_MODEL), BF16)],
        compiler_params=_cparams(("arbitrary", "arbitrary")),
        name="swiglu",
    )(x1, nw, w_gate, w_up, w_down)


def _layer(x2, norm1_w, w_in, conv_w, conv_b, i_bias, f_bias, m_norm_w, q_norm_w, k_norm_w,
           lambda_q1, lambda_k1, lambda_q2, lambda_k2, a_norm_w, w_out, norm2_w,
           w_gate, w_up, w_down):
    s = x2.shape[0]
    wt, wt_g = _wprep(w_in.T)
    proj, gates_t = _inproj(x2, norm1_w[None, :], wt, wt_g, i_bias, f_bias,
                            q_norm_w[None, :], k_norm_w[None, :])
    gates3 = gates_t.reshape(N_GATES, s // CHUNK, CHUNK)
    hm, w_out_b, w_down_b = _mlstm(proj, gates3, conv_w, conv_b[None, :],
                                   m_norm_w, w_out, w_down)
    ha, w_gate_b, w_up_b = _attn(proj, q_norm_w, k_norm_w, lambda_q1[None, :],
                                 lambda_k1[None, :], lambda_q2[None, :], lambda_k2[None, :],
                                 a_norm_w, w_gate, w_up)
    x1 = _outproj(hm, ha, w_out_b, x2)
    return _ffn(x1, norm2_w[None, :], w_gate_b, w_up_b, w_down_b)


def kernel(x, norm1_w, w_in, conv_w, conv_b, i_bias, f_bias, m_norm_w, q_norm_w, k_norm_w,
           lambda_q1, lambda_k1, lambda_q2, lambda_k2, a_norm_w, w_out, norm2_w,
           w_gate, w_up, w_down):
    b, s, d = x.shape
    assert d == D_MODEL and b == 1 and norm1_w.shape[0] == 1
    assert s % R_ML == 0 and s % TQ == 0 and s % TM_IN == 0 and s % TM_FF == 0
    y = _layer(x.reshape(s, d), norm1_w[0], w_in[0], conv_w[0], conv_b[0], i_bias[0], f_bias[0],
               m_norm_w[0], q_norm_w[0], k_norm_w[0], lambda_q1[0], lambda_k1[0],
               lambda_q2[0], lambda_k2[0], a_norm_w[0], w_out[0], norm2_w[0],
               w_gate[0], w_up[0], w_down[0])
    return y.reshape(b, s, d)
```

```python
import math

import jax
import jax.numpy as jnp
from jax import lax
from jax.experimental import pallas as pl
from jax.experimental.pallas import tpu as pltpu

F32 = jnp.float32
BF16 = jnp.bfloat16

D_MODEL = 2048
CHUNK = 64
SEC = 1024
N_SEC_LO = 3
M_HEADS = 4
M_DV = 256
M_DQK = 128
CONV_W = 4
A_HEADS = 4
A_DV = 256
A_DQK = 128
D_FF = 5632
RMS_EPS = 1e-6
NEG = -1e30
LAM_INIT = 0.8 - 0.6 * math.exp(-0.3 * 0)
LOG2E = 1.4426950408889634
MAX_FIXED_SHIFT = 60.0
GATE_PAD = 128
N_GATES = 2 * M_HEADS
HALO = 8
BF16_SUBLANES = 16

TR_PREP = 512
TM_IN = 1024
TM_OUT = 512
TM_FF = 1024
TF_FF = 512
R_ML = 1024
TQ = 1024
TK = TQ // 2

VMEM_LIMIT = 56 * 1024 * 1024


def _cparams(sem):
    return pltpu.CompilerParams(dimension_semantics=sem, vmem_limit_bytes=VMEM_LIMIT)


def _wprep_kernel(lo_ref, hi_ref, hi_next_ref, w_out, g_out):
    k = pl.program_id(0)
    n = pl.num_programs(0) // 2

    @pl.when(k < n)
    def _():
        w_out[...] = lo_ref[...].astype(BF16)

    @pl.when(k >= n)
    def _():
        w_out[...] = jnp.concatenate([hi_ref[N_GATES:, :], hi_next_ref[...]],
                                     axis=0).astype(BF16)

    @pl.when(k == n)
    def _():
        g_out[...] = hi_ref[0:GATE_PAD, :].astype(BF16)


def _wprep(wt):
    n_in, d = wt.shape
    g0 = N_SEC_LO * SEC
    assert n_in == 2 * g0 + N_GATES and g0 % TR_PREP == 0 and TR_PREP % N_GATES == 0
    n = g0 // TR_PREP
    per_blk = TR_PREP // N_GATES

    def hi_blk(k):
        return n + jnp.maximum(k - n, 0)

    return pl.pallas_call(
        _wprep_kernel,
        grid=(2 * n,),
        in_specs=[
            pl.BlockSpec((TR_PREP, d), lambda k: (jnp.minimum(k, n - 1), 0)),
            pl.BlockSpec((TR_PREP, d), lambda k: (hi_blk(k), 0)),
            pl.BlockSpec((N_GATES, d), lambda k: ((hi_blk(k) + 1) * per_blk, 0)),
        ],
        out_specs=[
            pl.BlockSpec((TR_PREP, d), lambda k: (k, 0)),
            pl.BlockSpec((GATE_PAD, d), lambda k: (0, 0)),
        ],
        out_shape=[
            jax.ShapeDtypeStruct((2 * g0, d), BF16),
            jax.ShapeDtypeStruct((GATE_PAD, d), BF16),
        ],
        compiler_params=_cparams(("arbitrary",)),
        name="wprep",
    )(wt, wt, wt)


def _dot_nt(a, b_t):
    return lax.dot_general(a, b_t, (((1,), (1,)), ((), ())), preferred_element_type=F32)


def _inproj_kernel(x_ref, nw_ref, w_ref, wg_ref, ib_ref, fb_ref, qw_ref, kw_ref,
                   out_ref, gt_ref, h_scr):
    j = pl.program_id(1)
    halves = (slice(0, SEC), slice(SEC, 2 * SEC))

    def plain(half):
        out_ref[:, half] = _dot_nt(h_scr[...], w_ref[half, :]).astype(BF16)

    def qk_normed(half, w):
        acc = _dot_nt(h_scr[...], w_ref[half, :])
        for g in range(SEC // A_DQK):
            a = acc[:, g * A_DQK:(g + 1) * A_DQK]
            ms = jnp.mean(a * a, axis=1, keepdims=True)
            cols = slice(half.start + g * A_DQK, half.start + (g + 1) * A_DQK)
            out_ref[:, cols] = (a * lax.rsqrt(ms + RMS_EPS) * w).astype(BF16)

    @pl.when(j == 0)
    def _():
        xf = x_ref[...]
        ms = jnp.mean(xf * xf, axis=1, keepdims=True)
        hb = (xf * lax.rsqrt(ms + RMS_EPS) * nw_ref[...]).astype(BF16)
        h_scr[...] = hb
        row = lax.broadcasted_iota(jnp.int32, (N_GATES, 1), 0)
        bias = jnp.zeros((N_GATES, 1), F32)
        for r in range(N_GATES):
            b_r = ib_ref[r] if r < M_HEADS else fb_ref[r - M_HEADS]
            bias = jnp.where(row == r, b_r, bias)
        gt_ref[...] = _dot_nt(hb, wg_ref[...]).T[0:N_GATES, :] + bias
        plain(halves[0])
        plain(halves[1])

    @pl.when(j == 1)
    def _():
        qk_normed(halves[1], qw_ref[...] * (A_DQK ** -0.5 * LOG2E))
        plain(halves[0])

    @pl.when(j == 2)
    def _():
        qk_normed(halves[0], kw_ref[...])
        plain(halves[1])


def _inproj(x2, nw, wt, wt_g, i_bias, f_bias, qw, kw):
    s = x2.shape[0]
    n_sec = wt.shape[0] // SEC
    assert n_sec == 2 * N_SEC_LO == 6
    return pl.pallas_call(
        _inproj_kernel,
        grid=(s // TM_IN, n_sec // 2),
        in_specs=[
            pl.BlockSpec((TM_IN, D_MODEL), lambda i, j: (i, 0)),
            pl.BlockSpec((1, D_MODEL), lambda i, j: (0, 0)),
            pl.BlockSpec((2 * SEC, D_MODEL), lambda i, j: (j, 0)),
            pl.BlockSpec((GATE_PAD, D_MODEL), lambda i, j: (0, 0)),
            pl.BlockSpec(memory_space=pltpu.SMEM),
            pl.BlockSpec(memory_space=pltpu.SMEM),
            pl.BlockSpec((1, A_DQK), lambda i, j: (0, 0)),
            pl.BlockSpec((1, A_DQK), lambda i, j: (0, 0)),
        ],
        out_specs=[
            pl.BlockSpec((TM_IN, 2 * SEC), lambda i, j: (i, j)),
            pl.BlockSpec((N_GATES, TM_IN), lambda i, j: (0, i)),
        ],
        out_shape=[
            jax.ShapeDtypeStruct((s, n_sec * SEC), BF16),
            jax.ShapeDtypeStruct((N_GATES, s), F32),
        ],
        scratch_shapes=[
            pltpu.VMEM((TM_IN, D_MODEL), BF16),
        ],
        compiler_params=_cparams(("arbitrary", "arbitrary")),
        name="inproj",
    )(x2, nw, wt, wt_g, i_bias, f_bias, qw, kw)


def _log_sigmoid(x):
    return jnp.minimum(x, 0.0) - jnp.log1p(jnp.exp(-jnp.abs(x)))


def _mlstm_kernel(qk_ref, halo_ref, v_ref, o_ref, g_ref, cw_ref, cb_ref, nw_ref,
                  wa_ref, wb_ref,
                  out_ref, wa_out_ref, wb_out_ref,
                  stage, act, b_scr, li_scr, ct_scr, n_scr, m_scr):
    i = pl.program_id(0)
    n_chunks = R_ML // CHUNK

    wa_out_ref[...] = wa_ref[...].astype(BF16)
    wb_out_ref[...] = wb_ref[...].astype(BF16)

    @pl.when(i == 0)
    def _():
        ct_scr[...] = jnp.zeros_like(ct_scr)
        n_scr[...] = jnp.zeros_like(n_scr)
        m_scr[...] = jnp.full_like(m_scr, NEG)

    halo = halo_ref[...].astype(F32)
    stage[0:HALO, :] = jnp.where(i == 0, 0.0, halo)
    stage[HALO:HALO + R_ML, :] = qk_ref[...].astype(F32)
    k_scale = M_DQK ** -0.5
    for cs in range(SEC // 128):
        sl = slice(cs * 128, (cs + 1) * 128)
        y = cb_ref[:, sl]
        for t in range(CONV_W):
            y = y + cw_ref[t:t + 1, sl] * stage[HALO - (CONV_W - 1) + t:HALO - (CONV_W - 1) + t + R_ML, sl]
        a = y * jax.nn.sigmoid(y)
        if cs >= M_HEADS:
            a = a * k_scale
        act[:, sl] = a.astype(BF16)

    li_scr[...] = g_ref[0:M_HEADS].reshape(M_HEADS * n_chunks, CHUNK)
    lf = _log_sigmoid(g_ref[M_HEADS:N_GATES].reshape(M_HEADS * n_chunks, CHUNK))
    r_i = lax.broadcasted_iota(jnp.int32, (CHUNK, CHUNK), 0)
    c_i = lax.broadcasted_iota(jnp.int32, (CHUNK, CHUNK), 1)
    upper = jnp.where(r_i <= c_i, 1.0, 0.0).astype(BF16)
    p0 = lf.astype(BF16)
    r1 = lf - p0.astype(F32)
    p1 = r1.astype(BF16)
    p2 = (r1 - p1.astype(F32)).astype(BF16)
    b_scr[...] = (jnp.dot(p0, upper, preferred_element_type=F32)
                  + jnp.dot(p1, upper, preferred_element_type=F32)
                  + jnp.dot(p2, upper, preferred_element_type=F32))

    t_i = lax.broadcasted_iota(jnp.int32, (R_ML, CHUNK), 0) % CHUNK
    s_i = lax.broadcasted_iota(jnp.int32, (R_ML, CHUNK), 1)
    eye = t_i == s_i
    tril = t_i >= s_i

    def rows_of(x, c):
        return x[c * CHUNK:(c + 1) * CHUNK]

    def per_chunk_rows(x):
        return jnp.concatenate(
            [jnp.broadcast_to(x[c:c + 1, :], (CHUNK, x.shape[1])) for c in range(n_chunks)], axis=0)

    def to_col(rows):
        return jnp.sum(jnp.where(eye, rows, 0.0), axis=1, keepdims=True)

    for h in range(M_HEADS):
        qs = slice(h * M_DQK, (h + 1) * M_DQK)
        ks = slice((M_HEADS + h) * M_DQK, (M_HEADS + h + 1) * M_DQK)
        vs = slice(h * M_DV, (h + 1) * M_DV)
        q = act[:, qs]
        k = act[:, ks]
        v = v_ref[:, vs]
        b_h = b_scr[h * n_chunks:(h + 1) * n_chunks, :]
        li_h = li_scr[h * n_chunks:(h + 1) * n_chunks, :]

        b_rows = per_chunk_rows(b_h)
        b_col = to_col(b_rows)
        d = jnp.where(tril, b_col - b_rows + per_chunk_rows(li_h), NEG)
        m_intra = jnp.max(d, axis=1, keepdims=True)
        s = jnp.concatenate(
            [lax.dot_general(rows_of(q, c), rows_of(k, c), (((1,), (1,)), ((), ())),
                             preferred_element_type=F32) for c in range(n_chunks)], axis=0)
        p = jnp.exp(d - m_intra) * s
        row_sum = jnp.sum(p, axis=1, keepdims=True)
        pb = p.astype(BF16)
        n_intra = jnp.concatenate(
            [jnp.dot(rows_of(pb, c), rows_of(v, c), preferred_element_type=F32)
             for c in range(n_chunks)], axis=0)

        g_tot = b_h[:, CHUNK - 1:CHUNK]
        a = g_tot - b_h + li_h
        m_loc = jnp.max(a, axis=1, keepdims=True)
        w_col = to_col(per_chunk_rows(jnp.exp(a - m_loc)))
        kw = k.astype(F32) * w_col
        kwb = kw.astype(BF16)
        ct_locs = [lax.dot_general(rows_of(kwb, c), rows_of(v, c), (((0,), (0,)), ((), ())),
                                   preferred_element_type=F32) for c in range(n_chunks)]
        n_locs = [jnp.sum(rows_of(kw, c), axis=0, keepdims=True) for c in range(n_chunks)]

        m = m_scr[h, 0:1, 0:1]
        ct = ct_scr[h]
        n = n_scr[h, 0:1, :]
        m_prevs, ct_prevs, n_prevs = [], [], []
        for c in range(n_chunks):
            m_prevs.append(m)
            ct_prevs.append(ct.astype(BF16))
            n_prevs.append(n)
            g_c = g_tot[c:c + 1, :]
            m_loc_c = m_loc[c:c + 1, :]
            m_new = jnp.maximum(g_c + m, m_loc_c)
            s_old = jnp.exp(g_c + m - m_new)
            s_loc = jnp.exp(m_loc_c - m_new)
            ct = s_old * ct + s_loc * ct_locs[c]
            n = s_old * n + s_loc * n_locs[c]
            m = m_new
        ct_scr[h] = ct
        n_scr[h] = jnp.broadcast_to(n, (8, M_DQK))
        m_scr[h] = jnp.broadcast_to(m, (8, 128))

        m_prev = per_chunk_rows(jnp.concatenate(m_prevs, axis=0))
        n_prev = per_chunk_rows(jnp.concatenate(n_prevs, axis=0))
        inter = jnp.concatenate(
            [jnp.dot(rows_of(q, c), ct_prevs[c], preferred_element_type=F32)
             for c in range(n_chunks)], axis=0)
        inter_log = b_col + m_prev
        m_t = jnp.maximum(inter_log, m_intra)
        s_intra = jnp.exp(m_intra - m_t)
        s_inter = jnp.exp(inter_log - m_t)
        num = s_intra * n_intra + s_inter * inter
        den = (s_intra * row_sum
               + s_inter * jnp.sum(q.astype(F32) * n_prev, axis=1, keepdims=True))
        hout = num / jnp.maximum(jnp.abs(den), jnp.exp(-m_t))

        ms = jnp.mean(hout * hout, axis=1, keepdims=True)
        hn = hout * lax.rsqrt(ms + RMS_EPS) * nw_ref[h:h + 1, :]
        out_ref[:, vs] = (hn * jax.nn.sigmoid(o_ref[:, vs].astype(F32))).astype(BF16)


def _row_slab(w, n_steps, step_of):
    rows, cols = w.shape
    assert rows % (n_steps * BF16_SUBLANES) == 0, (w.shape, n_steps)
    return pl.BlockSpec((rows // n_steps, cols), lambda *idx: (step_of(*idx), 0))


def _mlstm(proj, gates3, conv_w, conv_b, m_norm_w, w_a, w_b):
    s = proj.shape[0]
    n_chunks = R_ML // CHUNK
    hb = R_ML // HALO
    n_steps = s // R_ML
    return pl.pallas_call(
        _mlstm_kernel,
        grid=(n_steps,),
        in_specs=[
            pl.BlockSpec((R_ML, SEC), lambda i: (i, 0)),
            pl.BlockSpec((HALO, SEC), lambda i: (jnp.maximum(i * hb - 1, 0), 0)),
            pl.BlockSpec((R_ML, SEC), lambda i: (i, 1)),
            pl.BlockSpec((R_ML, SEC), lambda i: (i, 2)),
            pl.BlockSpec((N_GATES, n_chunks, CHUNK), lambda i: (0, i, 0)),
            pl.BlockSpec((CONV_W, SEC), lambda i: (0, 0)),
            pl.BlockSpec((1, SEC), lambda i: (0, 0)),
            pl.BlockSpec((M_HEADS, M_DV), lambda i: (0, 0)),
            _row_slab(w_a, n_steps, lambda i: i),
            _row_slab(w_b, n_steps, lambda i: i),
        ],
        out_specs=[
            pl.BlockSpec((R_ML, SEC), lambda i: (i, 0)),
            _row_slab(w_a, n_steps, lambda i: i),
            _row_slab(w_b, n_steps, lambda i: i),
        ],
        out_shape=[
            jax.ShapeDtypeStruct((s, SEC), BF16),
            jax.ShapeDtypeStruct(w_a.shape, BF16),
            jax.ShapeDtypeStruct(w_b.shape, BF16),
        ],
        scratch_shapes=[
            pltpu.VMEM((R_ML + HALO, SEC), F32),
            pltpu.VMEM((R_ML, SEC), BF16),
            pltpu.VMEM((M_HEADS * n_chunks, CHUNK), F32),
            pltpu.VMEM((M_HEADS * n_chunks, CHUNK), F32),
            pltpu.VMEM((M_HEADS, M_DQK, M_DV), F32),
            pltpu.VMEM((M_HEADS, 8, M_DQK), F32),
            pltpu.VMEM((M_HEADS, 8, 128), F32),
        ],
        compiler_params=_cparams(("arbitrary",)),
        name="mlstm",
    )(proj, proj, proj, proj, gates3, conv_w, conv_b, m_norm_w, w_a, w_b)


def _lane_partial_sum(p):
    acc = p[:, 0:128]
    for t in range(1, p.shape[1] // 128):
        acc = acc + p[:, t * 128:(t + 1) * 128]
    return acc


def _attn_kernel(par_ref, q_ref, k_ref, v_ref, lq1_ref, lk1_ref, lq2_ref, lk2_ref, nw_ref,
                 wa_ref, wb_ref,
                 out_ref, wa_out_ref, wb_out_ref,
                 m_scr, l_scr, acc_scr, sa_scr, sb_scr):
    i = pl.program_id(1)
    nq = pl.num_programs(1)
    shift = par_ref[0]
    fixed = par_ref[1] > 0.5

    def cast_weight_slabs():
        wa_out_ref[...] = wa_ref[...].astype(BF16)
        wb_out_ref[...] = wb_ref[...].astype(BF16)

    def reset_sums():
        l_scr[...] = jnp.zeros_like(l_scr)
        acc_scr[...] = jnp.zeros_like(acc_scr)

    ALL, LOWER = slice(0, TQ), slice(TK, TQ)

    def chunk_mask(n_rows):
        qc = lax.broadcasted_iota(jnp.int32, (n_rows, TK), 0) // CHUNK
        kc = lax.broadcasted_iota(jnp.int32, (n_rows, TK), 1) // CHUNK
        return kc <= qc

    def k_rows(j):
        return pl.ds(pl.multiple_of(j * TK, TK), TK)

    def scores(qi, j, c, rows):
        q0 = pl.multiple_of(qi * TQ + rows.start, TK)
        q = q_ref[pl.ds(q0, rows.stop - rows.start), c * A_DQK:(c + 1) * A_DQK]
        k = k_ref[k_rows(j), c * A_DQK:(c + 1) * A_DQK]
        return lax.dot_general(q, k, (((1,), (1,)), ((), ())), preferred_element_type=F32)

    def fixed_scores(qi, j, s_buf, rows=ALL):
        for c in range(2):
            s_buf[c, rows, :] = scores(qi, j, c, rows)

    def fixed_pv(j, s_buf, rows=ALL, masked=False):
        v = v_ref[k_rows(j), :]
        for c in range(2):
            p = jnp.exp2(s_buf[c, rows, :] - shift)
            if masked:
                p = jnp.where(chunk_mask(rows.stop - rows.start), p, 0.0)
            l_scr[c, rows, :] += _lane_partial_sum(p)
            acc_scr[c, rows, :] += jnp.dot(p.astype(BF16), v, preferred_element_type=F32)

    def online_tile(j, rows=ALL, masked=False):
        v = v_ref[k_rows(j), :]
        for c in range(2):
            s = scores(i, j, c, rows)
            if masked:
                s = jnp.where(chunk_mask(rows.stop - rows.start), s, NEG)
            m_prev = m_scr[c, rows, :]
            m_new = jnp.maximum(m_prev, jnp.max(s, axis=1, keepdims=True))
            alpha = jnp.exp2(m_prev - m_new)
            p = jnp.exp2(s - m_new[:, 0:1])
            l_scr[c, rows, :] = alpha * l_scr[c, rows, :] + _lane_partial_sum(p)
            acc_scr[c, rows, :] = (alpha[:, 0:1] * acc_scr[c, rows, :]
                                   + jnp.dot(p.astype(BF16), v, preferred_element_type=F32))
            m_scr[c, rows, :] = m_new

    def finalize():
        lam = (jnp.exp(jnp.sum(lq1_ref[...] * lk1_ref[...], axis=1, keepdims=True))
               - jnp.exp(jnp.sum(lq2_ref[...] * lk2_ref[...], axis=1, keepdims=True)) + LAM_INIT)
        inv0 = 1.0 / jnp.sum(l_scr[0], axis=1, keepdims=True)
        inv1 = lam / jnp.sum(l_scr[1], axis=1, keepdims=True)
        o = acc_scr[0] * inv0 - acc_scr[1] * inv1
        ms = jnp.mean(o * o, axis=1, keepdims=True)
        scale = lax.rsqrt(ms + RMS_EPS) * (1.0 - LAM_INIT)
        nw = nw_ref[pl.ds(pl.program_id(0), 1), :]
        out_ref[...] = (o * scale * nw).astype(BF16)

    @pl.when(fixed)
    def _():
        @pl.when(i == 0)
        def _():
            reset_sums()
            fixed_scores(0, 0, sa_scr)

        def pair(t, carry):
            j = 2 * t
            fixed_pv(j, sa_scr)
            fixed_scores(i, j + 1, sb_scr)
            fixed_pv(j + 1, sb_scr)
            fixed_scores(i, j + 2, sa_scr)
            return carry
        lax.fori_loop(0, i, pair, 0)

        def boundary_tiles(more_steps):
            cast_weight_slabs()
            fixed_pv(2 * i, sa_scr, masked=True)
            fixed_scores(i, 2 * i + 1, sb_scr, LOWER)
            fixed_pv(2 * i + 1, sb_scr, LOWER, masked=True)
            finalize()
            if more_steps:
                reset_sums()
                fixed_scores(i + 1, 0, sa_scr)

        @pl.when(i + 1 < nq)
        def _():
            boundary_tiles(True)

        @pl.when(i + 1 == nq)
        def _():
            boundary_tiles(False)

    @pl.when(jnp.logical_not(fixed))
    def _():
        cast_weight_slabs()
        reset_sums()
        m_scr[...] = jnp.full_like(m_scr, NEG)

        def body(j, carry):
            online_tile(j)
            return carry
        lax.fori_loop(0, 2 * i, body, 0)
        online_tile(2 * i, masked=True)
        online_tile(2 * i + 1, LOWER, masked=True)
        finalize()


def _attn(proj, q_norm_w, k_norm_w, lq1, lk1, lq2, lk2, a_norm_w, w_a, w_b):
    s = proj.shape[0]
    qb = 3 * SEC // A_DV
    kb = 4 * SEC // A_DV
    vb = 5 * SEC // A_DV
    bound = (A_DQK ** 0.5 * LOG2E) * jnp.max(jnp.abs(q_norm_w * k_norm_w))
    shift = jnp.ceil(bound * 1.02) + 1.0
    params = jnp.stack([shift, (shift <= MAX_FIXED_SHIFT).astype(F32)]).astype(F32)
    vec = pl.BlockSpec((1, A_DQK), lambda h, i: (0, 0))
    nq = s // TQ
    n_steps = A_HEADS * nq
    return pl.pallas_call(
        _attn_kernel,
        grid=(A_HEADS, nq),
        in_specs=[
            pl.BlockSpec(memory_space=pltpu.SMEM),
            pl.BlockSpec((s, 2 * A_DQK), lambda h, i: (0, qb + h)),
            pl.BlockSpec((s, 2 * A_DQK), lambda h, i: (0, kb + h)),
            pl.BlockSpec((s, A_DV), lambda h, i: (0, vb + h)),
            vec, vec, vec, vec,
            pl.BlockSpec((A_HEADS, A_DV), lambda h, i: (0, 0)),
            _row_slab(w_a, n_steps, lambda h, i: h * nq + i),
            _row_slab(w_b, n_steps, lambda h, i: h * nq + i),
        ],
        out_specs=[
            pl.BlockSpec((TQ, A_DV), lambda h, i: (i, h)),
            _row_slab(w_a, n_steps, lambda h, i: h * nq + i),
            _row_slab(w_b, n_steps, lambda h, i: h * nq + i),
        ],
        out_shape=[
            jax.ShapeDtypeStruct((s, A_HEADS * A_DV), BF16),
            jax.ShapeDtypeStruct(w_a.shape, BF16),
            jax.ShapeDtypeStruct(w_b.shape, BF16),
        ],
        scratch_shapes=[
            pltpu.VMEM((2, TQ, 128), F32),
            pltpu.VMEM((2, TQ, 128), F32),
            pltpu.VMEM((2, TQ, A_DV), F32),
            pltpu.VMEM((2, TQ, TK), F32),
            pltpu.VMEM((2, TQ, TK), F32),
        ],
        compiler_params=_cparams(("arbitrary", "arbitrary")),
        name="diffattn",
    )(params, proj, proj, proj, lq1, lk1, lq2, lk2, a_norm_w, w_a, w_b)


def _outproj_kernel(hm_ref, ha_ref, wt_ref, wb_ref, x_ref, out_ref):
    out_ref[...] = (x_ref[...]
                    + jnp.dot(hm_ref[...], wt_ref[...], preferred_element_type=F32)
                    + jnp.dot(ha_ref[...], wb_ref[...], preferred_element_type=F32))


def _outproj(hm, ha, w_out, x2):
    s = x2.shape[0]
    return pl.pallas_call(
        _outproj_kernel,
        grid=(s // TM_OUT,),
        in_specs=[
            pl.BlockSpec((TM_OUT, SEC), lambda i: (i, 0)),
            pl.BlockSpec((TM_OUT, SEC), lambda i: (i, 0)),
            pl.BlockSpec((SEC, D_MODEL), lambda i: (0, 0)),
            pl.BlockSpec((SEC, D_MODEL), lambda i: (1, 0)),
            pl.BlockSpec((TM_OUT, D_MODEL), lambda i: (i, 0)),
        ],
        out_specs=pl.BlockSpec((TM_OUT, D_MODEL), lambda i: (i, 0)),
        out_shape=jax.ShapeDtypeStruct((s, D_MODEL), F32),
        compiler_params=_cparams(("arbitrary",)),
        name="outproj",
    )(hm, ha, w_out, w_out, x2)


def _ffn_kernel(x_ref, nw_ref, wg_ref, wu_ref, wd_ref, out_ref, h_scr):
    j = pl.program_id(1)

    def tile(h):
        g = jnp.dot(h, wg_ref[...], preferred_element_type=F32)
        u = jnp.dot(h, wu_ref[...], preferred_element_type=F32)
        a = (g * jax.nn.sigmoid(g) * u).astype(BF16)
        return jnp.dot(a, wd_ref[...], preferred_element_type=F32)

    @pl.when(j == 0)
    def _():
        xf = x_ref[...]
        ms = jnp.mean(xf * xf, axis=1, keepdims=True)
        hb = (xf * lax.rsqrt(ms + RMS_EPS) * nw_ref[...]).astype(BF16)
        h_scr[...] = hb
        out_ref[...] = xf + tile(hb)

    @pl.when(j > 0)
    def _():
        out_ref[...] += tile(h_scr[...])


def _ffn(x1, nw, w_gate, w_up, w_down):
    s = x1.shape[0]
    return pl.pallas_call(
        _ffn_kernel,
        grid=(s // TM_FF, D_FF // TF_FF),
        in_specs=[
            pl.BlockSpec((TM_FF, D_MODEL), lambda i, j: (i, 0)),
            pl.BlockSpec((1, D_MODEL), lambda i, j: (0, 0)),
            pl.BlockSpec((D_MODEL, TF_FF), lambda i, j: (0, j)),
            pl.BlockSpec((D_MODEL, TF_FF), lambda i, j: (0, j)),
            pl.BlockSpec((TF_FF, D_MODEL), lambda i, j: (j, 0)),
        ],
        out_specs=pl.BlockSpec((TM_FF, D_MODEL), lambda i, j: (i, 0)),
        out_shape=jax.ShapeDtypeStruct((s, D_MODEL), F32),
        scratch_shapes=[pltpu.VMEM((TM_FF, D_MODEL), BF16)],
        compiler_params=_cparams(("arbitrary", "arbitrary")),
        name="swiglu",
    )(x1, nw, w_gate, w_up, w_down)


def _layer(x2, norm1_w, w_in, conv_w, conv_b, i_bias, f_bias, m_norm_w, q_norm_w, k_norm_w,
           lambda_q1, lambda_k1, lambda_q2, lambda_k2, a_norm_w, w_out, norm2_w,
           w_gate, w_up, w_down):
    s = x2.shape[0]
    wt, wt_g = _wprep(w_in.T)
    proj, gates_t = _inproj(x2, norm1_w[None, :], wt, wt_g, i_bias, f_bias,
                            q_norm_w[None, :], k_norm_w[None, :])
    gates3 = gates_t.reshape(N_GATES, s // CHUNK, CHUNK)
    hm, w_out_b, w_down_b = _mlstm(proj, gates3, conv_w, conv_b[None, :],
                                   m_norm_w, w_out, w_down)
    ha, w_gate_b, w_up_b = _attn(proj, q_norm_w, k_norm_w, lambda_q1[None, :],
                                 lambda_k1[None, :], lambda_q2[None, :], lambda_k2[None, :],
                                 a_norm_w, w_gate, w_up)
    x1 = _outproj(hm, ha, w_out_b, x2)
    return _ffn(x1, norm2_w[None, :], w_gate_b, w_up_b, w_down_b)


def kernel(x, norm1_w, w_in, conv_w, conv_b, i_bias, f_bias, m_norm_w, q_norm_w, k_norm_w,
           lambda_q1, lambda_k1, lambda_q2, lambda_k2, a_norm_w, w_out, norm2_w,
           w_gate, w_up, w_down):
    b, s, d = x.shape
    assert d == D_MODEL and b == 1 and norm1_w.shape[0] == 1
    assert s % R_ML == 0 and s % TQ == 0 and s % TM_IN == 0 and s % TM_FF == 0
    y = _layer(x.reshape(s, d), norm1_w[0], w_in[0], conv_w[0], conv_b[0], i_bias[0], f_bias[0],
               m_norm_w[0], q_norm_w[0], k_norm_w[0], lambda_q1[0], lambda_k1[0],
               lambda_q2[0], lambda_k2[0], a_norm_w[0], w_out[0], norm2_w[0],
               w_gate[0], w_up[0], w_down[0])
    return y.reshape(b, s, d)
```

```python
import math

import jax
import jax.numpy as jnp
from jax import lax
from jax.experimental import pallas as pl
from jax.experimental.pallas import tpu as pltpu

F32 = jnp.float32
BF16 = jnp.bfloat16

D_MODEL = 2048
CHUNK = 64
SEC = 1024
N_SEC_LO = 3
M_HEADS = 4
M_DV = 256
M_DQK = 128
CONV_W = 4
A_HEADS = 4
A_DV = 256
A_DQK = 128
D_FF = 5632
RMS_EPS = 1e-6
NEG = -1e30
LAM_INIT = 0.8 - 0.6 * math.exp(-0.3 * 0)
LOG2E = 1.4426950408889634
MAX_FIXED_SHIFT = 60.0
GATE_PAD = 128
N_GATES = 2 * M_HEADS
HALO = 8
BF16_SUBLANES = 16

TR_PREP = 512
TM_IN = 1024
TM_OUT = 512
TM_FF = 1024
TF_FF = 512
R_ML = 1024
TQ = 1024
TK = TQ // 2

VMEM_LIMIT = 56 * 1024 * 1024


def _cparams(sem):
    return pltpu.CompilerParams(dimension_semantics=sem, vmem_limit_bytes=VMEM_LIMIT)


def _wprep_kernel(lo_ref, hi_ref, hi_next_ref, w_out, g_out):
    k = pl.program_id(0)
    n = pl.num_programs(0) // 2

    @pl.when(k < n)
    def _():
        w_out[...] = lo_ref[...].astype(BF16)

    @pl.when(k >= n)
    def _():
        w_out[...] = jnp.concatenate([hi_ref[N_GATES:, :], hi_next_ref[...]],
                                     axis=0).astype(BF16)

    @pl.when(k == n)
    def _():
        g_out[...] = hi_ref[0:GATE_PAD, :].astype(BF16)


def _wprep(wt):
    n_in, d = wt.shape
    g0 = N_SEC_LO * SEC
    assert n_in == 2 * g0 + N_GATES and g0 % TR_PREP == 0 and TR_PREP % N_GATES == 0
    n = g0 // TR_PREP
    per_blk = TR_PREP // N_GATES

    def hi_blk(k):
        return n + jnp.maximum(k - n, 0)

    return pl.pallas_call(
        _wprep_kernel,
        grid=(2 * n,),
        in_specs=[
            pl.BlockSpec((TR_PREP, d), lambda k: (jnp.minimum(k, n - 1), 0)),
            pl.BlockSpec((TR_PREP, d), lambda k: (hi_blk(k), 0)),
            pl.BlockSpec((N_GATES, d), lambda k: ((hi_blk(k) + 1) * per_blk, 0)),
        ],
        out_specs=[
            pl.BlockSpec((TR_PREP, d), lambda k: (k, 0)),
            pl.BlockSpec((GATE_PAD, d), lambda k: (0, 0)),
        ],
        out_shape=[
            jax.ShapeDtypeStruct((2 * g0, d), BF16),
            jax.ShapeDtypeStruct((GATE_PAD, d), BF16),
        ],
        compiler_params=_cparams(("arbitrary",)),
        name="wprep",
    )(wt, wt, wt)


def _dot_nt(a, b_t):
    return lax.dot_general(a, b_t, (((1,), (1,)), ((), ())), preferred_element_type=F32)


def _inproj_kernel(x_ref, nw_ref, w_ref, wg_ref, ib_ref, fb_ref, qw_ref, kw_ref,
                   out_ref, gt_ref, h_scr):
    j = pl.program_id(1)
    halves = (slice(0, SEC), slice(SEC, 2 * SEC))

    def plain(half):
        out_ref[:, half] = _dot_nt(h_scr[...], w_ref[half, :]).astype(BF16)

    def qk_normed(half, w):
        acc = _dot_nt(h_scr[...], w_ref[half, :])
        for g in range(SEC // A_DQK):
            a = acc[:, g * A_DQK:(g + 1) * A_DQK]
            ms = jnp.mean(a * a, axis=1, keepdims=True)
            cols = slice(half.start + g * A_DQK, half.start + (g + 1) * A_DQK)
            out_ref[:, cols] = (a * lax.rsqrt(ms + RMS_EPS) * w).astype(BF16)

    @pl.when(j == 0)
    def _():
        xf = x_ref[...]
        ms = jnp.mean(xf * xf, axis=1, keepdims=True)
        hb = (xf * lax.rsqrt(ms + RMS_EPS) * nw_ref[...]).astype(BF16)
        h_scr[...] = hb
        row = lax.broadcasted_iota(jnp.int32, (N_GATES, 1), 0)
        bias = jnp.zeros((N_GATES, 1), F32)
        for r in range(N_GATES):
            b_r = ib_ref[r] if r < M_HEADS else fb_ref[r - M_HEADS]
            bias = jnp.where(row == r, b_r, bias)
        gt_ref[...] = _dot_nt(hb, wg_ref[...]).T[0:N_GATES, :] + bias
        plain(halves[0])
        plain(halves[1])

    @pl.when(j == 1)
    def _():
        qk_normed(halves[1], qw_ref[...] * (A_DQK ** -0.5 * LOG2E))
        plain(halves[0])

    @pl.when(j == 2)
    def _():
        qk_normed(halves[0], kw_ref[...])
        plain(halves[1])


def _inproj(x2, nw, wt, wt_g, i_bias, f_bias, qw, kw):
    s = x2.shape[0]
    n_sec = wt.shape[0] // SEC
    assert n_sec == 2 * N_SEC_LO == 6
    return pl.pallas_call(
        _inproj_kernel,
        grid=(s // TM_IN, n_sec // 2),
        in_specs=[
            pl.BlockSpec((TM_IN, D_MODEL), lambda i, j: (i, 0)),
            pl.BlockSpec((1, D_MODEL), lambda i, j: (0, 0)),
            pl.BlockSpec((2 * SEC, D_MODEL), lambda i, j: (j, 0)),
            pl.BlockSpec((GATE_PAD, D_MODEL), lambda i, j: (0, 0)),
            pl.BlockSpec(memory_space=pltpu.SMEM),
            pl.BlockSpec(memory_space=pltpu.SMEM),
            pl.BlockSpec((1, A_DQK), lambda i, j: (0, 0)),
            pl.BlockSpec((1, A_DQK), lambda i, j: (0, 0)),
        ],
        out_specs=[
            pl.BlockSpec((TM_IN, 2 * SEC), lambda i, j: (i, j)),
            pl.BlockSpec((N_GATES, TM_IN), lambda i, j: (0, i)),
        ],
        out_shape=[
            jax.ShapeDtypeStruct((s, n_sec * SEC), BF16),
            jax.ShapeDtypeStruct((N_GATES, s), F32),
        ],
        scratch_shapes=[
            pltpu.VMEM((TM_IN, D_MODEL), BF16),
        ],
        compiler_params=_cparams(("arbitrary", "arbitrary")),
        name="inproj",
    )(x2, nw, wt, wt_g, i_bias, f_bias, qw, kw)


def _log_sigmoid(x):
    return jnp.minimum(x, 0.0) - jnp.log1p(jnp.exp(-jnp.abs(x)))


def _mlstm_kernel(qk_ref, halo_ref, v_ref, o_ref, g_ref, cw_ref, cb_ref, nw_ref,
                  wa_ref, wb_ref,
                  out_ref, wa_out_ref, wb_out_ref,
                  stage, act, b_scr, li_scr, ct_scr, n_scr, m_scr):
    i = pl.program_id(0)
    n_chunks = R_ML // CHUNK

    wa_out_ref[...] = wa_ref[...].astype(BF16)
    wb_out_ref[...] = wb_ref[...].astype(BF16)

    @pl.when(i == 0)
    def _():
        ct_scr[...] = jnp.zeros_like(ct_scr)
        n_scr[...] = jnp.zeros_like(n_scr)
        m_scr[...] = jnp.full_like(m_scr, NEG)

    halo = halo_ref[...].astype(F32)
    stage[0:HALO, :] = jnp.where(i == 0, 0.0, halo)
    stage[HALO:HALO + R_ML, :] = qk_ref[...].astype(F32)
    k_scale = M_DQK ** -0.5
    for cs in range(SEC // 128):
        sl = slice(cs * 128, (cs + 1) * 128)
        y = cb_ref[:, sl]
        for t in range(CONV_W):
            y = y + cw_ref[t:t + 1, sl] * stage[HALO - (CONV_W - 1) + t:HALO - (CONV_W - 1) + t + R_ML, sl]
        a = y * jax.nn.sigmoid(y)
        if cs >= M_HEADS:
            a = a * k_scale
        act[:, sl] = a.astype(BF16)

    li_scr[...] = g_ref[0:M_HEADS].reshape(M_HEADS * n_chunks, CHUNK)
    lf = _log_sigmoid(g_ref[M_HEADS:N_GATES].reshape(M_HEADS * n_chunks, CHUNK))
    r_i = lax.broadcasted_iota(jnp.int32, (CHUNK, CHUNK), 0)
    c_i = lax.broadcasted_iota(jnp.int32, (CHUNK, CHUNK), 1)
    upper = jnp.where(r_i <= c_i, 1.0, 0.0).astype(BF16)
    p0 = lf.astype(BF16)
    r1 = lf - p0.astype(F32)
    p1 = r1.astype(BF16)
    p2 = (r1 - p1.astype(F32)).astype(BF16)
    b_scr[...] = (jnp.dot(p0, upper, preferred_element_type=F32)
                  + jnp.dot(p1, upper, preferred_element_type=F32)
                  + jnp.dot(p2, upper, preferred_element_type=F32))

    t_i = lax.broadcasted_iota(jnp.int32, (R_ML, CHUNK), 0) % CHUNK
    s_i = lax.broadcasted_iota(jnp.int32, (R_ML, CHUNK), 1)
    eye = t_i == s_i
    tril = t_i >= s_i

    def rows_of(x, c):
        return x[c * CHUNK:(c + 1) * CHUNK]

    def per_chunk_rows(x):
        return jnp.concatenate(
            [jnp.broadcast_to(x[c:c + 1, :], (CHUNK, x.shape[1])) for c in range(n_chunks)], axis=0)

    def to_col(rows):
        return jnp.sum(jnp.where(eye, rows, 0.0), axis=1, keepdims=True)

    for h in range(M_HEADS):
        qs = slice(h * M_DQK, (h + 1) * M_DQK)
        ks = slice((M_HEADS + h) * M_DQK, (M_HEADS + h + 1) * M_DQK)
        vs = slice(h * M_DV, (h + 1) * M_DV)
        q = act[:, qs]
        k = act[:, ks]
        v = v_ref[:, vs]
        b_h = b_scr[h * n_chunks:(h + 1) * n_chunks, :]
        li_h = li_scr[h * n_chunks:(h + 1) * n_chunks, :]

        b_rows = per_chunk_rows(b_h)
        b_col = to_col(b_rows)
        d = jnp.where(tril, b_col - b_rows + per_chunk_rows(li_h), NEG)
        m_intra = jnp.max(d, axis=1, keepdims=True)
        s = jnp.concatenate(
            [lax.dot_general(rows_of(q, c), rows_of(k, c), (((1,), (1,)), ((), ())),
                             preferred_element_type=F32) for c in range(n_chunks)], axis=0)
        p = jnp.exp(d - m_intra) * s
        row_sum = jnp.sum(p, axis=1, keepdims=True)
        pb = p.astype(BF16)
        n_intra = jnp.concatenate(
            [jnp.dot(rows_of(pb, c), rows_of(v, c), preferred_element_type=F32)
             for c in range(n_chunks)], axis=0)

        g_tot = b_h[:, CHUNK - 1:CHUNK]
        a = g_tot - b_h + li_h
        m_loc = jnp.max(a, axis=1, keepdims=True)
        w_col = to_col(per_chunk_rows(jnp.exp(a - m_loc)))
        kw = k.astype(F32) * w_col
        kwb = kw.astype(BF16)
        ct_locs = [lax.dot_general(rows_of(kwb, c), rows_of(v, c), (((0,), (0,)), ((), ())),
                                   preferred_element_type=F32) for c in range(n_chunks)]
        n_locs = [jnp.sum(rows_of(kw, c), axis=0, keepdims=True) for c in range(n_chunks)]

        m = m_scr[h, 0:1, 0:1]
        ct = ct_scr[h]
        n = n_scr[h, 0:1, :]
        m_prevs, ct_prevs, n_prevs = [], [], []
        for c in range(n_chunks):
            m_prevs.append(m)
            ct_prevs.append(ct.astype(BF16))
            n_prevs.append(n)
            g_c = g_tot[c:c + 1, :]
            m_loc_c = m_loc[c:c + 1, :]
            m_new = jnp.maximum(g_c + m, m_loc_c)
            s_old = jnp.exp(g_c + m - m_new)
            s_loc = jnp.exp(m_loc_c - m_new)
            ct = s_old * ct + s_loc * ct_locs[c]
            n = s_old * n + s_loc * n_locs[c]
            m = m_new
        ct_scr[h] = ct
        n_scr[h] = jnp.broadcast_to(n, (8, M_DQK))
        m_scr[h] = jnp.broadcast_to(m, (8, 128))

        m_prev = per_chunk_rows(jnp.concatenate(m_prevs, axis=0))
        n_prev = per_chunk_rows(jnp.concatenate(n_prevs, axis=0))
        inter = jnp.concatenate(
            [jnp.dot(rows_of(q, c), ct_prevs[c], preferred_element_type=F32)
             for c in range(n_chunks)], axis=0)
        inter_log = b_col + m_prev
        m_t = jnp.maximum(inter_log, m_intra)
        s_intra = jnp.exp(m_intra - m_t)
        s_inter = jnp.exp(inter_log - m_t)
        num = s_intra * n_intra + s_inter * inter
        den = (s_intra * row_sum
               + s_inter * jnp.sum(q.astype(F32) * n_prev, axis=1, keepdims=True))
        hout = num / jnp.maximum(jnp.abs(den), jnp.exp(-m_t))

        ms = jnp.mean(hout * hout, axis=1, keepdims=True)
        hn = hout * lax.rsqrt(ms + RMS_EPS) * nw_ref[h:h + 1, :]
        out_ref[:, vs] = (hn * jax.nn.sigmoid(o_ref[:, vs].astype(F32))).astype(BF16)


def _row_slab(w, n_steps, step_of):
    rows, cols = w.shape
    assert rows % (n_steps * BF16_SUBLANES) == 0, (w.shape, n_steps)
    return pl.BlockSpec((rows // n_steps, cols), lambda *idx: (step_of(*idx), 0))


def _mlstm(proj, gates3, conv_w, conv_b, m_norm_w, w_a, w_b):
    s = proj.shape[0]
    n_chunks = R_ML // CHUNK
    hb = R_ML // HALO
    n_steps = s // R_ML
    return pl.pallas_call(
        _mlstm_kernel,
        grid=(n_steps,),
        in_specs=[
            pl.BlockSpec((R_ML, SEC), lambda i: (i, 0)),
            pl.BlockSpec((HALO, SEC), lambda i: (jnp.maximum(i * hb - 1, 0), 0)),
            pl.BlockSpec((R_ML, SEC), lambda i: (i, 1)),
            pl.BlockSpec((R_ML, SEC), lambda i: (i, 2)),
            pl.BlockSpec((N_GATES, n_chunks, CHUNK), lambda i: (0, i, 0)),
            pl.BlockSpec((CONV_W, SEC), lambda i: (0, 0)),
            pl.BlockSpec((1, SEC), lambda i: (0, 0)),
            pl.BlockSpec((M_HEADS, M_DV), lambda i: (0, 0)),
            _row_slab(w_a, n_steps, lambda i: i),
            _row_slab(w_b, n_steps, lambda i: i),
        ],
        out_specs=[
            pl.BlockSpec((R_ML, SEC), lambda i: (i, 0)),
            _row_slab(w_a, n_steps, lambda i: i),
            _row_slab(w_b, n_steps, lambda i: i),
        ],
        out_shape=[
            jax.ShapeDtypeStruct((s, SEC), BF16),
            jax.ShapeDtypeStruct(w_a.shape, BF16),
            jax.ShapeDtypeStruct(w_b.shape, BF16),
        ],
        scratch_shapes=[
            pltpu.VMEM((R_ML + HALO, SEC), F32),
            pltpu.VMEM((R_ML, SEC), BF16),
            pltpu.VMEM((M_HEADS * n_chunks, CHUNK), F32),
            pltpu.VMEM((M_HEADS * n_chunks, CHUNK), F32),
            pltpu.VMEM((M_HEADS, M_DQK, M_DV), F32),
            pltpu.VMEM((M_HEADS, 8, M_DQK), F32),
            pltpu.VMEM((M_HEADS, 8, 128), F32),
        ],
        compiler_params=_cparams(("arbitrary",)),
        name="mlstm",
    )(proj, proj, proj, proj, gates3, conv_w, conv_b, m_norm_w, w_a, w_b)


def _lane_partial_sum(p):
    acc = p[:, 0:128]
    for t in range(1, p.shape[1] // 128):
        acc = acc + p[:, t * 128:(t + 1) * 128]
    return acc


def _attn_kernel(par_ref, q_ref, k_ref, v_ref, lq1_ref, lk1_ref, lq2_ref, lk2_ref, nw_ref,
                 wa_ref, wb_ref,
                 out_ref, wa_out_ref, wb_out_ref,
                 m_scr, l_scr, acc_scr, sa_scr, sb_scr):
    i = pl.program_id(1)
    nq = pl.num_programs(1)
    shift = par_ref[0]
    fixed = par_ref[1] > 0.5

    def cast_weight_slabs():
        wa_out_ref[...] = wa_ref[...].astype(BF16)
        wb_out_ref[...] = wb_ref[...].astype(BF16)

    def reset_sums():
        l_scr[...] = jnp.zeros_like(l_scr)
        acc_scr[...] = jnp.zeros_like(acc_scr)

    ALL, LOWER = slice(0, TQ), slice(TK, TQ)

    def chunk_mask(n_rows):
        qc = lax.broadcasted_iota(jnp.int32, (n_rows, TK), 0) // CHUNK
        kc = lax.broadcasted_iota(jnp.int32, (n_rows, TK), 1) // CHUNK
        return kc <= qc

    def k_rows(j):
        return pl.ds(pl.multiple_of(j * TK, TK), TK)

    def scores(qi, j, c, rows):
        q0 = pl.multiple_of(qi * TQ + rows.start, TK)
        q = q_ref[pl.ds(q0, rows.stop - rows.start), c * A_DQK:(c + 1) * A_DQK]
        k = k_ref[k_rows(j), c * A_DQK:(c + 1) * A_DQK]
        return lax.dot_general(q, k, (((1,), (1,)), ((), ())), preferred_element_type=F32)

    def fixed_scores(qi, j, s_buf, rows=ALL):
        for c in range(2):
            s_buf[c, rows, :] = scores(qi, j, c, rows)

    def fixed_pv(j, s_buf, rows=ALL, masked=False):
        v = v_ref[k_rows(j), :]
        for c in range(2):
            p = jnp.exp2(s_buf[c, rows, :] - shift)
            if masked:
                p = jnp.where(chunk_mask(rows.stop - rows.start), p, 0.0)
            l_scr[c, rows, :] += _lane_partial_sum(p)
            acc_scr[c, rows, :] += jnp.dot(p.astype(BF16), v, preferred_element_type=F32)

    def online_tile(j, rows=ALL, masked=False):
        v = v_ref[k_rows(j), :]
        for c in range(2):
            s = scores(i, j, c, rows)
            if masked:
                s = jnp.where(chunk_mask(rows.stop - rows.start), s, NEG)
            m_prev = m_scr[c, rows, :]
            m_new = jnp.maximum(m_prev, jnp.max(s, axis=1, keepdims=True))
            alpha = jnp.exp2(m_prev - m_new)
            p = jnp.exp2(s - m_new[:, 0:1])
            l_scr[c, rows, :] = alpha * l_scr[c, rows, :] + _lane_partial_sum(p)
            acc_scr[c, rows, :] = (alpha[:, 0:1] * acc_scr[c, rows, :]
                                   + jnp.dot(p.astype(BF16), v, preferred_element_type=F32))
            m_scr[c, rows, :] = m_new

    def finalize():
        lam = (jnp.exp(jnp.sum(lq1_ref[...] * lk1_ref[...], axis=1, keepdims=True))
               - jnp.exp(jnp.sum(lq2_ref[...] * lk2_ref[...], axis=1, keepdims=True)) + LAM_INIT)
        inv0 = 1.0 / jnp.sum(l_scr[0], axis=1, keepdims=True)
        inv1 = lam / jnp.sum(l_scr[1], axis=1, keepdims=True)
        o = acc_scr[0] * inv0 - acc_scr[1] * inv1
        ms = jnp.mean(o * o, axis=1, keepdims=True)
        scale = lax.rsqrt(ms + RMS_EPS) * (1.0 - LAM_INIT)
        nw = nw_ref[pl.ds(pl.program_id(0), 1), :]
        out_ref[...] = (o * scale * nw).astype(BF16)

    @pl.when(fixed)
    def _():
        @pl.when(i == 0)
        def _():
            reset_sums()
            fixed_scores(0, 0, sa_scr)

        def pair(t, carry):
            j = 2 * t
            fixed_pv(j, sa_scr)
            fixed_scores(i, j + 1, sb_scr)
            fixed_pv(j + 1, sb_scr)
            fixed_scores(i, j + 2, sa_scr)
            return carry
        lax.fori_loop(0, i, pair, 0)

        def boundary_tiles(more_steps):
            cast_weight_slabs()
            fixed_pv(2 * i, sa_scr, masked=True)
            fixed_scores(i, 2 * i + 1, sb_scr, LOWER)
            fixed_pv(2 * i + 1, sb_scr, LOWER, masked=True)
            finalize()
            if more_steps:
                reset_sums()
                fixed_scores(i + 1, 0, sa_scr)

        @pl.when(i + 1 < nq)
        def _():
            boundary_tiles(True)

        @pl.when(i + 1 == nq)
        def _():
            boundary_tiles(False)

    @pl.when(jnp.logical_not(fixed))
    def _():
        cast_weight_slabs()
        reset_sums()
        m_scr[...] = jnp.full_like(m_scr, NEG)

        def body(j, carry):
            online_tile(j)
            return carry
        lax.fori_loop(0, 2 * i, body, 0)
        online_tile(2 * i, masked=True)
        online_tile(2 * i + 1, LOWER, masked=True)
        finalize()


def _attn(proj, q_norm_w, k_norm_w, lq1, lk1, lq2, lk2, a_norm_w, w_a, w_b):
    s = proj.shape[0]
    qb = 3 * SEC // A_DV
    kb = 4 * SEC // A_DV
    vb = 5 * SEC // A_DV
    bound = (A_DQK ** 0.5 * LOG2E) * jnp.max(jnp.abs(q_norm_w * k_norm_w))
    shift = jnp.ceil(bound * 1.02) + 1.0
    params = jnp.stack([shift, (shift <= MAX_FIXED_SHIFT).astype(F32)]).astype(F32)
    vec = pl.BlockSpec((1, A_DQK), lambda h, i: (0, 0))
    nq = s // TQ
    n_steps = A_HEADS * nq
    return pl.pallas_call(
        _attn_kernel,
        grid=(A_HEADS, nq),
        in_specs=[
            pl.BlockSpec(memory_space=pltpu.SMEM),
            pl.BlockSpec((s, 2 * A_DQK), lambda h, i: (0, qb + h)),
            pl.BlockSpec((s, 2 * A_DQK), lambda h, i: (0, kb + h)),
            pl.BlockSpec((s, A_DV), lambda h, i: (0, vb + h)),
            vec, vec, vec, vec,
            pl.BlockSpec((A_HEADS, A_DV), lambda h, i: (0, 0)),
            _row_slab(w_a, n_steps, lambda h, i: h * nq + i),
            _row_slab(w_b, n_steps, lambda h, i: h * nq + i),
        ],
        out_specs=[
            pl.BlockSpec((TQ, A_DV), lambda h, i: (i, h)),
            _row_slab(w_a, n_steps, lambda h, i: h * nq + i),
            _row_slab(w_b, n_steps, lambda h, i: h * nq + i),
        ],
        out_shape=[
            jax.ShapeDtypeStruct((s, A_HEADS * A_DV), BF16),
            jax.ShapeDtypeStruct(w_a.shape, BF16),
            jax.ShapeDtypeStruct(w_b.shape, BF16),
        ],
        scratch_shapes=[
            pltpu.VMEM((2, TQ, 128), F32),
            pltpu.VMEM((2, TQ, 128), F32),
            pltpu.VMEM((2, TQ, A_DV), F32),
            pltpu.VMEM((2, TQ, TK), F32),
            pltpu.VMEM((2, TQ, TK), F32),
        ],
        compiler_params=_cparams(("arbitrary", "arbitrary")),
        name="diffattn",
    )(params, proj, proj, proj, lq1, lk1, lq2, lk2, a_norm_w, w_a, w_b)


def _outproj_kernel(hm_ref, ha_ref, wt_ref, wb_ref, x_ref, out_ref):
    out_ref[...] = (x_ref[...]
                    + jnp.dot(hm_ref[...], wt_ref[...], preferred_element_type=F32)
                    + jnp.dot(ha_ref[...], wb_ref[...], preferred_element_type=F32))


def _outproj(hm, ha, w_out, x2):
    s = x2.shape[0]
    return pl.pallas_call(
        _outproj_kernel,
        grid=(s // TM_OUT,),
        in_specs=[
            pl.BlockSpec((TM_OUT, SEC), lambda i: (i, 0)),
            pl.BlockSpec((TM_OUT, SEC), lambda i: (i, 0)),
            pl.BlockSpec((SEC, D_MODEL), lambda i: (0, 0)),
            pl.BlockSpec((SEC, D_MODEL), lambda i: (1, 0)),
            pl.BlockSpec((TM_OUT, D_MODEL), lambda i: (i, 0)),
        ],
        out_specs=pl.BlockSpec((TM_OUT, D_MODEL), lambda i: (i, 0)),
        out_shape=jax.ShapeDtypeStruct((s, D_MODEL), F32),
        compiler_params=_cparams(("arbitrary",)),
        name="outproj",
    )(hm, ha, w_out, w_out, x2)


def _ffn_kernel(x_hbm, nw_ref, wg_ref, wu_ref, wd_ref, out_ref, h_scr, x_buf, x_sem):
    i = pl.program_id(0)
    j = pl.program_id(1)

    def x_copy(blk):
        rows = pl.ds(pl.multiple_of(blk * TM_FF, TM_FF), TM_FF)
        return pltpu.make_async_copy(x_hbm.at[rows, :], x_buf, x_sem)

    def tile(h):
        g = jnp.dot(h, wg_ref[...], preferred_element_type=F32)
        u = jnp.dot(h, wu_ref[...], preferred_element_type=F32)
        a = (g * jax.nn.sigmoid(g) * u).astype(BF16)
        return jnp.dot(a, wd_ref[...], preferred_element_type=F32)

    @pl.when(j == 0)
    def _():
        @pl.when(i == 0)
        def _():
            x_copy(0).start()
        x_copy(i).wait()
        xf = x_buf[...]
        ms = jnp.mean(xf * xf, axis=1, keepdims=True)
        hb = (xf * lax.rsqrt(ms + RMS_EPS) * nw_ref[...]).astype(BF16)
        h_scr[...] = hb
        out_ref[...] = xf + tile(hb)

    @pl.when(jnp.logical_and(j == 1, i + 1 < pl.num_programs(0)))
    def _():
        x_copy(i + 1).start()

    @pl.when(j > 0)
    def _():
        out_ref[...] += tile(h_scr[...])


def _ffn(x1, nw, w_gate, w_up, w_down):
    s = x1.shape[0]
    return pl.pallas_call(
        _ffn_kernel,
        grid=(s // TM_FF, D_FF // TF_FF),
        in_specs=[
            pl.BlockSpec(memory_space=pl.ANY),
            pl.BlockSpec((1, D_MODEL), lambda i, j: (0, 0)),
            pl.BlockSpec((D_MODEL, TF_FF), lambda i, j: (0, j)),
            pl.BlockSpec((D_MODEL, TF_FF), lambda i, j: (0, j)),
            pl.BlockSpec((TF_FF, D_MODEL), lambda i, j: (j, 0)),
        ],
        out_specs=pl.BlockSpec((TM_FF, D_MODEL), lambda i, j: (i, 0)),
        out_shape=jax.ShapeDtypeStruct((s, D_MODEL), F32),
        scratch_shapes=[
            pltpu.VMEM((TM_FF, D_MODEL), BF16),
            pltpu.VMEM((TM_FF, D_MODEL), F32),
            pltpu.SemaphoreType.DMA(()),
        ],
        compiler_params=_cparams(("arbitrary", "arbitrary")),
        name="swiglu",
    )(x1, nw, w_gate, w_up, w_down)


def _layer(x2, norm1_w, w_in, conv_w, conv_b, i_bias, f_bias, m_norm_w, q_norm_w, k_norm_w,
           lambda_q1, lambda_k1, lambda_q2, lambda_k2, a_norm_w, w_out, norm2_w,
           w_gate, w_up, w_down):
    s = x2.shape[0]
    wt, wt_g = _wprep(w_in.T)
    proj, gates_t = _inproj(x2, norm1_w[None, :], wt, wt_g, i_bias, f_bias,
                            q_norm_w[None, :], k_norm_w[None, :])
    gates3 = gates_t.reshape(N_GATES, s // CHUNK, CHUNK)
    hm, w_out_b, w_down_b = _mlstm(proj, gates3, conv_w, conv_b[None, :],
                                   m_norm_w, w_out, w_down)
    ha, w_gate_b, w_up_b = _attn(proj, q_norm_w, k_norm_w, lambda_q1[None, :],
                                 lambda_k1[None, :], lambda_q2[None, :], lambda_k2[None, :],
                                 a_norm_w, w_gate, w_up)
    x1 = _outproj(hm, ha, w_out_b, x2)
    return _ffn(x1, norm2_w[None, :], w_gate_b, w_up_b, w_down_b)


def kernel(x, norm1_w, w_in, conv_w, conv_b, i_bias, f_bias, m_norm_w, q_norm_w, k_norm_w,
           lambda_q1, lambda_k1, lambda_q2, lambda_k2, a_norm_w, w_out, norm2_w,
           w_gate, w_up, w_down):
    b, s, d = x.shape
    assert d == D_MODEL and b == 1 and norm1_w.shape[0] == 1
    assert s % R_ML == 0 and s % TQ == 0 and s % TM_IN == 0 and s % TM_FF == 0
    y = _layer(x.reshape(s, d), norm1_w[0], w_in[0], conv_w[0], conv_b[0], i_bias[0], f_bias[0],
               m_norm_w[0], q_norm_w[0], k_norm_w[0], lambda_q1[0], lambda_k1[0],
               lambda_q2[0], lambda_k2[0], a_norm_w[0], w_out[0], norm2_w[0],
               w_gate[0], w_up[0], w_down[0])
    return y.reshape(b, s, d)
```

```python
import math

import jax
import jax.numpy as jnp
from jax import lax
from jax.experimental import pallas as pl
from jax.experimental.pallas import tpu as pltpu

F32 = jnp.float32
BF16 = jnp.bfloat16

D_MODEL = 2048
CHUNK = 64
SEC = 1024
N_SEC_LO = 3
M_HEADS = 4
M_DV = 256
M_DQK = 128
CONV_W = 4
A_HEADS = 4
A_DV = 256
A_DQK = 128
D_FF = 5632
RMS_EPS = 1e-6
NEG = -1e30
LAM_INIT = 0.8 - 0.6 * math.exp(-0.3 * 0)
LOG2E = 1.4426950408889634
MAX_FIXED_SHIFT = 60.0
GATE_PAD = 128
N_GATES = 2 * M_HEADS
HALO = 8
BF16_SUBLANES = 16

TR_PREP = 512
TM_IN = 1024
TM_OUT = 512
TM_FF = 1024
TF_FF = 512
FF_PER_STEP = 2
FF_STEPS = -(-(D_FF // TF_FF) // FF_PER_STEP)
FF_LAST = D_FF // TF_FF - (FF_STEPS - 1) * FF_PER_STEP
R_ML = 1024
TQ = 1024
TK = TQ // 2

VMEM_LIMIT = 56 * 1024 * 1024
VMEM_LIMIT_FF = 62 * 1024 * 1024


def _cparams(sem, vmem_limit=VMEM_LIMIT):
    return pltpu.CompilerParams(dimension_semantics=sem, vmem_limit_bytes=vmem_limit)


def _wprep_kernel(lo_ref, hi_ref, hi_next_ref, w_out, g_out):
    k = pl.program_id(0)
    n = pl.num_programs(0) // 2

    @pl.when(k < n)
    def _():
        w_out[...] = lo_ref[...].astype(BF16)

    @pl.when(k >= n)
    def _():
        w_out[...] = jnp.concatenate([hi_ref[N_GATES:, :], hi_next_ref[...]],
                                     axis=0).astype(BF16)

    @pl.when(k == n)
    def _():
        g_out[...] = hi_ref[0:GATE_PAD, :].astype(BF16)


def _wprep(wt):
    n_in, d = wt.shape
    g0 = N_SEC_LO * SEC
    assert n_in == 2 * g0 + N_GATES and g0 % TR_PREP == 0 and TR_PREP % N_GATES == 0
    n = g0 // TR_PREP
    per_blk = TR_PREP // N_GATES

    def hi_blk(k):
        return n + jnp.maximum(k - n, 0)

    return pl.pallas_call(
        _wprep_kernel,
        grid=(2 * n,),
        in_specs=[
            pl.BlockSpec((TR_PREP, d), lambda k: (jnp.minimum(k, n - 1), 0)),
            pl.BlockSpec((TR_PREP, d), lambda k: (hi_blk(k), 0)),
            pl.BlockSpec((N_GATES, d), lambda k: ((hi_blk(k) + 1) * per_blk, 0)),
        ],
        out_specs=[
            pl.BlockSpec((TR_PREP, d), lambda k: (k, 0)),
            pl.BlockSpec((GATE_PAD, d), lambda k: (0, 0)),
        ],
        out_shape=[
            jax.ShapeDtypeStruct((2 * g0, d), BF16),
            jax.ShapeDtypeStruct((GATE_PAD, d), BF16),
        ],
        compiler_params=_cparams(("arbitrary",)),
        name="wprep",
    )(wt, wt, wt)


def _dot_nt(a, b_t):
    return lax.dot_general(a, b_t, (((1,), (1,)), ((), ())), preferred_element_type=F32)


def _inproj_kernel(x_ref, nw_ref, w_ref, wg_ref, ib_ref, fb_ref, qw_ref, kw_ref,
                   out_ref, gt_ref, h_scr):
    j = pl.program_id(1)
    halves = (slice(0, SEC), slice(SEC, 2 * SEC))

    def plain(half):
        out_ref[:, half] = _dot_nt(h_scr[...], w_ref[half, :]).astype(BF16)

    def qk_normed(half, w):
        acc = _dot_nt(h_scr[...], w_ref[half, :])
        for g in range(SEC // A_DQK):
            a = acc[:, g * A_DQK:(g + 1) * A_DQK]
            ms = jnp.mean(a * a, axis=1, keepdims=True)
            cols = slice(half.start + g * A_DQK, half.start + (g + 1) * A_DQK)
            out_ref[:, cols] = (a * lax.rsqrt(ms + RMS_EPS) * w).astype(BF16)

    @pl.when(j == 0)
    def _():
        xf = x_ref[...]
        ms = jnp.mean(xf * xf, axis=1, keepdims=True)
        hb = (xf * lax.rsqrt(ms + RMS_EPS) * nw_ref[...]).astype(BF16)
        h_scr[...] = hb
        row = lax.broadcasted_iota(jnp.int32, (N_GATES, 1), 0)
        bias = jnp.zeros((N_GATES, 1), F32)
        for r in range(N_GATES):
            b_r = ib_ref[r] if r < M_HEADS else fb_ref[r - M_HEADS]
            bias = jnp.where(row == r, b_r, bias)
        gt_ref[...] = _dot_nt(hb, wg_ref[...]).T[0:N_GATES, :] + bias
        plain(halves[0])
        plain(halves[1])

    @pl.when(j == 1)
    def _():
        qk_normed(halves[1], qw_ref[...] * (A_DQK ** -0.5 * LOG2E))
        plain(halves[0])

    @pl.when(j == 2)
    def _():
        qk_normed(halves[0], kw_ref[...])
        plain(halves[1])


def _inproj(x2, nw, wt, wt_g, i_bias, f_bias, qw, kw):
    s = x2.shape[0]
    n_sec = wt.shape[0] // SEC
    assert n_sec == 2 * N_SEC_LO == 6
    return pl.pallas_call(
        _inproj_kernel,
        grid=(s // TM_IN, n_sec // 2),
        in_specs=[
            pl.BlockSpec((TM_IN, D_MODEL), lambda i, j: (i, 0)),
            pl.BlockSpec((1, D_MODEL), lambda i, j: (0, 0)),
            pl.BlockSpec((2 * SEC, D_MODEL), lambda i, j: (j, 0)),
            pl.BlockSpec((GATE_PAD, D_MODEL), lambda i, j: (0, 0)),
            pl.BlockSpec(memory_space=pltpu.SMEM),
            pl.BlockSpec(memory_space=pltpu.SMEM),
            pl.BlockSpec((1, A_DQK), lambda i, j: (0, 0)),
            pl.BlockSpec((1, A_DQK), lambda i, j: (0, 0)),
        ],
        out_specs=[
            pl.BlockSpec((TM_IN, 2 * SEC), lambda i, j: (i, j)),
            pl.BlockSpec((N_GATES, TM_IN), lambda i, j: (0, i)),
        ],
        out_shape=[
            jax.ShapeDtypeStruct((s, n_sec * SEC), BF16),
            jax.ShapeDtypeStruct((N_GATES, s), F32),
        ],
        scratch_shapes=[
            pltpu.VMEM((TM_IN, D_MODEL), BF16),
        ],
        compiler_params=_cparams(("arbitrary", "arbitrary")),
        name="inproj",
    )(x2, nw, wt, wt_g, i_bias, f_bias, qw, kw)


def _log_sigmoid(x):
    return jnp.minimum(x, 0.0) - jnp.log1p(jnp.exp(-jnp.abs(x)))


def _mlstm_kernel(qk_ref, halo_ref, v_ref, o_ref, g_ref, cw_ref, cb_ref, nw_ref,
                  wa_ref, wb_ref,
                  out_ref, wa_out_ref, wb_out_ref,
                  stage, act, b_scr, li_scr, ct_scr, n_scr, m_scr):
    i = pl.program_id(0)
    n_chunks = R_ML // CHUNK

    wa_out_ref[...] = wa_ref[...].astype(BF16)
    wb_out_ref[...] = wb_ref[...].astype(BF16)

    @pl.when(i == 0)
    def _():
        ct_scr[...] = jnp.zeros_like(ct_scr)
        n_scr[...] = jnp.zeros_like(n_scr)
        m_scr[...] = jnp.full_like(m_scr, NEG)

    halo = halo_ref[...].astype(F32)
    stage[0:HALO, :] = jnp.where(i == 0, 0.0, halo)
    stage[HALO:HALO + R_ML, :] = qk_ref[...].astype(F32)
    k_scale = M_DQK ** -0.5
    for cs in range(SEC // 128):
        sl = slice(cs * 128, (cs + 1) * 128)
        y = cb_ref[:, sl]
        for t in range(CONV_W):
            y = y + cw_ref[t:t + 1, sl] * stage[HALO - (CONV_W - 1) + t:HALO - (CONV_W - 1) + t + R_ML, sl]
        a = y * jax.nn.sigmoid(y)
        if cs >= M_HEADS:
            a = a * k_scale
        act[:, sl] = a.astype(BF16)

    li_scr[...] = g_ref[0:M_HEADS].reshape(M_HEADS * n_chunks, CHUNK)
    lf = _log_sigmoid(g_ref[M_HEADS:N_GATES].reshape(M_HEADS * n_chunks, CHUNK))
    r_i = lax.broadcasted_iota(jnp.int32, (CHUNK, CHUNK), 0)
    c_i = lax.broadcasted_iota(jnp.int32, (CHUNK, CHUNK), 1)
    upper = jnp.where(r_i <= c_i, 1.0, 0.0).astype(BF16)
    p0 = lf.astype(BF16)
    r1 = lf - p0.astype(F32)
    p1 = r1.astype(BF16)
    p2 = (r1 - p1.astype(F32)).astype(BF16)
    b_scr[...] = (jnp.dot(p0, upper, preferred_element_type=F32)
                  + jnp.dot(p1, upper, preferred_element_type=F32)
                  + jnp.dot(p2, upper, preferred_element_type=F32))

    t_i = lax.broadcasted_iota(jnp.int32, (R_ML, CHUNK), 0) % CHUNK
    s_i = lax.broadcasted_iota(jnp.int32, (R_ML, CHUNK), 1)
    eye = t_i == s_i
    tril = t_i >= s_i

    def rows_of(x, c):
        return x[c * CHUNK:(c + 1) * CHUNK]

    def per_chunk_rows(x):
        return jnp.concatenate(
            [jnp.broadcast_to(x[c:c + 1, :], (CHUNK, x.shape[1])) for c in range(n_chunks)], axis=0)

    def to_col(rows):
        return jnp.sum(jnp.where(eye, rows, 0.0), axis=1, keepdims=True)

    for h in range(M_HEADS):
        qs = slice(h * M_DQK, (h + 1) * M_DQK)
        ks = slice((M_HEADS + h) * M_DQK, (M_HEADS + h + 1) * M_DQK)
        vs = slice(h * M_DV, (h + 1) * M_DV)
        q = act[:, qs]
        k = act[:, ks]
        v = v_ref[:, vs]
        b_h = b_scr[h * n_chunks:(h + 1) * n_chunks, :]
        li_h = li_scr[h * n_chunks:(h + 1) * n_chunks, :]

        b_rows = per_chunk_rows(b_h)
        b_col = to_col(b_rows)
        d = jnp.where(tril, b_col - b_rows + per_chunk_rows(li_h), NEG)
        m_intra = jnp.max(d, axis=1, keepdims=True)
        s = jnp.concatenate(
            [lax.dot_general(rows_of(q, c), rows_of(k, c), (((1,), (1,)), ((), ())),
                             preferred_element_type=F32) for c in range(n_chunks)], axis=0)
        p = jnp.exp(d - m_intra) * s
        row_sum = jnp.sum(p, axis=1, keepdims=True)
        pb = p.astype(BF16)
        n_intra = jnp.concatenate(
            [jnp.dot(rows_of(pb, c), rows_of(v, c), preferred_element_type=F32)
             for c in range(n_chunks)], axis=0)

        g_tot = b_h[:, CHUNK - 1:CHUNK]
        a = g_tot - b_h + li_h
        m_loc = jnp.max(a, axis=1, keepdims=True)
        w_col = to_col(per_chunk_rows(jnp.exp(a - m_loc)))
        kw = k.astype(F32) * w_col
        kwb = kw.astype(BF16)
        ct_locs = [lax.dot_general(rows_of(kwb, c), rows_of(v, c), (((0,), (0,)), ((), ())),
                                   preferred_element_type=F32) for c in range(n_chunks)]
        n_locs = [jnp.sum(rows_of(kw, c), axis=0, keepdims=True) for c in range(n_chunks)]

        m = m_scr[h, 0:1, 0:1]
        ct = ct_scr[h]
        n = n_scr[h, 0:1, :]
        m_prevs, ct_prevs, n_prevs = [], [], []
        for c in range(n_chunks):
            m_prevs.append(m)
            ct_prevs.append(ct.astype(BF16))
            n_prevs.append(n)
            g_c = g_tot[c:c + 1, :]
            m_loc_c = m_loc[c:c + 1, :]
            m_new = jnp.maximum(g_c + m, m_loc_c)
            s_old = jnp.exp(g_c + m - m_new)
            s_loc = jnp.exp(m_loc_c - m_new)
            ct = s_old * ct + s_loc * ct_locs[c]
            n = s_old * n + s_loc * n_locs[c]
            m = m_new
        ct_scr[h] = ct
        n_scr[h] = jnp.broadcast_to(n, (8, M_DQK))
        m_scr[h] = jnp.broadcast_to(m, (8, 128))

        m_prev = per_chunk_rows(jnp.concatenate(m_prevs, axis=0))
        n_prev = per_chunk_rows(jnp.concatenate(n_prevs, axis=0))
        inter = jnp.concatenate(
            [jnp.dot(rows_of(q, c), ct_prevs[c], preferred_element_type=F32)
             for c in range(n_chunks)], axis=0)
        inter_log = b_col + m_prev
        m_t = jnp.maximum(inter_log, m_intra)
        s_intra = jnp.exp(m_intra - m_t)
        s_inter = jnp.exp(inter_log - m_t)
        num = s_intra * n_intra + s_inter * inter
        den = (s_intra * row_sum
               + s_inter * jnp.sum(q.astype(F32) * n_prev, axis=1, keepdims=True))
        hout = num / jnp.maximum(jnp.abs(den), jnp.exp(-m_t))

        ms = jnp.mean(hout * hout, axis=1, keepdims=True)
        hn = hout * lax.rsqrt(ms + RMS_EPS) * nw_ref[h:h + 1, :]
        out_ref[:, vs] = (hn * jax.nn.sigmoid(o_ref[:, vs].astype(F32))).astype(BF16)


def _row_slab(w, n_steps, step_of):
    rows, cols = w.shape
    assert rows % (n_steps * BF16_SUBLANES) == 0, (w.shape, n_steps)
    return pl.BlockSpec((rows // n_steps, cols), lambda *idx: (step_of(*idx), 0))


def _mlstm(proj, gates3, conv_w, conv_b, m_norm_w, w_a, w_b):
    s = proj.shape[0]
    n_chunks = R_ML // CHUNK
    hb = R_ML // HALO
    n_steps = s // R_ML
    return pl.pallas_call(
        _mlstm_kernel,
        grid=(n_steps,),
        in_specs=[
            pl.BlockSpec((R_ML, SEC), lambda i: (i, 0)),
            pl.BlockSpec((HALO, SEC), lambda i: (jnp.maximum(i * hb - 1, 0), 0)),
            pl.BlockSpec((R_ML, SEC), lambda i: (i, 1)),
            pl.BlockSpec((R_ML, SEC), lambda i: (i, 2)),
            pl.BlockSpec((N_GATES, n_chunks, CHUNK), lambda i: (0, i, 0)),
            pl.BlockSpec((CONV_W, SEC), lambda i: (0, 0)),
            pl.BlockSpec((1, SEC), lambda i: (0, 0)),
            pl.BlockSpec((M_HEADS, M_DV), lambda i: (0, 0)),
            _row_slab(w_a, n_steps, lambda i: i),
            _row_slab(w_b, n_steps, lambda i: i),
        ],
        out_specs=[
            pl.BlockSpec((R_ML, SEC), lambda i: (i, 0)),
            _row_slab(w_a, n_steps, lambda i: i),
            _row_slab(w_b, n_steps, lambda i: i),
        ],
        out_shape=[
            jax.ShapeDtypeStruct((s, SEC), BF16),
            jax.ShapeDtypeStruct(w_a.shape, BF16),
            jax.ShapeDtypeStruct(w_b.shape, BF16),
        ],
        scratch_shapes=[
            pltpu.VMEM((R_ML + HALO, SEC), F32),
            pltpu.VMEM((R_ML, SEC), BF16),
            pltpu.VMEM((M_HEADS * n_chunks, CHUNK), F32),
            pltpu.VMEM((M_HEADS * n_chunks, CHUNK), F32),
            pltpu.VMEM((M_HEADS, M_DQK, M_DV), F32),
            pltpu.VMEM((M_HEADS, 8, M_DQK), F32),
            pltpu.VMEM((M_HEADS, 8, 128), F32),
        ],
        compiler_params=_cparams(("arbitrary",)),
        name="mlstm",
    )(proj, proj, proj, proj, gates3, conv_w, conv_b, m_norm_w, w_a, w_b)


def _lane_partial_sum(p):
    acc = p[:, 0:128]
    for t in range(1, p.shape[1] // 128):
        acc = acc + p[:, t * 128:(t + 1) * 128]
    return acc


def _attn_kernel(par_ref, q_ref, k_ref, v_ref, lq1_ref, lk1_ref, lq2_ref, lk2_ref, nw_ref,
                 wa_ref, wb_ref,
                 out_ref, wa_out_ref, wb_out_ref,
                 m_scr, l_scr, acc_scr, sa_scr, sb_scr):
    i = pl.program_id(1)
    nq = pl.num_programs(1)
    shift = par_ref[0]
    fixed = par_ref[1] > 0.5

    def cast_weight_slabs():
        wa_out_ref[...] = wa_ref[...].astype(BF16)
        wb_out_ref[...] = wb_ref[...].astype(BF16)

    def reset_sums():
        l_scr[...] = jnp.zeros_like(l_scr)
        acc_scr[...] = jnp.zeros_like(acc_scr)

    ALL, LOWER = slice(0, TQ), slice(TK, TQ)

    def chunk_mask(n_rows):
        qc = lax.broadcasted_iota(jnp.int32, (n_rows, TK), 0) // CHUNK
        kc = lax.broadcasted_iota(jnp.int32, (n_rows, TK), 1) // CHUNK
        return kc <= qc

    def k_rows(j):
        return pl.ds(pl.multiple_of(j * TK, TK), TK)

    def scores(qi, j, c, rows):
        q0 = pl.multiple_of(qi * TQ + rows.start, TK)
        q = q_ref[pl.ds(q0, rows.stop - rows.start), c * A_DQK:(c + 1) * A_DQK]
        k = k_ref[k_rows(j), c * A_DQK:(c + 1) * A_DQK]
        return lax.dot_general(q, k, (((1,), (1,)), ((), ())), preferred_element_type=F32)

    def fixed_scores(qi, j, s_buf, rows=ALL):
        for c in range(2):
            s_buf[c, rows, :] = scores(qi, j, c, rows)

    def fixed_pv(j, s_buf, rows=ALL, masked=False):
        v = v_ref[k_rows(j), :]
        for c in range(2):
            p = jnp.exp2(s_buf[c, rows, :] - shift)
            if masked:
                p = jnp.where(chunk_mask(rows.stop - rows.start), p, 0.0)
            l_scr[c, rows, :] += _lane_partial_sum(p)
            acc_scr[c, rows, :] += jnp.dot(p.astype(BF16), v, preferred_element_type=F32)

    def online_tile(j, rows=ALL, masked=False):
        v = v_ref[k_rows(j), :]
        for c in range(2):
            s = scores(i, j, c, rows)
            if masked:
                s = jnp.where(chunk_mask(rows.stop - rows.start), s, NEG)
            m_prev = m_scr[c, rows, :]
            m_new = jnp.maximum(m_prev, jnp.max(s, axis=1, keepdims=True))
            alpha = jnp.exp2(m_prev - m_new)
            p = jnp.exp2(s - m_new[:, 0:1])
            l_scr[c, rows, :] = alpha * l_scr[c, rows, :] + _lane_partial_sum(p)
            acc_scr[c, rows, :] = (alpha[:, 0:1] * acc_scr[c, rows, :]
                                   + jnp.dot(p.astype(BF16), v, preferred_element_type=F32))
            m_scr[c, rows, :] = m_new

    def finalize():
        lam = (jnp.exp(jnp.sum(lq1_ref[...] * lk1_ref[...], axis=1, keepdims=True))
               - jnp.exp(jnp.sum(lq2_ref[...] * lk2_ref[...], axis=1, keepdims=True)) + LAM_INIT)
        inv0 = 1.0 / jnp.sum(l_scr[0], axis=1, keepdims=True)
        inv1 = lam / jnp.sum(l_scr[1], axis=1, keepdims=True)
        o = acc_scr[0] * inv0 - acc_scr[1] * inv1
        ms = jnp.mean(o * o, axis=1, keepdims=True)
        scale = lax.rsqrt(ms + RMS_EPS) * (1.0 - LAM_INIT)
        nw = nw_ref[pl.ds(pl.program_id(0), 1), :]
        out_ref[...] = (o * scale * nw).astype(BF16)

    @pl.when(fixed)
    def _():
        @pl.when(i == 0)
        def _():
            reset_sums()
            fixed_scores(0, 0, sa_scr)

        def pair(t, carry):
            j = 2 * t
            fixed_pv(j, sa_scr)
            fixed_scores(i, j + 1, sb_scr)
            fixed_pv(j + 1, sb_scr)
            fixed_scores(i, j + 2, sa_scr)
            return carry
        lax.fori_loop(0, i, pair, 0)

        def boundary_tiles(more_steps):
            cast_weight_slabs()
            fixed_pv(2 * i, sa_scr, masked=True)
            fixed_scores(i, 2 * i + 1, sb_scr, LOWER)
            fixed_pv(2 * i + 1, sb_scr, LOWER, masked=True)
            finalize()
            if more_steps:
                reset_sums()
                fixed_scores(i + 1, 0, sa_scr)

        @pl.when(i + 1 < nq)
        def _():
            boundary_tiles(True)

        @pl.when(i + 1 == nq)
        def _():
            boundary_tiles(False)

    @pl.when(jnp.logical_not(fixed))
    def _():
        cast_weight_slabs()
        reset_sums()
        m_scr[...] = jnp.full_like(m_scr, NEG)

        def body(j, carry):
            online_tile(j)
            return carry
        lax.fori_loop(0, 2 * i, body, 0)
        online_tile(2 * i, masked=True)
        online_tile(2 * i + 1, LOWER, masked=True)
        finalize()


def _attn(proj, q_norm_w, k_norm_w, lq1, lk1, lq2, lk2, a_norm_w, w_a, w_b):
    s = proj.shape[0]
    qb = 3 * SEC // A_DV
    kb = 4 * SEC // A_DV
    vb = 5 * SEC // A_DV
    bound = (A_DQK ** 0.5 * LOG2E) * jnp.max(jnp.abs(q_norm_w * k_norm_w))
    shift = jnp.ceil(bound * 1.02) + 1.0
    params = jnp.stack([shift, (shift <= MAX_FIXED_SHIFT).astype(F32)]).astype(F32)
    vec = pl.BlockSpec((1, A_DQK), lambda h, i: (0, 0))
    nq = s // TQ
    n_steps = A_HEADS * nq
    return pl.pallas_call(
        _attn_kernel,
        grid=(A_HEADS, nq),
        in_specs=[
            pl.BlockSpec(memory_space=pltpu.SMEM),
            pl.BlockSpec((s, 2 * A_DQK), lambda h, i: (0, qb + h)),
            pl.BlockSpec((s, 2 * A_DQK), lambda h, i: (0, kb + h)),
            pl.BlockSpec((s, A_DV), lambda h, i: (0, vb + h)),
            vec, vec, vec, vec,
            pl.BlockSpec((A_HEADS, A_DV), lambda h, i: (0, 0)),
            _row_slab(w_a, n_steps, lambda h, i: h * nq + i),
            _row_slab(w_b, n_steps, lambda h, i: h * nq + i),
        ],
        out_specs=[
            pl.BlockSpec((TQ, A_DV), lambda h, i: (i, h)),
            _row_slab(w_a, n_steps, lambda h, i: h * nq + i),
            _row_slab(w_b, n_steps, lambda h, i: h * nq + i),
        ],
        out_shape=[
            jax.ShapeDtypeStruct((s, A_HEADS * A_DV), BF16),
            jax.ShapeDtypeStruct(w_a.shape, BF16),
            jax.ShapeDtypeStruct(w_b.shape, BF16),
        ],
        scratch_shapes=[
            pltpu.VMEM((2, TQ, 128), F32),
            pltpu.VMEM((2, TQ, 128), F32),
            pltpu.VMEM((2, TQ, A_DV), F32),
            pltpu.VMEM((2, TQ, TK), F32),
            pltpu.VMEM((2, TQ, TK), F32),
        ],
        compiler_params=_cparams(("arbitrary", "arbitrary")),
        name="diffattn",
    )(params, proj, proj, proj, lq1, lk1, lq2, lk2, a_norm_w, w_a, w_b)


def _outproj_kernel(hm_ref, ha_ref, wt_ref, wb_ref, x_ref, out_ref):
    out_ref[...] = (x_ref[...]
                    + jnp.dot(hm_ref[...], wt_ref[...], preferred_element_type=F32)
                    + jnp.dot(ha_ref[...], wb_ref[...], preferred_element_type=F32))


def _outproj(hm, ha, w_out, x2):
    s = x2.shape[0]
    return pl.pallas_call(
        _outproj_kernel,
        grid=(s // TM_OUT,),
        in_specs=[
            pl.BlockSpec((TM_OUT, SEC), lambda i: (i, 0)),
            pl.BlockSpec((TM_OUT, SEC), lambda i: (i, 0)),
            pl.BlockSpec((SEC, D_MODEL), lambda i: (0, 0)),
            pl.BlockSpec((SEC, D_MODEL), lambda i: (1, 0)),
            pl.BlockSpec((TM_OUT, D_MODEL), lambda i: (i, 0)),
        ],
        out_specs=pl.BlockSpec((TM_OUT, D_MODEL), lambda i: (i, 0)),
        out_shape=jax.ShapeDtypeStruct((s, D_MODEL), F32),
        compiler_params=_cparams(("arbitrary",)),
        name="outproj",
    )(hm, ha, w_out, w_out, x2)


def _ffn_kernel(x_hbm, nw_ref, wg_ref, wu_ref, wd_ref, out_ref, h_scr, x_buf, x_sem):
    i = pl.program_id(0)
    j = pl.program_id(1)

    def x_copy(blk):
        rows = pl.ds(pl.multiple_of(blk * TM_FF, TM_FF), TM_FF)
        return pltpu.make_async_copy(x_hbm.at[rows, :], x_buf, x_sem)

    def tile(h, t):
        cols = slice(t * TF_FF, (t + 1) * TF_FF)
        g = jnp.dot(h, wg_ref[:, cols], preferred_element_type=F32)
        u = jnp.dot(h, wu_ref[:, cols], preferred_element_type=F32)
        a = (g * jax.nn.sigmoid(g) * u).astype(BF16)
        return jnp.dot(a, wd_ref[cols, :], preferred_element_type=F32)

    def tiles(h, n):
        acc = tile(h, 0)
        for t in range(1, n):
            acc = acc + tile(h, t)
        return acc

    @pl.when(j == 0)
    def _():
        @pl.when(i == 0)
        def _():
            x_copy(0).start()
        x_copy(i).wait()
        xf = x_buf[...]
        ms = jnp.mean(xf * xf, axis=1, keepdims=True)
        hb = (xf * lax.rsqrt(ms + RMS_EPS) * nw_ref[...]).astype(BF16)
        h_scr[...] = hb
        out_ref[...] = xf + tiles(hb, FF_PER_STEP)

    @pl.when(jnp.logical_and(j == 1, i + 1 < pl.num_programs(0)))
    def _():
        x_copy(i + 1).start()

    last = pl.num_programs(1) - 1

    @pl.when(jnp.logical_and(j > 0, j < last))
    def _():
        out_ref[...] += tiles(h_scr[...], FF_PER_STEP)

    @pl.when(j == last)
    def _():
        out_ref[...] += tiles(h_scr[...], FF_LAST)


def _ffn(x1, nw, w_gate, w_up, w_down):
    s = x1.shape[0]
    return pl.pallas_call(
        _ffn_kernel,
        grid=(s // TM_FF, FF_STEPS),
        in_specs=[
            pl.BlockSpec(memory_space=pl.ANY),
            pl.BlockSpec((1, D_MODEL), lambda i, j: (0, 0)),
            pl.BlockSpec((D_MODEL, FF_PER_STEP * TF_FF), lambda i, j: (0, j)),
            pl.BlockSpec((D_MODEL, FF_PER_STEP * TF_FF), lambda i, j: (0, j)),
            pl.BlockSpec((FF_PER_STEP * TF_FF, D_MODEL), lambda i, j: (j, 0)),
        ],
        out_specs=pl.BlockSpec((TM_FF, D_MODEL), lambda i, j: (i, 0)),
        out_shape=jax.ShapeDtypeStruct((s, D_MODEL), F32),
        scratch_shapes=[
            pltpu.VMEM((TM_FF, D_MODEL), BF16),
            pltpu.VMEM((TM_FF, D_MODEL), F32),
            pltpu.SemaphoreType.DMA(()),
        ],
        compiler_params=_cparams(("arbitrary", "arbitrary"), VMEM_LIMIT_FF),
        name="swiglu",
    )(x1, nw, w_gate, w_up, w_down)


def _layer(x2, norm1_w, w_in, conv_w, conv_b, i_bias, f_bias, m_norm_w, q_norm_w, k_norm_w,
           lambda_q1, lambda_k1, lambda_q2, lambda_k2, a_norm_w, w_out, norm2_w,
           w_gate, w_up, w_down):
    s = x2.shape[0]
    wt, wt_g = _wprep(w_in.T)
    proj, gates_t = _inproj(x2, norm1_w[None, :], wt, wt_g, i_bias, f_bias,
                            q_norm_w[None, :], k_norm_w[None, :])
    gates3 = gates_t.reshape(N_GATES, s // CHUNK, CHUNK)
    hm, w_out_b, w_down_b = _mlstm(proj, gates3, conv_w, conv_b[None, :],
                                   m_norm_w, w_out, w_down)
    ha, w_gate_b, w_up_b = _attn(proj, q_norm_w, k_norm_w, lambda_q1[None, :],
                                 lambda_k1[None, :], lambda_q2[None, :], lambda_k2[None, :],
                                 a_norm_w, w_gate, w_up)
    x1 = _outproj(hm, ha, w_out_b, x2)
    return _ffn(x1, norm2_w[None, :], w_gate_b, w_up_b, w_down_b)


def kernel(x, norm1_w, w_in, conv_w, conv_b, i_bias, f_bias, m_norm_w, q_norm_w, k_norm_w,
           lambda_q1, lambda_k1, lambda_q2, lambda_k2, a_norm_w, w_out, norm2_w,
           w_gate, w_up, w_down):
    b, s, d = x.shape
    assert d == D_MODEL and b == 1 and norm1_w.shape[0] == 1
    assert s % R_ML == 0 and s % TQ == 0 and s % TM_IN == 0 and s % TM_FF == 0
    y = _layer(x.reshape(s, d), norm1_w[0], w_in[0], conv_w[0], conv_b[0], i_bias[0], f_bias[0],
               m_norm_w[0], q_norm_w[0], k_norm_w[0], lambda_q1[0], lambda_k1[0],
               lambda_q2[0], lambda_k2[0], a_norm_w[0], w_out[0], norm2_w[0],
               w_gate[0], w_up[0], w_down[0])
    return y.reshape(b, s, d)
```
